```python
import math
import jax, jax.numpy as jnp
from jax import lax
import numpy as np

D_MODEL = 1024
BATCH = 4
SEQ = 4096
DEPTH = 1

MEM_LEN = 256
A_HEADS = 8
A_HEAD_DIM = 64
A_WIDTH = A_HEADS * A_HEAD_DIM
MOBA_BLOCK = 256
MOBA_TOPK = 3
MOBA_QCHUNK = 32
B_HEADS = 4
B_KEY_DIM = 64
B_VAL_DIM = 128
B_KEY_WIDTH = B_HEADS * B_KEY_DIM
B_WIDTH = B_HEADS * B_VAL_DIM
GLA_RANK = 16
GLA_TAU = 16.0
GLA_CHUNK = 64
C_HEADS = 4
C_HEAD_DIM = 128
C_WIDTH = C_HEADS * C_HEAD_DIM
MIX_WIDTH = A_WIDTH + B_WIDTH + C_WIDTH
REL_BUCKETS = 32
REL_MAX_DIST = 128
RMS_EPS = 1e-6
NEG = -1e30
IN_COLS = 4 * A_WIDTH + 2 * B_KEY_WIDTH + 2 * B_WIDTH + GLA_RANK + 2 * C_WIDTH

kernel_name = "hymba_moba_gla_xattn_layer"


def rmsnorm(x, w):
    xf = x.astype(jnp.float32)
    y = xf * lax.rsqrt(jnp.mean(xf * xf, axis=-1, keepdims=True) + RMS_EPS)
    return (y * w.astype(jnp.float32)).astype(x.dtype)


def t5_bucket(rel):
    n = jnp.maximum(rel, 0)
    max_exact = REL_BUCKETS // 2
    is_small = n < max_exact
    nf = jnp.maximum(n, max_exact).astype(jnp.float32)
    large = max_exact + (jnp.log(nf / max_exact) / math.log(REL_MAX_DIST / max_exact)
                         * (REL_BUCKETS - max_exact)).astype(jnp.int32)
    large = jnp.minimum(large, REL_BUCKETS - 1)
    return jnp.where(is_small, n, large)


def moba_attention(q, k, v, rel_bias):
    B, S, H, dh = q.shape
    s_pad = -(-S // MOBA_BLOCK) * MOBA_BLOCK
    pad = ((0, 0), (0, s_pad - S), (0, 0), (0, 0))
    q, k, v = jnp.pad(q, pad), jnp.pad(k, pad), jnp.pad(v, pad)
    nb = s_pad // MOBA_BLOCK
    topk = min(MOBA_TOPK, nb)
    n_chunks = s_pad // MOBA_QCHUNK
    scale = dh ** -0.5
    kbh = k.reshape(B, nb, MOBA_BLOCK, H, dh).transpose(0, 3, 1, 2, 4)
    vbh = v.reshape(B, nb, MOBA_BLOCK, H, dh).transpose(0, 3, 1, 2, 4)
    kmean = jnp.mean(kbh.astype(jnp.float32), axis=3)
    qch = q.reshape(B, n_chunks, MOBA_QCHUNK, H, dh).transpose(1, 0, 3, 2, 4)
    table_t = rel_bias.T.astype(jnp.float32)
    head_idx = jnp.arange(H)[None, :, None, None, None]
    blk_ar = jnp.arange(MOBA_BLOCK, dtype=jnp.int32)
    gather = jax.vmap(jax.vmap(lambda kk, ii: kk[ii]))

    def chunk_fn(args):
        qh, ci = args
        pos = ci * MOBA_QCHUNK + jnp.arange(MOBA_QCHUNK, dtype=jnp.int32)
        own = (ci * MOBA_QCHUNK) // MOBA_BLOCK
        gate = jnp.einsum('bhqd,bhnd->bhqn', qh.astype(jnp.float32), kmean)
        past = jnp.arange(nb) < own
        gate = jnp.where(past[None, None, None], gate, NEG)
        _, idx = lax.top_k(gate, topk)
        valid = idx < own
        k_sel = gather(kbh, idx)
        v_sel = gather(vbh, idx)
        key_pos = idx[..., None] * MOBA_BLOCK + blk_ar
        bias_sel = table_t[head_idx, t5_bucket(pos[None, None, :, None, None] - key_pos)]
        s_sel = jnp.einsum('bhqd,bhqnjd->bhqnj', qh, k_sel).astype(jnp.float32) * scale + bias_sel
        s_sel = jnp.where(valid[..., None], s_sel, NEG)
        k_own = lax.dynamic_index_in_dim(kbh, own, axis=2, keepdims=False)
        v_own = lax.dynamic_index_in_dim(vbh, own, axis=2, keepdims=False)
        own_pos = own * MOBA_BLOCK + blk_ar
        bias_own = table_t[:, t5_bucket(pos[:, None] - own_pos[None, :])]
        s_own = jnp.einsum('bhqd,bhjd->bhqj', qh, k_own).astype(jnp.float32) * scale + bias_own[None]
        causal = own_pos[None, :] <= pos[:, None]
        s_own = jnp.where(causal[None, None], s_own, NEG)
        logits = jnp.concatenate([s_own, s_sel.reshape(B, H, MOBA_QCHUNK, topk * MOBA_BLOCK)], axis=-1)
        p = jax.nn.softmax(logits, axis=-1).astype(v.dtype)
        p_own = p[..., :MOBA_BLOCK]
        p_sel = p[..., MOBA_BLOCK:].reshape(B, H, MOBA_QCHUNK, topk, MOBA_BLOCK)
        o = (jnp.einsum('bhqj,bhjd->bhqd', p_own, v_own)
             + jnp.einsum('bhqnj,bhqnjd->bhqd', p_sel, v_sel))
        return o.transpose(0, 2, 1, 3)

    out = lax.map(chunk_fn, (qch, jnp.arange(n_chunks, dtype=jnp.int32)))
    out = out.transpose(1, 0, 2, 3, 4).reshape(B, s_pad, H, dh)
    return out[:, :S]


def gla_attention(q, k, v, g):
    B, S, H, dk = q.shape
    dv = v.shape[-1]
    n = S // GLA_CHUNK

    def to_chunks(t):
        return t.astype(jnp.float32).reshape(B, n, GLA_CHUNK, H, t.shape[-1]).transpose(1, 0, 3, 2, 4)

    qs = to_chunks(q) * (dk ** -0.5)
    ks, vs, gs = to_chunks(k), to_chunks(v), to_chunks(g)
    mask = jnp.tril(jnp.ones((GLA_CHUNK, GLA_CHUNK), dtype=bool))[None, None, :, :, None]

    def step(state, inp):
        qc, kc, vc, gc = inp
        b = jnp.cumsum(gc, axis=2)
        o_inter = jnp.einsum('bhtk,bhkv->bhtv', qc * jnp.exp(b), state)
        diff = b[:, :, :, None, :] - b[:, :, None, :, :]
        decay = jnp.where(mask, jnp.exp(jnp.where(mask, diff, 0.0)), 0.0)
        att = jnp.einsum('bhtk,bhsk,bhtsk->bhts', qc, kc, decay)
        o_intra = jnp.einsum('bhts,bhsv->bhtv', att, vc)
        b_last = b[:, :, -1:, :]
        new_state = (jnp.exp(b_last[:, :, 0, :])[..., None] * state
                     + jnp.einsum('bhsk,bhsv->bhkv', kc * jnp.exp(b_last - b), vc))
        return new_state, o_inter + o_intra

    state0 = jnp.zeros((B, H, dk, dv), jnp.float32)
    _, ys = lax.scan(step, state0, (qs, ks, vs, gs))
    return ys.transpose(1, 0, 3, 2, 4).reshape(B, S, H, dv)


def memory_cross_attention(q, km, vm):
    dh = q.shape[-1]
    s = jnp.einsum('bshd,bmhd->bhsm', q, km).astype(jnp.float32) * (dh ** -0.5)
    p = jax.nn.softmax(s, axis=-1).astype(vm.dtype)
    return jnp.einsum('bhsm,bmhd->bshd', p, vm)


def setup_inputs(seed: int = 0) -> dict:
    key = jax.random.key(seed)
    ks = jax.random.split(key, 14)
    f32 = jnp.float32
    x = jax.random.normal(ks[0], (BATCH, SEQ, D_MODEL), f32)
    mem = jax.random.normal(ks[1], (BATCH, MEM_LEN, D_MODEL), f32)
    norm_w = 1.0 + 0.01 * jax.random.normal(ks[2], (DEPTH, D_MODEL), f32)
    w_in = jax.random.normal(ks[3], (DEPTH, D_MODEL, IN_COLS), f32) * D_MODEL ** -0.5
    w_alpha2 = jax.random.normal(ks[4], (DEPTH, GLA_RANK, B_KEY_WIDTH), f32) * GLA_RANK ** -0.5
    b_alpha = 0.1 * jax.random.normal(ks[5], (DEPTH, B_KEY_WIDTH), f32)
    gla_norm_w = 1.0 + 0.01 * jax.random.normal(ks[6], (DEPTH, B_VAL_DIM), f32)
    mem_norm_w = 1.0 + 0.01 * jax.random.normal(ks[7], (DEPTH, D_MODEL), f32)
    w_mem_kv = jax.random.normal(ks[8], (DEPTH, D_MODEL, 2 * C_WIDTH), f32) * D_MODEL ** -0.5
    w_out = jax.random.normal(ks[9], (DEPTH, MIX_WIDTH, D_MODEL), f32) * MIX_WIDTH ** -0.5
    rel_bias = 0.5 * jax.random.normal(ks[10], (REL_BUCKETS, A_HEADS), f32)
    final_norm_w = 1.0 + 0.01 * jax.random.normal(ks[11], (D_MODEL,), f32)
    return {"x": x, "mem": mem, "norm_w": norm_w, "w_in": w_in, "w_alpha2": w_alpha2,
            "b_alpha": b_alpha, "gla_norm_w": gla_norm_w, "mem_norm_w": mem_norm_w,
            "w_mem_kv": w_mem_kv, "w_out": w_out, "rel_bias": rel_bias,
            "final_norm_w": final_norm_w}


def reference(x, mem, norm_w, w_in, w_alpha2, b_alpha, gla_norm_w, mem_norm_w,
              w_mem_kv, w_out, rel_bias, final_norm_w):
    B, S, _ = x.shape
    sizes = (A_WIDTH, A_WIDTH, A_WIDTH, A_WIDTH,
             B_KEY_WIDTH, B_KEY_WIDTH, B_WIDTH, B_WIDTH, GLA_RANK,
             C_WIDTH, C_WIDTH)
    points = [int(p) for p in np.cumsum(sizes)[:-1]]
    h = x
    for l in range(DEPTH):
        u = rmsnorm(h, norm_w[l])
        proj = u @ w_in[l]
        qa, ka, va, ga, qb, kb, vb, gb, zb, qc, gc = jnp.split(proj, points, axis=-1)
        oa = moba_attention(qa.reshape(B, S, A_HEADS, A_HEAD_DIM),
                            ka.reshape(B, S, A_HEADS, A_HEAD_DIM),
                            va.reshape(B, S, A_HEADS, A_HEAD_DIM),
                            rel_bias).reshape(B, S, A_WIDTH)
        g_log = jax.nn.log_sigmoid((zb @ w_alpha2[l] + b_alpha[l]).astype(jnp.float32)) / GLA_TAU
        ob = gla_attention(qb.reshape(B, S, B_HEADS, B_KEY_DIM),
                           kb.reshape(B, S, B_HEADS, B_KEY_DIM),
                           vb.reshape(B, S, B_HEADS, B_VAL_DIM),
                           g_log.reshape(B, S, B_HEADS, B_KEY_DIM))
        ob = rmsnorm(ob, gla_norm_w[l]).reshape(B, S, B_WIDTH).astype(x.dtype)
        mkv = rmsnorm(mem, mem_norm_w[l]) @ w_mem_kv[l]
        km, vm = jnp.split(mkv, 2, axis=-1)
        M = mem.shape[1]
        oc = memory_cross_attention(qc.reshape(B, S, C_HEADS, C_HEAD_DIM),
                                    km.reshape(B, M, C_HEADS, C_HEAD_DIM),
                                    vm.reshape(B, M, C_HEADS, C_HEAD_DIM)).reshape(B, S, C_WIDTH)
        mixed = jnp.concatenate([oa * jax.nn.silu(ga), ob * jax.nn.silu(gb),
                                 oc * jax.nn.silu(gc)], axis=-1)
        h = h + mixed @ w_out[l]
    return rmsnorm(h, final_norm_w)
```

```python
import functools
import math

import jax
import jax.numpy as jnp
from jax import lax
from jax.experimental import pallas as pl
from jax.experimental.pallas import tpu as pltpu

F32 = jnp.float32
BF16 = jnp.bfloat16

D_MODEL = 1024
MEM_LEN = 256
A_HEADS = 8
A_HEAD_DIM = 64
A_WIDTH = 512
MOBA_BLOCK = 256
MOBA_TOPK = 3
B_HEADS = 4
B_KEY_DIM = 64
B_VAL_DIM = 128
B_KEY_WIDTH = 256
B_WIDTH = 512
GLA_RANK = 16
GLA_TAU = 16.0
C_HEADS = 4
C_HEAD_DIM = 128
C_WIDTH = 512
REL_BUCKETS = 32
REL_MAX_DIST = 128
RMS_EPS = 1e-6
NEG = -1e30

LANES = 128
ZB_PAD = LANES
PROJ_TM = 512
OUT_TM = 256
GLA_CHUNK = 256
VMEM_LIMIT = 56 * 1024 * 1024


def _dot(a, b):
    return jnp.dot(a, b, preferred_element_type=F32)


def _dot_nt(a, b):
    return lax.dot_general(a, b, (((1,), (1,)), ((), ())), preferred_element_type=F32)


def _rmsnorm_rows(x, w):
    return x * lax.rsqrt(jnp.mean(x * x, axis=-1, keepdims=True) + RMS_EPS) * w


def _memkv_kernel(mem_ref, nw_ref, w_ref, km_ref, vm_ref):
    u = _rmsnorm_rows(mem_ref[0], nw_ref[...]).astype(BF16)
    km_ref[0] = _dot(u, w_ref[:, :C_WIDTH]).astype(BF16)
    vm_ref[0] = _dot(u, w_ref[:, C_WIDTH:]).astype(BF16)


def _memkv(mem, nw, w_bf16):
    B = mem.shape[0]
    return pl.pallas_call(
        _memkv_kernel,
        grid=(B,),
        in_specs=[
            pl.BlockSpec((1, MEM_LEN, D_MODEL), lambda b: (b, 0, 0)),
            pl.BlockSpec((1, D_MODEL), lambda b: (0, 0)),
            pl.BlockSpec((D_MODEL, 2 * C_WIDTH), lambda b: (0, 0)),
        ],
        out_specs=[
            pl.BlockSpec((1, MEM_LEN, C_WIDTH), lambda b: (b, 0, 0)),
            pl.BlockSpec((1, MEM_LEN, C_WIDTH), lambda b: (b, 0, 0)),
        ],
        out_shape=[jax.ShapeDtypeStruct((B, MEM_LEN, C_WIDTH), BF16)] * 2,
        name="memkv",
    )(mem, nw, w_bf16)


def _t5_bucket(n):
    max_exact = REL_BUCKETS // 2
    nf = jnp.maximum(n, max_exact).astype(F32)
    large = max_exact + (jnp.log(nf / max_exact) / math.log(REL_MAX_DIST / max_exact)
                         * (REL_BUCKETS - max_exact)).astype(jnp.int32)
    large = jnp.minimum(large, REL_BUCKETS - 1)
    return jnp.where(n < max_exact, n, large)


def _bias_kernel(rb_ref, own_ref, prev_ref):
    h = pl.program_id(0)
    key = lax.broadcasted_iota(jnp.int32, (MOBA_BLOCK, MOBA_BLOCK), 0)
    qry = lax.broadcasted_iota(jnp.int32, (MOBA_BLOCK, MOBA_BLOCK), 1)
    rel = qry - key
    b_own = _t5_bucket(jnp.maximum(rel, 0))
    b_prev = _t5_bucket(rel + MOBA_BLOCK)
    own = jnp.zeros((MOBA_BLOCK, MOBA_BLOCK), F32)
    prev = jnp.zeros((MOBA_BLOCK, MOBA_BLOCK), F32)
    for bk in range(REL_BUCKETS):
        val = rb_ref[bk, h]
        own = jnp.where(b_own == bk, val, own)
        prev = jnp.where(b_prev == bk, val, prev)
    own_ref[0] = jnp.where(rel >= 0, own, NEG)
    prev_ref[0] = prev


def _bias_tiles(rel_bias):
    return pl.pallas_call(
        _bias_kernel,
        grid=(A_HEADS,),
        in_specs=[pl.BlockSpec(memory_space=pltpu.SMEM)],
        out_specs=[pl.BlockSpec((1, MOBA_BLOCK, MOBA_BLOCK), lambda h: (h, 0, 0))] * 2,
        out_shape=[jax.ShapeDtypeStruct((A_HEADS, MOBA_BLOCK, MOBA_BLOCK), F32)] * 2,
        name="t5bias",
    )(rel_bias)


_ROW_COLS = (("ka", A_WIDTH), ("ga", A_WIDTH), ("qb", B_KEY_WIDTH), ("kb", B_KEY_WIDTH),
             ("vb", B_WIDTH), ("gb", B_WIDTH), ("qc", C_WIDTH), ("gc", C_WIDTH), ("zb", ZB_PAD))
_ROW_OFF = {}
_off = 0
for _name, _width in _ROW_COLS:
    _ROW_OFF[_name] = (_off, _off + _width)
    _off += _width
ROW_COLS_TOTAL = _off


def _proj_kernel(x_ref, nw_ref, wt_ref, wr_ref, wa_ref, ba_ref,
                 qaT_ref, vaT_ref, ka_ref, ga_ref, qb_ref, kb_ref, vb_ref, gb_ref,
                 qc_ref, gc_ref, g_ref):
    u = _rmsnorm_rows(x_ref[0], nw_ref[...]).astype(BF16)
    qaT_ref[0] = _dot_nt(wt_ref[:A_WIDTH, :], u)
    vaT_ref[0] = _dot_nt(wt_ref[A_WIDTH:, :], u).astype(BF16)

    def row(name):
        lo, hi = _ROW_OFF[name]
        return _dot(u, wr_ref[:, lo:hi])

    ka_ref[0] = row("ka").astype(BF16)
    ga_ref[0] = row("ga")
    qb_ref[0] = row("qb")
    kb_ref[0] = row("kb")
    vb_ref[0] = row("vb")
    gb_ref[0] = row("gb")
    qc_ref[0] = row("qc")
    gc_ref[0] = row("gc")
    zb = row("zb")
    z = jnp.dot(zb, wa_ref[...], preferred_element_type=F32,
                precision=lax.Precision.HIGHEST) + ba_ref[...]
    g_ref[0] = (jnp.minimum(z, 0.0) - jnp.log1p(jnp.exp(-jnp.abs(z)))) * (1.0 / GLA_TAU)


def _proj(x, nw, wt, wr, wa, ba):
    B, S, D = x.shape
    tm = PROJ_TM
    row_spec = lambda width: pl.BlockSpec((1, tm, width), lambda b, i: (b, i, 0))
    col_spec = pl.BlockSpec((1, A_WIDTH, tm), lambda b, i: (b, 0, i))
    const = lambda shape: pl.BlockSpec(shape, lambda b, i: (0,) * len(shape))
    sds = jax.ShapeDtypeStruct
    return pl.pallas_call(
        _proj_kernel,
        grid=(B, S // tm),
        in_specs=[
            pl.BlockSpec((1, tm, D), lambda b, i: (b, i, 0)),
            const((1, D)),
            const((2 * A_WIDTH, D)),
            const((D, ROW_COLS_TOTAL)),
            const((ZB_PAD, B_KEY_WIDTH)),
            const((1, B_KEY_WIDTH)),
        ],
        out_specs=[col_spec, col_spec, row_spec(A_WIDTH), row_spec(A_WIDTH),
                   row_spec(B_KEY_WIDTH), row_spec(B_KEY_WIDTH), row_spec(B_WIDTH),
                   row_spec(B_WIDTH), row_spec(C_WIDTH), row_spec(C_WIDTH),
                   row_spec(B_KEY_WIDTH)],
        out_shape=[sds((B, A_WIDTH, S), F32), sds((B, A_WIDTH, S), BF16),
                   sds((B, S, A_WIDTH), BF16), sds((B, S, A_WIDTH), F32),
                   sds((B, S, B_KEY_WIDTH), F32), sds((B, S, B_KEY_WIDTH), F32),
                   sds((B, S, B_WIDTH), F32), sds((B, S, B_WIDTH), F32),
                   sds((B, S, C_WIDTH), F32), sds((B, S, C_WIDTH), F32),
                   sds((B, S, B_KEY_WIDTH), F32)],
        compiler_params=pltpu.CompilerParams(
            dimension_semantics=("arbitrary", "arbitrary"), vmem_limit_bytes=VMEM_LIMIT),
        name="proj",
    )(x, nw, wt, wr, wa, ba)


HEADS_PER_STEP = LANES // A_HEAD_DIM


def _moba_kernel(rb_ref, qT_ref, k_ref, vT_ref, own_ref, prev_ref, o_ref,
                 kmean_scr, pen_scr):
    pair = pl.program_id(1)
    i = pl.program_id(2)
    nb = kmean_scr.shape[0]
    blk = MOBA_BLOCK
    dh = A_HEAD_DIM

    @pl.when(i == 0)
    def _():
        for n in range(nb):
            kb = k_ref[0, n * blk:(n + 1) * blk, :].astype(F32)
            kmean_scr[n:n + 1, :] = jnp.mean(kb, axis=0, keepdims=True)

    qT = qT_ref[0]
    sub = lax.broadcasted_iota(jnp.int32, (LANES, blk), 0)
    lane = lax.broadcasted_iota(jnp.int32, (nb, LANES), 1)
    brow = lax.broadcasted_iota(jnp.int32, (nb, blk), 0)
    kmean = kmean_scr[...]
    qs = qT * (dh ** -0.5)

    qTh = []
    for hh in range(HEADS_PER_STEP):
        in_head = (sub >= hh * dh) & (sub < (hh + 1) * dh)
        qTh.append(jnp.where(in_head, qs, 0.0).astype(BF16))
        km_h = jnp.where((lane >= hh * dh) & (lane < (hh + 1) * dh), kmean, 0.0)
        gate = jnp.dot(km_h, qT, preferred_element_type=F32,
                       precision=lax.Precision.HIGHEST)
        g = jnp.where(brow < i, gate, NEG)
        sel = brow < 0
        browf = brow.astype(F32)
        for _ in range(MOBA_TOPK):
            m = jnp.max(g, axis=0, keepdims=True)
            idx = jnp.min(jnp.where(g == m, browf, float(nb)), axis=0, keepdims=True)
            pick = browf == idx
            sel = sel | pick
            g = jnp.where(pick, -jnp.inf, g)
        valid = sel & (brow < i)
        far_bias = rb_ref[REL_BUCKETS - 1, pair * HEADS_PER_STEP + hh]
        pen_scr[hh, 0:nb, :] = jnp.where(valid, far_bias, NEG)
        pen_scr[hh, nb:2 * nb, :] = jnp.where(valid, 0.0, NEG)

    def keys(j):
        return k_ref[0, pl.ds(pl.multiple_of(j * blk, blk), blk), :]

    def vals(j, hh):
        return vT_ref[0, hh * dh:(hh + 1) * dh, pl.ds(pl.multiple_of(j * blk, blk), blk)]

    k_own = keys(i)
    state = []
    for hh in range(HEADS_PER_STEP):
        s = _dot(k_own, qTh[hh]) + own_ref[hh]
        m = jnp.max(s, axis=0, keepdims=True)
        p = jnp.exp(s - m)
        l = jnp.sum(p, axis=0, keepdims=True)
        acc = _dot(vals(i, hh), p.astype(BF16))
        state += [m, l, acc]

    def update(j, hh, s, m, l, acc):
        m_new = jnp.maximum(m, jnp.max(s, axis=0, keepdims=True))
        alpha = jnp.exp(m - m_new)
        p = jnp.exp(s - m_new)
        l = alpha * l + jnp.sum(p, axis=0, keepdims=True)
        acc = alpha * acc + _dot(vals(j, hh), p.astype(BF16))
        return m_new, l, acc

    def prev_block(state):
        j = i - 1
        kj = keys(j)
        out = []
        for hh in range(HEADS_PER_STEP):
            m, l, acc = state[3 * hh:3 * hh + 3]
            s = _dot(kj, qTh[hh]) + prev_ref[hh] + pen_scr[hh, pl.ds(nb + j, 1), :]
            out += list(update(j, hh, s, m, l, acc))
        return tuple(out)

    state = lax.cond(i > 0, prev_block, lambda st: tuple(st), tuple(state))

    def far_block(j, state):
        kj = keys(j)
        out = []
        for hh in range(HEADS_PER_STEP):
            m, l, acc = state[3 * hh:3 * hh + 3]
            s = _dot(kj, qTh[hh]) + pen_scr[hh, pl.ds(j, 1), :]
            out += list(update(j, hh, s, m, l, acc))
        return tuple(out)

    state = lax.fori_loop(0, jnp.maximum(i - 1, 0), far_block, tuple(state))

    oT = jnp.concatenate([state[3 * hh + 2] / state[3 * hh + 1]
                          for hh in range(HEADS_PER_STEP)], axis=0)
    o_ref[0] = oT.T


def _moba(rel_bias, qaT, ka, vaT, bias_own, bias_prev):
    B, S, _ = ka.shape
    nb = S // MOBA_BLOCK
    n_pairs = A_HEADS // HEADS_PER_STEP
    return pl.pallas_call(
        _moba_kernel,
        grid=(B, n_pairs, nb),
        in_specs=[
            pl.BlockSpec(memory_space=pltpu.SMEM),
            pl.BlockSpec((1, LANES, MOBA_BLOCK), lambda b, p, i: (b, p, i)),
            pl.BlockSpec((1, S, LANES), lambda b, p, i: (b, 0, p)),
            pl.BlockSpec((1, LANES, S), lambda b, p, i: (b, p, 0)),
            pl.BlockSpec((HEADS_PER_STEP, MOBA_BLOCK, MOBA_BLOCK), lambda b, p, i: (p, 0, 0)),
            pl.BlockSpec((HEADS_PER_STEP, MOBA_BLOCK, MOBA_BLOCK), lambda b, p, i: (p, 0, 0)),
        ],
        out_specs=pl.BlockSpec((1, MOBA_BLOCK, LANES), lambda b, p, i: (b, i, p)),
        out_shape=jax.ShapeDtypeStruct((B, S, A_WIDTH), F32),
        scratch_shapes=[pltpu.VMEM((nb, LANES), F32),
                        pltpu.VMEM((HEADS_PER_STEP, 2 * nb, MOBA_BLOCK), F32)],
        compiler_params=pltpu.CompilerParams(
            dimension_semantics=("arbitrary", "arbitrary", "arbitrary"),
            vmem_limit_bytes=VMEM_LIMIT),
        name="moba",
    )(rel_bias, qaT, ka, vaT, bias_own, bias_prev)


def _gla_kernel(q_ref, k_ref, v_ref, g_ref, nw_ref, o_ref, state_scr, att_scr):
    c = pl.program_id(1)
    C = GLA_CHUNK
    dk, dv, H = B_KEY_DIM, B_VAL_DIM, B_HEADS
    n_levels = C.bit_length() - 1

    @pl.when(c == 0)
    def _():
        state_scr[...] = jnp.zeros_like(state_scr)

    q = q_ref[0] * (dk ** -0.5)
    k = k_ref[0]
    v = v_ref[0].astype(BF16)
    g = g_ref[0]

    row = lax.broadcasted_iota(jnp.int32, (C, H * dk), 0)
    lane = lax.broadcasted_iota(jnp.int32, (C, H * dk), 1)

    b = g
    sh = 1
    while sh < C:
        b = b + jnp.where(row >= sh, pltpu.roll(b, sh, 0), 0.0)
        sh *= 2

    tt = lax.broadcasted_iota(jnp.int32, (C, C), 0)
    ss = lax.broadcasted_iota(jnp.int32, (C, C), 1)
    txs = tt ^ ss
    level = jnp.full((C, C), -1, jnp.int32)
    for p in range(n_levels):
        level = level + (txs >= (1 << p)).astype(jnp.int32)
    level = jnp.where(ss <= tt, level, -2)

    head_masks = [(lane >= h * dk) & (lane < (h + 1) * dk) for h in range(H)]

    def scores(qq, kk, h):
        lo = (h * dk // LANES) * LANES
        kh = jnp.where(head_masks[h], kk, 0.0).astype(BF16)[:, lo:lo + LANES]
        return _dot_nt(qq.astype(BF16)[:, lo:lo + LANES], kh)

    on_diag = level == -1
    for h in range(H):
        att_scr[h] = jnp.where(on_diag, scores(q, k, h), 0.0)

    block_end = b
    for p in range(n_levels):
        m = 1 << p
        second_half = (row & m) != 0
        r = jnp.where(second_half, pltpu.roll(block_end, m, 0), block_end)
        d = b - r
        decay = jnp.exp(jnp.where(second_half, d, -d))
        qd = q * decay
        kd = k * decay
        at_level = level == p
        for h in range(H):
            att_scr[h] = jnp.where(at_level, scores(qd, kd, h), att_scr[h])
        if p + 1 < n_levels:
            block_end = jnp.where(second_half, block_end, pltpu.roll(block_end, C - m, 0))

    state = state_scr[...]
    o_inter = _dot((q * jnp.exp(b)).astype(BF16), state.astype(BF16))

    nw = nw_ref[...]
    for h in range(H):
        o_h = o_inter[:, h * dv:(h + 1) * dv] + _dot(att_scr[h].astype(BF16),
                                                      v[:, h * dv:(h + 1) * dv])
        o_ref[0, :, h * dv:(h + 1) * dv] = _rmsnorm_rows(o_h, nw)

    bT = b.T
    kT = k.T
    b_last = bT[:, C - 1:C]
    kdT = (kT * jnp.exp(b_last - bT)).astype(BF16)
    ds = _dot(kdT, v)
    srow = lax.broadcasted_iota(jnp.int32, (H * dk, H * dv), 0) // dk
    scol = lax.broadcasted_iota(jnp.int32, (H * dk, H * dv), 1) // dv
    state_scr[...] = jnp.exp(b_last) * state + jnp.where(srow == scol, ds, 0.0)


def _gla(qb, kb, vb, g, nw):
    B, S, _ = qb.shape
    C = GLA_CHUNK
    spec = lambda width: pl.BlockSpec((1, C, width), lambda b, c: (b, c, 0))
    return pl.pallas_call(
        _gla_kernel,
        grid=(B, S // C),
        in_specs=[spec(B_KEY_WIDTH), spec(B_KEY_WIDTH), spec(B_WIDTH), spec(B_KEY_WIDTH),
                  pl.BlockSpec((1, B_VAL_DIM), lambda b, c: (0, 0))],
        out_specs=spec(B_WIDTH),
        out_shape=jax.ShapeDtypeStruct((B, S, B_WIDTH), F32),
        scratch_shapes=[pltpu.VMEM((B_KEY_WIDTH, B_WIDTH), F32),
                        pltpu.VMEM((B_HEADS, C, C), F32)],
        compiler_params=pltpu.CompilerParams(
            dimension_semantics=("arbitrary", "arbitrary"), vmem_limit_bytes=VMEM_LIMIT),
        name="gla",
    )(qb, kb, vb, g, nw)


def _silu(x):
    return x * jax.nn.sigmoid(x)


def _out_kernel(x_ref, oa_ref, ga_ref, ob_ref, gb_ref, qc_ref, gc_ref, km_ref, vm_ref,
                wo_ref, fw_ref, o_ref):
    dh = C_HEAD_DIM
    qc = (qc_ref[0] * (dh ** -0.5)).astype(BF16)
    km = km_ref[0]
    vm = vm_ref[0]
    oc = []
    for h in range(C_HEADS):
        sl = slice(h * dh, (h + 1) * dh)
        s = _dot_nt(qc[:, sl], km[:, sl])
        p = jnp.exp(s - jnp.max(s, axis=-1, keepdims=True))
        l = jnp.sum(p, axis=-1, keepdims=True)
        oc.append(_dot(p.astype(BF16), vm[:, sl]) / l)
    oc = jnp.concatenate(oc, axis=-1)

    ma = (oa_ref[0] * _silu(ga_ref[0])).astype(BF16)
    mb = (ob_ref[0] * _silu(gb_ref[0])).astype(BF16)
    mc = (oc * _silu(gc_ref[0])).astype(BF16)
    h_new = (x_ref[0]
             + _dot(ma, wo_ref[0:A_WIDTH, :])
             + _dot(mb, wo_ref[A_WIDTH:A_WIDTH + B_WIDTH, :])
             + _dot(mc, wo_ref[A_WIDTH + B_WIDTH:, :]))
    o_ref[0] = _rmsnorm_rows(h_new, fw_ref[...])


def _out(x, oa, ga, ob, gb, qc, gc, km, vm, wo, fw):
    B, S, D = x.shape
    tm = OUT_TM
    row = lambda width: pl.BlockSpec((1, tm, width), lambda b, i: (b, i, 0))
    mem = pl.BlockSpec((1, MEM_LEN, C_WIDTH), lambda b, i: (b, 0, 0))
    return pl.pallas_call(
        _out_kernel,
        grid=(B, S // tm),
        in_specs=[row(D), row(A_WIDTH), row(A_WIDTH), row(B_WIDTH), row(B_WIDTH),
                  row(C_WIDTH), row(C_WIDTH), mem, mem,
                  pl.BlockSpec(wo.shape, lambda b, i: (0, 0)),
                  pl.BlockSpec((1, D), lambda b, i: (0, 0))],
        out_specs=row(D),
        out_shape=jax.ShapeDtypeStruct((B, S, D), F32),
        compiler_params=pltpu.CompilerParams(
            dimension_semantics=("arbitrary", "arbitrary"), vmem_limit_bytes=VMEM_LIMIT),
        name="out",
    )(x, oa, ga, ob, gb, qc, gc, km, vm, wo, fw)


def kernel(x, mem, norm_w, w_in, w_alpha2, b_alpha, gla_norm_w, mem_norm_w, w_mem_kv, w_out,
           rel_bias, final_norm_w):
    assert norm_w.shape[0] == 1, "single layer"
    w = w_in[0]
    sizes = (A_WIDTH, A_WIDTH, A_WIDTH, A_WIDTH, B_KEY_WIDTH, B_KEY_WIDTH, B_WIDTH, B_WIDTH,
             GLA_RANK, C_WIDTH, C_WIDTH)
    names = ("qa", "ka", "va", "ga", "qb", "kb", "vb", "gb", "zb", "qc", "gc")
    cols, off = {}, 0
    for n, s in zip(names, sizes):
        cols[n] = w[:, off:off + s]
        off += s
    cols["zb"] = jnp.pad(cols["zb"], ((0, 0), (0, ZB_PAD - GLA_RANK)))
    wt = jnp.concatenate([cols["qa"].T, cols["va"].T], axis=0).astype(BF16)
    wr = jnp.concatenate([cols[n] for n, _ in _ROW_COLS], axis=1).astype(BF16)
    wa = jnp.pad(w_alpha2[0], ((0, ZB_PAD - GLA_RANK), (0, 0)))

    km, vm = _memkv(mem, mem_norm_w[0][None, :], w_mem_kv[0].astype(BF16))
    bias_own, bias_prev = _bias_tiles(rel_bias)
    (qaT, vaT, ka, ga, qb, kb, vb, gb, qc, gc, g) = _proj(
        x, norm_w[0][None, :], wt, wr, wa, b_alpha[0][None, :])
    oa = _moba(rel_bias, qaT, ka, vaT, bias_own, bias_prev)
    ob = _gla(qb, kb, vb, g, gla_norm_w[0][None, :])
    return _out(x, oa, ga, ob, gb, qc, gc, km, vm, w_out[0].astype(BF16),
                final_norm_w[None, :])
```

```python
import functools
import math

import jax
import jax.numpy as jnp
from jax import lax
from jax.experimental import pallas as pl
from jax.experimental.pallas import tpu as pltpu

F32 = jnp.float32
BF16 = jnp.bfloat16

D_MODEL = 1024
MEM_LEN = 256
A_HEADS = 8
A_HEAD_DIM = 64
A_WIDTH = 512
MOBA_BLOCK = 256
MOBA_TOPK = 3
B_HEADS = 4
B_KEY_DIM = 64
B_VAL_DIM = 128
B_KEY_WIDTH = 256
B_WIDTH = 512
GLA_RANK = 16
GLA_TAU = 16.0
C_HEADS = 4
C_HEAD_DIM = 128
C_WIDTH = 512
REL_BUCKETS = 32
REL_MAX_DIST = 128
RMS_EPS = 1e-6
NEG = -1e30

LANES = 128
ZB_PAD = LANES
PROJ_TM = 512
OUT_TM = 256
GLA_CHUNK = 256
VMEM_LIMIT = 56 * 1024 * 1024
LOG2E = 1.0 / math.log(2.0)
HEADS_PER_TILE = LANES // A_HEAD_DIM
PV_ROWS = A_HEAD_DIM + 16
QK_LOOKAHEAD = 4


def _dot(a, b):
    return jnp.dot(a, b, preferred_element_type=F32)


def _dot_nt(a, b):
    return lax.dot_general(a, b, (((1,), (1,)), ((), ())), preferred_element_type=F32)


def _rmsnorm_rows(x, w):
    return x * lax.rsqrt(jnp.mean(x * x, axis=-1, keepdims=True) + RMS_EPS) * w


def _memkv_kernel(mem_ref, nw_ref, w_ref, km_ref, vm_ref):
    u = _rmsnorm_rows(mem_ref[0], nw_ref[...]).astype(BF16)
    km_ref[0] = _dot(u, w_ref[:, :C_WIDTH]).astype(BF16)
    vm_ref[0] = _dot(u, w_ref[:, C_WIDTH:]).astype(BF16)


def _memkv(mem, nw, w_bf16):
    B = mem.shape[0]
    return pl.pallas_call(
        _memkv_kernel,
        grid=(B,),
        in_specs=[
            pl.BlockSpec((1, MEM_LEN, D_MODEL), lambda b: (b, 0, 0)),
            pl.BlockSpec((1, D_MODEL), lambda b: (0, 0)),
            pl.BlockSpec((D_MODEL, 2 * C_WIDTH), lambda b: (0, 0)),
        ],
        out_specs=[
            pl.BlockSpec((1, MEM_LEN, C_WIDTH), lambda b: (b, 0, 0)),
            pl.BlockSpec((1, MEM_LEN, C_WIDTH), lambda b: (b, 0, 0)),
        ],
        out_shape=[jax.ShapeDtypeStruct((B, MEM_LEN, C_WIDTH), BF16)] * 2,
        name="memkv",
    )(mem, nw, w_bf16)


def _t5_bucket(n):
    max_exact = REL_BUCKETS // 2
    nf = jnp.maximum(n, max_exact).astype(F32)
    large = max_exact + (jnp.log(nf / max_exact) / math.log(REL_MAX_DIST / max_exact)
                         * (REL_BUCKETS - max_exact)).astype(jnp.int32)
    large = jnp.minimum(large, REL_BUCKETS - 1)
    return jnp.where(n < max_exact, n, large)


def _bias_kernel(rb_ref, own_ref, prev_ref):
    h = pl.program_id(0)
    key = lax.broadcasted_iota(jnp.int32, (MOBA_BLOCK, MOBA_BLOCK), 0)
    qry = lax.broadcasted_iota(jnp.int32, (MOBA_BLOCK, MOBA_BLOCK), 1)
    rel = qry - key
    b_own = _t5_bucket(jnp.maximum(rel, 0))
    b_prev = _t5_bucket(rel + MOBA_BLOCK)
    own = jnp.zeros((MOBA_BLOCK, MOBA_BLOCK), F32)
    prev = jnp.zeros((MOBA_BLOCK, MOBA_BLOCK), F32)
    for bk in range(REL_BUCKETS):
        val = rb_ref[bk, h] * LOG2E
        own = jnp.where(b_own == bk, val, own)
        prev = jnp.where(b_prev == bk, val, prev)
    own_ref[0] = jnp.where(rel >= 0, own, NEG)
    prev_ref[0] = prev


def _bias_tiles(rel_bias):
    return pl.pallas_call(
        _bias_kernel,
        grid=(A_HEADS,),
        in_specs=[pl.BlockSpec(memory_space=pltpu.SMEM)],
        out_specs=[pl.BlockSpec((1, MOBA_BLOCK, MOBA_BLOCK), lambda h: (h, 0, 0))] * 2,
        out_shape=[jax.ShapeDtypeStruct((A_HEADS, MOBA_BLOCK, MOBA_BLOCK), F32)] * 2,
        name="t5bias",
    )(rel_bias)


_ROW_COLS = (("ka", A_WIDTH), ("ga", A_WIDTH), ("qb", B_KEY_WIDTH), ("kb", B_KEY_WIDTH),
             ("vb", B_WIDTH), ("gb", B_WIDTH), ("qc", C_WIDTH), ("gc", C_WIDTH), ("zb", ZB_PAD))
_ROW_OFF = {}
_off = 0
for _name, _width in _ROW_COLS:
    _ROW_OFF[_name] = (_off, _off + _width)
    _off += _width
ROW_COLS_TOTAL = _off


def _proj_kernel(x_ref, nw_ref, wt_ref, wr_ref, wa_ref, ba_ref,
                 qaT_ref, vaT_ref, ka_ref, ga_ref, qb_ref, kb_ref, vb_ref, gb_ref,
                 qc_ref, gc_ref, g_ref):
    u = _rmsnorm_rows(x_ref[0], nw_ref[...]).astype(BF16)
    qaT_ref[0] = _dot_nt(wt_ref[:A_WIDTH, :], u)
    vaT = _dot_nt(wt_ref[A_WIDTH:, :], u).astype(BF16)
    ones = jnp.ones((PV_ROWS - A_HEAD_DIM, vaT.shape[1]), BF16)
    pieces = []
    for h in range(A_HEADS):
        pieces += [vaT[h * A_HEAD_DIM:(h + 1) * A_HEAD_DIM], ones]
    vaT_ref[0] = jnp.concatenate(pieces, axis=0)

    def row(name):
        lo, hi = _ROW_OFF[name]
        return _dot(u, wr_ref[:, lo:hi])

    ka_ref[0] = row("ka").astype(BF16)
    ga_ref[0] = row("ga")
    qb_ref[0] = row("qb")
    kb_ref[0] = row("kb")
    vb_ref[0] = row("vb")
    gb_ref[0] = row("gb")
    qc_ref[0] = row("qc")
    gc_ref[0] = row("gc")
    zb = row("zb")
    z = jnp.dot(zb, wa_ref[...], preferred_element_type=F32,
                precision=lax.Precision.HIGHEST) + ba_ref[...]
    g_ref[0] = (jnp.minimum(z, 0.0) - jnp.log1p(jnp.exp(-jnp.abs(z)))) * (1.0 / GLA_TAU)


def _proj(x, nw, wt, wr, wa, ba):
    B, S, D = x.shape
    tm = PROJ_TM
    row_spec = lambda width: pl.BlockSpec((1, tm, width), lambda b, i: (b, i, 0))
    col_spec = lambda rows: pl.BlockSpec((1, rows, tm), lambda b, i: (b, 0, i))
    const = lambda shape: pl.BlockSpec(shape, lambda b, i: (0,) * len(shape))
    sds = jax.ShapeDtypeStruct
    return pl.pallas_call(
        _proj_kernel,
        grid=(B, S // tm),
        in_specs=[
            pl.BlockSpec((1, tm, D), lambda b, i: (b, i, 0)),
            const((1, D)),
            const((2 * A_WIDTH, D)),
            const((D, ROW_COLS_TOTAL)),
            const((ZB_PAD, B_KEY_WIDTH)),
            const((1, B_KEY_WIDTH)),
        ],
        out_specs=[col_spec(A_WIDTH), col_spec(A_HEADS * PV_ROWS), row_spec(A_WIDTH), row_spec(A_WIDTH),
                   row_spec(B_KEY_WIDTH), row_spec(B_KEY_WIDTH), row_spec(B_WIDTH),
                   row_spec(B_WIDTH), row_spec(C_WIDTH), row_spec(C_WIDTH),
                   row_spec(B_KEY_WIDTH)],
        out_shape=[sds((B, A_WIDTH, S), F32), sds((B, A_HEADS * PV_ROWS, S), BF16),
                   sds((B, S, A_WIDTH), BF16), sds((B, S, A_WIDTH), F32),
                   sds((B, S, B_KEY_WIDTH), F32), sds((B, S, B_KEY_WIDTH), F32),
                   sds((B, S, B_WIDTH), F32), sds((B, S, B_WIDTH), F32),
                   sds((B, S, C_WIDTH), F32), sds((B, S, C_WIDTH), F32),
                   sds((B, S, B_KEY_WIDTH), F32)],
        compiler_params=pltpu.CompilerParams(
            dimension_semantics=("arbitrary", "arbitrary"), vmem_limit_bytes=VMEM_LIMIT),
        name="proj",
    )(x, nw, wt, wr, wa, ba)


def _moba_kernel(rb_ref, qT_ref, k_ref, vT_ref, own_ref, prev_ref, o_ref,
                 kmean_scr, pen_scr, qh_scr, m_scr, acc_scr):
    i = pl.program_id(1)
    nb = kmean_scr.shape[0]
    blk = MOBA_BLOCK
    dh = A_HEAD_DIM
    n_tiles = A_HEADS // HEADS_PER_TILE

    @pl.when(i == 0)
    def _():
        for n in range(nb):
            kb = k_ref[0, n * blk:(n + 1) * blk, :].astype(F32)
            kmean_scr[n:n + 1, :] = jnp.mean(kb, axis=0, keepdims=True)

    sub = lax.broadcasted_iota(jnp.int32, (LANES, blk), 0)
    lane = lax.broadcasted_iota(jnp.int32, (nb, LANES), 1)
    brow = lax.broadcasted_iota(jnp.int32, (nb, blk), 0)
    browf = brow.astype(F32)

    for h in range(A_HEADS):
        t, hh = divmod(h, HEADS_PER_TILE)
        qT = qT_ref[0, t * LANES:(t + 1) * LANES, :]
        in_head = (sub >= hh * dh) & (sub < (hh + 1) * dh)
        qh_scr[h] = jnp.where(in_head, qT * (dh ** -0.5 * LOG2E), 0.0).astype(BF16)
        km_h = jnp.where((lane >= hh * dh) & (lane < (hh + 1) * dh),
                         kmean_scr[:, t * LANES:(t + 1) * LANES], 0.0)
        gate = jnp.dot(km_h, qT, preferred_element_type=F32,
                       precision=lax.Precision.HIGHEST)
        g = jnp.where(brow < i, gate, NEG)
        sel = brow < 0
        for _ in range(MOBA_TOPK):
            m = jnp.max(g, axis=0, keepdims=True)
            idx = jnp.min(jnp.where(g == m, browf, float(nb)), axis=0, keepdims=True)
            pick = browf == idx
            sel = sel | pick
            g = jnp.where(pick, -jnp.inf, g)
        valid = sel & (brow < i)
        far_bias = rb_ref[REL_BUCKETS - 1, h] * LOG2E
        pen_scr[h, 0:nb, :] = jnp.where(valid, far_bias, NEG)
        pen_scr[h, nb:2 * nb, :] = jnp.where(valid, 0.0, NEG)

    def block(j, kind):
        rows = pl.ds(pl.multiple_of(j * blk, blk), blk)

        def logits(h):
            t = h // HEADS_PER_TILE
            s = _dot(k_ref[0, rows, t * LANES:(t + 1) * LANES], qh_scr[h])
            if kind == "own":
                return s + own_ref[h]
            if kind == "prev":
                return s + prev_ref[h] + pen_scr[h, pl.ds(nb + j, 1), :]
            return s + pen_scr[h, pl.ds(j, 1), :]

        def fold(h, s):
            bm = jnp.max(s, axis=0, keepdims=True)
            vj = vT_ref[0, h * PV_ROWS:(h + 1) * PV_ROWS, rows]
            if kind == "own":
                m_scr[h] = bm
                acc_scr[h] = _dot(vj, jnp.exp2(s - bm).astype(BF16))
            else:
                m_old = m_scr[h]
                m_new = jnp.maximum(m_old, bm)
                m_scr[h] = m_new
                acc_scr[h] = (jnp.exp2(m_old - m_new) * acc_scr[h]
                              + _dot(vj, jnp.exp2(s - m_new).astype(BF16)))

        pending = [logits(h) for h in range(QK_LOOKAHEAD)]
        for h in range(A_HEADS):
            if h + QK_LOOKAHEAD < A_HEADS:
                pending.append(logits(h + QK_LOOKAHEAD))
            fold(h, pending.pop(0))

    block(i, "own")

    @pl.when(i > 0)
    def _():
        block(i - 1, "prev")

    def far_block(j, carry):
        block(j, "far")
        return carry

    lax.fori_loop(0, jnp.maximum(i - 1, 0), far_block, 0)

    for t in range(n_tiles):
        oT = []
        for hh in range(HEADS_PER_TILE):
            acc = acc_scr[t * HEADS_PER_TILE + hh]
            oT.append(acc[:dh] / acc[dh:dh + 1])
        o_ref[0, :, t * LANES:(t + 1) * LANES] = jnp.concatenate(oT, axis=0).T


def _moba(rel_bias, qaT, ka, vaT, bias_own, bias_prev):
    B, S, _ = ka.shape
    nb = S // MOBA_BLOCK
    bias_spec = pl.BlockSpec((A_HEADS, MOBA_BLOCK, MOBA_BLOCK), lambda b, i: (0, 0, 0))
    return pl.pallas_call(
        _moba_kernel,
        grid=(B, nb),
        in_specs=[
            pl.BlockSpec(memory_space=pltpu.SMEM),
            pl.BlockSpec((1, A_WIDTH, MOBA_BLOCK), lambda b, i: (b, 0, i)),
            pl.BlockSpec((1, S, A_WIDTH), lambda b, i: (b, 0, 0)),
            pl.BlockSpec((1, A_HEADS * PV_ROWS, S), lambda b, i: (b, 0, 0)),
            bias_spec, bias_spec,
        ],
        out_specs=pl.BlockSpec((1, MOBA_BLOCK, A_WIDTH), lambda b, i: (b, i, 0)),
        out_shape=jax.ShapeDtypeStruct((B, S, A_WIDTH), F32),
        scratch_shapes=[pltpu.VMEM((nb, A_WIDTH), F32),
                        pltpu.VMEM((A_HEADS, 2 * nb, MOBA_BLOCK), F32),
                        pltpu.VMEM((A_HEADS, LANES, MOBA_BLOCK), BF16),
                        pltpu.VMEM((A_HEADS, 1, MOBA_BLOCK), F32),
                        pltpu.VMEM((A_HEADS, PV_ROWS, MOBA_BLOCK), F32)],
        compiler_params=pltpu.CompilerParams(
            dimension_semantics=("arbitrary", "arbitrary"),
            vmem_limit_bytes=VMEM_LIMIT),
        name="moba",
    )(rel_bias, qaT, ka, vaT, bias_own, bias_prev)


def _gla_kernel(q_ref, k_ref, v_ref, g_ref, nw_ref, o_ref, state_scr, att_scr):
    c = pl.program_id(1)
    C = GLA_CHUNK
    dk, dv, H = B_KEY_DIM, B_VAL_DIM, B_HEADS
    n_levels = C.bit_length() - 1

    @pl.when(c == 0)
    def _():
        state_scr[...] = jnp.zeros_like(state_scr)

    q = q_ref[0] * (dk ** -0.5)
    k = k_ref[0]
    v = v_ref[0].astype(BF16)
    g = g_ref[0]

    row = lax.broadcasted_iota(jnp.int32, (C, H * dk), 0)
    lane = lax.broadcasted_iota(jnp.int32, (C, H * dk), 1)

    b = g
    sh = 1
    while sh < C:
        b = b + jnp.where(row >= sh, pltpu.roll(b, sh, 0), 0.0)
        sh *= 2

    tt = lax.broadcasted_iota(jnp.int32, (C, C), 0)
    ss = lax.broadcasted_iota(jnp.int32, (C, C), 1)
    txs = tt ^ ss
    level = jnp.full((C, C), -1, jnp.int32)
    for p in range(n_levels):
        level = level + (txs >= (1 << p)).astype(jnp.int32)
    level = jnp.where(ss <= tt, level, -2)

    head_masks = [(lane >= h * dk) & (lane < (h + 1) * dk) for h in range(H)]

    def scores(qq, kk, h):
        lo = (h * dk // LANES) * LANES
        kh = jnp.where(head_masks[h], kk, 0.0).astype(BF16)[:, lo:lo + LANES]
        return _dot_nt(qq.astype(BF16)[:, lo:lo + LANES], kh)

    on_diag = level == -1
    for h in range(H):
        att_scr[h] = jnp.where(on_diag, scores(q, k, h), 0.0)

    block_end = b
    for p in range(n_levels):
        m = 1 << p
        second_half = (row & m) != 0
        r = jnp.where(second_half, pltpu.roll(block_end, m, 0), block_end)
        d = b - r
        decay = jnp.exp(jnp.where(second_half, d, -d))
        qd = q * decay
        kd = k * decay
        at_level = level == p
        for h in range(H):
            att_scr[h] = jnp.where(at_level, scores(qd, kd, h), att_scr[h])
        if p + 1 < n_levels:
            block_end = jnp.where(second_half, block_end, pltpu.roll(block_end, C - m, 0))

    state = state_scr[...]
    o_inter = _dot((q * jnp.exp(b)).astype(BF16), state.astype(BF16))

    nw = nw_ref[...]
    for h in range(H):
        o_h = o_inter[:, h * dv:(h + 1) * dv] + _dot(att_scr[h].astype(BF16),
                                                      v[:, h * dv:(h + 1) * dv])
        o_ref[0, :, h * dv:(h + 1) * dv] = _rmsnorm_rows(o_h, nw)

    bT = b.T
    kT = k.T
    b_last = bT[:, C - 1:C]
    kdT = (kT * jnp.exp(b_last - bT)).astype(BF16)
    ds = _dot(kdT, v)
    srow = lax.broadcasted_iota(jnp.int32, (H * dk, H * dv), 0) // dk
    scol = lax.broadcasted_iota(jnp.int32, (H * dk, H * dv), 1) // dv
    state_scr[...] = jnp.exp(b_last) * state + jnp.where(srow == scol, ds, 0.0)


def _gla(qb, kb, vb, g, nw):
    B, S, _ = qb.shape
    C = GLA_CHUNK
    spec = lambda width: pl.BlockSpec((1, C, width), lambda b, c: (b, c, 0))
    return pl.pallas_call(
        _gla_kernel,
        grid=(B, S // C),
        in_specs=[spec(B_KEY_WIDTH), spec(B_KEY_WIDTH), spec(B_WIDTH), spec(B_KEY_WIDTH),
                  pl.BlockSpec((1, B_VAL_DIM), lambda b, c: (0, 0))],
        out_specs=spec(B_WIDTH),
        out_shape=jax.ShapeDtypeStruct((B, S, B_WIDTH), F32),
        scratch_shapes=[pltpu.VMEM((B_KEY_WIDTH, B_WIDTH), F32),
                        pltpu.VMEM((B_HEADS, C, C), F32)],
        compiler_params=pltpu.CompilerParams(
            dimension_semantics=("arbitrary", "arbitrary"), vmem_limit_bytes=VMEM_LIMIT),
        name="gla",
    )(qb, kb, vb, g, nw)


def _silu(x):
    return x * jax.nn.sigmoid(x)


def _out_kernel(x_ref, oa_ref, ga_ref, ob_ref, gb_ref, qc_ref, gc_ref, km_ref, vm_ref,
                wo_ref, fw_ref, o_ref):
    dh = C_HEAD_DIM
    qc = (qc_ref[0] * (dh ** -0.5)).astype(BF16)
    km = km_ref[0]
    vm = vm_ref[0]
    oc = []
    for h in range(C_HEADS):
        sl = slice(h * dh, (h + 1) * dh)
        s = _dot_nt(qc[:, sl], km[:, sl])
        p = jnp.exp(s - jnp.max(s, axis=-1, keepdims=True))
        l = jnp.sum(p, axis=-1, keepdims=True)
        oc.append(_dot(p.astype(BF16), vm[:, sl]) / l)
    oc = jnp.concatenate(oc, axis=-1)

    ma = (oa_ref[0] * _silu(ga_ref[0])).astype(BF16)
    mb = (ob_ref[0] * _silu(gb_ref[0])).astype(BF16)
    mc = (oc * _silu(gc_ref[0])).astype(BF16)
    h_new = (x_ref[0]
             + _dot(ma, wo_ref[0:A_WIDTH, :])
             + _dot(mb, wo_ref[A_WIDTH:A_WIDTH + B_WIDTH, :])
             + _dot(mc, wo_ref[A_WIDTH + B_WIDTH:, :]))
    o_ref[0] = _rmsnorm_rows(h_new, fw_ref[...])


def _out(x, oa, ga, ob, gb, qc, gc, km, vm, wo, fw):
    B, S, D = x.shape
    tm = OUT_TM
    row = lambda width: pl.BlockSpec((1, tm, width), lambda b, i: (b, i, 0))
    mem = pl.BlockSpec((1, MEM_LEN, C_WIDTH), lambda b, i: (b, 0, 0))
    return pl.pallas_call(
        _out_kernel,
        grid=(B, S // tm),
        in_specs=[row(D), row(A_WIDTH), row(A_WIDTH), row(B_WIDTH), row(B_WIDTH),
                  row(C_WIDTH), row(C_WIDTH), mem, mem,
                  pl.BlockSpec(wo.shape, lambda b, i: (0, 0)),
                  pl.BlockSpec((1, D), lambda b, i: (0, 0))],
        out_specs=row(D),
        out_shape=jax.ShapeDtypeStruct((B, S, D), F32),
        compiler_params=pltpu.CompilerParams(
            dimension_semantics=("arbitrary", "arbitrary"), vmem_limit_bytes=VMEM_LIMIT),
        name="out",
    )(x, oa, ga, ob, gb, qc, gc, km, vm, wo, fw)


def kernel(x, mem, norm_w, w_in, w_alpha2, b_alpha, gla_norm_w, mem_norm_w, w_mem_kv, w_out,
           rel_bias, final_norm_w):
    assert norm_w.shape[0] == 1, "single layer"
    w = w_in[0]
    sizes = (A_WIDTH, A_WIDTH, A_WIDTH, A_WIDTH, B_KEY_WIDTH, B_KEY_WIDTH, B_WIDTH, B_WIDTH,
             GLA_RANK, C_WIDTH, C_WIDTH)
    names = ("qa", "ka", "va", "ga", "qb", "kb", "vb", "gb", "zb", "qc", "gc")
    cols, off = {}, 0
    for n, s in zip(names, sizes):
        cols[n] = w[:, off:off + s]
        off += s
    cols["zb"] = jnp.pad(cols["zb"], ((0, 0), (0, ZB_PAD - GLA_RANK)))
    wt = jnp.concatenate([cols["qa"].T, cols["va"].T], axis=0).astype(BF16)
    wr = jnp.concatenate([cols[n] for n, _ in _ROW_COLS], axis=1).astype(BF16)
    wa = jnp.pad(w_alpha2[0], ((0, ZB_PAD - GLA_RANK), (0, 0)))

    km, vm = _memkv(mem, mem_norm_w[0][None, :], w_mem_kv[0].astype(BF16))
    bias_own, bias_prev = _bias_tiles(rel_bias)
    (qaT, vaT, ka, ga, qb, kb, vb, gb, qc, gc, g) = _proj(
        x, norm_w[0][None, :], wt, wr, wa, b_alpha[0][None, :])
    oa = _moba(rel_bias, qaT, ka, vaT, bias_own, bias_prev)
    ob = _gla(qb, kb, vb, g, gla_norm_w[0][None, :])
    return _out(x, oa, ga, ob, gb, qc, gc, km, vm, w_out[0].astype(BF16),
                final_norm_w[None, :])
```

```python
import functools
import math

import jax
import jax.numpy as jnp
from jax import lax
from jax.experimental import pallas as pl
from jax.experimental.pallas import tpu as pltpu

F32 = jnp.float32
BF16 = jnp.bfloat16

D_MODEL = 1024
MEM_LEN = 256
A_HEADS = 8
A_HEAD_DIM = 64
A_WIDTH = 512
MOBA_BLOCK = 256
MOBA_TOPK = 3
B_HEADS = 4
B_KEY_DIM = 64
B_VAL_DIM = 128
B_KEY_WIDTH = 256
B_WIDTH = 512
GLA_RANK = 16
GLA_TAU = 16.0
C_HEADS = 4
C_HEAD_DIM = 128
C_WIDTH = 512
REL_BUCKETS = 32
REL_MAX_DIST = 128
RMS_EPS = 1e-6
NEG = -1e30

LANES = 128
ZB_PAD = LANES
PROJ_TM = 512
OUT_TM = 256
GLA_CHUNK = 256
VMEM_LIMIT = 56 * 1024 * 1024
LOG2E = 1.0 / math.log(2.0)
HEADS_PER_TILE = LANES // A_HEAD_DIM
PV_ROWS = A_HEAD_DIM + 16
MOBA_MAX_BLOCKS = 16
QK_LOOKAHEAD = 4


def _dot(a, b):
    return jnp.dot(a, b, preferred_element_type=F32)


def _dot_nt(a, b):
    return lax.dot_general(a, b, (((1,), (1,)), ((), ())), preferred_element_type=F32)


def _rmsnorm_rows(x, w):
    return x * lax.rsqrt(jnp.mean(x * x, axis=-1, keepdims=True) + RMS_EPS) * w


def _memkv_kernel(mem_ref, nw_ref, w_ref, km_ref, vm_ref):
    u = _rmsnorm_rows(mem_ref[0], nw_ref[...]).astype(BF16)
    km_ref[0] = _dot(u, w_ref[:, :C_WIDTH]).astype(BF16)
    vm_ref[0] = _dot(u, w_ref[:, C_WIDTH:]).astype(BF16)


def _memkv(mem, nw, w_bf16):
    B = mem.shape[0]
    return pl.pallas_call(
        _memkv_kernel,
        grid=(B,),
        in_specs=[
            pl.BlockSpec((1, MEM_LEN, D_MODEL), lambda b: (b, 0, 0)),
            pl.BlockSpec((1, D_MODEL), lambda b: (0, 0)),
            pl.BlockSpec((D_MODEL, 2 * C_WIDTH), lambda b: (0, 0)),
        ],
        out_specs=[
            pl.BlockSpec((1, MEM_LEN, C_WIDTH), lambda b: (b, 0, 0)),
            pl.BlockSpec((1, MEM_LEN, C_WIDTH), lambda b: (b, 0, 0)),
        ],
        out_shape=[jax.ShapeDtypeStruct((B, MEM_LEN, C_WIDTH), BF16)] * 2,
        name="memkv",
    )(mem, nw, w_bf16)


def _t5_bucket(n):
    max_exact = REL_BUCKETS // 2
    nf = jnp.maximum(n, max_exact).astype(F32)
    large = max_exact + (jnp.log(nf / max_exact) / math.log(REL_MAX_DIST / max_exact)
                         * (REL_BUCKETS - max_exact)).astype(jnp.int32)
    large = jnp.minimum(large, REL_BUCKETS - 1)
    return jnp.where(n < max_exact, n, large)


def _bias_kernel(rb_ref, own_ref, prev_ref):
    h = pl.program_id(0)
    key = lax.broadcasted_iota(jnp.int32, (MOBA_BLOCK, MOBA_BLOCK), 0)
    qry = lax.broadcasted_iota(jnp.int32, (MOBA_BLOCK, MOBA_BLOCK), 1)
    rel = qry - key
    b_own = _t5_bucket(jnp.maximum(rel, 0))
    b_prev = _t5_bucket(rel + MOBA_BLOCK)
    own = jnp.zeros((MOBA_BLOCK, MOBA_BLOCK), F32)
    prev = jnp.zeros((MOBA_BLOCK, MOBA_BLOCK), F32)
    for bk in range(REL_BUCKETS):
        val = rb_ref[bk, h] * LOG2E
        own = jnp.where(b_own == bk, val, own)
        prev = jnp.where(b_prev == bk, val, prev)
    own_ref[0] = jnp.where(rel >= 0, own, NEG)
    prev_ref[0] = prev


def _bias_tiles(rel_bias):
    return pl.pallas_call(
        _bias_kernel,
        grid=(A_HEADS,),
        in_specs=[pl.BlockSpec(memory_space=pltpu.SMEM)],
        out_specs=[pl.BlockSpec((1, MOBA_BLOCK, MOBA_BLOCK), lambda h: (h, 0, 0))] * 2,
        out_shape=[jax.ShapeDtypeStruct((A_HEADS, MOBA_BLOCK, MOBA_BLOCK), F32)] * 2,
        name="t5bias",
    )(rel_bias)


_ROW_COLS = (("ka", A_WIDTH), ("ga", A_WIDTH), ("qb", B_KEY_WIDTH), ("kb", B_KEY_WIDTH),
             ("vb", B_WIDTH), ("gb", B_WIDTH), ("qc", C_WIDTH), ("gc", C_WIDTH), ("zb", ZB_PAD))
_ROW_OFF = {}
_off = 0
for _name, _width in _ROW_COLS:
    _ROW_OFF[_name] = (_off, _off + _width)
    _off += _width
ROW_COLS_TOTAL = _off


def _proj_kernel(x_ref, nw_ref, wt_ref, wr_ref, wa_ref, ba_ref,
                 qaT_ref, vaT_ref, ka_ref, ga_ref, qb_ref, kb_ref, vb_ref, gb_ref,
                 qc_ref, gc_ref, g_ref):
    u = _rmsnorm_rows(x_ref[0], nw_ref[...]).astype(BF16)
    qaT_ref[0] = _dot_nt(wt_ref[:A_WIDTH, :], u)
    vaT = _dot_nt(wt_ref[A_WIDTH:, :], u).astype(BF16)
    ones = jnp.ones((PV_ROWS - A_HEAD_DIM, vaT.shape[1]), BF16)
    pieces = []
    for h in range(A_HEADS):
        pieces += [vaT[h * A_HEAD_DIM:(h + 1) * A_HEAD_DIM], ones]
    vaT_ref[0] = jnp.concatenate(pieces, axis=0)

    def row(name):
        lo, hi = _ROW_OFF[name]
        return _dot(u, wr_ref[:, lo:hi])

    ka = row("ka")
    tm = ka.shape[0]
    nb = MOBA_MAX_BLOCKS
    key_pos = pl.program_id(1) * tm + lax.broadcasted_iota(jnp.int32, (tm, LANES - A_HEAD_DIM), 0)
    blk_id = jnp.right_shift(key_pos, MOBA_BLOCK.bit_length() - 1)
    col = lax.broadcasted_iota(jnp.int32, (tm, LANES - A_HEAD_DIM), 1)
    onehot = jnp.where((col < 2 * nb) & ((col & (nb - 1)) == blk_id), 1.0, 0.0)
    pieces = []
    for h in range(A_HEADS):
        pieces += [ka[:, h * A_HEAD_DIM:(h + 1) * A_HEAD_DIM], onehot]
    ka_ref[0] = jnp.concatenate(pieces, axis=1).astype(BF16)
    ga_ref[0] = row("ga")
    qb_ref[0] = row("qb")
    kb_ref[0] = row("kb")
    vb_ref[0] = row("vb")
    gb_ref[0] = row("gb")
    qc_ref[0] = row("qc")
    gc_ref[0] = row("gc")
    zb = row("zb")
    z = jnp.dot(zb, wa_ref[...], preferred_element_type=F32,
                precision=lax.Precision.HIGHEST) + ba_ref[...]
    g_ref[0] = (jnp.minimum(z, 0.0) - jnp.log1p(jnp.exp(-jnp.abs(z)))) * (1.0 / GLA_TAU)


def _proj(x, nw, wt, wr, wa, ba):
    B, S, D = x.shape
    tm = PROJ_TM
    row_spec = lambda width: pl.BlockSpec((1, tm, width), lambda b, i: (b, i, 0))
    col_spec = lambda rows: pl.BlockSpec((1, rows, tm), lambda b, i: (b, 0, i))
    const = lambda shape: pl.BlockSpec(shape, lambda b, i: (0,) * len(shape))
    sds = jax.ShapeDtypeStruct
    return pl.pallas_call(
        _proj_kernel,
        grid=(B, S // tm),
        in_specs=[
            pl.BlockSpec((1, tm, D), lambda b, i: (b, i, 0)),
            const((1, D)),
            const((2 * A_WIDTH, D)),
            const((D, ROW_COLS_TOTAL)),
            const((ZB_PAD, B_KEY_WIDTH)),
            const((1, B_KEY_WIDTH)),
        ],
        out_specs=[col_spec(A_WIDTH), col_spec(A_HEADS * PV_ROWS), row_spec(A_HEADS * LANES),
                   row_spec(A_WIDTH),
                   row_spec(B_KEY_WIDTH), row_spec(B_KEY_WIDTH), row_spec(B_WIDTH),
                   row_spec(B_WIDTH), row_spec(C_WIDTH), row_spec(C_WIDTH),
                   row_spec(B_KEY_WIDTH)],
        out_shape=[sds((B, A_WIDTH, S), F32), sds((B, A_HEADS * PV_ROWS, S), BF16),
                   sds((B, S, A_HEADS * LANES), BF16), sds((B, S, A_WIDTH), F32),
                   sds((B, S, B_KEY_WIDTH), F32), sds((B, S, B_KEY_WIDTH), F32),
                   sds((B, S, B_WIDTH), F32), sds((B, S, B_WIDTH), F32),
                   sds((B, S, C_WIDTH), F32), sds((B, S, C_WIDTH), F32),
                   sds((B, S, B_KEY_WIDTH), F32)],
        compiler_params=pltpu.CompilerParams(
            dimension_semantics=("arbitrary", "arbitrary"), vmem_limit_bytes=VMEM_LIMIT),
        name="proj",
    )(x, nw, wt, wr, wa, ba)


def _moba_kernel(rb_ref, qT_ref, k_ref, vT_ref, own_ref, prev_ref, o_ref,
                 kmean_scr, kmbd_scr, qh_scr, m_scr, acc_scr, s_scr):
    i = pl.program_id(1)
    nb = kmean_scr.shape[0]
    blk = MOBA_BLOCK
    dh = A_HEAD_DIM
    n_tiles = A_HEADS // HEADS_PER_TILE

    @pl.when(i == 0)
    def _():
        for n in range(nb):
            kb = jnp.concatenate(
                [k_ref[0, n * blk:(n + 1) * blk, h * LANES:h * LANES + dh] for h in range(A_HEADS)],
                axis=1).astype(F32)
            kmean_scr[n:n + 1, :] = jnp.mean(kb, axis=0, keepdims=True)
        lane_head = jnp.right_shift(lax.broadcasted_iota(jnp.int32, (nb, A_WIDTH), 1),
                                    dh.bit_length() - 1)
        for h in range(A_HEADS):
            kmbd_scr[h * nb:(h + 1) * nb, :] = jnp.where(lane_head == h, kmean_scr[...], 0.0)

    gate_all = jnp.dot(kmbd_scr[...], qT_ref[0], preferred_element_type=F32,
                       precision=lax.Precision.HIGHEST)
    brow = lax.broadcasted_iota(jnp.int32, (nb, blk), 0)
    browf = brow.astype(F32)
    zeros = jnp.zeros((LANES - dh - 2 * nb, blk), F32)

    for h in range(A_HEADS):
        g = jnp.where(brow < i, gate_all[h * nb:(h + 1) * nb], NEG)
        sel = brow < 0
        for _ in range(MOBA_TOPK):
            m = jnp.max(g, axis=0, keepdims=True)
            idx = jnp.min(jnp.where(g == m, browf, float(nb)), axis=0, keepdims=True)
            pick = browf == idx
            sel = sel | pick
            g = jnp.where(pick, -jnp.inf, g)
        valid = sel & (brow < i)
        far_bias = rb_ref[REL_BUCKETS - 1, h] * LOG2E
        pen = jnp.where(brow < i - 1, jnp.where(valid, far_bias, NEG),
                        jnp.where(brow == i - 1, jnp.where(valid, 0.0, NEG), 0.0))
        pen_hi = pen.astype(BF16).astype(F32)
        q_h = qT_ref[0, h * dh:(h + 1) * dh, :] * (dh ** -0.5 * LOG2E)
        qh_scr[h] = jnp.concatenate([q_h, pen_hi, pen - pen_hi, zeros], axis=0).astype(BF16)

    no_prev = jnp.where(i == 0, NEG, 0.0)

    def logits(j, h, kind):
        rows = pl.ds(pl.multiple_of(j * blk, blk), blk)
        s = _dot(k_ref[0, rows, h * LANES:(h + 1) * LANES], qh_scr[h])
        if kind == "own":
            return s + own_ref[h]
        if kind == "prev":
            return s + (prev_ref[h] + no_prev)
        return s

    def fold(j, h, s, first):
        rows = pl.ds(pl.multiple_of(j * blk, blk), blk)
        bm = jnp.max(s, axis=0, keepdims=True)
        vj = vT_ref[0, h * PV_ROWS:(h + 1) * PV_ROWS, rows]
        if first:
            m_scr[h] = bm
            acc_scr[h] = _dot(vj, jnp.exp2(s - bm).astype(BF16))
        else:
            m_old = m_scr[h]
            m_new = jnp.maximum(m_old, bm)
            m_scr[h] = m_new
            acc_scr[h] = (jnp.exp2(m_old - m_new) * acc_scr[h]
                          + _dot(vj, jnp.exp2(s - m_new).astype(BF16)))

    n_far = i - 1
    j_prev = jnp.maximum(i - 1, 0)
    items = ([(i, "own", h) for h in range(A_HEADS)] + [(j_prev, "prev", h) for h in range(A_HEADS)]
             + [(0, "far", h) for h in range(QK_LOOKAHEAD)])
    n_fold = 2 * A_HEADS
    pending = []

    def issue(n):
        j, kind, h = items[n]
        s = logits(j, h, kind)
        if n < n_fold:
            pending.append(s)
        else:
            s_scr[h] = s

    for n in range(QK_LOOKAHEAD):
        issue(n)
    for n in range(n_fold):
        issue(n + QK_LOOKAHEAD)
        j, kind, h = items[n]
        fold(j, h, pending.pop(0), kind == "own")

    def far_block(j, carry):
        j_next = jnp.minimum(j + 1, n_far - 1)
        pending = [s_scr[h] for h in range(QK_LOOKAHEAD)]
        for h in range(A_HEADS):
            ahead = h + QK_LOOKAHEAD
            if ahead < A_HEADS:
                pending.append(logits(j, ahead, "far"))
            else:
                s_scr[ahead - A_HEADS] = logits(j_next, ahead - A_HEADS, "far")
            fold(j, h, pending.pop(0), False)
        return carry

    lax.fori_loop(0, jnp.maximum(n_far, 0), far_block, 0)

    for t in range(n_tiles):
        oT = []
        for hh in range(HEADS_PER_TILE):
            acc = acc_scr[t * HEADS_PER_TILE + hh]
            oT.append(acc[:dh] / acc[dh:dh + 1])
        o_ref[0, :, t * LANES:(t + 1) * LANES] = jnp.concatenate(oT, axis=0).T


def _moba(rel_bias, qaT, ka, vaT, bias_own, bias_prev):
    B, S, _ = ka.shape
    nb = S // MOBA_BLOCK
    assert nb == MOBA_MAX_BLOCKS
    bias_spec = pl.BlockSpec((A_HEADS, MOBA_BLOCK, MOBA_BLOCK), lambda b, i: (0, 0, 0))
    return pl.pallas_call(
        _moba_kernel,
        grid=(B, nb),
        in_specs=[
            pl.BlockSpec(memory_space=pltpu.SMEM),
            pl.BlockSpec((1, A_WIDTH, MOBA_BLOCK), lambda b, i: (b, 0, i)),
            pl.BlockSpec((1, S, A_HEADS * LANES), lambda b, i: (b, 0, 0)),
            pl.BlockSpec((1, A_HEADS * PV_ROWS, S), lambda b, i: (b, 0, 0)),
            bias_spec, bias_spec,
        ],
        out_specs=pl.BlockSpec((1, MOBA_BLOCK, A_WIDTH), lambda b, i: (b, i, 0)),
        out_shape=jax.ShapeDtypeStruct((B, S, A_WIDTH), F32),
        scratch_shapes=[pltpu.VMEM((nb, A_WIDTH), F32),
                        pltpu.VMEM((A_HEADS * nb, A_WIDTH), F32),
                        pltpu.VMEM((A_HEADS, LANES, MOBA_BLOCK), BF16),
                        pltpu.VMEM((A_HEADS, 1, MOBA_BLOCK), F32),
                        pltpu.VMEM((A_HEADS, PV_ROWS, MOBA_BLOCK), F32),
                        pltpu.VMEM((QK_LOOKAHEAD, MOBA_BLOCK, MOBA_BLOCK), F32)],
        compiler_params=pltpu.CompilerParams(
            dimension_semantics=("arbitrary", "arbitrary"),
            vmem_limit_bytes=VMEM_LIMIT),
        name="moba",
    )(rel_bias, qaT, ka, vaT, bias_own, bias_prev)


def _gla_kernel(q_ref, k_ref, v_ref, g_ref, nw_ref, o_ref, state_scr, att_scr):
    c = pl.program_id(1)
    C = GLA_CHUNK
    dk, dv, H = B_KEY_DIM, B_VAL_DIM, B_HEADS
    n_levels = C.bit_length() - 1

    @pl.when(c == 0)
    def _():
        state_scr[...] = jnp.zeros_like(state_scr)

    q = q_ref[0] * (dk ** -0.5)
    k = k_ref[0]
    v = v_ref[0].astype(BF16)
    g = g_ref[0]

    row = lax.broadcasted_iota(jnp.int32, (C, H * dk), 0)
    lane = lax.broadcasted_iota(jnp.int32, (C, H * dk), 1)

    b = g
    sh = 1
    while sh < C:
        b = b + jnp.where(row >= sh, pltpu.roll(b, sh, 0), 0.0)
        sh *= 2

    tt = lax.broadcasted_iota(jnp.int32, (C, C), 0)
    ss = lax.broadcasted_iota(jnp.int32, (C, C), 1)
    txs = tt ^ ss
    level = jnp.full((C, C), -1, jnp.int32)
    for p in range(n_levels):
        level = level + (txs >= (1 << p)).astype(jnp.int32)
    level = jnp.where(ss <= tt, level, -2)

    head_masks = [(lane >= h * dk) & (lane < (h + 1) * dk) for h in range(H)]

    def scores(qq, kk, h):
        lo = (h * dk // LANES) * LANES
        kh = jnp.where(head_masks[h], kk, 0.0).astype(BF16)[:, lo:lo + LANES]
        return _dot_nt(qq.astype(BF16)[:, lo:lo + LANES], kh)

    on_diag = level == -1
    for h in range(H):
        att_scr[h] = jnp.where(on_diag, scores(q, k, h), 0.0)

    block_end = b
    for p in range(n_levels):
        m = 1 << p
        second_half = (row & m) != 0
        r = jnp.where(second_half, pltpu.roll(block_end, m, 0), block_end)
        d = b - r
        decay = jnp.exp(jnp.where(second_half, d, -d))
        qd = q * decay
        kd = k * decay
        at_level = level == p
        for h in range(H):
            att_scr[h] = jnp.where(at_level, scores(qd, kd, h), att_scr[h])
        if p + 1 < n_levels:
            block_end = jnp.where(second_half, block_end, pltpu.roll(block_end, C - m, 0))

    state = state_scr[...]
    o_inter = _dot((q * jnp.exp(b)).astype(BF16), state.astype(BF16))

    nw = nw_ref[...]
    for h in range(H):
        o_h = o_inter[:, h * dv:(h + 1) * dv] + _dot(att_scr[h].astype(BF16),
                                                      v[:, h * dv:(h + 1) * dv])
        o_ref[0, :, h * dv:(h + 1) * dv] = _rmsnorm_rows(o_h, nw)

    bT = b.T
    kT = k.T
    b_last = bT[:, C - 1:C]
    kdT = (kT * jnp.exp(b_last - bT)).astype(BF16)
    ds = _dot(kdT, v)
    srow = lax.broadcasted_iota(jnp.int32, (H * dk, H * dv), 0) // dk
    scol = lax.broadcasted_iota(jnp.int32, (H * dk, H * dv), 1) // dv
    state_scr[...] = jnp.exp(b_last) * state + jnp.where(srow == scol, ds, 0.0)


def _gla(qb, kb, vb, g, nw):
    B, S, _ = qb.shape
    C = GLA_CHUNK
    spec = lambda width: pl.BlockSpec((1, C, width), lambda b, c: (b, c, 0))
    return pl.pallas_call(
        _gla_kernel,
        grid=(B, S // C),
        in_specs=[spec(B_KEY_WIDTH), spec(B_KEY_WIDTH), spec(B_WIDTH), spec(B_KEY_WIDTH),
                  pl.BlockSpec((1, B_VAL_DIM), lambda b, c: (0, 0))],
        out_specs=spec(B_WIDTH),
        out_shape=jax.ShapeDtypeStruct((B, S, B_WIDTH), F32),
        scratch_shapes=[pltpu.VMEM((B_KEY_WIDTH, B_WIDTH), F32),
                        pltpu.VMEM((B_HEADS, C, C), F32)],
        compiler_params=pltpu.CompilerParams(
            dimension_semantics=("arbitrary", "arbitrary"), vmem_limit_bytes=VMEM_LIMIT),
        name="gla",
    )(qb, kb, vb, g, nw)


def _silu(x):
    return x * jax.nn.sigmoid(x)


def _out_kernel(x_ref, oa_ref, ga_ref, ob_ref, gb_ref, qc_ref, gc_ref, km_ref, vm_ref,
                wo_ref, fw_ref, o_ref):
    dh = C_HEAD_DIM
    qc = (qc_ref[0] * (dh ** -0.5)).astype(BF16)
    km = km_ref[0]
    vm = vm_ref[0]
    oc = []
    for h in range(C_HEADS):
        sl = slice(h * dh, (h + 1) * dh)
        s = _dot_nt(qc[:, sl], km[:, sl])
        p = jnp.exp(s - jnp.max(s, axis=-1, keepdims=True))
        l = jnp.sum(p, axis=-1, keepdims=True)
        oc.append(_dot(p.astype(BF16), vm[:, sl]) / l)
    oc = jnp.concatenate(oc, axis=-1)

    ma = (oa_ref[0] * _silu(ga_ref[0])).astype(BF16)
    mb = (ob_ref[0] * _silu(gb_ref[0])).astype(BF16)
    mc = (oc * _silu(gc_ref[0])).astype(BF16)
    h_new = (x_ref[0]
             + _dot(ma, wo_ref[0:A_WIDTH, :])
             + _dot(mb, wo_ref[A_WIDTH:A_WIDTH + B_WIDTH, :])
             + _dot(mc, wo_ref[A_WIDTH + B_WIDTH:, :]))
    o_ref[0] = _rmsnorm_rows(h_new, fw_ref[...])


def _out(x, oa, ga, ob, gb, qc, gc, km, vm, wo, fw):
    B, S, D = x.shape
    tm = OUT_TM
    row = lambda width: pl.BlockSpec((1, tm, width), lambda b, i: (b, i, 0))
    mem = pl.BlockSpec((1, MEM_LEN, C_WIDTH), lambda b, i: (b, 0, 0))
    return pl.pallas_call(
        _out_kernel,
        grid=(B, S // tm),
        in_specs=[row(D), row(A_WIDTH), row(A_WIDTH), row(B_WIDTH), row(B_WIDTH),
                  row(C_WIDTH), row(C_WIDTH), mem, mem,
                  pl.BlockSpec(wo.shape, lambda b, i: (0, 0)),
                  pl.BlockSpec((1, D), lambda b, i: (0, 0))],
        out_specs=row(D),
        out_shape=jax.ShapeDtypeStruct((B, S, D), F32),
        compiler_params=pltpu.CompilerParams(
            dimension_semantics=("arbitrary", "arbitrary"), vmem_limit_bytes=VMEM_LIMIT),
        name="out",
    )(x, oa, ga, ob, gb, qc, gc, km, vm, wo, fw)


def kernel(x, mem, norm_w, w_in, w_alpha2, b_alpha, gla_norm_w, mem_norm_w, w_mem_kv, w_out,
           rel_bias, final_norm_w):
    assert norm_w.shape[0] == 1, "single layer"
    w = w_in[0]
    sizes = (A_WIDTH, A_WIDTH, A_WIDTH, A_WIDTH, B_KEY_WIDTH, B_KEY_WIDTH, B_WIDTH, B_WIDTH,
             GLA_RANK, C_WIDTH, C_WIDTH)
    names = ("qa", "ka", "va", "ga", "qb", "kb", "vb", "gb", "zb", "qc", "gc")
    cols, off = {}, 0
    for n, s in zip(names, sizes):
        cols[n] = w[:, off:off + s]
        off += s
    cols["zb"] = jnp.pad(cols["zb"], ((0, 0), (0, ZB_PAD - GLA_RANK)))
    wt = jnp.concatenate([cols["qa"].T, cols["va"].T], axis=0).astype(BF16)
    wr = jnp.concatenate([cols[n] for n, _ in _ROW_COLS], axis=1).astype(BF16)
    wa = jnp.pad(w_alpha2[0], ((0, ZB_PAD - GLA_RANK), (0, 0)))

    km, vm = _memkv(mem, mem_norm_w[0][None, :], w_mem_kv[0].astype(BF16))
    bias_own, bias_prev = _bias_tiles(rel_bias)
    (qaT, vaT, ka, ga, qb, kb, vb, gb, qc, gc, g) = _proj(
        x, norm_w[0][None, :], wt, wr, wa, b_alpha[0][None, :])
    oa = _moba(rel_bias, qaT, ka, vaT, bias_own, bias_prev)
    ob = _gla(qb, kb, vb, g, gla_norm_w[0][None, :])
    return _out(x, oa, ga, ob, gb, qc, gc, km, vm, w_out[0].astype(BF16),
                final_norm_w[None, :])
```

```python
import functools
import math

import jax
import jax.numpy as jnp
from jax import lax
from jax.experimental import pallas as pl
from jax.experimental.pallas import tpu as pltpu

F32 = jnp.float32
BF16 = jnp.bfloat16

D_MODEL = 1024
MEM_LEN = 256
A_HEADS = 8
A_HEAD_DIM = 64
A_WIDTH = 512
MOBA_BLOCK = 256
MOBA_TOPK = 3
B_HEADS = 4
B_KEY_DIM = 64
B_VAL_DIM = 128
B_KEY_WIDTH = 256
B_WIDTH = 512
GLA_RANK = 16
GLA_TAU = 16.0
C_HEADS = 4
C_HEAD_DIM = 128
C_WIDTH = 512
REL_BUCKETS = 32
REL_MAX_DIST = 128
RMS_EPS = 1e-6
NEG = -1e30

LANES = 128
ZB_PAD = LANES
PROJ_TM = 512
OUT_TM = 256
GLA_CHUNK = 256
VMEM_LIMIT = 56 * 1024 * 1024
LOG2E = 1.0 / math.log(2.0)
HEADS_PER_TILE = LANES // A_HEAD_DIM
PV_ROWS = A_HEAD_DIM + 16
MOBA_MAX_BLOCKS = 16
QK_LOOKAHEAD = 4


def _dot(a, b):
    return jnp.dot(a, b, preferred_element_type=F32)


def _dot_nt(a, b):
    return lax.dot_general(a, b, (((1,), (1,)), ((), ())), preferred_element_type=F32)


def _rmsnorm_rows(x, w):
    return x * lax.rsqrt(jnp.mean(x * x, axis=-1, keepdims=True) + RMS_EPS) * w


def _silu(x):
    return x * jax.nn.sigmoid(x)


def _memkv_kernel(mem_ref, nw_ref, w_ref, km_ref, vm_ref):
    u = _rmsnorm_rows(mem_ref[0], nw_ref[...]).astype(BF16)
    km_ref[0] = _dot(u, w_ref[:, :C_WIDTH]).astype(BF16)
    vm_ref[0] = _dot(u, w_ref[:, C_WIDTH:]).astype(BF16)


def _memkv(mem, nw, w_bf16):
    B = mem.shape[0]
    return pl.pallas_call(
        _memkv_kernel,
        grid=(B,),
        in_specs=[
            pl.BlockSpec((1, MEM_LEN, D_MODEL), lambda b: (b, 0, 0)),
            pl.BlockSpec((1, D_MODEL), lambda b: (0, 0)),
            pl.BlockSpec((D_MODEL, 2 * C_WIDTH), lambda b: (0, 0)),
        ],
        out_specs=[
            pl.BlockSpec((1, MEM_LEN, C_WIDTH), lambda b: (b, 0, 0)),
            pl.BlockSpec((1, MEM_LEN, C_WIDTH), lambda b: (b, 0, 0)),
        ],
        out_shape=[jax.ShapeDtypeStruct((B, MEM_LEN, C_WIDTH), BF16)] * 2,
        name="memkv",
    )(mem, nw, w_bf16)


def _t5_bucket(n):
    max_exact = REL_BUCKETS // 2
    nf = jnp.maximum(n, max_exact).astype(F32)
    large = max_exact + (jnp.log(nf / max_exact) / math.log(REL_MAX_DIST / max_exact)
                         * (REL_BUCKETS - max_exact)).astype(jnp.int32)
    large = jnp.minimum(large, REL_BUCKETS - 1)
    return jnp.where(n < max_exact, n, large)


def _bias_kernel(rb_ref, own_ref, prev_ref):
    h = pl.program_id(0)
    key = lax.broadcasted_iota(jnp.int32, (MOBA_BLOCK, MOBA_BLOCK), 0)
    qry = lax.broadcasted_iota(jnp.int32, (MOBA_BLOCK, MOBA_BLOCK), 1)
    rel = qry - key
    b_own = _t5_bucket(jnp.maximum(rel, 0))
    b_prev = _t5_bucket(rel + MOBA_BLOCK)
    own = jnp.zeros((MOBA_BLOCK, MOBA_BLOCK), F32)
    prev = jnp.zeros((MOBA_BLOCK, MOBA_BLOCK), F32)
    for bk in range(REL_BUCKETS):
        val = rb_ref[bk, h] * LOG2E
        own = jnp.where(b_own == bk, val, own)
        prev = jnp.where(b_prev == bk, val, prev)
    own_ref[0] = jnp.where(rel >= 0, own, NEG)
    prev_ref[0] = prev


def _bias_tiles(rel_bias):
    return pl.pallas_call(
        _bias_kernel,
        grid=(A_HEADS,),
        in_specs=[pl.BlockSpec(memory_space=pltpu.SMEM)],
        out_specs=[pl.BlockSpec((1, MOBA_BLOCK, MOBA_BLOCK), lambda h: (h, 0, 0))] * 2,
        out_shape=[jax.ShapeDtypeStruct((A_HEADS, MOBA_BLOCK, MOBA_BLOCK), F32)] * 2,
        name="t5bias",
    )(rel_bias)


_ROW_COLS = (("ka", A_WIDTH), ("ga", A_WIDTH), ("qb", B_KEY_WIDTH), ("kb", B_KEY_WIDTH),
             ("vb", B_WIDTH), ("gb", B_WIDTH), ("qc", C_WIDTH), ("gc", C_WIDTH), ("zb", ZB_PAD))
_ROW_OFF = {}
_off = 0
for _name, _width in _ROW_COLS:
    _ROW_OFF[_name] = (_off, _off + _width)
    _off += _width
ROW_COLS_TOTAL = _off


def _proj_kernel(x_ref, nw_ref, wt_ref, wr_ref, wa_ref, ba_ref,
                 qaT_ref, vaT_ref, ka_ref, ga_ref, qb_ref, kb_ref, vb_ref, gb_ref,
                 qc_ref, gc_ref, g_ref):
    u = _rmsnorm_rows(x_ref[0], nw_ref[...]).astype(BF16)
    qaT_ref[0] = _dot_nt(wt_ref[:A_WIDTH, :], u)
    vaT = _dot_nt(wt_ref[A_WIDTH:, :], u).astype(BF16)
    ones = jnp.ones((PV_ROWS - A_HEAD_DIM, vaT.shape[1]), BF16)
    pieces = []
    for h in range(A_HEADS):
        pieces += [vaT[h * A_HEAD_DIM:(h + 1) * A_HEAD_DIM], ones]
    vaT_ref[0] = jnp.concatenate(pieces, axis=0)

    def row(name):
        lo, hi = _ROW_OFF[name]
        return _dot(u, wr_ref[:, lo:hi])

    ka = row("ka")
    tm = ka.shape[0]
    nb = MOBA_MAX_BLOCKS
    key_pos = pl.program_id(1) * tm + lax.broadcasted_iota(jnp.int32, (tm, LANES - A_HEAD_DIM), 0)
    blk_id = jnp.right_shift(key_pos, MOBA_BLOCK.bit_length() - 1)
    col = lax.broadcasted_iota(jnp.int32, (tm, LANES - A_HEAD_DIM), 1)
    onehot = jnp.where((col < 2 * nb) & ((col & (nb - 1)) == blk_id), 1.0, 0.0)
    pieces = []
    for h in range(A_HEADS):
        pieces += [ka[:, h * A_HEAD_DIM:(h + 1) * A_HEAD_DIM], onehot]
    ka_ref[0] = jnp.concatenate(pieces, axis=1).astype(BF16)
    ga_ref[0] = _silu(row("ga")).astype(BF16)
    qb_ref[0] = (row("qb") * (B_KEY_DIM ** -0.5)).astype(BF16)
    kb_ref[0] = row("kb").astype(BF16)
    vb_ref[0] = row("vb").astype(BF16)
    gb_ref[0] = _silu(row("gb")).astype(BF16)
    qc_ref[0] = (row("qc") * (C_HEAD_DIM ** -0.5)).astype(BF16)
    gc_ref[0] = _silu(row("gc")).astype(BF16)
    zb = row("zb")
    z = jnp.dot(zb, wa_ref[...], preferred_element_type=F32,
                precision=lax.Precision.HIGHEST) + ba_ref[...]
    g_ref[0] = (jnp.minimum(z, 0.0) - jnp.log1p(jnp.exp(-jnp.abs(z)))) * (1.0 / GLA_TAU)


def _proj(x, nw, wt, wr, wa, ba):
    B, S, D = x.shape
    tm = PROJ_TM
    row_spec = lambda width: pl.BlockSpec((1, tm, width), lambda b, i: (b, i, 0))
    col_spec = lambda rows: pl.BlockSpec((1, rows, tm), lambda b, i: (b, 0, i))
    const = lambda shape: pl.BlockSpec(shape, lambda b, i: (0,) * len(shape))
    sds = jax.ShapeDtypeStruct
    return pl.pallas_call(
        _proj_kernel,
        grid=(B, S // tm),
        in_specs=[
            pl.BlockSpec((1, tm, D), lambda b, i: (b, i, 0)),
            const((1, D)),
            const((2 * A_WIDTH, D)),
            const((D, ROW_COLS_TOTAL)),
            const((ZB_PAD, B_KEY_WIDTH)),
            const((1, B_KEY_WIDTH)),
        ],
        out_specs=[col_spec(A_WIDTH), col_spec(A_HEADS * PV_ROWS), row_spec(A_HEADS * LANES),
                   row_spec(A_WIDTH),
                   row_spec(B_KEY_WIDTH), row_spec(B_KEY_WIDTH), row_spec(B_WIDTH),
                   row_spec(B_WIDTH), row_spec(C_WIDTH), row_spec(C_WIDTH),
                   row_spec(B_KEY_WIDTH)],
        out_shape=[sds((B, A_WIDTH, S), F32), sds((B, A_HEADS * PV_ROWS, S), BF16),
                   sds((B, S, A_HEADS * LANES), BF16), sds((B, S, A_WIDTH), BF16),
                   sds((B, S, B_KEY_WIDTH), BF16), sds((B, S, B_KEY_WIDTH), BF16),
                   sds((B, S, B_WIDTH), BF16), sds((B, S, B_WIDTH), BF16),
                   sds((B, S, C_WIDTH), BF16), sds((B, S, C_WIDTH), BF16),
                   sds((B, S, B_KEY_WIDTH), F32)],
        compiler_params=pltpu.CompilerParams(
            dimension_semantics=("arbitrary", "arbitrary"), vmem_limit_bytes=VMEM_LIMIT),
        name="proj",
    )(x, nw, wt, wr, wa, ba)


def _moba_kernel(rb_ref, qT_ref, k_ref, vT_ref, own_ref, prev_ref, sg_ref, o_ref,
                 kmean_scr, kmbd_scr, qh_scr, m_scr, acc_scr, s_scr):
    i = pl.program_id(1)
    nb = kmean_scr.shape[0]
    blk = MOBA_BLOCK
    dh = A_HEAD_DIM
    n_tiles = A_HEADS // HEADS_PER_TILE

    @pl.when(i == 0)
    def _():
        for n in range(nb):
            kb = jnp.concatenate(
                [k_ref[0, n * blk:(n + 1) * blk, h * LANES:h * LANES + dh] for h in range(A_HEADS)],
                axis=1).astype(F32)
            kmean_scr[n:n + 1, :] = jnp.mean(kb, axis=0, keepdims=True)
        lane_head = jnp.right_shift(lax.broadcasted_iota(jnp.int32, (nb, A_WIDTH), 1),
                                    dh.bit_length() - 1)
        for h in range(A_HEADS):
            km = jnp.where(lane_head == h, kmean_scr[...], 0.0)
            km_hi = km.astype(BF16)
            km_lo = (km - km_hi.astype(F32)).astype(BF16)
            kmbd_scr[h * nb:(h + 1) * nb, :] = jnp.concatenate([km_hi, km_lo, km_hi], axis=1)

    q_f32 = qT_ref[0]
    q_hi = q_f32.astype(BF16)
    q_lo = (q_f32 - q_hi.astype(F32)).astype(BF16)
    gate_all = _dot(kmbd_scr[...], jnp.concatenate([q_hi, q_hi, q_lo], axis=0))
    brow = lax.broadcasted_iota(jnp.int32, (nb, blk), 0)
    browf = brow.astype(F32)
    zeros = jnp.zeros((LANES - dh - 2 * nb, blk), F32)

    for h in range(A_HEADS):
        g = jnp.where(brow < i, gate_all[h * nb:(h + 1) * nb], NEG)
        sel = brow < 0
        for _ in range(MOBA_TOPK):
            m = jnp.max(g, axis=0, keepdims=True)
            idx = jnp.min(jnp.where(g == m, browf, float(nb)), axis=0, keepdims=True)
            pick = browf == idx
            sel = sel | pick
            g = jnp.where(pick, -jnp.inf, g)
        valid = sel & (brow < i)
        far_bias = rb_ref[REL_BUCKETS - 1, h] * LOG2E
        pen = jnp.where(brow < i - 1, jnp.where(valid, far_bias, NEG),
                        jnp.where(brow == i - 1, jnp.where(valid, 0.0, NEG), 0.0))
        pen_hi = pen.astype(BF16).astype(F32)
        q_h = qT_ref[0, h * dh:(h + 1) * dh, :] * (dh ** -0.5 * LOG2E)
        qh_scr[h] = jnp.concatenate([q_h, pen_hi, pen - pen_hi, zeros], axis=0).astype(BF16)

    no_prev = jnp.where(i == 0, NEG, 0.0)

    def logits(j, h, kind):
        rows = pl.ds(pl.multiple_of(j * blk, blk), blk)
        s = _dot(k_ref[0, rows, h * LANES:(h + 1) * LANES], qh_scr[h])
        if kind == "own":
            return s + own_ref[h]
        if kind == "prev":
            return s + (prev_ref[h] + no_prev)
        return s

    def fold(j, h, s, first):
        rows = pl.ds(pl.multiple_of(j * blk, blk), blk)
        bm = jnp.max(s, axis=0, keepdims=True)
        vj = vT_ref[0, h * PV_ROWS:(h + 1) * PV_ROWS, rows]
        if first:
            m_scr[h] = bm
            acc_scr[h] = _dot(vj, jnp.exp2(s - bm).astype(BF16))
        else:
            m_old = m_scr[h]
            m_new = jnp.maximum(m_old, bm)
            m_scr[h] = m_new
            acc_scr[h] = (jnp.exp2(m_old - m_new) * acc_scr[h]
                          + _dot(vj, jnp.exp2(s - m_new).astype(BF16)))

    n_far = i - 1
    j_prev = jnp.maximum(i - 1, 0)
    items = ([(i, "own", h) for h in range(A_HEADS)] + [(j_prev, "prev", h) for h in range(A_HEADS)]
             + [(0, "far", h) for h in range(QK_LOOKAHEAD)])
    n_fold = 2 * A_HEADS
    pending = []

    def issue(n):
        j, kind, h = items[n]
        s = logits(j, h, kind)
        if n < n_fold:
            pending.append(s)
        else:
            s_scr[h] = s

    for n in range(QK_LOOKAHEAD):
        issue(n)
    for n in range(n_fold):
        issue(n + QK_LOOKAHEAD)
        j, kind, h = items[n]
        fold(j, h, pending.pop(0), kind == "own")

    def far_block(j, carry):
        j_next = jnp.minimum(j + 1, n_far - 1)
        pending = [s_scr[h] for h in range(QK_LOOKAHEAD)]
        for h in range(A_HEADS):
            ahead = h + QK_LOOKAHEAD
            if ahead < A_HEADS:
                pending.append(logits(j, ahead, "far"))
            else:
                s_scr[ahead - A_HEADS] = logits(j_next, ahead - A_HEADS, "far")
            fold(j, h, pending.pop(0), False)
        return carry

    lax.fori_loop(0, jnp.maximum(n_far, 0), far_block, 0)

    for t in range(n_tiles):
        oT = []
        for hh in range(HEADS_PER_TILE):
            acc = acc_scr[t * HEADS_PER_TILE + hh]
            oT.append(acc[:dh] / acc[dh:dh + 1])
        cols = slice(t * LANES, (t + 1) * LANES)
        o_ref[0, :, cols] = (jnp.concatenate(oT, axis=0).T * sg_ref[0, :, cols]).astype(BF16)


def _moba(rel_bias, qaT, ka, vaT, bias_own, bias_prev, sga):
    B, S, _ = ka.shape
    nb = S // MOBA_BLOCK
    assert nb == MOBA_MAX_BLOCKS
    bias_spec = pl.BlockSpec((A_HEADS, MOBA_BLOCK, MOBA_BLOCK), lambda b, i: (0, 0, 0))
    return pl.pallas_call(
        _moba_kernel,
        grid=(B, nb),
        in_specs=[
            pl.BlockSpec(memory_space=pltpu.SMEM),
            pl.BlockSpec((1, A_WIDTH, MOBA_BLOCK), lambda b, i: (b, 0, i)),
            pl.BlockSpec((1, S, A_HEADS * LANES), lambda b, i: (b, 0, 0)),
            pl.BlockSpec((1, A_HEADS * PV_ROWS, S), lambda b, i: (b, 0, 0)),
            bias_spec, bias_spec,
            pl.BlockSpec((1, MOBA_BLOCK, A_WIDTH), lambda b, i: (b, i, 0)),
        ],
        out_specs=pl.BlockSpec((1, MOBA_BLOCK, A_WIDTH), lambda b, i: (b, i, 0)),
        out_shape=jax.ShapeDtypeStruct((B, S, A_WIDTH), BF16),
        scratch_shapes=[pltpu.VMEM((nb, A_WIDTH), F32),
                        pltpu.VMEM((A_HEADS * nb, 3 * A_WIDTH), BF16),
                        pltpu.VMEM((A_HEADS, LANES, MOBA_BLOCK), BF16),
                        pltpu.VMEM((A_HEADS, 1, MOBA_BLOCK), F32),
                        pltpu.VMEM((A_HEADS, PV_ROWS, MOBA_BLOCK), F32),
                        pltpu.VMEM((QK_LOOKAHEAD, MOBA_BLOCK, MOBA_BLOCK), F32)],
        compiler_params=pltpu.CompilerParams(
            dimension_semantics=("arbitrary", "arbitrary"),
            vmem_limit_bytes=VMEM_LIMIT),
        name="moba",
    )(rel_bias, qaT, ka, vaT, bias_own, bias_prev, sga)


def _gla_kernel(q_ref, k_ref, v_ref, g_ref, sg_ref, nw_ref, o_ref, state_scr, att_scr):
    c = pl.program_id(1)
    C = GLA_CHUNK
    dk, dv, H = B_KEY_DIM, B_VAL_DIM, B_HEADS
    n_levels = C.bit_length() - 1

    @pl.when(c == 0)
    def _():
        state_scr[...] = jnp.zeros_like(state_scr)

    q = q_ref[0].astype(F32)
    k = k_ref[0].astype(F32)
    v = v_ref[0]
    g = g_ref[0]

    row = lax.broadcasted_iota(jnp.int32, (C, H * dk), 0)
    lane = lax.broadcasted_iota(jnp.int32, (C, H * dk), 1)

    b = g
    sh = 1
    while sh < C:
        b = b + jnp.where(row >= sh, pltpu.roll(b, sh, 0), 0.0)
        sh *= 2

    tt = lax.broadcasted_iota(jnp.int32, (C, C), 0)
    ss = lax.broadcasted_iota(jnp.int32, (C, C), 1)
    txs = tt ^ ss
    level = jnp.full((C, C), -1, jnp.int32)
    for p in range(n_levels):
        level = level + (txs >= (1 << p)).astype(jnp.int32)
    level = jnp.where(ss <= tt, level, -2)

    head_masks = [(lane >= h * dk) & (lane < (h + 1) * dk) for h in range(H)]

    def scores(qq, kk, h):
        lo = (h * dk // LANES) * LANES
        kh = jnp.where(head_masks[h], kk, 0.0).astype(BF16)[:, lo:lo + LANES]
        return _dot_nt(qq.astype(BF16)[:, lo:lo + LANES], kh)

    on_diag = level == -1
    for h in range(H):
        att_scr[h] = jnp.where(on_diag, scores(q, k, h), 0.0)

    block_end = b
    for p in range(n_levels):
        m = 1 << p
        second_half = (row & m) != 0
        r = jnp.where(second_half, pltpu.roll(block_end, m, 0), block_end)
        d = b - r
        decay = jnp.exp(jnp.where(second_half, d, -d))
        qd = q * decay
        kd = k * decay
        at_level = level == p
        for h in range(H):
            att_scr[h] = jnp.where(at_level, scores(qd, kd, h), att_scr[h])
        if p + 1 < n_levels:
            block_end = jnp.where(second_half, block_end, pltpu.roll(block_end, C - m, 0))

    state = state_scr[...]
    o_inter = _dot((q * jnp.exp(b)).astype(BF16), state.astype(BF16))

    nw = nw_ref[...]
    for h in range(H):
        o_h = o_inter[:, h * dv:(h + 1) * dv] + _dot(att_scr[h].astype(BF16),
                                                      v[:, h * dv:(h + 1) * dv])
        cols = slice(h * dv, (h + 1) * dv)
        o_ref[0, :, cols] = (_rmsnorm_rows(o_h, nw) * sg_ref[0, :, cols]).astype(BF16)

    bT = b.T
    kT = k.T
    b_last = bT[:, C - 1:C]
    kdT = (kT * jnp.exp(b_last - bT)).astype(BF16)
    ds = _dot(kdT, v)
    srow = lax.broadcasted_iota(jnp.int32, (H * dk, H * dv), 0) // dk
    scol = lax.broadcasted_iota(jnp.int32, (H * dk, H * dv), 1) // dv
    state_scr[...] = jnp.exp(b_last) * state + jnp.where(srow == scol, ds, 0.0)


def _gla(qb, kb, vb, g, sgb, nw):
    B, S, _ = qb.shape
    C = GLA_CHUNK
    spec = lambda width: pl.BlockSpec((1, C, width), lambda b, c: (b, c, 0))
    return pl.pallas_call(
        _gla_kernel,
        grid=(B, S // C),
        in_specs=[spec(B_KEY_WIDTH), spec(B_KEY_WIDTH), spec(B_WIDTH), spec(B_KEY_WIDTH),
                  spec(B_WIDTH), pl.BlockSpec((1, B_VAL_DIM), lambda b, c: (0, 0))],
        out_specs=spec(B_WIDTH),
        out_shape=jax.ShapeDtypeStruct((B, S, B_WIDTH), BF16),
        scratch_shapes=[pltpu.VMEM((B_KEY_WIDTH, B_WIDTH), F32),
                        pltpu.VMEM((B_HEADS, C, C), F32)],
        compiler_params=pltpu.CompilerParams(
            dimension_semantics=("arbitrary", "arbitrary"), vmem_limit_bytes=VMEM_LIMIT),
        name="gla",
    )(qb, kb, vb, g, sgb, nw)


def _out_kernel(x_ref, ma_ref, mb_ref, qc_ref, sgc_ref, km_ref, vm_ref, wo_ref, fw_ref, o_ref):
    dh = C_HEAD_DIM
    qc = qc_ref[0]
    km = km_ref[0]
    vm = vm_ref[0]
    oc = []
    for h in range(C_HEADS):
        sl = slice(h * dh, (h + 1) * dh)
        s = _dot_nt(qc[:, sl], km[:, sl])
        p = jnp.exp(s - jnp.max(s, axis=-1, keepdims=True))
        l = jnp.sum(p, axis=-1, keepdims=True)
        oc.append(_dot(p.astype(BF16), vm[:, sl]) / l)
    oc = jnp.concatenate(oc, axis=-1)

    mc = (oc * sgc_ref[0]).astype(BF16)
    h_new = (x_ref[0]
             + _dot(ma_ref[0], wo_ref[0:A_WIDTH, :])
             + _dot(mb_ref[0], wo_ref[A_WIDTH:A_WIDTH + B_WIDTH, :])
             + _dot(mc, wo_ref[A_WIDTH + B_WIDTH:, :]))
    o_ref[0] = _rmsnorm_rows(h_new, fw_ref[...])


def _out(x, ma, mb, qc, sgc, km, vm, wo, fw):
    B, S, D = x.shape
    tm = OUT_TM
    row = lambda width: pl.BlockSpec((1, tm, width), lambda b, i: (b, i, 0))
    mem = pl.BlockSpec((1, MEM_LEN, C_WIDTH), lambda b, i: (b, 0, 0))
    return pl.pallas_call(
        _out_kernel,
        grid=(B, S // tm),
        in_specs=[row(D), row(A_WIDTH), row(B_WIDTH), row(C_WIDTH), row(C_WIDTH), mem, mem,
                  pl.BlockSpec(wo.shape, lambda b, i: (0, 0)),
                  pl.BlockSpec((1, D), lambda b, i: (0, 0))],
        out_specs=row(D),
        out_shape=jax.ShapeDtypeStruct((B, S, D), F32),
        compiler_params=pltpu.CompilerParams(
            dimension_semantics=("arbitrary", "arbitrary"), vmem_limit_bytes=VMEM_LIMIT),
        name="out",
    )(x, ma, mb, qc, sgc, km, vm, wo, fw)


def kernel(x, mem, norm_w, w_in, w_alpha2, b_alpha, gla_norm_w, mem_norm_w, w_mem_kv, w_out,
           rel_bias, final_norm_w):
    assert norm_w.shape[0] == 1, "single layer"
    w = w_in[0]
    sizes = (A_WIDTH, A_WIDTH, A_WIDTH, A_WIDTH, B_KEY_WIDTH, B_KEY_WIDTH, B_WIDTH, B_WIDTH,
             GLA_RANK, C_WIDTH, C_WIDTH)
    names = ("qa", "ka", "va", "ga", "qb", "kb", "vb", "gb", "zb", "qc", "gc")
    cols, off = {}, 0
    for n, s in zip(names, sizes):
        cols[n] = w[:, off:off + s]
        off += s
    cols["zb"] = jnp.pad(cols["zb"], ((0, 0), (0, ZB_PAD - GLA_RANK)))
    wt = jnp.concatenate([cols["qa"].T, cols["va"].T], axis=0).astype(BF16)
    wr = jnp.concatenate([cols[n] for n, _ in _ROW_COLS], axis=1).astype(BF16)
    wa = jnp.pad(w_alpha2[0], ((0, ZB_PAD - GLA_RANK), (0, 0)))

    km, vm = _memkv(mem, mem_norm_w[0][None, :], w_mem_kv[0].astype(BF16))
    bias_own, bias_prev = _bias_tiles(rel_bias)
    (qaT, vaT, ka, sga, qb, kb, vb, sgb, qc, sgc, g) = _proj(
        x, norm_w[0][None, :], wt, wr, wa, b_alpha[0][None, :])
    ma = _moba(rel_bias, qaT, ka, vaT, bias_own, bias_prev, sga)
    mb = _gla(qb, kb, vb, g, sgb, gla_norm_w[0][None, :])
    return _out(x, ma, mb, qc, sgc, km, vm, w_out[0].astype(BF16), final_norm_w[None, :])
```

```python
import functools
import math

import jax
import jax.numpy as jnp
from jax import lax
from jax.experimental import pallas as pl
from jax.experimental.pallas import tpu as pltpu

F32 = jnp.float32
BF16 = jnp.bfloat16

D_MODEL = 1024
MEM_LEN = 256
A_HEADS = 8
A_HEAD_DIM = 64
A_WIDTH = 512
MOBA_BLOCK = 256
MOBA_TOPK = 3
B_HEADS = 4
B_KEY_DIM = 64
B_VAL_DIM = 128
B_KEY_WIDTH = 256
B_WIDTH = 512
GLA_RANK = 16
GLA_TAU = 16.0
C_HEADS = 4
C_HEAD_DIM = 128
C_WIDTH = 512
REL_BUCKETS = 32
REL_MAX_DIST = 128
RMS_EPS = 1e-6
NEG = -1e30

LANES = 128
ZB_PAD = LANES
PROJ_TM = 512
OUT_TM = 256
GLA_CHUNK = 256
VMEM_LIMIT = 56 * 1024 * 1024
LOG2E = 1.0 / math.log(2.0)
HEADS_PER_TILE = LANES // A_HEAD_DIM
PV_ROWS = A_HEAD_DIM + 16
MOBA_MAX_BLOCKS = 16
QK_LOOKAHEAD = 4


def _dot(a, b):
    return jnp.dot(a, b, preferred_element_type=F32)


def _dot_nt(a, b):
    return lax.dot_general(a, b, (((1,), (1,)), ((), ())), preferred_element_type=F32)


def _rmsnorm_rows(x, w):
    return x * lax.rsqrt(jnp.mean(x * x, axis=-1, keepdims=True) + RMS_EPS) * w


def _silu(x):
    return x * jax.nn.sigmoid(x)


def _memkv_kernel(mem_ref, nw_ref, w_ref, km_ref, vm_ref):
    u = _rmsnorm_rows(mem_ref[0], nw_ref[...]).astype(BF16)
    km_ref[0] = _dot(u, w_ref[:, :C_WIDTH]).astype(BF16)
    vm_ref[0] = _dot(u, w_ref[:, C_WIDTH:]).astype(BF16)


def _memkv(mem, nw, w_bf16):
    B = mem.shape[0]
    return pl.pallas_call(
        _memkv_kernel,
        grid=(B,),
        in_specs=[
            pl.BlockSpec((1, MEM_LEN, D_MODEL), lambda b: (b, 0, 0)),
            pl.BlockSpec((1, D_MODEL), lambda b: (0, 0)),
            pl.BlockSpec((D_MODEL, 2 * C_WIDTH), lambda b: (0, 0)),
        ],
        out_specs=[
            pl.BlockSpec((1, MEM_LEN, C_WIDTH), lambda b: (b, 0, 0)),
            pl.BlockSpec((1, MEM_LEN, C_WIDTH), lambda b: (b, 0, 0)),
        ],
        out_shape=[jax.ShapeDtypeStruct((B, MEM_LEN, C_WIDTH), BF16)] * 2,
        name="memkv",
    )(mem, nw, w_bf16)


def _t5_bucket(n):
    max_exact = REL_BUCKETS // 2
    nf = jnp.maximum(n, max_exact).astype(F32)
    large = max_exact + (jnp.log(nf / max_exact) / math.log(REL_MAX_DIST / max_exact)
                         * (REL_BUCKETS - max_exact)).astype(jnp.int32)
    large = jnp.minimum(large, REL_BUCKETS - 1)
    return jnp.where(n < max_exact, n, large)


def _bias_kernel(rb_ref, own_ref, prev_ref):
    h = pl.program_id(0)
    key = lax.broadcasted_iota(jnp.int32, (MOBA_BLOCK, MOBA_BLOCK), 0)
    qry = lax.broadcasted_iota(jnp.int32, (MOBA_BLOCK, MOBA_BLOCK), 1)
    rel = qry - key
    b_own = _t5_bucket(jnp.maximum(rel, 0))
    b_prev = _t5_bucket(rel + MOBA_BLOCK)
    own = jnp.zeros((MOBA_BLOCK, MOBA_BLOCK), F32)
    prev = jnp.zeros((MOBA_BLOCK, MOBA_BLOCK), F32)
    for bk in range(REL_BUCKETS):
        val = rb_ref[bk, h] * LOG2E
        own = jnp.where(b_own == bk, val, own)
        prev = jnp.where(b_prev == bk, val, prev)
    own_ref[0] = jnp.where(rel >= 0, own, NEG)
    prev_ref[0] = prev


def _bias_tiles(rel_bias):
    return pl.pallas_call(
        _bias_kernel,
        grid=(A_HEADS,),
        in_specs=[pl.BlockSpec(memory_space=pltpu.SMEM)],
        out_specs=[pl.BlockSpec((1, MOBA_BLOCK, MOBA_BLOCK), lambda h: (h, 0, 0))] * 2,
        out_shape=[jax.ShapeDtypeStruct((A_HEADS, MOBA_BLOCK, MOBA_BLOCK), F32)] * 2,
        name="t5bias",
    )(rel_bias)


_ROW_COLS = (("qa", A_WIDTH), ("ka", A_WIDTH), ("va", A_WIDTH), ("ga", A_WIDTH),
             ("qb", B_KEY_WIDTH), ("kb", B_KEY_WIDTH), ("vb", B_WIDTH), ("gb", B_WIDTH),
             ("qc", C_WIDTH), ("gc", C_WIDTH), ("zb", ZB_PAD))
_ROW_OFF = {}
_off = 0
for _name, _width in _ROW_COLS:
    _ROW_OFF[_name] = (_off, _off + _width)
    _off += _width
ROW_COLS_TOTAL = _off


def _proj_kernel(x_ref, nw_ref, wr_ref, wa_ref, ba_ref,
                 qaT_ref, vaT_ref, ka_ref, ga_ref, qb_ref, kb_ref, vb_ref, gb_ref,
                 qc_ref, gc_ref, g_ref):
    u = _rmsnorm_rows(x_ref[0], nw_ref[...]).astype(BF16)

    def row(name):
        lo, hi = _ROW_OFF[name]
        return _dot(u, wr_ref[:, lo:hi])

    zb = row("zb")
    zb_hi = zb.astype(BF16)
    zb_lo = (zb - zb_hi.astype(F32)).astype(BF16)
    wa = wa_ref[...]
    wa_hi = wa.astype(BF16)
    wa_lo = (wa - wa_hi.astype(F32)).astype(BF16)
    z = _dot(jnp.concatenate([zb_hi, zb_lo, zb_hi], axis=1),
             jnp.concatenate([wa_hi, wa_hi, wa_lo], axis=0)) + ba_ref[...]
    g_ref[0] = (jnp.minimum(z, 0.0) - jnp.log1p(jnp.exp(-jnp.abs(z)))) * (1.0 / GLA_TAU)

    qaT_ref[0] = row("qa").T
    vaT = row("va").T.astype(BF16)
    ones = jnp.ones((PV_ROWS - A_HEAD_DIM, vaT.shape[1]), BF16)
    pieces = []
    for h in range(A_HEADS):
        pieces += [vaT[h * A_HEAD_DIM:(h + 1) * A_HEAD_DIM], ones]
    vaT_ref[0] = jnp.concatenate(pieces, axis=0)

    ka = row("ka")
    tm = ka.shape[0]
    nb = MOBA_MAX_BLOCKS
    key_pos = pl.program_id(1) * tm + lax.broadcasted_iota(jnp.int32, (tm, LANES - A_HEAD_DIM), 0)
    blk_id = jnp.right_shift(key_pos, MOBA_BLOCK.bit_length() - 1)
    col = lax.broadcasted_iota(jnp.int32, (tm, LANES - A_HEAD_DIM), 1)
    onehot = jnp.where((col < 2 * nb) & ((col & (nb - 1)) == blk_id), 1.0, 0.0)
    pieces = []
    for h in range(A_HEADS):
        pieces += [ka[:, h * A_HEAD_DIM:(h + 1) * A_HEAD_DIM], onehot]
    ka_ref[0] = jnp.concatenate(pieces, axis=1).astype(BF16)
    ga_ref[0] = _silu(row("ga")).astype(BF16)
    qb_ref[0] = (row("qb") * (B_KEY_DIM ** -0.5)).astype(BF16)
    kb_ref[0] = row("kb").astype(BF16)
    vb_ref[0] = row("vb").astype(BF16)
    gb_ref[0] = _silu(row("gb")).astype(BF16)
    qc_ref[0] = (row("qc") * (C_HEAD_DIM ** -0.5)).astype(BF16)
    gc_ref[0] = _silu(row("gc")).astype(BF16)


def _proj(x, nw, wr, wa, ba):
    B, S, D = x.shape
    tm = PROJ_TM
    row_spec = lambda width: pl.BlockSpec((1, tm, width), lambda b, i: (b, i, 0))
    col_spec = lambda rows: pl.BlockSpec((1, rows, tm), lambda b, i: (b, 0, i))
    const = lambda shape: pl.BlockSpec(shape, lambda b, i: (0,) * len(shape))
    sds = jax.ShapeDtypeStruct
    return pl.pallas_call(
        _proj_kernel,
        grid=(B, S // tm),
        in_specs=[
            pl.BlockSpec((1, tm, D), lambda b, i: (b, i, 0)),
            const((1, D)),
            const((D, ROW_COLS_TOTAL)),
            const((ZB_PAD, B_KEY_WIDTH)),
            const((1, B_KEY_WIDTH)),
        ],
        out_specs=[col_spec(A_WIDTH), col_spec(A_HEADS * PV_ROWS), row_spec(A_HEADS * LANES),
                   row_spec(A_WIDTH),
                   row_spec(B_KEY_WIDTH), row_spec(B_KEY_WIDTH), row_spec(B_WIDTH),
                   row_spec(B_WIDTH), row_spec(C_WIDTH), row_spec(C_WIDTH),
                   row_spec(B_KEY_WIDTH)],
        out_shape=[sds((B, A_WIDTH, S), F32), sds((B, A_HEADS * PV_ROWS, S), BF16),
                   sds((B, S, A_HEADS * LANES), BF16), sds((B, S, A_WIDTH), BF16),
                   sds((B, S, B_KEY_WIDTH), BF16), sds((B, S, B_KEY_WIDTH), BF16),
                   sds((B, S, B_WIDTH), BF16), sds((B, S, B_WIDTH), BF16),
                   sds((B, S, C_WIDTH), BF16), sds((B, S, C_WIDTH), BF16),
                   sds((B, S, B_KEY_WIDTH), F32)],
        compiler_params=pltpu.CompilerParams(
            dimension_semantics=("arbitrary", "arbitrary"), vmem_limit_bytes=VMEM_LIMIT),
        name="proj",
    )(x, nw, wr, wa, ba)


def _moba_kernel(rb_ref, qT_ref, k_ref, vT_ref, own_ref, prev_ref, sg_ref, o_ref,
                 kmean_scr, kmbd_scr, qh_scr, m_scr, acc_scr, s_scr):
    i = pl.program_id(1)
    nb = kmean_scr.shape[0]
    blk = MOBA_BLOCK
    dh = A_HEAD_DIM
    n_tiles = A_HEADS // HEADS_PER_TILE

    @pl.when(i == 0)
    def _():
        for n in range(nb):
            kb = jnp.concatenate(
                [k_ref[0, n * blk:(n + 1) * blk, h * LANES:h * LANES + dh] for h in range(A_HEADS)],
                axis=1).astype(F32)
            kmean_scr[n:n + 1, :] = jnp.mean(kb, axis=0, keepdims=True)
        lane_head = jnp.right_shift(lax.broadcasted_iota(jnp.int32, (nb, A_WIDTH), 1),
                                    dh.bit_length() - 1)
        for h in range(A_HEADS):
            km = jnp.where(lane_head == h, kmean_scr[...], 0.0)
            km_hi = km.astype(BF16)
            km_lo = (km - km_hi.astype(F32)).astype(BF16)
            kmbd_scr[h * nb:(h + 1) * nb, :] = jnp.concatenate([km_hi, km_lo, km_hi], axis=1)

    q_f32 = qT_ref[0]
    q_hi = q_f32.astype(BF16)
    q_lo = (q_f32 - q_hi.astype(F32)).astype(BF16)
    gate_all = _dot(kmbd_scr[...], jnp.concatenate([q_hi, q_hi, q_lo], axis=0))
    brow = lax.broadcasted_iota(jnp.int32, (nb, blk), 0)
    browf = brow.astype(F32)
    zeros = jnp.zeros((LANES - dh - 2 * nb, blk), F32)

    for h in range(A_HEADS):
        g = jnp.where(brow < i, gate_all[h * nb:(h + 1) * nb], NEG)
        sel = brow < 0
        for _ in range(MOBA_TOPK):
            m = jnp.max(g, axis=0, keepdims=True)
            idx = jnp.min(jnp.where(g == m, browf, float(nb)), axis=0, keepdims=True)
            pick = browf == idx
            sel = sel | pick
            g = jnp.where(pick, -jnp.inf, g)
        valid = sel & (brow < i)
        far_bias = rb_ref[REL_BUCKETS - 1, h] * LOG2E
        pen = jnp.where(brow < i - 1, jnp.where(valid, far_bias, NEG),
                        jnp.where(brow == i - 1, jnp.where(valid, 0.0, NEG), 0.0))
        pen_hi = pen.astype(BF16).astype(F32)
        q_h = qT_ref[0, h * dh:(h + 1) * dh, :] * (dh ** -0.5 * LOG2E)
        qh_scr[h] = jnp.concatenate([q_h, pen_hi, pen - pen_hi, zeros], axis=0).astype(BF16)

    no_prev = jnp.where(i == 0, NEG, 0.0)

    def logits(j, h, kind):
        rows = pl.ds(pl.multiple_of(j * blk, blk), blk)
        s = _dot(k_ref[0, rows, h * LANES:(h + 1) * LANES], qh_scr[h])
        if kind == "own":
            return s + own_ref[h]
        if kind == "prev":
            return s + (prev_ref[h] + no_prev)
        return s

    def fold(j, h, s, first):
        rows = pl.ds(pl.multiple_of(j * blk, blk), blk)
        bm = jnp.max(s, axis=0, keepdims=True)
        vj = vT_ref[0, h * PV_ROWS:(h + 1) * PV_ROWS, rows]
        if first:
            m_scr[h] = bm
            acc_scr[h] = _dot(vj, jnp.exp2(s - bm).astype(BF16))
        else:
            m_old = m_scr[h]
            m_new = jnp.maximum(m_old, bm)
            m_scr[h] = m_new
            acc_scr[h] = (jnp.exp2(m_old - m_new) * acc_scr[h]
                          + _dot(vj, jnp.exp2(s - m_new).astype(BF16)))

    n_far = i - 1
    j_prev = jnp.maximum(i - 1, 0)
    items = ([(i, "own", h) for h in range(A_HEADS)] + [(j_prev, "prev", h) for h in range(A_HEADS)]
             + [(0, "far", h) for h in range(QK_LOOKAHEAD)])
    n_fold = 2 * A_HEADS
    pending = []

    def issue(n):
        j, kind, h = items[n]
        s = logits(j, h, kind)
        if n < n_fold:
            pending.append(s)
        else:
            s_scr[h] = s

    for n in range(QK_LOOKAHEAD):
        issue(n)
    for n in range(n_fold):
        issue(n + QK_LOOKAHEAD)
        j, kind, h = items[n]
        fold(j, h, pending.pop(0), kind == "own")

    def far_block(j, carry):
        j_next = jnp.minimum(j + 1, n_far - 1)
        pending = [s_scr[h] for h in range(QK_LOOKAHEAD)]
        for h in range(A_HEADS):
            ahead = h + QK_LOOKAHEAD
            if ahead < A_HEADS:
                pending.append(logits(j, ahead, "far"))
            else:
                s_scr[ahead - A_HEADS] = logits(j_next, ahead - A_HEADS, "far")
            fold(j, h, pending.pop(0), False)
        return carry

    lax.fori_loop(0, jnp.maximum(n_far, 0), far_block, 0)

    for t in range(n_tiles):
        oT = []
        for hh in range(HEADS_PER_TILE):
            acc = acc_scr[t * HEADS_PER_TILE + hh]
            oT.append(acc[:dh] / acc[dh:dh + 1])
        cols = slice(t * LANES, (t + 1) * LANES)
        o_ref[0, :, cols] = (jnp.concatenate(oT, axis=0).T * sg_ref[0, :, cols]).astype(BF16)


def _moba(rel_bias, qaT, ka, vaT, bias_own, bias_prev, sga):
    B, S, _ = ka.shape
    nb = S // MOBA_BLOCK
    assert nb == MOBA_MAX_BLOCKS
    bias_spec = pl.BlockSpec((A_HEADS, MOBA_BLOCK, MOBA_BLOCK), lambda b, i: (0, 0, 0))
    return pl.pallas_call(
        _moba_kernel,
        grid=(B, nb),
        in_specs=[
            pl.BlockSpec(memory_space=pltpu.SMEM),
            pl.BlockSpec((1, A_WIDTH, MOBA_BLOCK), lambda b, i: (b, 0, i)),
            pl.BlockSpec((1, S, A_HEADS * LANES), lambda b, i: (b, 0, 0)),
            pl.BlockSpec((1, A_HEADS * PV_ROWS, S), lambda b, i: (b, 0, 0)),
            bias_spec, bias_spec,
            pl.BlockSpec((1, MOBA_BLOCK, A_WIDTH), lambda b, i: (b, i, 0)),
        ],
        out_specs=pl.BlockSpec((1, MOBA_BLOCK, A_WIDTH), lambda b, i: (b, i, 0)),
        out_shape=jax.ShapeDtypeStruct((B, S, A_WIDTH), BF16),
        scratch_shapes=[pltpu.VMEM((nb, A_WIDTH), F32),
                        pltpu.VMEM((A_HEADS * nb, 3 * A_WIDTH), BF16),
                        pltpu.VMEM((A_HEADS, LANES, MOBA_BLOCK), BF16),
                        pltpu.VMEM((A_HEADS, 1, MOBA_BLOCK), F32),
                        pltpu.VMEM((A_HEADS, PV_ROWS, MOBA_BLOCK), F32),
                        pltpu.VMEM((QK_LOOKAHEAD, MOBA_BLOCK, MOBA_BLOCK), F32)],
        compiler_params=pltpu.CompilerParams(
            dimension_semantics=("arbitrary", "arbitrary"),
            vmem_limit_bytes=VMEM_LIMIT),
        name="moba",
    )(rel_bias, qaT, ka, vaT, bias_own, bias_prev, sga)


def _gla_kernel(q_ref, k_ref, v_ref, g_ref, sg_ref, nw_ref, o_ref, state_scr, att_scr):
    c = pl.program_id(1)
    C = GLA_CHUNK
    dk, dv, H = B_KEY_DIM, B_VAL_DIM, B_HEADS
    n_levels = C.bit_length() - 1

    @pl.when(c == 0)
    def _():
        state_scr[...] = jnp.zeros_like(state_scr)

    q = q_ref[0].astype(F32)
    k = k_ref[0].astype(F32)
    v = v_ref[0]
    g = g_ref[0]

    row = lax.broadcasted_iota(jnp.int32, (C, H * dk), 0)
    lane = lax.broadcasted_iota(jnp.int32, (C, H * dk), 1)

    b = g
    sh = 1
    while sh < C:
        b = b + jnp.where(row >= sh, pltpu.roll(b, sh, 0), 0.0)
        sh *= 2

    tt = lax.broadcasted_iota(jnp.int32, (C, C), 0)
    ss = lax.broadcasted_iota(jnp.int32, (C, C), 1)
    txs = tt ^ ss
    level = jnp.full((C, C), -1, jnp.int32)
    for p in range(n_levels):
        level = level + (txs >= (1 << p)).astype(jnp.int32)
    level = jnp.where(ss <= tt, level, -2)

    head_masks = [(lane >= h * dk) & (lane < (h + 1) * dk) for h in range(H)]

    def scores(qq, kk, h):
        lo = (h * dk // LANES) * LANES
        kh = jnp.where(head_masks[h], kk, 0.0).astype(BF16)[:, lo:lo + LANES]
        return _dot_nt(qq.astype(BF16)[:, lo:lo + LANES], kh)

    on_diag = level == -1
    for h in range(H):
        att_scr[h] = jnp.where(on_diag, scores(q, k, h), 0.0)

    block_end = b
    for p in range(n_levels):
        m = 1 << p
        second_half = (row & m) != 0
        r = jnp.where(second_half, pltpu.roll(block_end, m, 0), block_end)
        d = b - r
        decay = jnp.exp(jnp.where(second_half, d, -d))
        qd = q * decay
        kd = k * decay
        at_level = level == p
        for h in range(H):
            att_scr[h] = jnp.where(at_level, scores(qd, kd, h), att_scr[h])
        if p + 1 < n_levels:
            block_end = jnp.where(second_half, block_end, pltpu.roll(block_end, C - m, 0))

    state = state_scr[...]
    o_inter = _dot((q * jnp.exp(b)).astype(BF16), state.astype(BF16))

    nw = nw_ref[...]
    for h in range(H):
        o_h = o_inter[:, h * dv:(h + 1) * dv] + _dot(att_scr[h].astype(BF16),
                                                      v[:, h * dv:(h + 1) * dv])
        cols = slice(h * dv, (h + 1) * dv)
        o_ref[0, :, cols] = (_rmsnorm_rows(o_h, nw) * sg_ref[0, :, cols]).astype(BF16)

    bT = b.T
    kT = k.T
    b_last = bT[:, C - 1:C]
    kdT = (kT * jnp.exp(b_last - bT)).astype(BF16)
    ds = _dot(kdT, v)
    srow = lax.broadcasted_iota(jnp.int32, (H * dk, H * dv), 0) // dk
    scol = lax.broadcasted_iota(jnp.int32, (H * dk, H * dv), 1) // dv
    state_scr[...] = jnp.exp(b_last) * state + jnp.where(srow == scol, ds, 0.0)


def _gla(qb, kb, vb, g, sgb, nw):
    B, S, _ = qb.shape
    C = GLA_CHUNK
    spec = lambda width: pl.BlockSpec((1, C, width), lambda b, c: (b, c, 0))
    return pl.pallas_call(
        _gla_kernel,
        grid=(B, S // C),
        in_specs=[spec(B_KEY_WIDTH), spec(B_KEY_WIDTH), spec(B_WIDTH), spec(B_KEY_WIDTH),
                  spec(B_WIDTH), pl.BlockSpec((1, B_VAL_DIM), lambda b, c: (0, 0))],
        out_specs=spec(B_WIDTH),
        out_shape=jax.ShapeDtypeStruct((B, S, B_WIDTH), BF16),
        scratch_shapes=[pltpu.VMEM((B_KEY_WIDTH, B_WIDTH), F32),
                        pltpu.VMEM((B_HEADS, C, C), F32)],
        compiler_params=pltpu.CompilerParams(
            dimension_semantics=("arbitrary", "arbitrary"), vmem_limit_bytes=VMEM_LIMIT),
        name="gla",
    )(qb, kb, vb, g, sgb, nw)


def _out_kernel(x_ref, ma_ref, mb_ref, qc_ref, sgc_ref, km_ref, vm_ref, wo_ref, fw_ref, o_ref):
    dh = C_HEAD_DIM
    qc = qc_ref[0]
    km = km_ref[0]
    vm = vm_ref[0]
    oc = []
    for h in range(C_HEADS):
        sl = slice(h * dh, (h + 1) * dh)
        s = _dot_nt(qc[:, sl], km[:, sl])
        p = jnp.exp(s - jnp.max(s, axis=-1, keepdims=True))
        l = jnp.sum(p, axis=-1, keepdims=True)
        oc.append(_dot(p.astype(BF16), vm[:, sl]) / l)
    oc = jnp.concatenate(oc, axis=-1)

    mc = (oc * sgc_ref[0]).astype(BF16)
    h_new = (x_ref[0]
             + _dot(ma_ref[0], wo_ref[0:A_WIDTH, :])
             + _dot(mb_ref[0], wo_ref[A_WIDTH:A_WIDTH + B_WIDTH, :])
             + _dot(mc, wo_ref[A_WIDTH + B_WIDTH:, :]))
    o_ref[0] = _rmsnorm_rows(h_new, fw_ref[...])


def _out(x, ma, mb, qc, sgc, km, vm, wo, fw):
    B, S, D = x.shape
    tm = OUT_TM
    row = lambda width: pl.BlockSpec((1, tm, width), lambda b, i: (b, i, 0))
    mem = pl.BlockSpec((1, MEM_LEN, C_WIDTH), lambda b, i: (b, 0, 0))
    return pl.pallas_call(
        _out_kernel,
        grid=(B, S // tm),
        in_specs=[row(D), row(A_WIDTH), row(B_WIDTH), row(C_WIDTH), row(C_WIDTH), mem, mem,
                  pl.BlockSpec(wo.shape, lambda b, i: (0, 0)),
                  pl.BlockSpec((1, D), lambda b, i: (0, 0))],
        out_specs=row(D),
        out_shape=jax.ShapeDtypeStruct((B, S, D), F32),
        compiler_params=pltpu.CompilerParams(
            dimension_semantics=("arbitrary", "arbitrary"), vmem_limit_bytes=VMEM_LIMIT),
        name="out",
    )(x, ma, mb, qc, sgc, km, vm, wo, fw)


def kernel(x, mem, norm_w, w_in, w_alpha2, b_alpha, gla_norm_w, mem_norm_w, w_mem_kv, w_out,
           rel_bias, final_norm_w):
    assert norm_w.shape[0] == 1, "single layer"
    w = w_in[0]
    sizes = (A_WIDTH, A_WIDTH, A_WIDTH, A_WIDTH, B_KEY_WIDTH, B_KEY_WIDTH, B_WIDTH, B_WIDTH,
             GLA_RANK, C_WIDTH, C_WIDTH)
    names = ("qa", "ka", "va", "ga", "qb", "kb", "vb", "gb", "zb", "qc", "gc")
    cols, off = {}, 0
    for n, s in zip(names, sizes):
        cols[n] = w[:, off:off + s]
        off += s
    cols["zb"] = jnp.pad(cols["zb"], ((0, 0), (0, ZB_PAD - GLA_RANK)))
    wr = jnp.concatenate([cols[n] for n, _ in _ROW_COLS], axis=1).astype(BF16)
    wa = jnp.pad(w_alpha2[0], ((0, ZB_PAD - GLA_RANK), (0, 0)))

    km, vm = _memkv(mem, mem_norm_w[0][None, :], w_mem_kv[0].astype(BF16))
    bias_own, bias_prev = _bias_tiles(rel_bias)
    (qaT, vaT, ka, sga, qb, kb, vb, sgb, qc, sgc, g) = _proj(
        x, norm_w[0][None, :], wr, wa, b_alpha[0][None, :])
    ma = _moba(rel_bias, qaT, ka, vaT, bias_own, bias_prev, sga)
    mb = _gla(qb, kb, vb, g, sgb, gla_norm_w[0][None, :])
    return _out(x, ma, mb, qc, sgc, km, vm, w_out[0].astype(BF16), final_norm_w[None, :])
```

```python
import functools
import math

import jax
import jax.numpy as jnp
from jax import lax
from jax.experimental import pallas as pl
from jax.experimental.pallas import tpu as pltpu

F32 = jnp.float32
BF16 = jnp.bfloat16

D_MODEL = 1024
MEM_LEN = 256
A_HEADS = 8
A_HEAD_DIM = 64
A_WIDTH = 512
MOBA_BLOCK = 256
MOBA_TOPK = 3
B_HEADS = 4
B_KEY_DIM = 64
B_VAL_DIM = 128
B_KEY_WIDTH = 256
B_WIDTH = 512
GLA_RANK = 16
GLA_TAU = 16.0
C_HEADS = 4
C_HEAD_DIM = 128
C_WIDTH = 512
REL_BUCKETS = 32
REL_MAX_DIST = 128
RMS_EPS = 1e-6
NEG = -1e30

LANES = 128
ZB_PAD = LANES
PROJ_TM = 512
OUT_TM = 256
GLA_CHUNK = 256
VMEM_LIMIT = 56 * 1024 * 1024
LOG2E = 1.0 / math.log(2.0)
HEADS_PER_TILE = LANES // A_HEAD_DIM
PV_ROWS = A_HEAD_DIM + 16
MOBA_MAX_BLOCKS = 16
QK_LOOKAHEAD = 4
FAR_UNROLL = 2


def _dot(a, b):
    return jnp.dot(a, b, preferred_element_type=F32)


def _dot_nt(a, b):
    return lax.dot_general(a, b, (((1,), (1,)), ((), ())), preferred_element_type=F32)


def _rmsnorm_rows(x, w):
    return x * lax.rsqrt(jnp.mean(x * x, axis=-1, keepdims=True) + RMS_EPS) * w


def _silu(x):
    return x * jax.nn.sigmoid(x)


def _memkv_kernel(mem_ref, nw_ref, w_ref, km_ref, vm_ref):
    u = _rmsnorm_rows(mem_ref[0], nw_ref[...]).astype(BF16)
    km_ref[0] = _dot(u, w_ref[:, :C_WIDTH]).astype(BF16)
    vm_ref[0] = _dot(u, w_ref[:, C_WIDTH:]).astype(BF16)


def _memkv(mem, nw, w_bf16):
    B = mem.shape[0]
    return pl.pallas_call(
        _memkv_kernel,
        grid=(B,),
        in_specs=[
            pl.BlockSpec((1, MEM_LEN, D_MODEL), lambda b: (b, 0, 0)),
            pl.BlockSpec((1, D_MODEL), lambda b: (0, 0)),
            pl.BlockSpec((D_MODEL, 2 * C_WIDTH), lambda b: (0, 0)),
        ],
        out_specs=[
            pl.BlockSpec((1, MEM_LEN, C_WIDTH), lambda b: (b, 0, 0)),
            pl.BlockSpec((1, MEM_LEN, C_WIDTH), lambda b: (b, 0, 0)),
        ],
        out_shape=[jax.ShapeDtypeStruct((B, MEM_LEN, C_WIDTH), BF16)] * 2,
        name="memkv",
    )(mem, nw, w_bf16)


def _t5_bucket(n):
    max_exact = REL_BUCKETS // 2
    nf = jnp.maximum(n, max_exact).astype(F32)
    large = max_exact + jnp.floor(jnp.log(nf / max_exact) / math.log(REL_MAX_DIST / max_exact)
                                  * (REL_BUCKETS - max_exact)).astype(jnp.int32)
    large = jnp.minimum(large, REL_BUCKETS - 1)
    return jnp.where(n < max_exact, n, large)


def _bias_kernel(rb_ref, own_ref, prev_ref):
    h = pl.program_id(0)
    key = lax.broadcasted_iota(jnp.int32, (MOBA_BLOCK, MOBA_BLOCK), 0)
    qry = lax.broadcasted_iota(jnp.int32, (MOBA_BLOCK, MOBA_BLOCK), 1)
    rel = qry - key
    b_own = _t5_bucket(jnp.maximum(rel, 0))
    b_prev = _t5_bucket(rel + MOBA_BLOCK)
    own = jnp.zeros((MOBA_BLOCK, MOBA_BLOCK), F32)
    prev = jnp.zeros((MOBA_BLOCK, MOBA_BLOCK), F32)
    for bk in range(REL_BUCKETS):
        val = rb_ref[bk, h] * LOG2E
        own = jnp.where(b_own == bk, val, own)
        prev = jnp.where(b_prev == bk, val, prev)
    own_ref[0] = jnp.where(rel >= 0, own, NEG)
    prev_ref[0] = prev


def _bias_tiles(rel_bias):
    return pl.pallas_call(
        _bias_kernel,
        grid=(A_HEADS,),
        in_specs=[pl.BlockSpec(memory_space=pltpu.SMEM)],
        out_specs=[pl.BlockSpec((1, MOBA_BLOCK, MOBA_BLOCK), lambda h: (h, 0, 0))] * 2,
        out_shape=[jax.ShapeDtypeStruct((A_HEADS, MOBA_BLOCK, MOBA_BLOCK), F32)] * 2,
        name="t5bias",
    )(rel_bias)


_ROW_COLS = (("qa", A_WIDTH), ("ka", A_WIDTH), ("va", A_WIDTH), ("ga", A_WIDTH),
             ("qb", B_KEY_WIDTH), ("kb", B_KEY_WIDTH), ("vb", B_WIDTH), ("gb", B_WIDTH),
             ("qc", C_WIDTH), ("gc", C_WIDTH), ("zb", ZB_PAD))
_ROW_OFF = {}
_off = 0
for _name, _width in _ROW_COLS:
    _ROW_OFF[_name] = (_off, _off + _width)
    _off += _width
ROW_COLS_TOTAL = _off


def _proj_kernel(x_ref, nw_ref, wr_ref, wa_ref, ba_ref,
                 qaT_ref, vaT_ref, ka_ref, ga_ref, qb_ref, kb_ref, vb_ref, gb_ref,
                 qc_ref, gc_ref, g_ref):
    u = _rmsnorm_rows(x_ref[0], nw_ref[...]).astype(BF16)

    def row(name):
        lo, hi = _ROW_OFF[name]
        return _dot(u, wr_ref[:, lo:hi])

    zb = row("zb")
    zb_hi = zb.astype(BF16)
    zb_lo = (zb - zb_hi.astype(F32)).astype(BF16)
    wa = wa_ref[...]
    wa_hi = wa.astype(BF16)
    wa_lo = (wa - wa_hi.astype(F32)).astype(BF16)
    z = _dot(jnp.concatenate([zb_hi, zb_lo, zb_hi], axis=1),
             jnp.concatenate([wa_hi, wa_hi, wa_lo], axis=0)) + ba_ref[...]
    g_ref[0] = (jnp.minimum(z, 0.0) - jnp.log1p(jnp.exp(-jnp.abs(z)))) * (1.0 / GLA_TAU)

    qaT_ref[0] = row("qa").T
    vaT = row("va").T.astype(BF16)
    ones = jnp.ones((PV_ROWS - A_HEAD_DIM, vaT.shape[1]), BF16)
    pieces = []
    for h in range(A_HEADS):
        pieces += [vaT[h * A_HEAD_DIM:(h + 1) * A_HEAD_DIM], ones]
    vaT_ref[0] = jnp.concatenate(pieces, axis=0)

    ka = row("ka")
    tm = ka.shape[0]
    nb = MOBA_MAX_BLOCKS
    key_pos = pl.program_id(1) * tm + lax.broadcasted_iota(jnp.int32, (tm, LANES - A_HEAD_DIM), 0)
    blk_id = jnp.right_shift(key_pos, MOBA_BLOCK.bit_length() - 1)
    col = lax.broadcasted_iota(jnp.int32, (tm, LANES - A_HEAD_DIM), 1)
    onehot = jnp.where((col < 2 * nb) & ((col & (nb - 1)) == blk_id), 1.0, 0.0)
    pieces = []
    for h in range(A_HEADS):
        pieces += [ka[:, h * A_HEAD_DIM:(h + 1) * A_HEAD_DIM], onehot]
    ka_ref[0] = jnp.concatenate(pieces, axis=1).astype(BF16)
    ga_ref[0] = _silu(row("ga")).astype(BF16)
    qb_ref[0] = (row("qb") * (B_KEY_DIM ** -0.5)).astype(BF16)
    kb_ref[0] = row("kb").astype(BF16)
    vb_ref[0] = row("vb").astype(BF16)
    gb_ref[0] = _silu(row("gb")).astype(BF16)
    qc_ref[0] = (row("qc") * (C_HEAD_DIM ** -0.5)).astype(BF16)
    gc_ref[0] = _silu(row("gc")).astype(BF16)


def _proj(x, nw, wr, wa, ba):
    B, S, D = x.shape
    tm = PROJ_TM
    row_spec = lambda width: pl.BlockSpec((1, tm, width), lambda b, i: (b, i, 0))
    col_spec = lambda rows: pl.BlockSpec((1, rows, tm), lambda b, i: (b, 0, i))
    const = lambda shape: pl.BlockSpec(shape, lambda b, i: (0,) * len(shape))
    sds = jax.ShapeDtypeStruct
    return pl.pallas_call(
        _proj_kernel,
        grid=(B, S // tm),
        in_specs=[
            pl.BlockSpec((1, tm, D), lambda b, i: (b, i, 0)),
            const((1, D)),
            const((D, ROW_COLS_TOTAL)),
            const((ZB_PAD, B_KEY_WIDTH)),
            const((1, B_KEY_WIDTH)),
        ],
        out_specs=[col_spec(A_WIDTH), col_spec(A_HEADS * PV_ROWS), row_spec(A_HEADS * LANES),
                   row_spec(A_WIDTH),
                   row_spec(B_KEY_WIDTH), row_spec(B_KEY_WIDTH), row_spec(B_WIDTH),
                   row_spec(B_WIDTH), row_spec(C_WIDTH), row_spec(C_WIDTH),
                   row_spec(B_KEY_WIDTH)],
        out_shape=[sds((B, A_WIDTH, S), F32), sds((B, A_HEADS * PV_ROWS, S), BF16),
                   sds((B, S, A_HEADS * LANES), BF16), sds((B, S, A_WIDTH), BF16),
                   sds((B, S, B_KEY_WIDTH), BF16), sds((B, S, B_KEY_WIDTH), BF16),
                   sds((B, S, B_WIDTH), BF16), sds((B, S, B_WIDTH), BF16),
                   sds((B, S, C_WIDTH), BF16), sds((B, S, C_WIDTH), BF16),
                   sds((B, S, B_KEY_WIDTH), F32)],
        compiler_params=pltpu.CompilerParams(
            dimension_semantics=("arbitrary", "arbitrary"), vmem_limit_bytes=VMEM_LIMIT),
        name="proj",
    )(x, nw, wr, wa, ba)


def _moba_kernel(rb_ref, qT_ref, k_ref, vT_ref, own_ref, prev_ref, sg_ref, o_ref,
                 kmean_scr, kmbd_scr, qh_scr, m_scr, acc_scr, s_scr):
    i = pl.program_id(1)
    nb = kmean_scr.shape[0]
    blk = MOBA_BLOCK
    dh = A_HEAD_DIM
    n_tiles = A_HEADS // HEADS_PER_TILE

    @pl.when(i == 0)
    def _():
        for n in range(nb):
            kb = jnp.concatenate(
                [k_ref[0, n * blk:(n + 1) * blk, h * LANES:h * LANES + dh] for h in range(A_HEADS)],
                axis=1).astype(F32)
            kmean_scr[n:n + 1, :] = jnp.mean(kb, axis=0, keepdims=True)
        lane_head = jnp.right_shift(lax.broadcasted_iota(jnp.int32, (nb, A_WIDTH), 1),
                                    dh.bit_length() - 1)
        for h in range(A_HEADS):
            km = jnp.where(lane_head == h, kmean_scr[...], 0.0)
            km_hi = km.astype(BF16)
            km_lo = (km - km_hi.astype(F32)).astype(BF16)
            kmbd_scr[h * nb:(h + 1) * nb, :] = jnp.concatenate([km_hi, km_lo, km_hi], axis=1)

    q_f32 = qT_ref[0]
    q_hi = q_f32.astype(BF16)
    q_lo = (q_f32 - q_hi.astype(F32)).astype(BF16)
    gate_all = _dot(kmbd_scr[...], jnp.concatenate([q_hi, q_hi, q_lo], axis=0))
    brow = lax.broadcasted_iota(jnp.int32, (nb, blk), 0)
    browf = brow.astype(F32)
    zeros = jnp.zeros((LANES - dh - 2 * nb, blk), F32)

    for h in range(A_HEADS):
        g = jnp.where(brow < i, gate_all[h * nb:(h + 1) * nb], NEG)
        sel = brow < 0
        for _ in range(MOBA_TOPK):
            m = jnp.max(g, axis=0, keepdims=True)
            idx = jnp.min(jnp.where(g == m, browf, float(nb)), axis=0, keepdims=True)
            pick = browf == idx
            sel = sel | pick
            g = jnp.where(pick, -jnp.inf, g)
        valid = sel & (brow < i)
        far_bias = rb_ref[REL_BUCKETS - 1, h] * LOG2E
        pen = jnp.where(brow < i - 1, jnp.where(valid, far_bias, NEG),
                        jnp.where(brow == i - 1, jnp.where(valid, 0.0, NEG), 0.0))
        pen_hi = pen.astype(BF16).astype(F32)
        q_h = qT_ref[0, h * dh:(h + 1) * dh, :] * (dh ** -0.5 * LOG2E)
        qh_scr[h] = jnp.concatenate([q_h, pen_hi, pen - pen_hi, zeros], axis=0).astype(BF16)

    no_prev = jnp.where(i == 0, NEG, 0.0)

    def logits(j, h, kind):
        rows = pl.ds(pl.multiple_of(j * blk, blk), blk)
        s = _dot(k_ref[0, rows, h * LANES:(h + 1) * LANES], qh_scr[h])
        if kind == "own":
            return s + own_ref[h]
        if kind == "prev":
            return s + (prev_ref[h] + no_prev)
        return s

    def fold(j, h, s, first):
        rows = pl.ds(pl.multiple_of(j * blk, blk), blk)
        bm = jnp.max(s, axis=0, keepdims=True)
        vj = vT_ref[0, h * PV_ROWS:(h + 1) * PV_ROWS, rows]
        if first:
            m_scr[h] = bm
            acc_scr[h] = _dot(vj, jnp.exp2(s - bm).astype(BF16))
        else:
            m_old = m_scr[h]
            m_new = jnp.maximum(m_old, bm)
            m_scr[h] = m_new
            acc_scr[h] = (jnp.exp2(m_old - m_new) * acc_scr[h]
                          + _dot(vj, jnp.exp2(s - m_new).astype(BF16)))

    n_far = i - 1
    j_prev = jnp.maximum(i - 1, 0)
    items = ([(i, "own", h) for h in range(A_HEADS)] + [(j_prev, "prev", h) for h in range(A_HEADS)]
             + [(0, "far", h) for h in range(QK_LOOKAHEAD)])
    n_fold = 2 * A_HEADS
    pending = []

    def issue(n):
        j, kind, h = items[n]
        s = logits(j, h, kind)
        if n < n_fold:
            pending.append(s)
        else:
            s_scr[h] = s

    for n in range(QK_LOOKAHEAD):
        issue(n)
    for n in range(n_fold):
        issue(n + QK_LOOKAHEAD)
        j, kind, h = items[n]
        fold(j, h, pending.pop(0), kind == "own")

    def far_group(j0, n_blocks):
        j_after = jnp.minimum(j0 + n_blocks, n_far - 1)
        stream = ([(j0 + t, h) for t in range(n_blocks) for h in range(A_HEADS)]
                  + [(j_after, h) for h in range(QK_LOOKAHEAD)])
        n_items = n_blocks * A_HEADS
        pending = [s_scr[h] for h in range(QK_LOOKAHEAD)]
        for n in range(n_items):
            j, h = stream[n + QK_LOOKAHEAD]
            if n + QK_LOOKAHEAD < n_items:
                pending.append(logits(j, h, "far"))
            else:
                s_scr[h] = logits(j, h, "far")
            j, h = stream[n]
            fold(j, h, pending.pop(0), False)

    def far_groups(g, carry):
        far_group(g * FAR_UNROLL, FAR_UNROLL)
        return carry

    n_groups = jnp.maximum(n_far, 0) // FAR_UNROLL
    lax.fori_loop(0, n_groups, far_groups, 0)
    for rest in range(1, FAR_UNROLL):
        @pl.when(n_far - n_groups * FAR_UNROLL == rest)
        def _():
            far_group(n_groups * FAR_UNROLL, rest)

    for t in range(n_tiles):
        oT = []
        for hh in range(HEADS_PER_TILE):
            acc = acc_scr[t * HEADS_PER_TILE + hh]
            oT.append(acc[:dh] / acc[dh:dh + 1])
        cols = slice(t * LANES, (t + 1) * LANES)
        o_ref[0, :, cols] = (jnp.concatenate(oT, axis=0).T * sg_ref[0, :, cols]).astype(BF16)


def _moba(rel_bias, qaT, ka, vaT, bias_own, bias_prev, sga):
    B, S, _ = ka.shape
    nb = S // MOBA_BLOCK
    assert nb == MOBA_MAX_BLOCKS
    bias_spec = pl.BlockSpec((A_HEADS, MOBA_BLOCK, MOBA_BLOCK), lambda b, i: (0, 0, 0))
    return pl.pallas_call(
        _moba_kernel,
        grid=(B, nb),
        in_specs=[
            pl.BlockSpec(memory_space=pltpu.SMEM),
            pl.BlockSpec((1, A_WIDTH, MOBA_BLOCK), lambda b, i: (b, 0, i)),
            pl.BlockSpec((1, S, A_HEADS * LANES), lambda b, i: (b, 0, 0)),
            pl.BlockSpec((1, A_HEADS * PV_ROWS, S), lambda b, i: (b, 0, 0)),
            bias_spec, bias_spec,
            pl.BlockSpec((1, MOBA_BLOCK, A_WIDTH), lambda b, i: (b, i, 0)),
        ],
        out_specs=pl.BlockSpec((1, MOBA_BLOCK, A_WIDTH), lambda b, i: (b, i, 0)),
        out_shape=jax.ShapeDtypeStruct((B, S, A_WIDTH), BF16),
        scratch_shapes=[pltpu.VMEM((nb, A_WIDTH), F32),
                        pltpu.VMEM((A_HEADS * nb, 3 * A_WIDTH), BF16),
                        pltpu.VMEM((A_HEADS, LANES, MOBA_BLOCK), BF16),
                        pltpu.VMEM((A_HEADS, 1, MOBA_BLOCK), F32),
                        pltpu.VMEM((A_HEADS, PV_ROWS, MOBA_BLOCK), F32),
                        pltpu.VMEM((QK_LOOKAHEAD, MOBA_BLOCK, MOBA_BLOCK), F32)],
        compiler_params=pltpu.CompilerParams(
            dimension_semantics=("arbitrary", "arbitrary"),
            vmem_limit_bytes=VMEM_LIMIT),
        name="moba",
    )(rel_bias, qaT, ka, vaT, bias_own, bias_prev, sga)


def _gla_kernel(q_ref, k_ref, v_ref, g_ref, sg_ref, nw_ref, o_ref, state_scr, att_scr):
    c = pl.program_id(1)
    C = GLA_CHUNK
    dk, dv, H = B_KEY_DIM, B_VAL_DIM, B_HEADS
    n_levels = C.bit_length() - 1

    @pl.when(c == 0)
    def _():
        state_scr[...] = jnp.zeros_like(state_scr)

    q = q_ref[0].astype(F32)
    k = k_ref[0].astype(F32)
    v = v_ref[0]
    g = g_ref[0]

    row = lax.broadcasted_iota(jnp.int32, (C, H * dk), 0)
    lane = lax.broadcasted_iota(jnp.int32, (C, H * dk), 1)

    b = g
    sh = 1
    while sh < C:
        b = b + jnp.where(row >= sh, pltpu.roll(b, sh, 0), 0.0)
        sh *= 2

    tt = lax.broadcasted_iota(jnp.int32, (C, C), 0)
    ss = lax.broadcasted_iota(jnp.int32, (C, C), 1)
    txs = tt ^ ss
    level = jnp.full((C, C), -1, jnp.int32)
    for p in range(n_levels):
        level = level + (txs >= (1 << p)).astype(jnp.int32)
    level = jnp.where(ss <= tt, level, -2)

    head_masks = [(lane >= h * dk) & (lane < (h + 1) * dk) for h in range(H)]

    def scores(qq, kk, h):
        lo = (h * dk // LANES) * LANES
        kh = jnp.where(head_masks[h], kk, 0.0).astype(BF16)[:, lo:lo + LANES]
        return _dot_nt(qq.astype(BF16)[:, lo:lo + LANES], kh)

    on_diag = level == -1
    for h in range(H):
        att_scr[h] = jnp.where(on_diag, scores(q, k, h), 0.0)

    block_end = b
    for p in range(n_levels):
        m = 1 << p
        second_half = (row & m) != 0
        r = jnp.where(second_half, pltpu.roll(block_end, m, 0), block_end)
        d = b - r
        decay = jnp.exp(jnp.where(second_half, d, -d))
        qd = q * decay
        kd = k * decay
        at_level = level == p
        for h in range(H):
            att_scr[h] = jnp.where(at_level, scores(qd, kd, h), att_scr[h])
        if p + 1 < n_levels:
            block_end = jnp.where(second_half, block_end, pltpu.roll(block_end, C - m, 0))

    state = state_scr[...]
    o_inter = _dot((q * jnp.exp(b)).astype(BF16), state.astype(BF16))

    nw = nw_ref[...]
    for h in range(H):
        o_h = o_inter[:, h * dv:(h + 1) * dv] + _dot(att_scr[h].astype(BF16),
                                                      v[:, h * dv:(h + 1) * dv])
        cols = slice(h * dv, (h + 1) * dv)
        o_ref[0, :, cols] = (_rmsnorm_rows(o_h, nw) * sg_ref[0, :, cols]).astype(BF16)

    bT = b.T
    kT = k.T
    b_last = bT[:, C - 1:C]
    kdT = (kT * jnp.exp(b_last - bT)).astype(BF16)
    ds = _dot(kdT, v)
    srow = lax.broadcasted_iota(jnp.int32, (H * dk, H * dv), 0) // dk
    scol = lax.broadcasted_iota(jnp.int32, (H * dk, H * dv), 1) // dv
    state_scr[...] = jnp.exp(b_last) * state + jnp.where(srow == scol, ds, 0.0)


def _gla(qb, kb, vb, g, sgb, nw):
    B, S, _ = qb.shape
    C = GLA_CHUNK
    spec = lambda width: pl.BlockSpec((1, C, width), lambda b, c: (b, c, 0))
    return pl.pallas_call(
        _gla_kernel,
        grid=(B, S // C),
        in_specs=[spec(B_KEY_WIDTH), spec(B_KEY_WIDTH), spec(B_WIDTH), spec(B_KEY_WIDTH),
                  spec(B_WIDTH), pl.BlockSpec((1, B_VAL_DIM), lambda b, c: (0, 0))],
        out_specs=spec(B_WIDTH),
        out_shape=jax.ShapeDtypeStruct((B, S, B_WIDTH), BF16),
        scratch_shapes=[pltpu.VMEM((B_KEY_WIDTH, B_WIDTH), F32),
                        pltpu.VMEM((B_HEADS, C, C), F32)],
        compiler_params=pltpu.CompilerParams(
            dimension_semantics=("arbitrary", "arbitrary"), vmem_limit_bytes=VMEM_LIMIT),
        name="gla",
    )(qb, kb, vb, g, sgb, nw)


def _out_kernel(x_ref, ma_ref, mb_ref, qc_ref, sgc_ref, km_ref, vm_ref, wo_ref, fw_ref, o_ref):
    dh = C_HEAD_DIM
    qc = qc_ref[0]
    km = km_ref[0]
    vm = vm_ref[0]
    oc = []
    for h in range(C_HEADS):
        sl = slice(h * dh, (h + 1) * dh)
        s = _dot_nt(qc[:, sl], km[:, sl])
        p = jnp.exp(s - jnp.max(s, axis=-1, keepdims=True))
        l = jnp.sum(p, axis=-1, keepdims=True)
        oc.append(_dot(p.astype(BF16), vm[:, sl]) / l)
    oc = jnp.concatenate(oc, axis=-1)

    mc = (oc * sgc_ref[0]).astype(BF16)
    h_new = (x_ref[0]
             + _dot(ma_ref[0], wo_ref[0:A_WIDTH, :])
             + _dot(mb_ref[0], wo_ref[A_WIDTH:A_WIDTH + B_WIDTH, :])
             + _dot(mc, wo_ref[A_WIDTH + B_WIDTH:, :]))
    o_ref[0] = _rmsnorm_rows(h_new, fw_ref[...])


def _out(x, ma, mb, qc, sgc, km, vm, wo, fw):
    B, S, D = x.shape
    tm = OUT_TM
    row = lambda width: pl.BlockSpec((1, tm, width), lambda b, i: (b, i, 0))
    mem = pl.BlockSpec((1, MEM_LEN, C_WIDTH), lambda b, i: (b, 0, 0))
    return pl.pallas_call(
        _out_kernel,
        grid=(B, S // tm),
        in_specs=[row(D), row(A_WIDTH), row(B_WIDTH), row(C_WIDTH), row(C_WIDTH), mem, mem,
                  pl.BlockSpec(wo.shape, lambda b, i: (0, 0)),
                  pl.BlockSpec((1, D), lambda b, i: (0, 0))],
        out_specs=row(D),
        out_shape=jax.ShapeDtypeStruct((B, S, D), F32),
        compiler_params=pltpu.CompilerParams(
            dimension_semantics=("arbitrary", "arbitrary"), vmem_limit_bytes=VMEM_LIMIT),
        name="out",
    )(x, ma, mb, qc, sgc, km, vm, wo, fw)


def kernel(x, mem, norm_w, w_in, w_alpha2, b_alpha, gla_norm_w, mem_norm_w, w_mem_kv, w_out,
           rel_bias, final_norm_w):
    assert norm_w.shape[0] == 1, "single layer"
    w = w_in[0]
    sizes = (A_WIDTH, A_WIDTH, A_WIDTH, A_WIDTH, B_KEY_WIDTH, B_KEY_WIDTH, B_WIDTH, B_WIDTH,
             GLA_RANK, C_WIDTH, C_WIDTH)
    names = ("qa", "ka", "va", "ga", "qb", "kb", "vb", "gb", "zb", "qc", "gc")
    cols, off = {}, 0
    for n, s in zip(names, sizes):
        cols[n] = w[:, off:off + s]
        off += s
    cols["zb"] = jnp.pad(cols["zb"], ((0, 0), (0, ZB_PAD - GLA_RANK)))
    wr = jnp.concatenate([cols[n] for n, _ in _ROW_COLS], axis=1).astype(BF16)
    wa = jnp.pad(w_alpha2[0], ((0, ZB_PAD - GLA_RANK), (0, 0)))

    km, vm = _memkv(mem, mem_norm_w[0][None, :], w_mem_kv[0].astype(BF16))
    bias_own, bias_prev = _bias_tiles(rel_bias)
    (qaT, vaT, ka, sga, qb, kb, vb, sgb, qc, sgc, g) = _proj(
        x, norm_w[0][None, :], wr, wa, b_alpha[0][None, :])
    ma = _moba(rel_bias, qaT, ka, vaT, bias_own, bias_prev, sga)
    mb = _gla(qb, kb, vb, g, sgb, gla_norm_w[0][None, :])
    return _out(x, ma, mb, qc, sgc, km, vm, w_out[0].astype(BF16), final_norm_w[None, :])
```

```python
import functools
import math

import jax
import jax.numpy as jnp
from jax import lax
from jax.experimental import pallas as pl
from jax.experimental.pallas import tpu as pltpu

F32 = jnp.float32
BF16 = jnp.bfloat16

D_MODEL = 1024
MEM_LEN = 256
A_HEADS = 8
A_HEAD_DIM = 64
A_WIDTH = 512
MOBA_BLOCK = 256
MOBA_TOPK = 3
B_HEADS = 4
B_KEY_DIM = 64
B_VAL_DIM = 128
B_KEY_WIDTH = 256
B_WIDTH = 512
GLA_RANK = 16
GLA_TAU = 16.0
C_HEADS = 4
C_HEAD_DIM = 128
C_WIDTH = 512
MIX_WIDTH = A_WIDTH + B_WIDTH + C_WIDTH
REL_BUCKETS = 32
REL_MAX_DIST = 128
RMS_EPS = 1e-6
NEG = -1e30

LANES = 128
SUBLANES = 8
ZB_PAD = LANES
PROJ_TM = 512
OUT_TM = 512
GLA_CHUNK = 256
VMEM_LIMIT = 56 * 1024 * 1024
LOG2E = 1.0 / math.log(2.0)
HEADS_PER_TILE = LANES // A_HEAD_DIM
PV_ROWS = A_HEAD_DIM + 16
MOBA_MAX_BLOCKS = 16
QK_LOOKAHEAD = 4
FAR_UNROLL = 2


def _dot(a, b):
    return jnp.dot(a, b, preferred_element_type=F32)


def _dot_nt(a, b):
    return lax.dot_general(a, b, (((1,), (1,)), ((), ())), preferred_element_type=F32)


def _rmsnorm_rows(x, w):
    return x * lax.rsqrt(jnp.mean(x * x, axis=-1, keepdims=True) + RMS_EPS) * w


def _silu(x):
    return x * jax.nn.sigmoid(x)


def _memkv_kernel(mem_ref, nw_ref, w_ref, km_ref, vm_ref):
    u = _rmsnorm_rows(mem_ref[0], nw_ref[...]).astype(BF16)
    km_ref[0] = _dot(u, w_ref[0, :, :C_WIDTH].astype(BF16)).astype(BF16)
    vm_ref[0] = _dot(u, w_ref[0, :, C_WIDTH:].astype(BF16)).astype(BF16)


def _memkv(mem, nw, w_mem_kv):
    B = mem.shape[0]
    return pl.pallas_call(
        _memkv_kernel,
        grid=(B,),
        in_specs=[
            pl.BlockSpec((1, MEM_LEN, D_MODEL), lambda b: (b, 0, 0)),
            pl.BlockSpec((1, D_MODEL), lambda b: (0, 0)),
            pl.BlockSpec(w_mem_kv.shape, lambda b: (0, 0, 0)),
        ],
        out_specs=[
            pl.BlockSpec((1, MEM_LEN, C_WIDTH), lambda b: (b, 0, 0)),
            pl.BlockSpec((1, MEM_LEN, C_WIDTH), lambda b: (b, 0, 0)),
        ],
        out_shape=[jax.ShapeDtypeStruct((B, MEM_LEN, C_WIDTH), BF16)] * 2,
        name="memkv",
    )(mem, nw, w_mem_kv)


def _t5_bucket(n):
    max_exact = REL_BUCKETS // 2
    nf = jnp.maximum(n, max_exact).astype(F32)
    large = max_exact + jnp.floor(jnp.log(nf / max_exact) / math.log(REL_MAX_DIST / max_exact)
                                  * (REL_BUCKETS - max_exact)).astype(jnp.int32)
    large = jnp.minimum(large, REL_BUCKETS - 1)
    return jnp.where(n < max_exact, n, large)


def _bias_kernel(rb_ref, own_ref, prev_ref):
    h = pl.program_id(0)
    key = lax.broadcasted_iota(jnp.int32, (MOBA_BLOCK, MOBA_BLOCK), 0)
    qry = lax.broadcasted_iota(jnp.int32, (MOBA_BLOCK, MOBA_BLOCK), 1)
    rel = qry - key
    b_own = _t5_bucket(jnp.maximum(rel, 0))
    b_prev = _t5_bucket(rel + MOBA_BLOCK)
    own = jnp.zeros((MOBA_BLOCK, MOBA_BLOCK), F32)
    prev = jnp.zeros((MOBA_BLOCK, MOBA_BLOCK), F32)
    for bk in range(REL_BUCKETS):
        val = rb_ref[bk, h] * LOG2E
        own = jnp.where(b_own == bk, val, own)
        prev = jnp.where(b_prev == bk, val, prev)
    own_ref[0] = jnp.where(rel >= 0, own, NEG)
    prev_ref[0] = prev


def _bias_tiles(rel_bias):
    return pl.pallas_call(
        _bias_kernel,
        grid=(A_HEADS,),
        in_specs=[pl.BlockSpec(memory_space=pltpu.SMEM)],
        out_specs=[pl.BlockSpec((1, MOBA_BLOCK, MOBA_BLOCK), lambda h: (h, 0, 0))] * 2,
        out_shape=[jax.ShapeDtypeStruct((A_HEADS, MOBA_BLOCK, MOBA_BLOCK), F32)] * 2,
        name="t5bias",
    )(rel_bias)


_ROW_COLS = (("qa", A_WIDTH), ("ka", A_WIDTH), ("va", A_WIDTH), ("ga", A_WIDTH),
             ("qb", B_KEY_WIDTH), ("kb", B_KEY_WIDTH), ("vb", B_WIDTH), ("gb", B_WIDTH),
             ("qc", C_WIDTH), ("gc", C_WIDTH), ("zb", ZB_PAD))
_ROW_OFF = {}
_off = 0
for _name, _width in _ROW_COLS:
    _ROW_OFF[_name] = (_off, _off + _width)
    _off += _width
ROW_COLS_TOTAL = _off


W_IN_ALIGNED = 4 * A_WIDTH + 2 * B_KEY_WIDTH + 2 * B_WIDTH


def _proj_kernel(x_ref, nw_ref, w_ref, wa_ref, ba_ref,
                 qaT_ref, vaT_ref, ka_ref, ga_ref, qb_ref, kb_ref, vb_ref, gb_ref,
                 qc_ref, gc_ref, g_ref, wr_ref):
    @pl.when((pl.program_id(0) == 0) & (pl.program_id(1) == 0))
    def _():
        chunk = 4 * LANES
        for lo in range(0, W_IN_ALIGNED, chunk):
            wr_ref[:, lo:lo + chunk] = w_ref[0, :, lo:lo + chunk].astype(BF16)
        tail = W_IN_ALIGNED + GLA_RANK
        for lo in range(0, 2 * C_WIDTH, chunk):
            wr_ref[:, W_IN_ALIGNED + lo:W_IN_ALIGNED + lo + chunk] = (
                w_ref[0, :, tail + lo:tail + lo + chunk].astype(BF16))
        zb_tile = w_ref[0, :, W_IN_ALIGNED:W_IN_ALIGNED + ZB_PAD]
        zb_lane = lax.broadcasted_iota(jnp.int32, zb_tile.shape, 1)
        wr_ref[:, _ROW_OFF["zb"][0]:] = jnp.where(zb_lane < GLA_RANK, zb_tile, 0.0).astype(BF16)

    u = _rmsnorm_rows(x_ref[0], nw_ref[...]).astype(BF16)

    def row(name):
        lo, hi = _ROW_OFF[name]
        return _dot(u, wr_ref[:, lo:hi])

    zb = row("zb")
    zb_hi = zb.astype(BF16)
    zb_lo = (zb - zb_hi.astype(F32)).astype(BF16)
    wa = wa_ref[...]
    wa_hi = wa.astype(BF16)
    wa_lo = (wa - wa_hi.astype(F32)).astype(BF16)
    z = _dot(jnp.concatenate([zb_hi, zb_lo, zb_hi], axis=1),
             jnp.concatenate([wa_hi, wa_hi, wa_lo], axis=0)) + ba_ref[...]
    g_ref[0] = (jnp.minimum(z, 0.0) - jnp.log1p(jnp.exp(-jnp.abs(z)))) * (LOG2E / GLA_TAU)

    qaT_ref[0] = row("qa").T
    vaT = row("va").T.astype(BF16)
    ones = jnp.ones((PV_ROWS - A_HEAD_DIM, vaT.shape[1]), BF16)
    pieces = []
    for h in range(A_HEADS):
        pieces += [vaT[h * A_HEAD_DIM:(h + 1) * A_HEAD_DIM], ones]
    vaT_ref[0] = jnp.concatenate(pieces, axis=0)

    ka = row("ka")
    tm = ka.shape[0]
    nb = MOBA_MAX_BLOCKS
    key_pos = pl.program_id(1) * tm + lax.broadcasted_iota(jnp.int32, (tm, LANES - A_HEAD_DIM), 0)
    blk_id = jnp.right_shift(key_pos, MOBA_BLOCK.bit_length() - 1)
    col = lax.broadcasted_iota(jnp.int32, (tm, LANES - A_HEAD_DIM), 1)
    onehot = jnp.where((col < 2 * nb) & ((col & (nb - 1)) == blk_id), 1.0, 0.0)
    pieces = []
    for h in range(A_HEADS):
        pieces += [ka[:, h * A_HEAD_DIM:(h + 1) * A_HEAD_DIM], onehot]
    ka_ref[0] = jnp.concatenate(pieces, axis=1).astype(BF16)
    ga_ref[0] = _silu(row("ga")).astype(BF16)
    qb_ref[0] = (row("qb") * (B_KEY_DIM ** -0.5)).astype(BF16)
    kb_ref[0] = row("kb").astype(BF16)
    vb_ref[0] = row("vb").astype(BF16)
    gb_ref[0] = _silu(row("gb")).astype(BF16)
    qc_ref[0] = (row("qc") * (C_HEAD_DIM ** -0.5)).astype(BF16)
    gc_ref[0] = _silu(row("gc")).astype(BF16)


def _proj(x, nw, w_in, wa, ba):
    B, S, D = x.shape
    tm = PROJ_TM
    row_spec = lambda width: pl.BlockSpec((1, tm, width), lambda b, i: (b, i, 0))
    col_spec = lambda rows: pl.BlockSpec((1, rows, tm), lambda b, i: (b, 0, i))
    const = lambda shape: pl.BlockSpec(shape, lambda b, i: (0,) * len(shape))
    sds = jax.ShapeDtypeStruct
    return pl.pallas_call(
        _proj_kernel,
        grid=(B, S // tm),
        in_specs=[
            pl.BlockSpec((1, tm, D), lambda b, i: (b, i, 0)),
            const((1, D)),
            pl.BlockSpec(w_in.shape, lambda b, i: (0, 0, 0), pipeline_mode=pl.Buffered(1)),
            const((ZB_PAD, B_KEY_WIDTH)),
            const((1, B_KEY_WIDTH)),
        ],
        out_specs=[col_spec(A_WIDTH), col_spec(A_HEADS * PV_ROWS), row_spec(A_HEADS * LANES),
                   row_spec(A_WIDTH),
                   row_spec(B_KEY_WIDTH), row_spec(B_KEY_WIDTH), row_spec(B_WIDTH),
                   row_spec(B_WIDTH), row_spec(C_WIDTH), row_spec(C_WIDTH),
                   row_spec(B_KEY_WIDTH)],
        out_shape=[sds((B, A_WIDTH, S), F32), sds((B, A_HEADS * PV_ROWS, S), BF16),
                   sds((B, S, A_HEADS * LANES), BF16), sds((B, S, A_WIDTH), BF16),
                   sds((B, S, B_KEY_WIDTH), BF16), sds((B, S, B_KEY_WIDTH), BF16),
                   sds((B, S, B_WIDTH), BF16), sds((B, S, B_WIDTH), BF16),
                   sds((B, S, C_WIDTH), BF16), sds((B, S, C_WIDTH), BF16),
                   sds((B, S, B_KEY_WIDTH), F32)],
        scratch_shapes=[pltpu.VMEM((D, ROW_COLS_TOTAL), BF16)],
        compiler_params=pltpu.CompilerParams(
            dimension_semantics=("arbitrary", "arbitrary"), vmem_limit_bytes=VMEM_LIMIT),
        name="proj",
    )(x, nw, w_in, wa, ba)


def _moba_kernel(rb_ref, qT_ref, k_ref, vT_ref, own_ref, prev_ref, sg_ref, o_ref,
                 kmean_scr, kmbd_scr, qh_scr, m_scr, acc_scr, s_scr):
    i = pl.program_id(1)
    nb = kmean_scr.shape[0]
    blk = MOBA_BLOCK
    dh = A_HEAD_DIM
    n_tiles = A_HEADS // HEADS_PER_TILE

    @pl.when(i == 0)
    def _():
        for n in range(nb):
            kb = jnp.concatenate(
                [k_ref[0, n * blk:(n + 1) * blk, h * LANES:h * LANES + dh] for h in range(A_HEADS)],
                axis=1).astype(F32)
            kmean_scr[n:n + 1, :] = jnp.mean(kb, axis=0, keepdims=True)
        lane_head = jnp.right_shift(lax.broadcasted_iota(jnp.int32, (nb, A_WIDTH), 1),
                                    dh.bit_length() - 1)
        for h in range(A_HEADS):
            km = jnp.where(lane_head == h, kmean_scr[...], 0.0)
            km_hi = km.astype(BF16)
            km_lo = (km - km_hi.astype(F32)).astype(BF16)
            kmbd_scr[h * nb:(h + 1) * nb, :] = jnp.concatenate([km_hi, km_lo, km_hi], axis=1)

    q_f32 = qT_ref[0]
    q_hi = q_f32.astype(BF16)
    q_lo = (q_f32 - q_hi.astype(F32)).astype(BF16)
    gate_all = _dot(kmbd_scr[...], jnp.concatenate([q_hi, q_hi, q_lo], axis=0))
    brow = lax.broadcasted_iota(jnp.int32, (nb, blk), 0)
    browf = brow.astype(F32)
    zeros = jnp.zeros((LANES - dh - 2 * nb, blk), F32)

    for h in range(A_HEADS):
        g = jnp.where(brow < i, gate_all[h * nb:(h + 1) * nb], NEG)
        sel = brow < 0
        for _ in range(MOBA_TOPK):
            m = jnp.max(g, axis=0, keepdims=True)
            idx = jnp.min(jnp.where(g == m, browf, float(nb)), axis=0, keepdims=True)
            pick = browf == idx
            sel = sel | pick
            g = jnp.where(pick, -jnp.inf, g)
        valid = sel & (brow < i)
        far_bias = rb_ref[REL_BUCKETS - 1, h] * LOG2E
        pen = jnp.where(brow < i - 1, jnp.where(valid, far_bias, NEG),
                        jnp.where(brow == i - 1, jnp.where(valid, 0.0, NEG), 0.0))
        pen_hi = pen.astype(BF16).astype(F32)
        q_h = qT_ref[0, h * dh:(h + 1) * dh, :] * (dh ** -0.5 * LOG2E)
        qh_scr[h] = jnp.concatenate([q_h, pen_hi, pen - pen_hi, zeros], axis=0).astype(BF16)

    no_prev = jnp.where(i == 0, NEG, 0.0)

    def logits(j, h, kind):
        rows = pl.ds(pl.multiple_of(j * blk, blk), blk)
        s = _dot(k_ref[0, rows, h * LANES:(h + 1) * LANES], qh_scr[h])
        if kind == "own":
            return s + own_ref[h]
        if kind == "prev":
            return s + (prev_ref[h] + no_prev)
        return s

    def fold(j, h, s, first):
        rows = pl.ds(pl.multiple_of(j * blk, blk), blk)
        bm = jnp.max(s, axis=0, keepdims=True)
        vj = vT_ref[0, h * PV_ROWS:(h + 1) * PV_ROWS, rows]
        if first:
            m_scr[h] = bm
            acc_scr[h] = _dot(vj, jnp.exp2(s - bm).astype(BF16))
        else:
            m_old = m_scr[h]
            m_new = jnp.maximum(m_old, bm)
            m_scr[h] = m_new
            acc_scr[h] = (jnp.exp2(m_old - m_new) * acc_scr[h]
                          + _dot(vj, jnp.exp2(s - m_new).astype(BF16)))

    n_far = i - 1
    j_prev = jnp.maximum(i - 1, 0)
    items = ([(i, "own", h) for h in range(A_HEADS)] + [(j_prev, "prev", h) for h in range(A_HEADS)]
             + [(0, "far", h) for h in range(QK_LOOKAHEAD)])
    n_fold = 2 * A_HEADS
    pending = []

    def issue(n):
        j, kind, h = items[n]
        s = logits(j, h, kind)
        if n < n_fold:
            pending.append(s)
        else:
            s_scr[h] = s

    for n in range(QK_LOOKAHEAD):
        issue(n)
    for n in range(n_fold):
        issue(n + QK_LOOKAHEAD)
        j, kind, h = items[n]
        fold(j, h, pending.pop(0), kind == "own")

    def far_group(j0, n_blocks):
        j_after = jnp.minimum(j0 + n_blocks, n_far - 1)
        stream = ([(j0 + t, h) for t in range(n_blocks) for h in range(A_HEADS)]
                  + [(j_after, h) for h in range(QK_LOOKAHEAD)])
        n_items = n_blocks * A_HEADS
        pending = [s_scr[h] for h in range(QK_LOOKAHEAD)]
        for n in range(n_items):
            j, h = stream[n + QK_LOOKAHEAD]
            if n + QK_LOOKAHEAD < n_items:
                pending.append(logits(j, h, "far"))
            else:
                s_scr[h] = logits(j, h, "far")
            j, h = stream[n]
            fold(j, h, pending.pop(0), False)

    def far_groups(g, carry):
        far_group(g * FAR_UNROLL, FAR_UNROLL)
        return carry

    n_groups = jnp.maximum(n_far, 0) // FAR_UNROLL
    lax.fori_loop(0, n_groups, far_groups, 0)
    for rest in range(1, FAR_UNROLL):
        @pl.when(n_far - n_groups * FAR_UNROLL == rest)
        def _():
            far_group(n_groups * FAR_UNROLL, rest)

    for t in range(n_tiles):
        oT = []
        for hh in range(HEADS_PER_TILE):
            acc = acc_scr[t * HEADS_PER_TILE + hh]
            oT.append(acc[:dh] / acc[dh:dh + 1])
        cols = slice(t * LANES, (t + 1) * LANES)
        o_ref[0, :, cols] = (jnp.concatenate(oT, axis=0).T * sg_ref[0, :, cols]).astype(BF16)


def _moba(rel_bias, qaT, ka, vaT, bias_own, bias_prev, sga):
    B, S, _ = ka.shape
    nb = S // MOBA_BLOCK
    assert nb == MOBA_MAX_BLOCKS
    bias_spec = pl.BlockSpec((A_HEADS, MOBA_BLOCK, MOBA_BLOCK), lambda b, i: (0, 0, 0))
    return pl.pallas_call(
        _moba_kernel,
        grid=(B, nb),
        in_specs=[
            pl.BlockSpec(memory_space=pltpu.SMEM),
            pl.BlockSpec((1, A_WIDTH, MOBA_BLOCK), lambda b, i: (b, 0, i)),
            pl.BlockSpec((1, S, A_HEADS * LANES), lambda b, i: (b, 0, 0)),
            pl.BlockSpec((1, A_HEADS * PV_ROWS, S), lambda b, i: (b, 0, 0)),
            bias_spec, bias_spec,
            pl.BlockSpec((1, MOBA_BLOCK, A_WIDTH), lambda b, i: (b, i, 0)),
        ],
        out_specs=pl.BlockSpec((1, MOBA_BLOCK, A_WIDTH), lambda b, i: (b, i, 0)),
        out_shape=jax.ShapeDtypeStruct((B, S, A_WIDTH), BF16),
        scratch_shapes=[pltpu.VMEM((nb, A_WIDTH), F32),
                        pltpu.VMEM((A_HEADS * nb, 3 * A_WIDTH), BF16),
                        pltpu.VMEM((A_HEADS, LANES, MOBA_BLOCK), BF16),
                        pltpu.VMEM((A_HEADS, 1, MOBA_BLOCK), F32),
                        pltpu.VMEM((A_HEADS, PV_ROWS, MOBA_BLOCK), F32),
                        pltpu.VMEM((QK_LOOKAHEAD, MOBA_BLOCK, MOBA_BLOCK), F32)],
        compiler_params=pltpu.CompilerParams(
            dimension_semantics=("arbitrary", "arbitrary"),
            vmem_limit_bytes=VMEM_LIMIT),
        name="moba",
    )(rel_bias, qaT, ka, vaT, bias_own, bias_prev, sga)


def _gla_kernel(q_ref, k_ref, v_ref, g_ref, sg_ref, nw_ref, o_ref, state_scr, att_scr):
    c = pl.program_id(1)
    C = GLA_CHUNK
    dk, dv, H = B_KEY_DIM, B_VAL_DIM, B_HEADS
    n_levels = C.bit_length() - 1

    @pl.when(c == 0)
    def _():
        state_scr[...] = jnp.zeros_like(state_scr)

    q = q_ref[0].astype(F32)
    k = k_ref[0].astype(F32)
    v = v_ref[0]
    g = g_ref[0]

    row = lax.broadcasted_iota(jnp.int32, (C, H * dk), 0)

    b = g
    sh = 1
    while sh < C:
        if sh % SUBLANES == 0:
            b = b + jnp.concatenate([jnp.zeros((sh, H * dk), F32), b[:C - sh]], axis=0)
        else:
            b = b + jnp.where(row >= sh, pltpu.roll(b, sh, 0), 0.0)
        sh *= 2

    tt = lax.broadcasted_iota(jnp.int32, (C, C), 0)
    ss = lax.broadcasted_iota(jnp.int32, (C, C), 1)
    txs = tt ^ ss
    level = jnp.full((C, C), -1, jnp.int32)
    for p in range(n_levels):
        level = level + (txs >= (1 << p)).astype(jnp.int32)
    level = jnp.where(ss <= tt, level, -2)

    lane = lax.broadcasted_iota(jnp.int32, (C, H * dk), 1)
    head_masks = [(lane >= h * dk) & (lane < (h + 1) * dk) for h in range(H)]

    def scores(qq, kk, h):
        lo = (h * dk // LANES) * LANES
        kh = jnp.where(head_masks[h], kk, 0.0).astype(BF16)[:, lo:lo + LANES]
        return _dot_nt(qq.astype(BF16)[:, lo:lo + LANES], kh)

    on_diag = level == -1
    for h in range(H):
        att_scr[h] = jnp.where(on_diag, scores(q, k, h), 0.0)

    block_end = b
    for p in range(n_levels):
        m = 1 << p
        second_half = (row & m) != 0
        r = jnp.where(second_half, pltpu.roll(block_end, m, 0), block_end)
        decay = jnp.exp2(-jnp.abs(b - r))
        qd = q * decay
        kd = k * decay
        at_level = level == p
        for h in range(H):
            att_scr[h] = jnp.where(at_level, scores(qd, kd, h), att_scr[h])
        if p + 1 < n_levels:
            block_end = jnp.where(second_half, block_end, pltpu.roll(block_end, C - m, 0))

    state = state_scr[...]
    o_inter = _dot((q * jnp.exp2(b)).astype(BF16), state.astype(BF16))

    nw = nw_ref[...]
    for h in range(H):
        o_h = o_inter[:, h * dv:(h + 1) * dv] + _dot(att_scr[h].astype(BF16),
                                                      v[:, h * dv:(h + 1) * dv])
        cols = slice(h * dv, (h + 1) * dv)
        o_ref[0, :, cols] = (_rmsnorm_rows(o_h, nw) * sg_ref[0, :, cols]).astype(BF16)

    bT = b.T
    kT = k.T
    b_last = bT[:, C - 1:C]
    kdT = (kT * jnp.exp2(b_last - bT)).astype(BF16)
    ds = _dot(kdT, v)
    srow = lax.broadcasted_iota(jnp.int32, (H * dk, H * dv), 0) // dk
    scol = lax.broadcasted_iota(jnp.int32, (H * dk, H * dv), 1) // dv
    state_scr[...] = jnp.exp2(b_last) * state + jnp.where(srow == scol, ds, 0.0)


def _gla(qb, kb, vb, g, sgb, nw):
    B, S, _ = qb.shape
    C = GLA_CHUNK
    spec = lambda width: pl.BlockSpec((1, C, width), lambda b, c: (b, c, 0))
    return pl.pallas_call(
        _gla_kernel,
        grid=(B, S // C),
        in_specs=[spec(B_KEY_WIDTH), spec(B_KEY_WIDTH), spec(B_WIDTH), spec(B_KEY_WIDTH),
                  spec(B_WIDTH), pl.BlockSpec((1, B_VAL_DIM), lambda b, c: (0, 0))],
        out_specs=spec(B_WIDTH),
        out_shape=jax.ShapeDtypeStruct((B, S, B_WIDTH), BF16),
        scratch_shapes=[pltpu.VMEM((B_KEY_WIDTH, B_WIDTH), F32),
                        pltpu.VMEM((B_HEADS, C, C), F32)],
        compiler_params=pltpu.CompilerParams(
            dimension_semantics=("arbitrary", "arbitrary"), vmem_limit_bytes=VMEM_LIMIT),
        name="gla",
    )(qb, kb, vb, g, sgb, nw)


def _out_kernel(x_ref, ma_ref, mb_ref, qc_ref, sgc_ref, km_ref, vm_ref, w_ref, fw_ref, o_ref,
                wo_ref):
    @pl.when((pl.program_id(0) == 0) & (pl.program_id(1) == 0))
    def _():
        for lo in range(0, MIX_WIDTH, 4 * LANES):
            wo_ref[lo:lo + 4 * LANES, :] = w_ref[0, lo:lo + 4 * LANES, :].astype(BF16)

    dh = C_HEAD_DIM
    qc = qc_ref[0]
    km = km_ref[0]
    vm = vm_ref[0]
    heads = [slice(h * dh, (h + 1) * dh) for h in range(C_HEADS)]
    scores = [_dot_nt(qc[:, sl], km[:, sl]) for sl in heads]
    h_ab = (_dot(ma_ref[0], wo_ref[0:A_WIDTH, :])
            + _dot(mb_ref[0], wo_ref[A_WIDTH:A_WIDTH + B_WIDTH, :]))
    oc = []
    for s, sl in zip(scores, heads):
        p = jnp.exp(s - jnp.max(s, axis=-1, keepdims=True))
        l = jnp.sum(p, axis=-1, keepdims=True)
        oc.append(_dot(p.astype(BF16), vm[:, sl]) / l)
    mc = (jnp.concatenate(oc, axis=-1) * sgc_ref[0]).astype(BF16)
    h_new = x_ref[0] + h_ab + _dot(mc, wo_ref[A_WIDTH + B_WIDTH:, :])
    o_ref[0] = _rmsnorm_rows(h_new, fw_ref[...])


def _out(x, ma, mb, qc, sgc, km, vm, w_out, fw):
    B, S, D = x.shape
    tm = OUT_TM
    row = lambda width: pl.BlockSpec((1, tm, width), lambda b, i: (b, i, 0))
    mem = pl.BlockSpec((1, MEM_LEN, C_WIDTH), lambda b, i: (b, 0, 0))
    return pl.pallas_call(
        _out_kernel,
        grid=(B, S // tm),
        in_specs=[row(D), row(A_WIDTH), row(B_WIDTH), row(C_WIDTH), row(C_WIDTH), mem, mem,
                  pl.BlockSpec(w_out.shape, lambda b, i: (0, 0, 0), pipeline_mode=pl.Buffered(1)),
                  pl.BlockSpec((1, D), lambda b, i: (0, 0))],
        out_specs=row(D),
        out_shape=jax.ShapeDtypeStruct((B, S, D), F32),
        scratch_shapes=[pltpu.VMEM((MIX_WIDTH, D), BF16)],
        compiler_params=pltpu.CompilerParams(
            dimension_semantics=("arbitrary", "arbitrary"), vmem_limit_bytes=VMEM_LIMIT),
        name="out",
    )(x, ma, mb, qc, sgc, km, vm, w_out, fw)


def kernel(x, mem, norm_w, w_in, w_alpha2, b_alpha, gla_norm_w, mem_norm_w, w_mem_kv, w_out,
           rel_bias, final_norm_w):
    assert norm_w.shape[0] == 1, "single layer"
    assert w_in.shape[2] == W_IN_ALIGNED + GLA_RANK + 2 * C_WIDTH
    wa = jnp.pad(w_alpha2[0], ((0, ZB_PAD - GLA_RANK), (0, 0)))

    km, vm = _memkv(mem, mem_norm_w[0][None, :], w_mem_kv)
    bias_own, bias_prev = _bias_tiles(rel_bias)
    (qaT, vaT, ka, sga, qb, kb, vb, sgb, qc, sgc, g) = _proj(
        x, norm_w[0][None, :], w_in, wa, b_alpha[0][None, :])
    ma = _moba(rel_bias, qaT, ka, vaT, bias_own, bias_prev, sga)
    mb = _gla(qb, kb, vb, g, sgb, gla_norm_w[0][None, :])
    return _out(x, ma, mb, qc, sgc, km, vm, w_out, final_norm_w[None, :])
```

```python
import functools
import math

import jax
import jax.numpy as jnp
from jax import lax
from jax.experimental import pallas as pl
from jax.experimental.pallas import tpu as pltpu

F32 = jnp.float32
BF16 = jnp.bfloat16

D_MODEL = 1024
MEM_LEN = 256
A_HEADS = 8
A_HEAD_DIM = 64
A_WIDTH = 512
MOBA_BLOCK = 256
MOBA_TOPK = 3
B_HEADS = 4
B_KEY_DIM = 64
B_VAL_DIM = 128
B_KEY_WIDTH = 256
B_WIDTH = 512
GLA_RANK = 16
GLA_TAU = 16.0
C_HEADS = 4
C_HEAD_DIM = 128
C_WIDTH = 512
MIX_WIDTH = A_WIDTH + B_WIDTH + C_WIDTH
REL_BUCKETS = 32
REL_MAX_DIST = 128
RMS_EPS = 1e-6
NEG = -1e30

LANES = 128
SUBLANES = 8
ZB_PAD = LANES
PROJ_TM = 512
OUT_TM = 512
GLA_CHUNK = 256
VMEM_LIMIT = 56 * 1024 * 1024
LOG2E = 1.0 / math.log(2.0)
HEADS_PER_TILE = LANES // A_HEAD_DIM
PV_ROWS = A_HEAD_DIM + 16
MOBA_MAX_BLOCKS = 16
QK_LOOKAHEAD = 4
FAR_UNROLL = 4


def _dot(a, b):
    return jnp.dot(a, b, preferred_element_type=F32)


def _dot_nt(a, b):
    return lax.dot_general(a, b, (((1,), (1,)), ((), ())), preferred_element_type=F32)


def _rmsnorm_rows(x, w):
    return x * lax.rsqrt(jnp.mean(x * x, axis=-1, keepdims=True) + RMS_EPS) * w


def _silu(x):
    return x * jax.nn.sigmoid(x)


def _memkv_kernel(mem_ref, nw_ref, w_ref, km_ref, vm_ref):
    u = _rmsnorm_rows(mem_ref[0], nw_ref[...]).astype(BF16)
    km_ref[0] = _dot(u, w_ref[0, :, :C_WIDTH].astype(BF16)).astype(BF16)
    vm_ref[0] = _dot(u, w_ref[0, :, C_WIDTH:].astype(BF16)).astype(BF16)


def _memkv(mem, nw, w_mem_kv):
    B = mem.shape[0]
    return pl.pallas_call(
        _memkv_kernel,
        grid=(B,),
        in_specs=[
            pl.BlockSpec((1, MEM_LEN, D_MODEL), lambda b: (b, 0, 0)),
            pl.BlockSpec((1, D_MODEL), lambda b: (0, 0)),
            pl.BlockSpec(w_mem_kv.shape, lambda b: (0, 0, 0)),
        ],
        out_specs=[
            pl.BlockSpec((1, MEM_LEN, C_WIDTH), lambda b: (b, 0, 0)),
            pl.BlockSpec((1, MEM_LEN, C_WIDTH), lambda b: (b, 0, 0)),
        ],
        out_shape=[jax.ShapeDtypeStruct((B, MEM_LEN, C_WIDTH), BF16)] * 2,
        name="memkv",
    )(mem, nw, w_mem_kv)


def _t5_bucket(n):
    max_exact = REL_BUCKETS // 2
    nf = jnp.maximum(n, max_exact).astype(F32)
    large = max_exact + jnp.floor(jnp.log(nf / max_exact) / math.log(REL_MAX_DIST / max_exact)
                                  * (REL_BUCKETS - max_exact)).astype(jnp.int32)
    large = jnp.minimum(large, REL_BUCKETS - 1)
    return jnp.where(n < max_exact, n, large)


def _bias_kernel(rb_ref, own_ref, prev_ref):
    h = pl.program_id(0)
    key = lax.broadcasted_iota(jnp.int32, (MOBA_BLOCK, MOBA_BLOCK), 0)
    qry = lax.broadcasted_iota(jnp.int32, (MOBA_BLOCK, MOBA_BLOCK), 1)
    rel = qry - key
    b_own = _t5_bucket(jnp.maximum(rel, 0))
    b_prev = _t5_bucket(rel + MOBA_BLOCK)
    own = jnp.zeros((MOBA_BLOCK, MOBA_BLOCK), F32)
    prev = jnp.zeros((MOBA_BLOCK, MOBA_BLOCK), F32)
    for bk in range(REL_BUCKETS):
        val = rb_ref[bk, h] * LOG2E
        own = jnp.where(b_own == bk, val, own)
        prev = jnp.where(b_prev == bk, val, prev)
    own_ref[0] = jnp.where(rel >= 0, own, NEG)
    prev_ref[0] = prev


def _bias_tiles(rel_bias):
    return pl.pallas_call(
        _bias_kernel,
        grid=(A_HEADS,),
        in_specs=[pl.BlockSpec(memory_space=pltpu.SMEM)],
        out_specs=[pl.BlockSpec((1, MOBA_BLOCK, MOBA_BLOCK), lambda h: (h, 0, 0))] * 2,
        out_shape=[jax.ShapeDtypeStruct((A_HEADS, MOBA_BLOCK, MOBA_BLOCK), F32)] * 2,
        name="t5bias",
    )(rel_bias)


_ROW_COLS = (("qa", A_WIDTH), ("ka", A_WIDTH), ("va", A_WIDTH), ("ga", A_WIDTH),
             ("qb", B_KEY_WIDTH), ("kb", B_KEY_WIDTH), ("vb", B_WIDTH), ("gb", B_WIDTH),
             ("qc", C_WIDTH), ("gc", C_WIDTH), ("zb", ZB_PAD))
_ROW_OFF = {}
_off = 0
for _name, _width in _ROW_COLS:
    _ROW_OFF[_name] = (_off, _off + _width)
    _off += _width
ROW_COLS_TOTAL = _off


W_IN_ALIGNED = 4 * A_WIDTH + 2 * B_KEY_WIDTH + 2 * B_WIDTH


def _proj_kernel(x_ref, nw_ref, w_ref, wa_ref, ba_ref,
                 qaT_ref, vaT_ref, ka_ref, ga_ref, qb_ref, kb_ref, vb_ref, gb_ref,
                 qc_ref, gc_ref, g_ref, wr_ref):
    @pl.when((pl.program_id(0) == 0) & (pl.program_id(1) == 0))
    def _():
        chunk = 4 * LANES
        for lo in range(0, W_IN_ALIGNED, chunk):
            wr_ref[:, lo:lo + chunk] = w_ref[:, lo:lo + chunk].astype(BF16)
        tail = W_IN_ALIGNED + GLA_RANK
        for lo in range(0, 2 * C_WIDTH, chunk):
            wr_ref[:, W_IN_ALIGNED + lo:W_IN_ALIGNED + lo + chunk] = (
                w_ref[:, tail + lo:tail + lo + chunk].astype(BF16))
        zb_tile = w_ref[:, W_IN_ALIGNED:W_IN_ALIGNED + ZB_PAD]
        zb_lane = lax.broadcasted_iota(jnp.int32, zb_tile.shape, 1)
        wr_ref[:, _ROW_OFF["zb"][0]:] = jnp.where(zb_lane < GLA_RANK, zb_tile, 0.0).astype(BF16)

    u = _rmsnorm_rows(x_ref[0], nw_ref[...]).astype(BF16)

    def row(name):
        lo, hi = _ROW_OFF[name]
        return _dot(u, wr_ref[:, lo:hi])

    zb = row("zb")
    zb_hi = zb.astype(BF16)
    zb_lo = (zb - zb_hi.astype(F32)).astype(BF16)
    wa = wa_ref[...]
    wa_hi = wa.astype(BF16)
    wa_lo = (wa - wa_hi.astype(F32)).astype(BF16)
    z = _dot(jnp.concatenate([zb_hi, zb_lo, zb_hi], axis=1),
             jnp.concatenate([wa_hi, wa_hi, wa_lo], axis=0)) + ba_ref[...]
    g_ref[0] = (jnp.minimum(z, 0.0) - jnp.log1p(jnp.exp(-jnp.abs(z)))) * (LOG2E / GLA_TAU)

    qaT_ref[0] = row("qa").T
    vaT = row("va").T.astype(BF16)
    ones = jnp.ones((PV_ROWS - A_HEAD_DIM, vaT.shape[1]), BF16)
    pieces = []
    for h in range(A_HEADS):
        pieces += [vaT[h * A_HEAD_DIM:(h + 1) * A_HEAD_DIM], ones]
    vaT_ref[0] = jnp.concatenate(pieces, axis=0)

    ka = row("ka")
    tm = ka.shape[0]
    nb = MOBA_MAX_BLOCKS
    key_pos = pl.program_id(1) * tm + lax.broadcasted_iota(jnp.int32, (tm, LANES - A_HEAD_DIM), 0)
    blk_id = jnp.right_shift(key_pos, MOBA_BLOCK.bit_length() - 1)
    col = lax.broadcasted_iota(jnp.int32, (tm, LANES - A_HEAD_DIM), 1)
    onehot = jnp.where((col < 2 * nb) & ((col & (nb - 1)) == blk_id), 1.0, 0.0)
    pieces = []
    for h in range(A_HEADS):
        pieces += [ka[:, h * A_HEAD_DIM:(h + 1) * A_HEAD_DIM], onehot]
    ka_ref[0] = jnp.concatenate(pieces, axis=1).astype(BF16)
    ga_ref[0] = _silu(row("ga")).astype(BF16)
    qb_ref[0] = (row("qb") * (B_KEY_DIM ** -0.5)).astype(BF16)
    kb_ref[0] = row("kb").astype(BF16)
    vb_ref[0] = row("vb").astype(BF16)
    gb_ref[0] = _silu(row("gb")).astype(BF16)
    qc_ref[0] = (row("qc") * (C_HEAD_DIM ** -0.5)).astype(BF16)
    gc_ref[0] = _silu(row("gc")).astype(BF16)


def _proj(x, nw, w_in, wa, ba):
    B, S, D = x.shape
    tm = PROJ_TM
    row_spec = lambda width: pl.BlockSpec((1, tm, width), lambda b, i: (b, i, 0))
    col_spec = lambda rows: pl.BlockSpec((1, rows, tm), lambda b, i: (b, 0, i))
    const = lambda shape: pl.BlockSpec(shape, lambda b, i: (0,) * len(shape))
    sds = jax.ShapeDtypeStruct
    return pl.pallas_call(
        _proj_kernel,
        grid=(B, S // tm),
        in_specs=[
            pl.BlockSpec((1, tm, D), lambda b, i: (b, i, 0)),
            const((1, D)),
            pl.BlockSpec(w_in.shape, lambda b, i: (0, 0), pipeline_mode=pl.Buffered(1)),
            const((ZB_PAD, B_KEY_WIDTH)),
            const((1, B_KEY_WIDTH)),
        ],
        out_specs=[col_spec(A_WIDTH), col_spec(A_HEADS * PV_ROWS), row_spec(A_HEADS * LANES),
                   row_spec(A_WIDTH),
                   row_spec(B_KEY_WIDTH), row_spec(B_KEY_WIDTH), row_spec(B_WIDTH),
                   row_spec(B_WIDTH), row_spec(C_WIDTH), row_spec(C_WIDTH),
                   row_spec(B_KEY_WIDTH)],
        out_shape=[sds((B, A_WIDTH, S), F32), sds((B, A_HEADS * PV_ROWS, S), BF16),
                   sds((B, S, A_HEADS * LANES), BF16), sds((B, S, A_WIDTH), BF16),
                   sds((B, S, B_KEY_WIDTH), BF16), sds((B, S, B_KEY_WIDTH), BF16),
                   sds((B, S, B_WIDTH), BF16), sds((B, S, B_WIDTH), BF16),
                   sds((B, S, C_WIDTH), BF16), sds((B, S, C_WIDTH), BF16),
                   sds((B, S, B_KEY_WIDTH), F32)],
        scratch_shapes=[pltpu.VMEM((D, ROW_COLS_TOTAL), BF16)],
        compiler_params=pltpu.CompilerParams(
            dimension_semantics=("arbitrary", "arbitrary"), vmem_limit_bytes=VMEM_LIMIT),
        name="proj",
    )(x, nw, w_in, wa, ba)


def _moba_kernel(rb_ref, qT_ref, k_ref, vT_ref, own_ref, prev_ref, sg_ref, o_ref,
                 kmean_scr, kmbd_scr, qh_scr, m_scr, acc_scr, s_scr):
    i = pl.program_id(1)
    nb = kmean_scr.shape[0]
    blk = MOBA_BLOCK
    dh = A_HEAD_DIM
    n_tiles = A_HEADS // HEADS_PER_TILE

    @pl.when(i == 0)
    def _():
        for n in range(nb):
            kb = jnp.concatenate(
                [k_ref[0, n * blk:(n + 1) * blk, h * LANES:h * LANES + dh] for h in range(A_HEADS)],
                axis=1).astype(F32)
            kmean_scr[n:n + 1, :] = jnp.mean(kb, axis=0, keepdims=True)
        lane_head = jnp.right_shift(lax.broadcasted_iota(jnp.int32, (nb, A_WIDTH), 1),
                                    dh.bit_length() - 1)
        for h in range(A_HEADS):
            km = jnp.where(lane_head == h, kmean_scr[...], 0.0)
            km_hi = km.astype(BF16)
            km_lo = (km - km_hi.astype(F32)).astype(BF16)
            kmbd_scr[h * nb:(h + 1) * nb, :] = jnp.concatenate([km_hi, km_lo, km_hi], axis=1)

    q_f32 = qT_ref[0]
    q_hi = q_f32.astype(BF16)
    q_lo = (q_f32 - q_hi.astype(F32)).astype(BF16)
    gate_all = _dot(kmbd_scr[...], jnp.concatenate([q_hi, q_hi, q_lo], axis=0))
    brow = lax.broadcasted_iota(jnp.int32, (nb, blk), 0)
    browf = brow.astype(F32)
    zeros = jnp.zeros((LANES - dh - 2 * nb, blk), F32)

    for h in range(A_HEADS):
        g = jnp.where(brow < i, gate_all[h * nb:(h + 1) * nb], NEG)
        sel = brow < 0
        for _ in range(MOBA_TOPK):
            m = jnp.max(g, axis=0, keepdims=True)
            idx = jnp.min(jnp.where(g == m, browf, float(nb)), axis=0, keepdims=True)
            pick = browf == idx
            sel = sel | pick
            g = jnp.where(pick, -jnp.inf, g)
        valid = sel & (brow < i)
        far_bias = rb_ref[REL_BUCKETS - 1, h] * LOG2E
        pen = jnp.where(brow < i - 1, jnp.where(valid, far_bias, NEG),
                        jnp.where(brow == i - 1, jnp.where(valid, 0.0, NEG), 0.0))
        pen_hi = pen.astype(BF16).astype(F32)
        q_h = qT_ref[0, h * dh:(h + 1) * dh, :] * (dh ** -0.5 * LOG2E)
        qh_scr[h] = jnp.concatenate([q_h, pen_hi, pen - pen_hi, zeros], axis=0).astype(BF16)

    no_prev = jnp.where(i == 0, NEG, 0.0)

    def logits(j, h, kind):
        rows = pl.ds(pl.multiple_of(j * blk, blk), blk)
        s = _dot(k_ref[0, rows, h * LANES:(h + 1) * LANES], qh_scr[h])
        if kind == "own":
            return s + own_ref[h]
        if kind == "prev":
            return s + (prev_ref[h] + no_prev)
        return s

    def fold(j, h, s, first):
        rows = pl.ds(pl.multiple_of(j * blk, blk), blk)
        bm = jnp.max(s, axis=0, keepdims=True)
        vj = vT_ref[0, h * PV_ROWS:(h + 1) * PV_ROWS, rows]
        if first:
            m_scr[h] = bm
            acc_scr[h] = _dot(vj, jnp.exp2(s - bm).astype(BF16))
        else:
            m_old = m_scr[h]
            m_new = jnp.maximum(m_old, bm)
            m_scr[h] = m_new
            acc_scr[h] = (jnp.exp2(m_old - m_new) * acc_scr[h]
                          + _dot(vj, jnp.exp2(s - m_new).astype(BF16)))

    n_far = i - 1
    j_prev = jnp.maximum(i - 1, 0)
    items = ([(i, "own", h) for h in range(A_HEADS)] + [(j_prev, "prev", h) for h in range(A_HEADS)]
             + [(0, "far", h) for h in range(QK_LOOKAHEAD)])
    n_fold = 2 * A_HEADS
    pending = []

    def issue(n):
        j, kind, h = items[n]
        s = logits(j, h, kind)
        if n < n_fold:
            pending.append(s)
        else:
            s_scr[h] = s

    for n in range(QK_LOOKAHEAD):
        issue(n)
    for n in range(n_fold):
        issue(n + QK_LOOKAHEAD)
        j, kind, h = items[n]
        fold(j, h, pending.pop(0), kind == "own")

    def far_group(j0, n_blocks):
        j_after = jnp.minimum(j0 + n_blocks, n_far - 1)
        stream = ([(j0 + t, h) for t in range(n_blocks) for h in range(A_HEADS)]
                  + [(j_after, h) for h in range(QK_LOOKAHEAD)])
        n_items = n_blocks * A_HEADS
        pending = [s_scr[h] for h in range(QK_LOOKAHEAD)]
        for n in range(n_items):
            j, h = stream[n + QK_LOOKAHEAD]
            if n + QK_LOOKAHEAD < n_items:
                pending.append(logits(j, h, "far"))
            else:
                s_scr[h] = logits(j, h, "far")
            j, h = stream[n]
            fold(j, h, pending.pop(0), False)

    def far_groups(g, carry):
        far_group(g * FAR_UNROLL, FAR_UNROLL)
        return carry

    n_groups = jnp.maximum(n_far, 0) // FAR_UNROLL
    lax.fori_loop(0, n_groups, far_groups, 0)
    for rest in range(1, FAR_UNROLL):
        @pl.when(n_far - n_groups * FAR_UNROLL == rest)
        def _():
            far_group(n_groups * FAR_UNROLL, rest)

    for t in range(n_tiles):
        oT = []
        for hh in range(HEADS_PER_TILE):
            acc = acc_scr[t * HEADS_PER_TILE + hh]
            oT.append(acc[:dh] / acc[dh:dh + 1])
        cols = slice(t * LANES, (t + 1) * LANES)
        o_ref[0, :, cols] = (jnp.concatenate(oT, axis=0).T * sg_ref[0, :, cols]).astype(BF16)


def _moba(rel_bias, qaT, ka, vaT, bias_own, bias_prev, sga):
    B, S, _ = ka.shape
    nb = S // MOBA_BLOCK
    assert nb == MOBA_MAX_BLOCKS
    bias_spec = pl.BlockSpec((A_HEADS, MOBA_BLOCK, MOBA_BLOCK), lambda b, i: (0, 0, 0))
    return pl.pallas_call(
        _moba_kernel,
        grid=(B, nb),
        in_specs=[
            pl.BlockSpec(memory_space=pltpu.SMEM),
            pl.BlockSpec((1, A_WIDTH, MOBA_BLOCK), lambda b, i: (b, 0, i)),
            pl.BlockSpec((1, S, A_HEADS * LANES), lambda b, i: (b, 0, 0)),
            pl.BlockSpec((1, A_HEADS * PV_ROWS, S), lambda b, i: (b, 0, 0)),
            bias_spec, bias_spec,
            pl.BlockSpec((1, MOBA_BLOCK, A_WIDTH), lambda b, i: (b, i, 0)),
        ],
        out_specs=pl.BlockSpec((1, MOBA_BLOCK, A_WIDTH), lambda b, i: (b, i, 0)),
        out_shape=jax.ShapeDtypeStruct((B, S, A_WIDTH), BF16),
        scratch_shapes=[pltpu.VMEM((nb, A_WIDTH), F32),
                        pltpu.VMEM((A_HEADS * nb, 3 * A_WIDTH), BF16),
                        pltpu.VMEM((A_HEADS, LANES, MOBA_BLOCK), BF16),
                        pltpu.VMEM((A_HEADS, 1, MOBA_BLOCK), F32),
                        pltpu.VMEM((A_HEADS, PV_ROWS, MOBA_BLOCK), F32),
                        pltpu.VMEM((QK_LOOKAHEAD, MOBA_BLOCK, MOBA_BLOCK), F32)],
        compiler_params=pltpu.CompilerParams(
            dimension_semantics=("arbitrary", "arbitrary"),
            vmem_limit_bytes=VMEM_LIMIT),
        name="moba",
    )(rel_bias, qaT, ka, vaT, bias_own, bias_prev, sga)


def _gla_kernel(q_ref, k_ref, v_ref, g_ref, sg_ref, nw_ref, o_ref, state_scr, att_scr):
    c = pl.program_id(1)
    C = GLA_CHUNK
    dk, dv, H = B_KEY_DIM, B_VAL_DIM, B_HEADS
    n_levels = C.bit_length() - 1

    @pl.when(c == 0)
    def _():
        state_scr[...] = jnp.zeros_like(state_scr)

    q = q_ref[0].astype(F32)
    k = k_ref[0].astype(F32)
    v = v_ref[0]
    g = g_ref[0]

    row = lax.broadcasted_iota(jnp.int32, (C, H * dk), 0)

    b = g
    sh = 1
    while sh < C:
        if sh % SUBLANES == 0:
            b = b + jnp.concatenate([jnp.zeros((sh, H * dk), F32), b[:C - sh]], axis=0)
        else:
            b = b + jnp.where(row >= sh, pltpu.roll(b, sh, 0), 0.0)
        sh *= 2

    tt = lax.broadcasted_iota(jnp.int32, (C, C), 0)
    ss = lax.broadcasted_iota(jnp.int32, (C, C), 1)
    txs = tt ^ ss
    level = jnp.full((C, C), -1, jnp.int32)
    for p in range(n_levels):
        level = level + (txs >= (1 << p)).astype(jnp.int32)
    level = jnp.where(ss <= tt, level, -2)

    lane = lax.broadcasted_iota(jnp.int32, (C, H * dk), 1)
    head_masks = [(lane >= h * dk) & (lane < (h + 1) * dk) for h in range(H)]

    def scores(qq, kk, h):
        lo = (h * dk // LANES) * LANES
        kh = jnp.where(head_masks[h], kk, 0.0).astype(BF16)[:, lo:lo + LANES]
        return _dot_nt(qq.astype(BF16)[:, lo:lo + LANES], kh)

    on_diag = level == -1
    for h in range(H):
        att_scr[h] = jnp.where(on_diag, scores(q, k, h), 0.0)

    block_end = b
    for p in range(n_levels):
        m = 1 << p
        second_half = (row & m) != 0
        r = jnp.where(second_half, pltpu.roll(block_end, m, 0), block_end)
        decay = jnp.exp2(-jnp.abs(b - r))
        qd = q * decay
        kd = k * decay
        at_level = level == p
        for h in range(H):
            att_scr[h] = jnp.where(at_level, scores(qd, kd, h), att_scr[h])
        if p + 1 < n_levels:
            block_end = jnp.where(second_half, block_end, pltpu.roll(block_end, C - m, 0))

    state = state_scr[...]
    o_inter = _dot((q * jnp.exp2(b)).astype(BF16), state.astype(BF16))

    nw = nw_ref[...]
    for h in range(H):
        o_h = o_inter[:, h * dv:(h + 1) * dv] + _dot(att_scr[h].astype(BF16),
                                                      v[:, h * dv:(h + 1) * dv])
        cols = slice(h * dv, (h + 1) * dv)
        o_ref[0, :, cols] = (_rmsnorm_rows(o_h, nw) * sg_ref[0, :, cols]).astype(BF16)

    bT = b.T
    kT = k.T
    b_last = bT[:, C - 1:C]
    kdT = (kT * jnp.exp2(b_last - bT)).astype(BF16)
    ds = _dot(kdT, v)
    srow = lax.broadcasted_iota(jnp.int32, (H * dk, H * dv), 0) // dk
    scol = lax.broadcasted_iota(jnp.int32, (H * dk, H * dv), 1) // dv
    state_scr[...] = jnp.exp2(b_last) * state + jnp.where(srow == scol, ds, 0.0)


def _gla(qb, kb, vb, g, sgb, nw):
    B, S, _ = qb.shape
    C = GLA_CHUNK
    spec = lambda width: pl.BlockSpec((1, C, width), lambda b, c: (b, c, 0))
    return pl.pallas_call(
        _gla_kernel,
        grid=(B, S // C),
        in_specs=[spec(B_KEY_WIDTH), spec(B_KEY_WIDTH), spec(B_WIDTH), spec(B_KEY_WIDTH),
                  spec(B_WIDTH), pl.BlockSpec((1, B_VAL_DIM), lambda b, c: (0, 0))],
        out_specs=spec(B_WIDTH),
        out_shape=jax.ShapeDtypeStruct((B, S, B_WIDTH), BF16),
        scratch_shapes=[pltpu.VMEM((B_KEY_WIDTH, B_WIDTH), F32),
                        pltpu.VMEM((B_HEADS, C, C), F32)],
        compiler_params=pltpu.CompilerParams(
            dimension_semantics=("arbitrary", "arbitrary"), vmem_limit_bytes=VMEM_LIMIT),
        name="gla",
    )(qb, kb, vb, g, sgb, nw)


def _out_kernel(x_ref, ma_ref, mb_ref, qc_ref, sgc_ref, km_ref, vm_ref, w_ref, fw_ref, o_ref,
                wo_ref):
    @pl.when((pl.program_id(0) == 0) & (pl.program_id(1) == 0))
    def _():
        for lo in range(0, MIX_WIDTH, 4 * LANES):
            wo_ref[lo:lo + 4 * LANES, :] = w_ref[0, lo:lo + 4 * LANES, :].astype(BF16)

    dh = C_HEAD_DIM
    qc = qc_ref[0]
    km = km_ref[0]
    vm = vm_ref[0]
    heads = [slice(h * dh, (h + 1) * dh) for h in range(C_HEADS)]
    scores = [_dot_nt(qc[:, sl], km[:, sl]) for sl in heads]
    h_ab = (_dot(ma_ref[0], wo_ref[0:A_WIDTH, :])
            + _dot(mb_ref[0], wo_ref[A_WIDTH:A_WIDTH + B_WIDTH, :]))
    oc = []
    for s, sl in zip(scores, heads):
        p = jnp.exp(s - jnp.max(s, axis=-1, keepdims=True))
        l = jnp.sum(p, axis=-1, keepdims=True)
        oc.append(_dot(p.astype(BF16), vm[:, sl]) / l)
    mc = (jnp.concatenate(oc, axis=-1) * sgc_ref[0]).astype(BF16)
    h_new = x_ref[0] + h_ab + _dot(mc, wo_ref[A_WIDTH + B_WIDTH:, :])
    o_ref[0] = _rmsnorm_rows(h_new, fw_ref[...])


def _out(x, ma, mb, qc, sgc, km, vm, w_out, fw):
    B, S, D = x.shape
    tm = OUT_TM
    row = lambda width: pl.BlockSpec((1, tm, width), lambda b, i: (b, i, 0))
    mem = pl.BlockSpec((1, MEM_LEN, C_WIDTH), lambda b, i: (b, 0, 0))
    return pl.pallas_call(
        _out_kernel,
        grid=(B, S // tm),
        in_specs=[row(D), row(A_WIDTH), row(B_WIDTH), row(C_WIDTH), row(C_WIDTH), mem, mem,
                  pl.BlockSpec(w_out.shape, lambda b, i: (0, 0, 0), pipeline_mode=pl.Buffered(1)),
                  pl.BlockSpec((1, D), lambda b, i: (0, 0))],
        out_specs=row(D),
        out_shape=jax.ShapeDtypeStruct((B, S, D), F32),
        scratch_shapes=[pltpu.VMEM((MIX_WIDTH, D), BF16)],
        compiler_params=pltpu.CompilerParams(
            dimension_semantics=("arbitrary", "arbitrary"), vmem_limit_bytes=VMEM_LIMIT),
        name="out",
    )(x, ma, mb, qc, sgc, km, vm, w_out, fw)


def kernel(x, mem, norm_w, w_in, w_alpha2, b_alpha, gla_norm_w, mem_norm_w, w_mem_kv, w_out,
           rel_bias, final_norm_w):
    assert norm_w.shape[0] == 1, "single layer"
    assert w_in.shape[2] == W_IN_ALIGNED + GLA_RANK + 2 * C_WIDTH
    wa = jnp.pad(w_alpha2[0], ((0, ZB_PAD - GLA_RANK), (0, 0)))

    km, vm = _memkv(mem, mem_norm_w[0][None, :], w_mem_kv)
    bias_own, bias_prev = _bias_tiles(rel_bias)
    (qaT, vaT, ka, sga, qb, kb, vb, sgb, qc, sgc, g) = _proj(
        x, norm_w[0][None, :], w_in[0], wa, b_alpha[0][None, :])
    ma = _moba(rel_bias, qaT, ka, vaT, bias_own, bias_prev, sga)
    mb = _gla(qb, kb, vb, g, sgb, gla_norm_w[0][None, :])
    return _out(x, ma, mb, qc, sgc, km, vm, w_out, final_norm_w[None, :])
```

```python
import functools
import math

import jax
import jax.numpy as jnp
from jax import lax
from jax.experimental import pallas as pl
from jax.experimental.pallas import tpu as pltpu

F32 = jnp.float32
BF16 = jnp.bfloat16

D_MODEL = 1024
MEM_LEN = 256
A_HEADS = 8
A_HEAD_DIM = 64
A_WIDTH = 512
MOBA_BLOCK = 256
MOBA_TOPK = 3
B_HEADS = 4
B_KEY_DIM = 64
B_VAL_DIM = 128
B_KEY_WIDTH = 256
B_WIDTH = 512
GLA_RANK = 16
GLA_TAU = 16.0
C_HEADS = 4
C_HEAD_DIM = 128
C_WIDTH = 512
MIX_WIDTH = A_WIDTH + B_WIDTH + C_WIDTH
REL_BUCKETS = 32
REL_MAX_DIST = 128
RMS_EPS = 1e-6
NEG = -1e30

LANES = 128
SUBLANES = 8
ZB_PAD = LANES
PROJ_TM = 512
OUT_TM = 512
GLA_CHUNK = 256
VMEM_LIMIT = 56 * 1024 * 1024
LOG2E = 1.0 / math.log(2.0)
HEADS_PER_TILE = LANES // A_HEAD_DIM
PV_ROWS = A_HEAD_DIM + 16
MOBA_MAX_BLOCKS = 16
QK_LOOKAHEAD = 4
FAR_UNROLL = 4


def _dot(a, b):
    return jnp.dot(a, b, preferred_element_type=F32)


def _dot_nt(a, b):
    return lax.dot_general(a, b, (((1,), (1,)), ((), ())), preferred_element_type=F32)


def _rmsnorm_rows(x, w):
    return x * lax.rsqrt(jnp.mean(x * x, axis=-1, keepdims=True) + RMS_EPS) * w


def _silu(x):
    return x * jax.nn.sigmoid(x)


def _memkv_kernel(mem_ref, nw_ref, w_ref, km_ref, vm_ref):
    u = _rmsnorm_rows(mem_ref[0], nw_ref[...]).astype(BF16)
    km_ref[0] = _dot(u, w_ref[0, :, :C_WIDTH].astype(BF16)).astype(BF16)
    vm_ref[0] = _dot(u, w_ref[0, :, C_WIDTH:].astype(BF16)).astype(BF16)


def _memkv(mem, nw, w_mem_kv):
    B = mem.shape[0]
    return pl.pallas_call(
        _memkv_kernel,
        grid=(B,),
        in_specs=[
            pl.BlockSpec((1, MEM_LEN, D_MODEL), lambda b: (b, 0, 0)),
            pl.BlockSpec((1, D_MODEL), lambda b: (0, 0)),
            pl.BlockSpec(w_mem_kv.shape, lambda b: (0, 0, 0)),
        ],
        out_specs=[
            pl.BlockSpec((1, MEM_LEN, C_WIDTH), lambda b: (b, 0, 0)),
            pl.BlockSpec((1, MEM_LEN, C_WIDTH), lambda b: (b, 0, 0)),
        ],
        out_shape=[jax.ShapeDtypeStruct((B, MEM_LEN, C_WIDTH), BF16)] * 2,
        name="memkv",
    )(mem, nw, w_mem_kv)


def _t5_bucket(n):
    max_exact = REL_BUCKETS // 2
    nf = jnp.maximum(n, max_exact).astype(F32)
    large = max_exact + jnp.floor(jnp.log(nf / max_exact) / math.log(REL_MAX_DIST / max_exact)
                                  * (REL_BUCKETS - max_exact)).astype(jnp.int32)
    large = jnp.minimum(large, REL_BUCKETS - 1)
    return jnp.where(n < max_exact, n, large)


def _bias_kernel(rb_ref, own_ref, prev_ref):
    h = pl.program_id(0)
    key = lax.broadcasted_iota(jnp.int32, (MOBA_BLOCK, MOBA_BLOCK), 0)
    qry = lax.broadcasted_iota(jnp.int32, (MOBA_BLOCK, MOBA_BLOCK), 1)
    rel = qry - key
    b_own = _t5_bucket(jnp.maximum(rel, 0))
    b_prev = _t5_bucket(rel + MOBA_BLOCK)
    own = jnp.zeros((MOBA_BLOCK, MOBA_BLOCK), F32)
    prev = jnp.zeros((MOBA_BLOCK, MOBA_BLOCK), F32)
    for bk in range(REL_BUCKETS):
        val = rb_ref[bk, h] * LOG2E
        own = jnp.where(b_own == bk, val, own)
        prev = jnp.where(b_prev == bk, val, prev)
    own_ref[0] = jnp.where(rel >= 0, own, NEG)
    prev_ref[0] = prev


def _bias_tiles(rel_bias):
    return pl.pallas_call(
        _bias_kernel,
        grid=(A_HEADS,),
        in_specs=[pl.BlockSpec(memory_space=pltpu.SMEM)],
        out_specs=[pl.BlockSpec((1, MOBA_BLOCK, MOBA_BLOCK), lambda h: (h, 0, 0))] * 2,
        out_shape=[jax.ShapeDtypeStruct((A_HEADS, MOBA_BLOCK, MOBA_BLOCK), F32)] * 2,
        name="t5bias",
    )(rel_bias)


_ROW_COLS = (("qa", A_WIDTH), ("ka", A_WIDTH), ("va", A_WIDTH), ("ga", A_WIDTH),
             ("qb", B_KEY_WIDTH), ("kb", B_KEY_WIDTH), ("vb", B_WIDTH), ("gb", B_WIDTH),
             ("qc", C_WIDTH), ("gc", C_WIDTH), ("zb", ZB_PAD))
_ROW_OFF = {}
_off = 0
for _name, _width in _ROW_COLS:
    _ROW_OFF[_name] = (_off, _off + _width)
    _off += _width
ROW_COLS_TOTAL = _off


W_IN_ALIGNED = 4 * A_WIDTH + 2 * B_KEY_WIDTH + 2 * B_WIDTH


def _proj_kernel(x_ref, nw_ref, w_ref, wa_ref, ba_ref,
                 qaT_ref, vaT_ref, ka_ref, ga_ref, qb_ref, kb_ref, vb_ref, gb_ref,
                 qc_ref, gc_ref, g_ref, wr_ref):
    @pl.when((pl.program_id(0) == 0) & (pl.program_id(1) == 0))
    def _():
        chunk = 4 * LANES
        for lo in range(0, W_IN_ALIGNED, chunk):
            wr_ref[lo:lo + chunk, :] = w_ref[lo:lo + chunk, :].astype(BF16)
        tail = W_IN_ALIGNED + GLA_RANK
        for lo in range(0, 2 * C_WIDTH, chunk):
            wr_ref[W_IN_ALIGNED + lo:W_IN_ALIGNED + lo + chunk, :] = (
                w_ref[tail + lo:tail + lo + chunk, :].astype(BF16))
        zb_lo = _ROW_OFF["zb"][0]
        wr_ref[zb_lo:zb_lo + GLA_RANK, :] = w_ref[W_IN_ALIGNED:tail, :].astype(BF16)
        wr_ref[zb_lo + GLA_RANK:, :] = jnp.zeros((ZB_PAD - GLA_RANK, wr_ref.shape[1]), BF16)

    u = _rmsnorm_rows(x_ref[0], nw_ref[...]).astype(BF16)

    def row(name):
        lo, hi = _ROW_OFF[name]
        return _dot_nt(u, wr_ref[lo:hi, :])

    def col(name):
        lo, hi = _ROW_OFF[name]
        return _dot_nt(wr_ref[lo:hi, :], u)

    zb = row("zb")
    zb_hi = zb.astype(BF16)
    zb_lo = (zb - zb_hi.astype(F32)).astype(BF16)
    wa = wa_ref[...]
    wa_hi = wa.astype(BF16)
    wa_lo = (wa - wa_hi.astype(F32)).astype(BF16)
    z = _dot(jnp.concatenate([zb_hi, zb_lo, zb_hi], axis=1),
             jnp.concatenate([wa_hi, wa_hi, wa_lo], axis=0)) + ba_ref[...]
    g_ref[0] = (jnp.minimum(z, 0.0) - jnp.log1p(jnp.exp(-jnp.abs(z)))) * (LOG2E / GLA_TAU)

    qaT_ref[0] = col("qa")
    vaT = col("va").astype(BF16)
    ones = jnp.ones((PV_ROWS - A_HEAD_DIM, vaT.shape[1]), BF16)
    pieces = []
    for h in range(A_HEADS):
        pieces += [vaT[h * A_HEAD_DIM:(h + 1) * A_HEAD_DIM], ones]
    vaT_ref[0] = jnp.concatenate(pieces, axis=0)

    ka = row("ka")
    tm = ka.shape[0]
    nb = MOBA_MAX_BLOCKS
    key_pos = pl.program_id(1) * tm + lax.broadcasted_iota(jnp.int32, (tm, LANES - A_HEAD_DIM), 0)
    blk_id = jnp.right_shift(key_pos, MOBA_BLOCK.bit_length() - 1)
    col = lax.broadcasted_iota(jnp.int32, (tm, LANES - A_HEAD_DIM), 1)
    onehot = jnp.where((col < 2 * nb) & ((col & (nb - 1)) == blk_id), 1.0, 0.0)
    pieces = []
    for h in range(A_HEADS):
        pieces += [ka[:, h * A_HEAD_DIM:(h + 1) * A_HEAD_DIM], onehot]
    ka_ref[0] = jnp.concatenate(pieces, axis=1).astype(BF16)
    ga_ref[0] = _silu(row("ga")).astype(BF16)
    qb_ref[0] = (row("qb") * (B_KEY_DIM ** -0.5)).astype(BF16)
    kb_ref[0] = row("kb").astype(BF16)
    vb_ref[0] = row("vb").astype(BF16)
    gb_ref[0] = _silu(row("gb")).astype(BF16)
    qc_ref[0] = (row("qc") * (C_HEAD_DIM ** -0.5)).astype(BF16)
    gc_ref[0] = _silu(row("gc")).astype(BF16)


def _proj(x, nw, w_in, wa, ba):
    B, S, D = x.shape
    tm = PROJ_TM
    row_spec = lambda width: pl.BlockSpec((1, tm, width), lambda b, i: (b, i, 0))
    col_spec = lambda rows: pl.BlockSpec((1, rows, tm), lambda b, i: (b, 0, i))
    const = lambda shape: pl.BlockSpec(shape, lambda b, i: (0,) * len(shape))
    sds = jax.ShapeDtypeStruct
    return pl.pallas_call(
        _proj_kernel,
        grid=(B, S // tm),
        in_specs=[
            pl.BlockSpec((1, tm, D), lambda b, i: (b, i, 0)),
            const((1, D)),
            pl.BlockSpec(w_in.shape, lambda b, i: (0, 0), pipeline_mode=pl.Buffered(1)),
            const((ZB_PAD, B_KEY_WIDTH)),
            const((1, B_KEY_WIDTH)),
        ],
        out_specs=[col_spec(A_WIDTH), col_spec(A_HEADS * PV_ROWS), row_spec(A_HEADS * LANES),
                   row_spec(A_WIDTH),
                   row_spec(B_KEY_WIDTH), row_spec(B_KEY_WIDTH), row_spec(B_WIDTH),
                   row_spec(B_WIDTH), row_spec(C_WIDTH), row_spec(C_WIDTH),
                   row_spec(B_KEY_WIDTH)],
        out_shape=[sds((B, A_WIDTH, S), F32), sds((B, A_HEADS * PV_ROWS, S), BF16),
                   sds((B, S, A_HEADS * LANES), BF16), sds((B, S, A_WIDTH), BF16),
                   sds((B, S, B_KEY_WIDTH), BF16), sds((B, S, B_KEY_WIDTH), BF16),
                   sds((B, S, B_WIDTH), BF16), sds((B, S, B_WIDTH), BF16),
                   sds((B, S, C_WIDTH), BF16), sds((B, S, C_WIDTH), BF16),
                   sds((B, S, B_KEY_WIDTH), F32)],
        scratch_shapes=[pltpu.VMEM((ROW_COLS_TOTAL, D), BF16)],
        compiler_params=pltpu.CompilerParams(
            dimension_semantics=("arbitrary", "arbitrary"), vmem_limit_bytes=VMEM_LIMIT),
        name="proj",
    )(x, nw, w_in, wa, ba)


def _moba_kernel(rb_ref, qT_ref, k_ref, vT_ref, own_ref, prev_ref, sg_ref, o_ref,
                 kmean_scr, kmbd_scr, qh_scr, m_scr, acc_scr, s_scr):
    i = pl.program_id(1)
    nb = kmean_scr.shape[0]
    blk = MOBA_BLOCK
    dh = A_HEAD_DIM
    n_tiles = A_HEADS // HEADS_PER_TILE

    @pl.when(i == 0)
    def _():
        for n in range(nb):
            kb = jnp.concatenate(
                [k_ref[0, n * blk:(n + 1) * blk, h * LANES:h * LANES + dh] for h in range(A_HEADS)],
                axis=1).astype(F32)
            kmean_scr[n:n + 1, :] = jnp.mean(kb, axis=0, keepdims=True)
        lane_head = jnp.right_shift(lax.broadcasted_iota(jnp.int32, (nb, A_WIDTH), 1),
                                    dh.bit_length() - 1)
        for h in range(A_HEADS):
            km = jnp.where(lane_head == h, kmean_scr[...], 0.0)
            km_hi = km.astype(BF16)
            km_lo = (km - km_hi.astype(F32)).astype(BF16)
            kmbd_scr[h * nb:(h + 1) * nb, :] = jnp.concatenate([km_hi, km_lo, km_hi], axis=1)

    q_f32 = qT_ref[0]
    q_hi = q_f32.astype(BF16)
    q_lo = (q_f32 - q_hi.astype(F32)).astype(BF16)
    gate_all = _dot(kmbd_scr[...], jnp.concatenate([q_hi, q_hi, q_lo], axis=0))
    brow = lax.broadcasted_iota(jnp.int32, (nb, blk), 0)
    browf = brow.astype(F32)
    zeros = jnp.zeros((LANES - dh - 2 * nb, blk), F32)

    for h in range(A_HEADS):
        g = jnp.where(brow < i, gate_all[h * nb:(h + 1) * nb], NEG)
        sel = brow < 0
        for _ in range(MOBA_TOPK):
            m = jnp.max(g, axis=0, keepdims=True)
            idx = jnp.min(jnp.where(g == m, browf, float(nb)), axis=0, keepdims=True)
            pick = browf == idx
            sel = sel | pick
            g = jnp.where(pick, -jnp.inf, g)
        valid = sel & (brow < i)
        far_bias = rb_ref[REL_BUCKETS - 1, h] * LOG2E
        pen = jnp.where(brow < i - 1, jnp.where(valid, far_bias, NEG),
                        jnp.where(brow == i - 1, jnp.where(valid, 0.0, NEG), 0.0))
        pen_hi = pen.astype(BF16).astype(F32)
        q_h = qT_ref[0, h * dh:(h + 1) * dh, :] * (dh ** -0.5 * LOG2E)
        qh_scr[h] = jnp.concatenate([q_h, pen_hi, pen - pen_hi, zeros], axis=0).astype(BF16)

    no_prev = jnp.where(i == 0, NEG, 0.0)

    def logits(j, h, kind):
        rows = pl.ds(pl.multiple_of(j * blk, blk), blk)
        s = _dot(k_ref[0, rows, h * LANES:(h + 1) * LANES], qh_scr[h])
        if kind == "own":
            return s + own_ref[h]
        if kind == "prev":
            return s + (prev_ref[h] + no_prev)
        return s

    def fold(j, h, s, first):
        rows = pl.ds(pl.multiple_of(j * blk, blk), blk)
        bm = jnp.max(s, axis=0, keepdims=True)
        vj = vT_ref[0, h * PV_ROWS:(h + 1) * PV_ROWS, rows]
        if first:
            m_scr[h] = bm
            acc_scr[h] = _dot(vj, jnp.exp2(s - bm).astype(BF16))
        else:
            m_old = m_scr[h]
            m_new = jnp.maximum(m_old, bm)
            m_scr[h] = m_new
            acc_scr[h] = (jnp.exp2(m_old - m_new) * acc_scr[h]
                          + _dot(vj, jnp.exp2(s - m_new).astype(BF16)))

    n_far = i - 1
    j_prev = jnp.maximum(i - 1, 0)
    items = ([(i, "own", h) for h in range(A_HEADS)] + [(j_prev, "prev", h) for h in range(A_HEADS)]
             + [(0, "far", h) for h in range(QK_LOOKAHEAD)])
    n_fold = 2 * A_HEADS
    pending = []

    def issue(n):
        j, kind, h = items[n]
        s = logits(j, h, kind)
        if n < n_fold:
            pending.append(s)
        else:
            s_scr[h] = s

    for n in range(QK_LOOKAHEAD):
        issue(n)
    for n in range(n_fold):
        issue(n + QK_LOOKAHEAD)
        j, kind, h = items[n]
        fold(j, h, pending.pop(0), kind == "own")

    def far_group(j0, n_blocks):
        j_after = jnp.minimum(j0 + n_blocks, n_far - 1)
        stream = ([(j0 + t, h) for t in range(n_blocks) for h in range(A_HEADS)]
                  + [(j_after, h) for h in range(QK_LOOKAHEAD)])
        n_items = n_blocks * A_HEADS
        pending = [s_scr[h] for h in range(QK_LOOKAHEAD)]
        for n in range(n_items):
            j, h = stream[n + QK_LOOKAHEAD]
            if n + QK_LOOKAHEAD < n_items:
                pending.append(logits(j, h, "far"))
            else:
                s_scr[h] = logits(j, h, "far")
            j, h = stream[n]
            fold(j, h, pending.pop(0), False)

    def far_groups(g, carry):
        far_group(g * FAR_UNROLL, FAR_UNROLL)
        return carry

    n_groups = jnp.maximum(n_far, 0) // FAR_UNROLL
    lax.fori_loop(0, n_groups, far_groups, 0)
    for rest in range(1, FAR_UNROLL):
        @pl.when(n_far - n_groups * FAR_UNROLL == rest)
        def _():
            far_group(n_groups * FAR_UNROLL, rest)

    for t in range(n_tiles):
        oT = []
        for hh in range(HEADS_PER_TILE):
            acc = acc_scr[t * HEADS_PER_TILE + hh]
            oT.append(acc[:dh] / acc[dh:dh + 1])
        cols = slice(t * LANES, (t + 1) * LANES)
        o_ref[0, :, cols] = (jnp.concatenate(oT, axis=0).T * sg_ref[0, :, cols]).astype(BF16)


def _moba(rel_bias, qaT, ka, vaT, bias_own, bias_prev, sga):
    B, S, _ = ka.shape
    nb = S // MOBA_BLOCK
    assert nb == MOBA_MAX_BLOCKS
    bias_spec = pl.BlockSpec((A_HEADS, MOBA_BLOCK, MOBA_BLOCK), lambda b, i: (0, 0, 0))
    return pl.pallas_call(
        _moba_kernel,
        grid=(B, nb),
        in_specs=[
            pl.BlockSpec(memory_space=pltpu.SMEM),
            pl.BlockSpec((1, A_WIDTH, MOBA_BLOCK), lambda b, i: (b, 0, i)),
            pl.BlockSpec((1, S, A_HEADS * LANES), lambda b, i: (b, 0, 0)),
            pl.BlockSpec((1, A_HEADS * PV_ROWS, S), lambda b, i: (b, 0, 0)),
            bias_spec, bias_spec,
            pl.BlockSpec((1, MOBA_BLOCK, A_WIDTH), lambda b, i: (b, i, 0)),
        ],
        out_specs=pl.BlockSpec((1, MOBA_BLOCK, A_WIDTH), lambda b, i: (b, i, 0)),
        out_shape=jax.ShapeDtypeStruct((B, S, A_WIDTH), BF16),
        scratch_shapes=[pltpu.VMEM((nb, A_WIDTH), F32),
                        pltpu.VMEM((A_HEADS * nb, 3 * A_WIDTH), BF16),
                        pltpu.VMEM((A_HEADS, LANES, MOBA_BLOCK), BF16),
                        pltpu.VMEM((A_HEADS, 1, MOBA_BLOCK), F32),
                        pltpu.VMEM((A_HEADS, PV_ROWS, MOBA_BLOCK), F32),
                        pltpu.VMEM((QK_LOOKAHEAD, MOBA_BLOCK, MOBA_BLOCK), F32)],
        compiler_params=pltpu.CompilerParams(
            dimension_semantics=("arbitrary", "arbitrary"),
            vmem_limit_bytes=VMEM_LIMIT),
        name="moba",
    )(rel_bias, qaT, ka, vaT, bias_own, bias_prev, sga)


def _gla_kernel(q_ref, k_ref, v_ref, g_ref, sg_ref, nw_ref, o_ref, state_scr, att_scr):
    c = pl.program_id(1)
    C = GLA_CHUNK
    dk, dv, H = B_KEY_DIM, B_VAL_DIM, B_HEADS
    n_levels = C.bit_length() - 1

    @pl.when(c == 0)
    def _():
        state_scr[...] = jnp.zeros_like(state_scr)

    q = q_ref[0].astype(F32)
    k = k_ref[0].astype(F32)
    v = v_ref[0]
    g = g_ref[0]

    row = lax.broadcasted_iota(jnp.int32, (C, H * dk), 0)

    b = g
    sh = 1
    while sh < C:
        if sh % SUBLANES == 0:
            b = b + jnp.concatenate([jnp.zeros((sh, H * dk), F32), b[:C - sh]], axis=0)
        else:
            b = b + jnp.where(row >= sh, pltpu.roll(b, sh, 0), 0.0)
        sh *= 2

    tt = lax.broadcasted_iota(jnp.int32, (C, C), 0)
    ss = lax.broadcasted_iota(jnp.int32, (C, C), 1)
    txs = tt ^ ss
    level = jnp.full((C, C), -1, jnp.int32)
    for p in range(n_levels):
        level = level + (txs >= (1 << p)).astype(jnp.int32)
    level = jnp.where(ss <= tt, level, -2)

    lane = lax.broadcasted_iota(jnp.int32, (C, H * dk), 1)
    head_masks = [(lane >= h * dk) & (lane < (h + 1) * dk) for h in range(H)]

    def scores(qq, kk, h):
        lo = (h * dk // LANES) * LANES
        kh = jnp.where(head_masks[h], kk, 0.0).astype(BF16)[:, lo:lo + LANES]
        return _dot_nt(qq.astype(BF16)[:, lo:lo + LANES], kh)

    on_diag = level == -1
    for h in range(H):
        att_scr[h] = jnp.where(on_diag, scores(q, k, h), 0.0)

    block_end = b
    for p in range(n_levels):
        m = 1 << p
        second_half = (row & m) != 0
        r = jnp.where(second_half, pltpu.roll(block_end, m, 0), block_end)
        decay = jnp.exp2(-jnp.abs(b - r))
        qd = q * decay
        kd = k * decay
        at_level = level == p
        for h in range(H):
            att_scr[h] = jnp.where(at_level, scores(qd, kd, h), att_scr[h])
        if p + 1 < n_levels:
            block_end = jnp.where(second_half, block_end, pltpu.roll(block_end, C - m, 0))

    state = state_scr[...]
    o_inter = _dot((q * jnp.exp2(b)).astype(BF16), state.astype(BF16))

    nw = nw_ref[...]
    for h in range(H):
        o_h = o_inter[:, h * dv:(h + 1) * dv] + _dot(att_scr[h].astype(BF16),
                                                      v[:, h * dv:(h + 1) * dv])
        cols = slice(h * dv, (h + 1) * dv)
        o_ref[0, :, cols] = (_rmsnorm_rows(o_h, nw) * sg_ref[0, :, cols]).astype(BF16)

    bT = b.T
    kT = k.T
    b_last = bT[:, C - 1:C]
    kdT = (kT * jnp.exp2(b_last - bT)).astype(BF16)
    ds = _dot(kdT, v)
    srow = lax.broadcasted_iota(jnp.int32, (H * dk, H * dv), 0) // dk
    scol = lax.broadcasted_iota(jnp.int32, (H * dk, H * dv), 1) // dv
    state_scr[...] = jnp.exp2(b_last) * state + jnp.where(srow == scol, ds, 0.0)


def _gla(qb, kb, vb, g, sgb, nw):
    B, S, _ = qb.shape
    C = GLA_CHUNK
    spec = lambda width: pl.BlockSpec((1, C, width), lambda b, c: (b, c, 0))
    return pl.pallas_call(
        _gla_kernel,
        grid=(B, S // C),
        in_specs=[spec(B_KEY_WIDTH), spec(B_KEY_WIDTH), spec(B_WIDTH), spec(B_KEY_WIDTH),
                  spec(B_WIDTH), pl.BlockSpec((1, B_VAL_DIM), lambda b, c: (0, 0))],
        out_specs=spec(B_WIDTH),
        out_shape=jax.ShapeDtypeStruct((B, S, B_WIDTH), BF16),
        scratch_shapes=[pltpu.VMEM((B_KEY_WIDTH, B_WIDTH), F32),
                        pltpu.VMEM((B_HEADS, C, C), F32)],
        compiler_params=pltpu.CompilerParams(
            dimension_semantics=("arbitrary", "arbitrary"), vmem_limit_bytes=VMEM_LIMIT),
        name="gla",
    )(qb, kb, vb, g, sgb, nw)


def _out_kernel(x_ref, ma_ref, mb_ref, qc_ref, sgc_ref, km_ref, vm_ref, w_ref, fw_ref, o_ref,
                wo_ref):
    @pl.when((pl.program_id(0) == 0) & (pl.program_id(1) == 0))
    def _():
        for lo in range(0, MIX_WIDTH, 4 * LANES):
            wo_ref[lo:lo + 4 * LANES, :] = w_ref[0, lo:lo + 4 * LANES, :].astype(BF16)

    dh = C_HEAD_DIM
    qc = qc_ref[0]
    km = km_ref[0]
    vm = vm_ref[0]
    heads = [slice(h * dh, (h + 1) * dh) for h in range(C_HEADS)]
    scores = [_dot_nt(qc[:, sl], km[:, sl]) for sl in heads]
    h_ab = (_dot(ma_ref[0], wo_ref[0:A_WIDTH, :])
            + _dot(mb_ref[0], wo_ref[A_WIDTH:A_WIDTH + B_WIDTH, :]))
    oc = []
    for s, sl in zip(scores, heads):
        p = jnp.exp(s - jnp.max(s, axis=-1, keepdims=True))
        l = jnp.sum(p, axis=-1, keepdims=True)
        oc.append(_dot(p.astype(BF16), vm[:, sl]) / l)
    mc = (jnp.concatenate(oc, axis=-1) * sgc_ref[0]).astype(BF16)
    h_new = x_ref[0] + h_ab + _dot(mc, wo_ref[A_WIDTH + B_WIDTH:, :])
    o_ref[0] = _rmsnorm_rows(h_new, fw_ref[...])


def _out(x, ma, mb, qc, sgc, km, vm, w_out, fw):
    B, S, D = x.shape
    tm = OUT_TM
    row = lambda width: pl.BlockSpec((1, tm, width), lambda b, i: (b, i, 0))
    mem = pl.BlockSpec((1, MEM_LEN, C_WIDTH), lambda b, i: (b, 0, 0))
    return pl.pallas_call(
        _out_kernel,
        grid=(B, S // tm),
        in_specs=[row(D), row(A_WIDTH), row(B_WIDTH), row(C_WIDTH), row(C_WIDTH), mem, mem,
                  pl.BlockSpec(w_out.shape, lambda b, i: (0, 0, 0), pipeline_mode=pl.Buffered(1)),
                  pl.BlockSpec((1, D), lambda b, i: (0, 0))],
        out_specs=row(D),
        out_shape=jax.ShapeDtypeStruct((B, S, D), F32),
        scratch_shapes=[pltpu.VMEM((MIX_WIDTH, D), BF16)],
        compiler_params=pltpu.CompilerParams(
            dimension_semantics=("arbitrary", "arbitrary"), vmem_limit_bytes=VMEM_LIMIT),
        name="out",
    )(x, ma, mb, qc, sgc, km, vm, w_out, fw)


def kernel(x, mem, norm_w, w_in, w_alpha2, b_alpha, gla_norm_w, mem_norm_w, w_mem_kv, w_out,
           rel_bias, final_norm_w):
    assert norm_w.shape[0] == 1, "single layer"
    assert w_in.shape[2] == W_IN_ALIGNED + GLA_RANK + 2 * C_WIDTH
    wa = jnp.pad(w_alpha2[0], ((0, ZB_PAD - GLA_RANK), (0, 0)))

    km, vm = _memkv(mem, mem_norm_w[0][None, :], w_mem_kv)
    bias_own, bias_prev = _bias_tiles(rel_bias)
    (qaT, vaT, ka, sga, qb, kb, vb, sgb, qc, sgc, g) = _proj(
        x, norm_w[0][None, :], w_in[0].T, wa, b_alpha[0][None, :])
    ma = _moba(rel_bias, qaT, ka, vaT, bias_own, bias_prev, sga)
    mb = _gla(qb, kb, vb, g, sgb, gla_norm_w[0][None, :])
    return _out(x, ma, mb, qc, sgc, km, vm, w_out, final_norm_w[None, :])
```

```python
import functools
import math

import jax
import jax.numpy as jnp
from jax import lax
from jax.experimental import pallas as pl
from jax.experimental.pallas import tpu as pltpu

F32 = jnp.float32
BF16 = jnp.bfloat16

D_MODEL = 1024
MEM_LEN = 256
A_HEADS = 8
A_HEAD_DIM = 64
A_WIDTH = 512
MOBA_BLOCK = 256
MOBA_TOPK = 3
B_HEADS = 4
B_KEY_DIM = 64
B_VAL_DIM = 128
B_KEY_WIDTH = 256
B_WIDTH = 512
GLA_RANK = 16
GLA_TAU = 16.0
C_HEADS = 4
C_HEAD_DIM = 128
C_WIDTH = 512
MIX_WIDTH = A_WIDTH + B_WIDTH + C_WIDTH
REL_BUCKETS = 32
REL_MAX_DIST = 128
RMS_EPS = 1e-6
NEG = -1e30

LANES = 128
SUBLANES = 8
BF16_ROWS = 16
ZB_PAD = LANES
PROJ_TM = 512
OUT_TM = 512
GLA_CHUNK = 256
VMEM_LIMIT = 56 * 1024 * 1024
LOG2E = 1.0 / math.log(2.0)
HEADS_PER_TILE = LANES // A_HEAD_DIM
PV_ROWS = A_HEAD_DIM + 16
MOBA_MAX_BLOCKS = 16
QK_LOOKAHEAD = 4
FAR_UNROLL = 4


def _dot(a, b):
    return jnp.dot(a, b, preferred_element_type=F32)


def _dot_nt(a, b):
    return lax.dot_general(a, b, (((1,), (1,)), ((), ())), preferred_element_type=F32)


def _rmsnorm_rows(x, w):
    return x * lax.rsqrt(jnp.mean(x * x, axis=-1, keepdims=True) + RMS_EPS) * w


def _silu(x):
    return x * jax.nn.sigmoid(x)


def _memkv_kernel(mem_ref, nw_ref, w_ref, km_ref, vm_ref):
    u = _rmsnorm_rows(mem_ref[0], nw_ref[...]).astype(BF16)
    km_ref[0] = _dot(u, w_ref[0, :, :C_WIDTH].astype(BF16)).astype(BF16)
    vm_ref[0] = _dot(u, w_ref[0, :, C_WIDTH:].astype(BF16)).astype(BF16)


def _memkv(mem, nw, w_mem_kv):
    B = mem.shape[0]
    return pl.pallas_call(
        _memkv_kernel,
        grid=(B,),
        in_specs=[
            pl.BlockSpec((1, MEM_LEN, D_MODEL), lambda b: (b, 0, 0)),
            pl.BlockSpec((1, D_MODEL), lambda b: (0, 0)),
            pl.BlockSpec(w_mem_kv.shape, lambda b: (0, 0, 0)),
        ],
        out_specs=[
            pl.BlockSpec((1, MEM_LEN, C_WIDTH), lambda b: (b, 0, 0)),
            pl.BlockSpec((1, MEM_LEN, C_WIDTH), lambda b: (b, 0, 0)),
        ],
        out_shape=[jax.ShapeDtypeStruct((B, MEM_LEN, C_WIDTH), BF16)] * 2,
        name="memkv",
    )(mem, nw, w_mem_kv)


def _t5_bucket(n):
    max_exact = REL_BUCKETS // 2
    nf = jnp.maximum(n, max_exact).astype(F32)
    large = max_exact + jnp.floor(jnp.log(nf / max_exact) / math.log(REL_MAX_DIST / max_exact)
                                  * (REL_BUCKETS - max_exact)).astype(jnp.int32)
    large = jnp.minimum(large, REL_BUCKETS - 1)
    return jnp.where(n < max_exact, n, large)


def _bias_kernel(rb_ref, own_ref, prev_ref):
    h = pl.program_id(0)
    key = lax.broadcasted_iota(jnp.int32, (MOBA_BLOCK, MOBA_BLOCK), 0)
    qry = lax.broadcasted_iota(jnp.int32, (MOBA_BLOCK, MOBA_BLOCK), 1)
    rel = qry - key
    b_own = _t5_bucket(jnp.maximum(rel, 0))
    b_prev = _t5_bucket(rel + MOBA_BLOCK)
    own = jnp.zeros((MOBA_BLOCK, MOBA_BLOCK), F32)
    prev = jnp.zeros((MOBA_BLOCK, MOBA_BLOCK), F32)
    for bk in range(REL_BUCKETS):
        val = rb_ref[bk, h] * LOG2E
        own = jnp.where(b_own == bk, val, own)
        prev = jnp.where(b_prev == bk, val, prev)
    own_ref[0] = jnp.where(rel >= 0, own, NEG)
    prev_ref[0] = prev


def _bias_tiles(rel_bias):
    return pl.pallas_call(
        _bias_kernel,
        grid=(A_HEADS,),
        in_specs=[pl.BlockSpec(memory_space=pltpu.SMEM)],
        out_specs=[pl.BlockSpec((1, MOBA_BLOCK, MOBA_BLOCK), lambda h: (h, 0, 0))] * 2,
        out_shape=[jax.ShapeDtypeStruct((A_HEADS, MOBA_BLOCK, MOBA_BLOCK), F32)] * 2,
        name="t5bias",
    )(rel_bias)


_ROW_COLS = (("qa", A_WIDTH), ("ka", A_WIDTH), ("va", A_WIDTH), ("ga", A_WIDTH),
             ("qb", B_KEY_WIDTH), ("kb", B_KEY_WIDTH), ("vb", B_WIDTH), ("gb", B_WIDTH),
             ("qc", C_WIDTH), ("gc", C_WIDTH), ("zb", ZB_PAD))
_ROW_OFF = {}
_off = 0
for _name, _width in _ROW_COLS:
    _ROW_OFF[_name] = (_off, _off + _width)
    _off += _width
ROW_COLS_TOTAL = _off


W_IN_ALIGNED = 4 * A_WIDTH + 2 * B_KEY_WIDTH + 2 * B_WIDTH


def _proj_kernel(x_ref, nw_ref, w_ref, wa_ref, ba_ref,
                 qaT_ref, vaT_ref, ka_ref, ga_ref, qb_ref, kb_ref, vb_ref, gb_ref,
                 qc_ref, gc_ref, g_ref, wr_ref):
    @pl.when((pl.program_id(0) == 0) & (pl.program_id(1) == 0))
    def _():
        chunk = 4 * LANES
        for lo in range(0, W_IN_ALIGNED, chunk):
            wr_ref[lo:lo + chunk, :] = w_ref[lo:lo + chunk, :].astype(BF16)
        tail = W_IN_ALIGNED + GLA_RANK
        for lo in range(0, 2 * C_WIDTH, chunk):
            wr_ref[W_IN_ALIGNED + lo:W_IN_ALIGNED + lo + chunk, :] = (
                w_ref[tail + lo:tail + lo + chunk, :].astype(BF16))
        zb_lo = _ROW_OFF["zb"][0]
        wr_ref[zb_lo:zb_lo + GLA_RANK, :] = w_ref[W_IN_ALIGNED:tail, :].astype(BF16)
        wr_ref[zb_lo + GLA_RANK:, :] = jnp.zeros((ZB_PAD - GLA_RANK, wr_ref.shape[1]), BF16)

    u = _rmsnorm_rows(x_ref[0], nw_ref[...]).astype(BF16)

    def row(name):
        lo, hi = _ROW_OFF[name]
        return _dot_nt(u, wr_ref[lo:hi, :])

    def col(name):
        lo, hi = _ROW_OFF[name]
        return _dot_nt(wr_ref[lo:hi, :], u)

    zb = row("zb")
    zb_hi = zb.astype(BF16)
    zb_lo = (zb - zb_hi.astype(F32)).astype(BF16)
    wa = wa_ref[...]
    wa_hi = wa.astype(BF16)
    wa_lo = (wa - wa_hi.astype(F32)).astype(BF16)
    z = _dot(jnp.concatenate([zb_hi, zb_lo, zb_hi], axis=1),
             jnp.concatenate([wa_hi, wa_hi, wa_lo], axis=0)) + ba_ref[...]
    g_ref[0] = (jnp.minimum(z, 0.0) - jnp.log1p(jnp.exp(-jnp.abs(z)))) * (LOG2E / GLA_TAU)

    qaT_ref[0] = col("qa")
    vaT = col("va").astype(BF16)
    ones = jnp.ones((PV_ROWS - A_HEAD_DIM, vaT.shape[1]), BF16)
    pieces = []
    for h in range(A_HEADS):
        pieces += [vaT[h * A_HEAD_DIM:(h + 1) * A_HEAD_DIM], ones]
    vaT_ref[0] = jnp.concatenate(pieces, axis=0)

    ka = row("ka")
    tm = ka.shape[0]
    nb = MOBA_MAX_BLOCKS
    key_pos = pl.program_id(1) * tm + lax.broadcasted_iota(jnp.int32, (tm, LANES - A_HEAD_DIM), 0)
    blk_id = jnp.right_shift(key_pos, MOBA_BLOCK.bit_length() - 1)
    col = lax.broadcasted_iota(jnp.int32, (tm, LANES - A_HEAD_DIM), 1)
    onehot = jnp.where((col < 2 * nb) & ((col & (nb - 1)) == blk_id), 1.0, 0.0)
    pieces = []
    for h in range(A_HEADS):
        pieces += [ka[:, h * A_HEAD_DIM:(h + 1) * A_HEAD_DIM], onehot]
    ka_ref[0] = jnp.concatenate(pieces, axis=1).astype(BF16)
    ga_ref[0] = _silu(row("ga")).astype(BF16)
    qb_ref[0] = (row("qb") * (B_KEY_DIM ** -0.5)).astype(BF16)
    kb_ref[0] = row("kb").astype(BF16)
    vb_ref[0] = row("vb").astype(BF16)
    gb_ref[0] = _silu(row("gb")).astype(BF16)
    qc_ref[0] = (row("qc") * (C_HEAD_DIM ** -0.5)).astype(BF16)
    gc_ref[0] = _silu(row("gc")).astype(BF16)


def _proj(x, nw, w_in, wa, ba):
    B, S, D = x.shape
    tm = PROJ_TM
    row_spec = lambda width: pl.BlockSpec((1, tm, width), lambda b, i: (b, i, 0))
    col_spec = lambda rows: pl.BlockSpec((1, rows, tm), lambda b, i: (b, 0, i))
    const = lambda shape: pl.BlockSpec(shape, lambda b, i: (0,) * len(shape))
    sds = jax.ShapeDtypeStruct
    return pl.pallas_call(
        _proj_kernel,
        grid=(B, S // tm),
        in_specs=[
            pl.BlockSpec((1, tm, D), lambda b, i: (b, i, 0)),
            const((1, D)),
            pl.BlockSpec(w_in.shape, lambda b, i: (0, 0), pipeline_mode=pl.Buffered(1)),
            const((ZB_PAD, B_KEY_WIDTH)),
            const((1, B_KEY_WIDTH)),
        ],
        out_specs=[col_spec(A_WIDTH), col_spec(A_HEADS * PV_ROWS), row_spec(A_HEADS * LANES),
                   row_spec(A_WIDTH),
                   row_spec(B_KEY_WIDTH), row_spec(B_KEY_WIDTH), row_spec(B_WIDTH),
                   row_spec(B_WIDTH), row_spec(C_WIDTH), row_spec(C_WIDTH),
                   row_spec(B_KEY_WIDTH)],
        out_shape=[sds((B, A_WIDTH, S), F32), sds((B, A_HEADS * PV_ROWS, S), BF16),
                   sds((B, S, A_HEADS * LANES), BF16), sds((B, S, A_WIDTH), BF16),
                   sds((B, S, B_KEY_WIDTH), BF16), sds((B, S, B_KEY_WIDTH), BF16),
                   sds((B, S, B_WIDTH), BF16), sds((B, S, B_WIDTH), BF16),
                   sds((B, S, C_WIDTH), BF16), sds((B, S, C_WIDTH), BF16),
                   sds((B, S, B_KEY_WIDTH), F32)],
        scratch_shapes=[pltpu.VMEM((ROW_COLS_TOTAL, D), BF16)],
        compiler_params=pltpu.CompilerParams(
            dimension_semantics=("arbitrary", "arbitrary"), vmem_limit_bytes=VMEM_LIMIT),
        name="proj",
    )(x, nw, w_in, wa, ba)


def _moba_kernel(rb_ref, qT_ref, k_ref, vT_ref, own_ref, prev_ref, sg_ref, o_ref,
                 kmean_scr, kmbd_scr, qh_scr, m_scr, acc_scr, s_scr):
    i = pl.program_id(1)
    nb = kmean_scr.shape[0]
    blk = MOBA_BLOCK
    dh = A_HEAD_DIM
    n_tiles = A_HEADS // HEADS_PER_TILE

    @pl.when(i == 0)
    def _():
        for n in range(nb):
            kb = jnp.concatenate(
                [k_ref[0, n * blk:(n + 1) * blk, h * LANES:h * LANES + dh] for h in range(A_HEADS)],
                axis=1).astype(F32)
            kmean_scr[n:n + 1, :] = jnp.mean(kb, axis=0, keepdims=True)
        lane_head = jnp.right_shift(lax.broadcasted_iota(jnp.int32, (nb, A_WIDTH), 1),
                                    dh.bit_length() - 1)
        for h in range(A_HEADS):
            km = jnp.where(lane_head == h, kmean_scr[...], 0.0)
            km_hi = km.astype(BF16)
            km_lo = (km - km_hi.astype(F32)).astype(BF16)
            kmbd_scr[h * nb:(h + 1) * nb, :] = jnp.concatenate([km_hi, km_lo, km_hi], axis=1)

    q_f32 = qT_ref[0]
    q_hi = q_f32.astype(BF16)
    q_lo = (q_f32 - q_hi.astype(F32)).astype(BF16)
    gate_all = _dot(kmbd_scr[...], jnp.concatenate([q_hi, q_hi, q_lo], axis=0))
    brow = lax.broadcasted_iota(jnp.int32, (nb, blk), 0)
    browf = brow.astype(F32)
    zeros = jnp.zeros((LANES - dh - 2 * nb, blk), F32)

    for h in range(A_HEADS):
        g = jnp.where(brow < i, gate_all[h * nb:(h + 1) * nb], NEG)
        sel = brow < 0
        for _ in range(MOBA_TOPK):
            m = jnp.max(g, axis=0, keepdims=True)
            idx = jnp.min(jnp.where(g == m, browf, float(nb)), axis=0, keepdims=True)
            pick = browf == idx
            sel = sel | pick
            g = jnp.where(pick, -jnp.inf, g)
        valid = sel & (brow < i)
        far_bias = rb_ref[REL_BUCKETS - 1, h] * LOG2E
        pen = jnp.where(brow < i - 1, jnp.where(valid, far_bias, NEG),
                        jnp.where(brow == i - 1, jnp.where(valid, 0.0, NEG), 0.0))
        pen_hi = pen.astype(BF16).astype(F32)
        q_h = qT_ref[0, h * dh:(h + 1) * dh, :] * (dh ** -0.5 * LOG2E)
        qh_scr[h] = jnp.concatenate([q_h, pen_hi, pen - pen_hi, zeros], axis=0).astype(BF16)

    no_prev = jnp.where(i == 0, NEG, 0.0)

    def logits(j, h, kind):
        rows = pl.ds(pl.multiple_of(j * blk, blk), blk)
        s = _dot(k_ref[0, rows, h * LANES:(h + 1) * LANES], qh_scr[h])
        if kind == "own":
            return s + own_ref[h]
        if kind == "prev":
            return s + (prev_ref[h] + no_prev)
        return s

    def fold(j, h, s, first):
        rows = pl.ds(pl.multiple_of(j * blk, blk), blk)
        bm = jnp.max(s, axis=0, keepdims=True)
        vj = vT_ref[0, h * PV_ROWS:(h + 1) * PV_ROWS, rows]
        if first:
            m_scr[h] = bm
            acc_scr[h] = _dot(vj, jnp.exp2(s - bm).astype(BF16))
        else:
            m_old = m_scr[h]
            m_new = jnp.maximum(m_old, bm)
            m_scr[h] = m_new
            acc_scr[h] = (jnp.exp2(m_old - m_new) * acc_scr[h]
                          + _dot(vj, jnp.exp2(s - m_new).astype(BF16)))

    n_far = i - 1
    j_prev = jnp.maximum(i - 1, 0)
    items = ([(i, "own", h) for h in range(A_HEADS)] + [(j_prev, "prev", h) for h in range(A_HEADS)]
             + [(0, "far", h) for h in range(QK_LOOKAHEAD)])
    n_fold = 2 * A_HEADS
    pending = []

    def issue(n):
        j, kind, h = items[n]
        s = logits(j, h, kind)
        if n < n_fold:
            pending.append(s)
        else:
            s_scr[h] = s

    for n in range(QK_LOOKAHEAD):
        issue(n)
    for n in range(n_fold):
        issue(n + QK_LOOKAHEAD)
        j, kind, h = items[n]
        fold(j, h, pending.pop(0), kind == "own")

    def far_group(j0, n_blocks):
        j_after = jnp.minimum(j0 + n_blocks, n_far - 1)
        stream = ([(j0 + t, h) for t in range(n_blocks) for h in range(A_HEADS)]
                  + [(j_after, h) for h in range(QK_LOOKAHEAD)])
        n_items = n_blocks * A_HEADS
        pending = [s_scr[h] for h in range(QK_LOOKAHEAD)]
        for n in range(n_items):
            j, h = stream[n + QK_LOOKAHEAD]
            if n + QK_LOOKAHEAD < n_items:
                pending.append(logits(j, h, "far"))
            else:
                s_scr[h] = logits(j, h, "far")
            j, h = stream[n]
            fold(j, h, pending.pop(0), False)

    def far_groups(g, carry):
        far_group(g * FAR_UNROLL, FAR_UNROLL)
        return carry

    n_groups = jnp.maximum(n_far, 0) // FAR_UNROLL
    lax.fori_loop(0, n_groups, far_groups, 0)
    for rest in range(1, FAR_UNROLL):
        @pl.when(n_far - n_groups * FAR_UNROLL == rest)
        def _():
            far_group(n_groups * FAR_UNROLL, rest)

    for t in range(n_tiles):
        oT = []
        for hh in range(HEADS_PER_TILE):
            acc = acc_scr[t * HEADS_PER_TILE + hh]
            oT.append(acc[:dh] / acc[dh:dh + 1])
        cols = slice(t * LANES, (t + 1) * LANES)
        o_ref[0, :, cols] = (jnp.concatenate(oT, axis=0).T * sg_ref[0, :, cols]).astype(BF16)


def _moba(rel_bias, qaT, ka, vaT, bias_own, bias_prev, sga):
    B, S, _ = ka.shape
    nb = S // MOBA_BLOCK
    assert nb == MOBA_MAX_BLOCKS
    bias_spec = pl.BlockSpec((A_HEADS, MOBA_BLOCK, MOBA_BLOCK), lambda b, i: (0, 0, 0))
    return pl.pallas_call(
        _moba_kernel,
        grid=(B, nb),
        in_specs=[
            pl.BlockSpec(memory_space=pltpu.SMEM),
            pl.BlockSpec((1, A_WIDTH, MOBA_BLOCK), lambda b, i: (b, 0, i)),
            pl.BlockSpec((1, S, A_HEADS * LANES), lambda b, i: (b, 0, 0)),
            pl.BlockSpec((1, A_HEADS * PV_ROWS, S), lambda b, i: (b, 0, 0)),
            bias_spec, bias_spec,
            pl.BlockSpec((1, MOBA_BLOCK, A_WIDTH), lambda b, i: (b, i, 0)),
        ],
        out_specs=pl.BlockSpec((1, MOBA_BLOCK, A_WIDTH), lambda b, i: (b, i, 0)),
        out_shape=jax.ShapeDtypeStruct((B, S, A_WIDTH), BF16),
        scratch_shapes=[pltpu.VMEM((nb, A_WIDTH), F32),
                        pltpu.VMEM((A_HEADS * nb, 3 * A_WIDTH), BF16),
                        pltpu.VMEM((A_HEADS, LANES, MOBA_BLOCK), BF16),
                        pltpu.VMEM((A_HEADS, 1, MOBA_BLOCK), F32),
                        pltpu.VMEM((A_HEADS, PV_ROWS, MOBA_BLOCK), F32),
                        pltpu.VMEM((QK_LOOKAHEAD, MOBA_BLOCK, MOBA_BLOCK), F32)],
        compiler_params=pltpu.CompilerParams(
            dimension_semantics=("arbitrary", "arbitrary"),
            vmem_limit_bytes=VMEM_LIMIT),
        name="moba",
    )(rel_bias, qaT, ka, vaT, bias_own, bias_prev, sga)


def _gla_kernel(q_ref, k_ref, v_ref, g_ref, sg_ref, nw_ref, o_ref, state_scr, att_scr):
    c = pl.program_id(1)
    C = GLA_CHUNK
    dk, dv, H = B_KEY_DIM, B_VAL_DIM, B_HEADS
    n_levels = C.bit_length() - 1

    @pl.when(c == 0)
    def _():
        state_scr[...] = jnp.zeros_like(state_scr)

    q = q_ref[0].astype(F32)
    k = k_ref[0].astype(F32)
    v = v_ref[0]
    g = g_ref[0]

    row = lax.broadcasted_iota(jnp.int32, (C, H * dk), 0)
    tt = lax.broadcasted_iota(jnp.int32, (C, C), 0)
    ss = lax.broadcasted_iota(jnp.int32, (C, C), 1)

    g_hi = g.astype(BF16)
    g_mid = (g - g_hi.astype(F32)).astype(BF16)
    g_lo = (g - g_hi.astype(F32) - g_mid.astype(F32)).astype(BF16)
    tril = jnp.where(ss <= tt, 1.0, 0.0).astype(BF16)
    b3 = _dot(tril, jnp.concatenate([g_hi, g_mid, g_lo], axis=1))
    b = b3[:, :H * dk] + b3[:, H * dk:2 * H * dk] + b3[:, 2 * H * dk:]

    def roll_rows(x, shift):
        if shift % C < SUBLANES:
            x3 = x.reshape(C // SUBLANES, SUBLANES, x.shape[1])
            return pltpu.roll(x3, shift % C, 1).reshape(x.shape)
        if (C - shift % C) < SUBLANES:
            x3 = x.reshape(C // SUBLANES, SUBLANES, x.shape[1])
            return pltpu.roll(x3, SUBLANES - (C - shift % C), 1).reshape(x.shape)
        return pltpu.roll(x, shift, 0)

    txs = tt ^ ss
    level = jnp.full((C, C), -1, jnp.int32)
    for p in range(n_levels):
        level = level + (txs >= (1 << p)).astype(jnp.int32)
    level = jnp.where(ss <= tt, level, -2)

    heads_per_tile = LANES // dk
    lane = lax.broadcasted_iota(jnp.int32, (C, H * dk), 1)
    head_in_tile = jnp.right_shift(lane, dk.bit_length() - 1) & (heads_per_tile - 1)

    def split_heads(kk):
        return [jnp.where(head_in_tile == hh, kk, 0.0).astype(BF16) for hh in range(heads_per_tile)]

    def scores(q_bf, k_split, h):
        t, hh = divmod(h, heads_per_tile)
        cols = slice(t * LANES, (t + 1) * LANES)
        return _dot_nt(q_bf[:, cols], k_split[hh][:, cols])

    on_diag = level == -1
    q_bf, k_split = q.astype(BF16), split_heads(k)
    for h in range(H):
        att_scr[h] = jnp.where(on_diag, scores(q_bf, k_split, h), 0.0)

    block_end = b
    for p in range(n_levels):
        m = 1 << p
        second_half = (row & m) != 0
        r = jnp.where(second_half, roll_rows(block_end, m), block_end)
        decay = jnp.exp2(-jnp.abs(b - r))
        kd_split = split_heads(k * decay)
        if m >= BF16_ROWS:
            blocks = [slice(lo, lo + m) for lo in range(m, C, 2 * m)]
            take = lambda x: jnp.concatenate([x[rows] for rows in blocks], axis=0)
            qd_bf = (take(q) * take(decay)).astype(BF16)
            at_level = take(level) == p
            for h in range(H):
                s_lvl = scores(qd_bf, kd_split, h)
                for n, rows in enumerate(blocks):
                    sub = slice(n * m, (n + 1) * m)
                    att_scr[h, rows, :] = jnp.where(at_level[sub], s_lvl[sub], att_scr[h, rows, :])
        else:
            qd_bf = (q * decay).astype(BF16)
            at_level = level == p
            for h in range(H):
                att_scr[h] = jnp.where(at_level, scores(qd_bf, kd_split, h), att_scr[h])
        if p + 1 < n_levels:
            block_end = jnp.where(second_half, block_end, roll_rows(block_end, C - m))

    state = state_scr[...]
    o_inter = _dot((q * jnp.exp2(b)).astype(BF16), state.astype(BF16))

    nw = nw_ref[...]
    for h in range(H):
        o_h = o_inter[:, h * dv:(h + 1) * dv] + _dot(att_scr[h].astype(BF16),
                                                      v[:, h * dv:(h + 1) * dv])
        cols = slice(h * dv, (h + 1) * dv)
        o_ref[0, :, cols] = (_rmsnorm_rows(o_h, nw) * sg_ref[0, :, cols]).astype(BF16)

    bT = b.T
    kT = k.T
    b_last = bT[:, C - 1:C]
    kdT = (kT * jnp.exp2(b_last - bT)).astype(BF16)
    ds = _dot(kdT, v)
    srow = lax.broadcasted_iota(jnp.int32, (H * dk, H * dv), 0) // dk
    scol = lax.broadcasted_iota(jnp.int32, (H * dk, H * dv), 1) // dv
    state_scr[...] = jnp.exp2(b_last) * state + jnp.where(srow == scol, ds, 0.0)


def _gla(qb, kb, vb, g, sgb, nw):
    B, S, _ = qb.shape
    C = GLA_CHUNK
    spec = lambda width: pl.BlockSpec((1, C, width), lambda b, c: (b, c, 0))
    return pl.pallas_call(
        _gla_kernel,
        grid=(B, S // C),
        in_specs=[spec(B_KEY_WIDTH), spec(B_KEY_WIDTH), spec(B_WIDTH), spec(B_KEY_WIDTH),
                  spec(B_WIDTH), pl.BlockSpec((1, B_VAL_DIM), lambda b, c: (0, 0))],
        out_specs=spec(B_WIDTH),
        out_shape=jax.ShapeDtypeStruct((B, S, B_WIDTH), BF16),
        scratch_shapes=[pltpu.VMEM((B_KEY_WIDTH, B_WIDTH), F32),
                        pltpu.VMEM((B_HEADS, C, C), F32)],
        compiler_params=pltpu.CompilerParams(
            dimension_semantics=("arbitrary", "arbitrary"), vmem_limit_bytes=VMEM_LIMIT),
        name="gla",
    )(qb, kb, vb, g, sgb, nw)


def _out_kernel(x_ref, ma_ref, mb_ref, qc_ref, sgc_ref, km_ref, vm_ref, w_ref, fw_ref, o_ref,
                wo_ref):
    @pl.when((pl.program_id(0) == 0) & (pl.program_id(1) == 0))
    def _():
        for lo in range(0, MIX_WIDTH, 4 * LANES):
            wo_ref[lo:lo + 4 * LANES, :] = w_ref[0, lo:lo + 4 * LANES, :].astype(BF16)

    dh = C_HEAD_DIM
    qc = qc_ref[0]
    km = km_ref[0]
    vm = vm_ref[0]
    heads = [slice(h * dh, (h + 1) * dh) for h in range(C_HEADS)]
    scores = [_dot_nt(qc[:, sl], km[:, sl]) for sl in heads]
    h_ab = (_dot(ma_ref[0], wo_ref[0:A_WIDTH, :])
            + _dot(mb_ref[0], wo_ref[A_WIDTH:A_WIDTH + B_WIDTH, :]))
    oc = []
    for s, sl in zip(scores, heads):
        p = jnp.exp(s - jnp.max(s, axis=-1, keepdims=True))
        l = jnp.sum(p, axis=-1, keepdims=True)
        oc.append(_dot(p.astype(BF16), vm[:, sl]) / l)
    mc = (jnp.concatenate(oc, axis=-1) * sgc_ref[0]).astype(BF16)
    h_new = x_ref[0] + h_ab + _dot(mc, wo_ref[A_WIDTH + B_WIDTH:, :])
    o_ref[0] = _rmsnorm_rows(h_new, fw_ref[...])


def _out(x, ma, mb, qc, sgc, km, vm, w_out, fw):
    B, S, D = x.shape
    tm = OUT_TM
    row = lambda width: pl.BlockSpec((1, tm, width), lambda b, i: (b, i, 0))
    mem = pl.BlockSpec((1, MEM_LEN, C_WIDTH), lambda b, i: (b, 0, 0))
    return pl.pallas_call(
        _out_kernel,
        grid=(B, S // tm),
        in_specs=[row(D), row(A_WIDTH), row(B_WIDTH), row(C_WIDTH), row(C_WIDTH), mem, mem,
                  pl.BlockSpec(w_out.shape, lambda b, i: (0, 0, 0), pipeline_mode=pl.Buffered(1)),
                  pl.BlockSpec((1, D), lambda b, i: (0, 0))],
        out_specs=row(D),
        out_shape=jax.ShapeDtypeStruct((B, S, D), F32),
        scratch_shapes=[pltpu.VMEM((MIX_WIDTH, D), BF16)],
        compiler_params=pltpu.CompilerParams(
            dimension_semantics=("arbitrary", "arbitrary"), vmem_limit_bytes=VMEM_LIMIT),
        name="out",
    )(x, ma, mb, qc, sgc, km, vm, w_out, fw)


def kernel(x, mem, norm_w, w_in, w_alpha2, b_alpha, gla_norm_w, mem_norm_w, w_mem_kv, w_out,
           rel_bias, final_norm_w):
    assert norm_w.shape[0] == 1, "single layer"
    assert w_in.shape[2] == W_IN_ALIGNED + GLA_RANK + 2 * C_WIDTH
    wa = jnp.pad(w_alpha2[0], ((0, ZB_PAD - GLA_RANK), (0, 0)))

    km, vm = _memkv(mem, mem_norm_w[0][None, :], w_mem_kv)
    bias_own, bias_prev = _bias_tiles(rel_bias)
    (qaT, vaT, ka, sga, qb, kb, vb, sgb, qc, sgc, g) = _proj(
        x, norm_w[0][None, :], w_in[0].T, wa, b_alpha[0][None, :])
    ma = _moba(rel_bias, qaT, ka, vaT, bias_own, bias_prev, sga)
    mb = _gla(qb, kb, vb, g, sgb, gla_norm_w[0][None, :])
    return _out(x, ma, mb, qc, sgc, km, vm, w_out, final_norm_w[None, :])
```

```python
import functools
import math

import jax
import jax.numpy as jnp
from jax import lax
from jax.experimental import pallas as pl
from jax.experimental.pallas import tpu as pltpu

F32 = jnp.float32
BF16 = jnp.bfloat16

D_MODEL = 1024
MEM_LEN = 256
A_HEADS = 8
A_HEAD_DIM = 64
A_WIDTH = 512
MOBA_BLOCK = 256
MOBA_TOPK = 3
B_HEADS = 4
B_KEY_DIM = 64
B_VAL_DIM = 128
B_KEY_WIDTH = 256
B_WIDTH = 512
GLA_RANK = 16
GLA_TAU = 16.0
C_HEADS = 4
C_HEAD_DIM = 128
C_WIDTH = 512
MIX_WIDTH = A_WIDTH + B_WIDTH + C_WIDTH
REL_BUCKETS = 32
REL_MAX_DIST = 128
RMS_EPS = 1e-6
NEG = -1e30

LANES = 128
SUBLANES = 8
BF16_ROWS = 16
ZB_PAD = LANES
PROJ_TM = 512
OUT_TM = 512
GLA_CHUNK = 256
VMEM_LIMIT = 56 * 1024 * 1024
LOG2E = 1.0 / math.log(2.0)
HEADS_PER_TILE = LANES // A_HEAD_DIM
PV_ROWS = A_HEAD_DIM + 16
MOBA_MAX_BLOCKS = 16
QK_LOOKAHEAD = 4
FAR_UNROLL = 4


def _dot(a, b):
    return jnp.dot(a, b, preferred_element_type=F32)


def _dot_nt(a, b):
    return lax.dot_general(a, b, (((1,), (1,)), ((), ())), preferred_element_type=F32)


def _rmsnorm_rows(x, w):
    return x * lax.rsqrt(jnp.mean(x * x, axis=-1, keepdims=True) + RMS_EPS) * w


def _silu(x):
    return x * jax.nn.sigmoid(x)


def _memkv_kernel(mem_ref, nw_ref, w_ref, km_ref, vm_ref):
    u = _rmsnorm_rows(mem_ref[0], nw_ref[...]).astype(BF16)
    km_ref[0] = _dot(u, w_ref[0, :, :C_WIDTH].astype(BF16)).astype(BF16)
    vm_ref[0] = _dot(u, w_ref[0, :, C_WIDTH:].astype(BF16)).astype(BF16)


def _memkv(mem, nw, w_mem_kv):
    B = mem.shape[0]
    return pl.pallas_call(
        _memkv_kernel,
        grid=(B,),
        in_specs=[
            pl.BlockSpec((1, MEM_LEN, D_MODEL), lambda b: (b, 0, 0)),
            pl.BlockSpec((1, D_MODEL), lambda b: (0, 0)),
            pl.BlockSpec(w_mem_kv.shape, lambda b: (0, 0, 0)),
        ],
        out_specs=[
            pl.BlockSpec((1, MEM_LEN, C_WIDTH), lambda b: (b, 0, 0)),
            pl.BlockSpec((1, MEM_LEN, C_WIDTH), lambda b: (b, 0, 0)),
        ],
        out_shape=[jax.ShapeDtypeStruct((B, MEM_LEN, C_WIDTH), BF16)] * 2,
        name="memkv",
    )(mem, nw, w_mem_kv)


def _t5_bucket(n):
    max_exact = REL_BUCKETS // 2
    nf = jnp.maximum(n, max_exact).astype(F32)
    large = max_exact + jnp.floor(jnp.log(nf / max_exact) / math.log(REL_MAX_DIST / max_exact)
                                  * (REL_BUCKETS - max_exact)).astype(jnp.int32)
    large = jnp.minimum(large, REL_BUCKETS - 1)
    return jnp.where(n < max_exact, n, large)


def _bias_kernel(rb_ref, own_ref, prev_ref):
    h = pl.program_id(0)
    key = lax.broadcasted_iota(jnp.int32, (MOBA_BLOCK, MOBA_BLOCK), 0)
    qry = lax.broadcasted_iota(jnp.int32, (MOBA_BLOCK, MOBA_BLOCK), 1)
    rel = qry - key
    b_own = _t5_bucket(jnp.maximum(rel, 0))
    b_prev = _t5_bucket(rel + MOBA_BLOCK)
    own = jnp.zeros((MOBA_BLOCK, MOBA_BLOCK), F32)
    prev = jnp.zeros((MOBA_BLOCK, MOBA_BLOCK), F32)
    for bk in range(REL_BUCKETS):
        val = rb_ref[bk, h] * LOG2E
        own = jnp.where(b_own == bk, val, own)
        prev = jnp.where(b_prev == bk, val, prev)
    own_ref[0] = jnp.where(rel >= 0, own, NEG)
    prev_ref[0] = prev


def _bias_tiles(rel_bias):
    return pl.pallas_call(
        _bias_kernel,
        grid=(A_HEADS,),
        in_specs=[pl.BlockSpec(memory_space=pltpu.SMEM)],
        out_specs=[pl.BlockSpec((1, MOBA_BLOCK, MOBA_BLOCK), lambda h: (h, 0, 0))] * 2,
        out_shape=[jax.ShapeDtypeStruct((A_HEADS, MOBA_BLOCK, MOBA_BLOCK), F32)] * 2,
        name="t5bias",
    )(rel_bias)


_ROW_COLS = (("qa", A_WIDTH), ("ka", A_WIDTH), ("va", A_WIDTH), ("ga", A_WIDTH),
             ("qb", B_KEY_WIDTH), ("kb", B_KEY_WIDTH), ("vb", B_WIDTH), ("gb", B_WIDTH),
             ("qc", C_WIDTH), ("gc", C_WIDTH), ("zb", ZB_PAD))
_ROW_OFF = {}
_off = 0
for _name, _width in _ROW_COLS:
    _ROW_OFF[_name] = (_off, _off + _width)
    _off += _width
ROW_COLS_TOTAL = _off


W_IN_ALIGNED = 4 * A_WIDTH + 2 * B_KEY_WIDTH + 2 * B_WIDTH


def _proj_kernel(x_ref, nw_ref, w_ref, wa_ref, ba_ref,
                 qaT_ref, vaT_ref, ka_ref, ga_ref, qb_ref, kb_ref, vb_ref, gb_ref,
                 qc_ref, gc_ref, g_ref, wr_ref):
    @pl.when((pl.program_id(0) == 0) & (pl.program_id(1) == 0))
    def _():
        chunk = 4 * LANES
        for lo in range(0, W_IN_ALIGNED, chunk):
            wr_ref[lo:lo + chunk, :] = w_ref[lo:lo + chunk, :].astype(BF16)
        tail = W_IN_ALIGNED + GLA_RANK
        for lo in range(0, 2 * C_WIDTH, chunk):
            wr_ref[W_IN_ALIGNED + lo:W_IN_ALIGNED + lo + chunk, :] = (
                w_ref[tail + lo:tail + lo + chunk, :].astype(BF16))
        zb_lo = _ROW_OFF["zb"][0]
        wr_ref[zb_lo:zb_lo + GLA_RANK, :] = w_ref[W_IN_ALIGNED:tail, :].astype(BF16)
        wr_ref[zb_lo + GLA_RANK:, :] = jnp.zeros((ZB_PAD - GLA_RANK, wr_ref.shape[1]), BF16)

    u = _rmsnorm_rows(x_ref[0], nw_ref[...]).astype(BF16)

    def row(name):
        lo, hi = _ROW_OFF[name]
        return _dot_nt(u, wr_ref[lo:hi, :])

    def col(name):
        lo, hi = _ROW_OFF[name]
        return _dot_nt(wr_ref[lo:hi, :], u)

    zb = row("zb")
    zb_hi = zb.astype(BF16)
    zb_lo = (zb - zb_hi.astype(F32)).astype(BF16)
    wa = wa_ref[...]
    wa_hi = wa.astype(BF16)
    wa_lo = (wa - wa_hi.astype(F32)).astype(BF16)
    z = _dot(jnp.concatenate([zb_hi, zb_lo, zb_hi], axis=1),
             jnp.concatenate([wa_hi, wa_hi, wa_lo], axis=0)) + ba_ref[...]
    g_ref[0] = (jnp.minimum(z, 0.0) - jnp.log1p(jnp.exp(-jnp.abs(z)))) * (LOG2E / GLA_TAU)

    qaT_ref[0] = col("qa")
    vaT = col("va").astype(BF16)
    ones = jnp.ones((PV_ROWS - A_HEAD_DIM, vaT.shape[1]), BF16)
    pieces = []
    for h in range(A_HEADS):
        pieces += [vaT[h * A_HEAD_DIM:(h + 1) * A_HEAD_DIM], ones]
    vaT_ref[0] = jnp.concatenate(pieces, axis=0)

    ka = row("ka")
    tm = ka.shape[0]
    nb = MOBA_MAX_BLOCKS
    key_pos = pl.program_id(1) * tm + lax.broadcasted_iota(jnp.int32, (tm, LANES - A_HEAD_DIM), 0)
    blk_id = jnp.right_shift(key_pos, MOBA_BLOCK.bit_length() - 1)
    col = lax.broadcasted_iota(jnp.int32, (tm, LANES - A_HEAD_DIM), 1)
    onehot = jnp.where((col < 2 * nb) & ((col & (nb - 1)) == blk_id), 1.0, 0.0)
    pieces = []
    for h in range(A_HEADS):
        pieces += [ka[:, h * A_HEAD_DIM:(h + 1) * A_HEAD_DIM], onehot]
    ka_ref[0] = jnp.concatenate(pieces, axis=1).astype(BF16)
    ga_ref[0] = _silu(row("ga")).astype(BF16)
    qb_ref[0] = (row("qb") * (B_KEY_DIM ** -0.5)).astype(BF16)
    kb_ref[0] = row("kb").astype(BF16)
    vb_ref[0] = row("vb").astype(BF16)
    gb_ref[0] = _silu(row("gb")).astype(BF16)
    qc_ref[0] = (row("qc") * (C_HEAD_DIM ** -0.5)).astype(BF16)
    gc_ref[0] = _silu(row("gc")).astype(BF16)


def _proj(x, nw, w_in, wa, ba):
    B, S, D = x.shape
    tm = PROJ_TM
    row_spec = lambda width: pl.BlockSpec((1, tm, width), lambda b, i: (b, i, 0))
    col_spec = lambda rows: pl.BlockSpec((1, rows, tm), lambda b, i: (b, 0, i))
    const = lambda shape: pl.BlockSpec(shape, lambda b, i: (0,) * len(shape))
    sds = jax.ShapeDtypeStruct
    return pl.pallas_call(
        _proj_kernel,
        grid=(B, S // tm),
        in_specs=[
            pl.BlockSpec((1, tm, D), lambda b, i: (b, i, 0)),
            const((1, D)),
            pl.BlockSpec(w_in.shape, lambda b, i: (0, 0), pipeline_mode=pl.Buffered(1)),
            const((ZB_PAD, B_KEY_WIDTH)),
            const((1, B_KEY_WIDTH)),
        ],
        out_specs=[col_spec(A_WIDTH), col_spec(A_HEADS * PV_ROWS), row_spec(A_HEADS * LANES),
                   row_spec(A_WIDTH),
                   row_spec(B_KEY_WIDTH), row_spec(B_KEY_WIDTH), row_spec(B_WIDTH),
                   row_spec(B_WIDTH), row_spec(C_WIDTH), row_spec(C_WIDTH),
                   row_spec(B_KEY_WIDTH)],
        out_shape=[sds((B, A_WIDTH, S), F32), sds((B, A_HEADS * PV_ROWS, S), BF16),
                   sds((B, S, A_HEADS * LANES), BF16), sds((B, S, A_WIDTH), BF16),
                   sds((B, S, B_KEY_WIDTH), BF16), sds((B, S, B_KEY_WIDTH), BF16),
                   sds((B, S, B_WIDTH), BF16), sds((B, S, B_WIDTH), BF16),
                   sds((B, S, C_WIDTH), BF16), sds((B, S, C_WIDTH), BF16),
                   sds((B, S, B_KEY_WIDTH), F32)],
        scratch_shapes=[pltpu.VMEM((ROW_COLS_TOTAL, D), BF16)],
        compiler_params=pltpu.CompilerParams(
            dimension_semantics=("arbitrary", "arbitrary"), vmem_limit_bytes=VMEM_LIMIT),
        name="proj",
    )(x, nw, w_in, wa, ba)


def _moba_kernel(rb_ref, qT_ref, k_ref, vT_ref, own_ref, prev_ref, sg_ref, o_ref,
                 kmean_scr, kmbd_scr, qo_scr, qh_scr, m_scr, acc_scr, s_scr):
    i = pl.program_id(1)
    nb = kmean_scr.shape[0]
    blk = MOBA_BLOCK
    dh = A_HEAD_DIM
    n_tiles = A_HEADS // HEADS_PER_TILE

    @pl.when(i == 0)
    def _():
        for n in range(nb):
            kb = jnp.concatenate(
                [k_ref[0, n * blk:(n + 1) * blk, h * LANES:h * LANES + dh] for h in range(A_HEADS)],
                axis=1).astype(F32)
            kmean_scr[n:n + 1, :] = jnp.mean(kb, axis=0, keepdims=True)
        lane_head = jnp.right_shift(lax.broadcasted_iota(jnp.int32, (nb, A_WIDTH), 1),
                                    dh.bit_length() - 1)
        for h in range(A_HEADS):
            km = jnp.where(lane_head == h, kmean_scr[...], 0.0)
            km_hi = km.astype(BF16)
            km_lo = (km - km_hi.astype(F32)).astype(BF16)
            kmbd_scr[h * nb:(h + 1) * nb, :] = jnp.concatenate([km_hi, km_lo, km_hi], axis=1)

    q_f32 = qT_ref[0]
    q_hi = q_f32.astype(BF16)
    q_lo = (q_f32 - q_hi.astype(F32)).astype(BF16)
    gate_all = _dot(kmbd_scr[...], jnp.concatenate([q_hi, q_hi, q_lo], axis=0))
    brow = lax.broadcasted_iota(jnp.int32, (nb, blk), 0)
    browf = brow.astype(F32)
    zeros = jnp.zeros((LANES - dh - 2 * nb, blk), F32)

    for h in range(A_HEADS):
        q_h = qT_ref[0, h * dh:(h + 1) * dh, :] * (dh ** -0.5 * LOG2E)
        qo_scr[h] = jnp.concatenate([q_h, jnp.zeros((LANES - dh, blk), F32)], axis=0).astype(BF16)

    for h in range(A_HEADS):
        g = jnp.where(brow < i, gate_all[h * nb:(h + 1) * nb], NEG)
        sel = brow < 0
        for _ in range(MOBA_TOPK):
            m = jnp.max(g, axis=0, keepdims=True)
            idx = jnp.min(jnp.where(g == m, browf, float(nb)), axis=0, keepdims=True)
            pick = browf == idx
            sel = sel | pick
            g = jnp.where(pick, -jnp.inf, g)
        valid = sel & (brow < i)
        far_bias = rb_ref[REL_BUCKETS - 1, h] * LOG2E
        pen = jnp.where(brow < i - 1, jnp.where(valid, far_bias, NEG),
                        jnp.where(brow == i - 1, jnp.where(valid, 0.0, NEG), 0.0))
        pen_hi = pen.astype(BF16).astype(F32)
        q_h = qT_ref[0, h * dh:(h + 1) * dh, :] * (dh ** -0.5 * LOG2E)
        qh_scr[h] = jnp.concatenate([q_h, pen_hi, pen - pen_hi, zeros], axis=0).astype(BF16)

    no_prev = jnp.where(i == 0, NEG, 0.0)

    def logits(j, h, kind):
        rows = pl.ds(pl.multiple_of(j * blk, blk), blk)
        keys = k_ref[0, rows, h * LANES:(h + 1) * LANES]
        if kind == "own":
            return _dot(keys, qo_scr[h]) + own_ref[h]
        s = _dot(keys, qh_scr[h])
        if kind == "prev":
            return s + (prev_ref[h] + no_prev)
        return s

    def fold(j, h, s, first):
        rows = pl.ds(pl.multiple_of(j * blk, blk), blk)
        bm = jnp.max(s, axis=0, keepdims=True)
        vj = vT_ref[0, h * PV_ROWS:(h + 1) * PV_ROWS, rows]
        if first:
            m_scr[h] = bm
            acc_scr[h] = _dot(vj, jnp.exp2(s - bm).astype(BF16))
        else:
            m_old = m_scr[h]
            m_new = jnp.maximum(m_old, bm)
            m_scr[h] = m_new
            acc_scr[h] = (jnp.exp2(m_old - m_new) * acc_scr[h]
                          + _dot(vj, jnp.exp2(s - m_new).astype(BF16)))

    n_far = i - 1
    j_prev = jnp.maximum(i - 1, 0)
    items = ([(i, "own", h) for h in range(A_HEADS)] + [(j_prev, "prev", h) for h in range(A_HEADS)]
             + [(0, "far", h) for h in range(QK_LOOKAHEAD)])
    n_fold = 2 * A_HEADS
    pending = []

    def issue(n):
        j, kind, h = items[n]
        s = logits(j, h, kind)
        if n < n_fold:
            pending.append(s)
        else:
            s_scr[h] = s

    for n in range(QK_LOOKAHEAD):
        issue(n)
    for n in range(n_fold):
        issue(n + QK_LOOKAHEAD)
        j, kind, h = items[n]
        fold(j, h, pending.pop(0), kind == "own")

    def far_group(j0, n_blocks):
        j_after = jnp.minimum(j0 + n_blocks, n_far - 1)
        stream = ([(j0 + t, h) for t in range(n_blocks) for h in range(A_HEADS)]
                  + [(j_after, h) for h in range(QK_LOOKAHEAD)])
        n_items = n_blocks * A_HEADS
        pending = [s_scr[h] for h in range(QK_LOOKAHEAD)]
        for n in range(n_items):
            j, h = stream[n + QK_LOOKAHEAD]
            if n + QK_LOOKAHEAD < n_items:
                pending.append(logits(j, h, "far"))
            else:
                s_scr[h] = logits(j, h, "far")
            j, h = stream[n]
            fold(j, h, pending.pop(0), False)

    def far_groups(g, carry):
        far_group(g * FAR_UNROLL, FAR_UNROLL)
        return carry

    n_groups = jnp.maximum(n_far, 0) // FAR_UNROLL
    lax.fori_loop(0, n_groups, far_groups, 0)
    for rest in range(1, FAR_UNROLL):
        @pl.when(n_far - n_groups * FAR_UNROLL == rest)
        def _():
            far_group(n_groups * FAR_UNROLL, rest)

    for t in range(n_tiles):
        oT = []
        for hh in range(HEADS_PER_TILE):
            acc = acc_scr[t * HEADS_PER_TILE + hh]
            oT.append(acc[:dh] / acc[dh:dh + 1])
        cols = slice(t * LANES, (t + 1) * LANES)
        o_ref[0, :, cols] = (jnp.concatenate(oT, axis=0).T * sg_ref[0, :, cols]).astype(BF16)


def _moba(rel_bias, qaT, ka, vaT, bias_own, bias_prev, sga):
    B, S, _ = ka.shape
    nb = S // MOBA_BLOCK
    assert nb == MOBA_MAX_BLOCKS
    bias_spec = pl.BlockSpec((A_HEADS, MOBA_BLOCK, MOBA_BLOCK), lambda b, i: (0, 0, 0))
    return pl.pallas_call(
        _moba_kernel,
        grid=(B, nb),
        in_specs=[
            pl.BlockSpec(memory_space=pltpu.SMEM),
            pl.BlockSpec((1, A_WIDTH, MOBA_BLOCK), lambda b, i: (b, 0, i)),
            pl.BlockSpec((1, S, A_HEADS * LANES), lambda b, i: (b, 0, 0)),
            pl.BlockSpec((1, A_HEADS * PV_ROWS, S), lambda b, i: (b, 0, 0)),
            bias_spec, bias_spec,
            pl.BlockSpec((1, MOBA_BLOCK, A_WIDTH), lambda b, i: (b, i, 0)),
        ],
        out_specs=pl.BlockSpec((1, MOBA_BLOCK, A_WIDTH), lambda b, i: (b, i, 0)),
        out_shape=jax.ShapeDtypeStruct((B, S, A_WIDTH), BF16),
        scratch_shapes=[pltpu.VMEM((nb, A_WIDTH), F32),
                        pltpu.VMEM((A_HEADS * nb, 3 * A_WIDTH), BF16),
                        pltpu.VMEM((A_HEADS, LANES, MOBA_BLOCK), BF16),
                        pltpu.VMEM((A_HEADS, LANES, MOBA_BLOCK), BF16),
                        pltpu.VMEM((A_HEADS, 1, MOBA_BLOCK), F32),
                        pltpu.VMEM((A_HEADS, PV_ROWS, MOBA_BLOCK), F32),
                        pltpu.VMEM((QK_LOOKAHEAD, MOBA_BLOCK, MOBA_BLOCK), F32)],
        compiler_params=pltpu.CompilerParams(
            dimension_semantics=("arbitrary", "arbitrary"),
            vmem_limit_bytes=VMEM_LIMIT),
        name="moba",
    )(rel_bias, qaT, ka, vaT, bias_own, bias_prev, sga)


def _gla_kernel(q_ref, k_ref, v_ref, g_ref, sg_ref, nw_ref, o_ref, state_scr, att_scr):
    c = pl.program_id(1)
    C = GLA_CHUNK
    dk, dv, H = B_KEY_DIM, B_VAL_DIM, B_HEADS
    n_levels = C.bit_length() - 1

    @pl.when(c == 0)
    def _():
        state_scr[...] = jnp.zeros_like(state_scr)

    q = q_ref[0].astype(F32)
    k = k_ref[0].astype(F32)
    v = v_ref[0]
    g = g_ref[0]

    row = lax.broadcasted_iota(jnp.int32, (C, H * dk), 0)
    tt = lax.broadcasted_iota(jnp.int32, (C, C), 0)
    ss = lax.broadcasted_iota(jnp.int32, (C, C), 1)

    g_hi = g.astype(BF16)
    g_mid = (g - g_hi.astype(F32)).astype(BF16)
    g_lo = (g - g_hi.astype(F32) - g_mid.astype(F32)).astype(BF16)
    tril = jnp.where(ss <= tt, 1.0, 0.0).astype(BF16)
    b3 = _dot(tril, jnp.concatenate([g_hi, g_mid, g_lo], axis=1))
    b = b3[:, :H * dk] + b3[:, H * dk:2 * H * dk] + b3[:, 2 * H * dk:]

    def roll_rows(x, shift):
        if shift % C < SUBLANES:
            x3 = x.reshape(C // SUBLANES, SUBLANES, x.shape[1])
            return pltpu.roll(x3, shift % C, 1).reshape(x.shape)
        if (C - shift % C) < SUBLANES:
            x3 = x.reshape(C // SUBLANES, SUBLANES, x.shape[1])
            return pltpu.roll(x3, SUBLANES - (C - shift % C), 1).reshape(x.shape)
        return pltpu.roll(x, shift, 0)

    txs = tt ^ ss
    level = jnp.full((C, C), -1, jnp.int32)
    for p in range(n_levels):
        level = level + (txs >= (1 << p)).astype(jnp.int32)
    level = jnp.where(ss <= tt, level, -2)

    heads_per_tile = LANES // dk
    lane = lax.broadcasted_iota(jnp.int32, (C, H * dk), 1)
    head_in_tile = jnp.right_shift(lane, dk.bit_length() - 1) & (heads_per_tile - 1)

    def split_heads(kk):
        return [jnp.where(head_in_tile == hh, kk, 0.0).astype(BF16) for hh in range(heads_per_tile)]

    def scores(q_bf, k_split, h):
        t, hh = divmod(h, heads_per_tile)
        cols = slice(t * LANES, (t + 1) * LANES)
        return _dot_nt(q_bf[:, cols], k_split[hh][:, cols])

    on_diag = level == -1
    q_bf, k_split = q.astype(BF16), split_heads(k)
    for h in range(H):
        att_scr[h] = jnp.where(on_diag, scores(q_bf, k_split, h), 0.0)

    block_end = b
    for p in range(n_levels):
        m = 1 << p
        second_half = (row & m) != 0
        r = jnp.where(second_half, roll_rows(block_end, m), block_end)
        decay = jnp.exp2(-jnp.abs(b - r))
        kd_split = split_heads(k * decay)
        if m >= BF16_ROWS:
            blocks = [slice(lo, lo + m) for lo in range(m, C, 2 * m)]
            take = lambda x: jnp.concatenate([x[rows] for rows in blocks], axis=0)
            qd_bf = (take(q) * take(decay)).astype(BF16)
            for h in range(H):
                s_lvl = scores(qd_bf, kd_split, h)
                for n, rows in enumerate(blocks):
                    keys = slice(rows.start - m, rows.start)
                    att_scr[h, rows, keys] = s_lvl[n * m:(n + 1) * m, keys]
        else:
            qd_bf = (q * decay).astype(BF16)
            at_level = level == p
            for h in range(H):
                att_scr[h] = jnp.where(at_level, scores(qd_bf, kd_split, h), att_scr[h])
        if p + 1 < n_levels:
            block_end = jnp.where(second_half, block_end, roll_rows(block_end, C - m))

    state = state_scr[...]
    o_inter = _dot((q * jnp.exp2(b)).astype(BF16), state.astype(BF16))

    nw = nw_ref[...]
    for h in range(H):
        o_h = o_inter[:, h * dv:(h + 1) * dv] + _dot(att_scr[h].astype(BF16),
                                                      v[:, h * dv:(h + 1) * dv])
        cols = slice(h * dv, (h + 1) * dv)
        o_ref[0, :, cols] = (_rmsnorm_rows(o_h, nw) * sg_ref[0, :, cols]).astype(BF16)

    bT = b.T
    kT = k.T
    b_last = bT[:, C - 1:C]
    kdT = (kT * jnp.exp2(b_last - bT)).astype(BF16)
    ds = _dot(kdT, v)
    srow = lax.broadcasted_iota(jnp.int32, (H * dk, H * dv), 0) // dk
    scol = lax.broadcasted_iota(jnp.int32, (H * dk, H * dv), 1) // dv
    state_scr[...] = jnp.exp2(b_last) * state + jnp.where(srow == scol, ds, 0.0)


def _gla(qb, kb, vb, g, sgb, nw):
    B, S, _ = qb.shape
    C = GLA_CHUNK
    spec = lambda width: pl.BlockSpec((1, C, width), lambda b, c: (b, c, 0))
    return pl.pallas_call(
        _gla_kernel,
        grid=(B, S // C),
        in_specs=[spec(B_KEY_WIDTH), spec(B_KEY_WIDTH), spec(B_WIDTH), spec(B_KEY_WIDTH),
                  spec(B_WIDTH), pl.BlockSpec((1, B_VAL_DIM), lambda b, c: (0, 0))],
        out_specs=spec(B_WIDTH),
        out_shape=jax.ShapeDtypeStruct((B, S, B_WIDTH), BF16),
        scratch_shapes=[pltpu.VMEM((B_KEY_WIDTH, B_WIDTH), F32),
                        pltpu.VMEM((B_HEADS, C, C), F32)],
        compiler_params=pltpu.CompilerParams(
            dimension_semantics=("arbitrary", "arbitrary"), vmem_limit_bytes=VMEM_LIMIT),
        name="gla",
    )(qb, kb, vb, g, sgb, nw)


def _out_kernel(x_ref, ma_ref, mb_ref, qc_ref, sgc_ref, km_ref, vm_ref, w_ref, fw_ref, o_ref,
                wo_ref):
    @pl.when((pl.program_id(0) == 0) & (pl.program_id(1) == 0))
    def _():
        for lo in range(0, MIX_WIDTH, 4 * LANES):
            wo_ref[lo:lo + 4 * LANES, :] = w_ref[0, lo:lo + 4 * LANES, :].astype(BF16)

    dh = C_HEAD_DIM
    qc = qc_ref[0]
    km = km_ref[0]
    vm = vm_ref[0]
    heads = [slice(h * dh, (h + 1) * dh) for h in range(C_HEADS)]
    scores = [_dot_nt(qc[:, sl], km[:, sl]) for sl in heads]
    h_ab = (_dot(ma_ref[0], wo_ref[0:A_WIDTH, :])
            + _dot(mb_ref[0], wo_ref[A_WIDTH:A_WIDTH + B_WIDTH, :]))
    oc = []
    for s, sl in zip(scores, heads):
        p = jnp.exp(s - jnp.max(s, axis=-1, keepdims=True))
        l = jnp.sum(p, axis=-1, keepdims=True)
        oc.append(_dot(p.astype(BF16), vm[:, sl]) / l)
    mc = (jnp.concatenate(oc, axis=-1) * sgc_ref[0]).astype(BF16)
    h_new = x_ref[0] + h_ab + _dot(mc, wo_ref[A_WIDTH + B_WIDTH:, :])
    o_ref[0] = _rmsnorm_rows(h_new, fw_ref[...])


def _out(x, ma, mb, qc, sgc, km, vm, w_out, fw):
    B, S, D = x.shape
    tm = OUT_TM
    row = lambda width: pl.BlockSpec((1, tm, width), lambda b, i: (b, i, 0))
    mem = pl.BlockSpec((1, MEM_LEN, C_WIDTH), lambda b, i: (b, 0, 0))
    return pl.pallas_call(
        _out_kernel,
        grid=(B, S // tm),
        in_specs=[row(D), row(A_WIDTH), row(B_WIDTH), row(C_WIDTH), row(C_WIDTH), mem, mem,
                  pl.BlockSpec(w_out.shape, lambda b, i: (0, 0, 0), pipeline_mode=pl.Buffered(1)),
                  pl.BlockSpec((1, D), lambda b, i: (0, 0))],
        out_specs=row(D),
        out_shape=jax.ShapeDtypeStruct((B, S, D), F32),
        scratch_shapes=[pltpu.VMEM((MIX_WIDTH, D), BF16)],
        compiler_params=pltpu.CompilerParams(
            dimension_semantics=("arbitrary", "arbitrary"), vmem_limit_bytes=VMEM_LIMIT),
        name="out",
    )(x, ma, mb, qc, sgc, km, vm, w_out, fw)


def kernel(x, mem, norm_w, w_in, w_alpha2, b_alpha, gla_norm_w, mem_norm_w, w_mem_kv, w_out,
           rel_bias, final_norm_w):
    assert norm_w.shape[0] == 1, "single layer"
    assert w_in.shape[2] == W_IN_ALIGNED + GLA_RANK + 2 * C_WIDTH
    wa = jnp.pad(w_alpha2[0], ((0, ZB_PAD - GLA_RANK), (0, 0)))

    km, vm = _memkv(mem, mem_norm_w[0][None, :], w_mem_kv)
    bias_own, bias_prev = _bias_tiles(rel_bias)
    (qaT, vaT, ka, sga, qb, kb, vb, sgb, qc, sgc, g) = _proj(
        x, norm_w[0][None, :], w_in[0].T, wa, b_alpha[0][None, :])
    ma = _moba(rel_bias, qaT, ka, vaT, bias_own, bias_prev, sga)
    mb = _gla(qb, kb, vb, g, sgb, gla_norm_w[0][None, :])
    return _out(x, ma, mb, qc, sgc, km, vm, w_out, final_norm_w[None, :])
```

```python
import functools
import math

import jax
import jax.numpy as jnp
from jax import lax
from jax.experimental import pallas as pl
from jax.experimental.pallas import tpu as pltpu

F32 = jnp.float32
BF16 = jnp.bfloat16

D_MODEL = 1024
MEM_LEN = 256
A_HEADS = 8
A_HEAD_DIM = 64
A_WIDTH = 512
MOBA_BLOCK = 256
MOBA_TOPK = 3
B_HEADS = 4
B_KEY_DIM = 64
B_VAL_DIM = 128
B_KEY_WIDTH = 256
B_WIDTH = 512
GLA_RANK = 16
GLA_TAU = 16.0
C_HEADS = 4
C_HEAD_DIM = 128
C_WIDTH = 512
MIX_WIDTH = A_WIDTH + B_WIDTH + C_WIDTH
REL_BUCKETS = 32
REL_MAX_DIST = 128
RMS_EPS = 1e-6
NEG = -1e30

LANES = 128
SUBLANES = 8
ZB_PAD = LANES
PROJ_TM = 512
OUT_TM = 512
GLA_CHUNK = 256
VMEM_LIMIT = 56 * 1024 * 1024
LOG2E = 1.0 / math.log(2.0)
HEADS_PER_TILE = LANES // A_HEAD_DIM
PV_ROWS = A_HEAD_DIM + 16
MOBA_MAX_BLOCKS = 16
QK_LOOKAHEAD = 4
FAR_UNROLL = 4


def _dot(a, b):
    return jnp.dot(a, b, preferred_element_type=F32)


def _dot_nt(a, b):
    return lax.dot_general(a, b, (((1,), (1,)), ((), ())), preferred_element_type=F32)


def _rmsnorm_rows(x, w):
    return x * lax.rsqrt(jnp.mean(x * x, axis=-1, keepdims=True) + RMS_EPS) * w


def _silu(x):
    return x * jax.nn.sigmoid(x)


def _memkv_kernel(mem_ref, nw_ref, w_ref, km_ref, vm_ref):
    u = _rmsnorm_rows(mem_ref[0], nw_ref[...]).astype(BF16)
    km_ref[0] = _dot(u, w_ref[0, :, :C_WIDTH].astype(BF16)).astype(BF16)
    vm_ref[0] = _dot(u, w_ref[0, :, C_WIDTH:].astype(BF16)).astype(BF16)


def _memkv(mem, nw, w_mem_kv):
    B = mem.shape[0]
    return pl.pallas_call(
        _memkv_kernel,
        grid=(B,),
        in_specs=[
            pl.BlockSpec((1, MEM_LEN, D_MODEL), lambda b: (b, 0, 0)),
            pl.BlockSpec((1, D_MODEL), lambda b: (0, 0)),
            pl.BlockSpec(w_mem_kv.shape, lambda b: (0, 0, 0)),
        ],
        out_specs=[
            pl.BlockSpec((1, MEM_LEN, C_WIDTH), lambda b: (b, 0, 0)),
            pl.BlockSpec((1, MEM_LEN, C_WIDTH), lambda b: (b, 0, 0)),
        ],
        out_shape=[jax.ShapeDtypeStruct((B, MEM_LEN, C_WIDTH), BF16)] * 2,
        name="memkv",
    )(mem, nw, w_mem_kv)


def _t5_bucket(n):
    max_exact = REL_BUCKETS // 2
    nf = jnp.maximum(n, max_exact).astype(F32)
    large = max_exact + jnp.floor(jnp.log(nf / max_exact) / math.log(REL_MAX_DIST / max_exact)
                                  * (REL_BUCKETS - max_exact)).astype(jnp.int32)
    large = jnp.minimum(large, REL_BUCKETS - 1)
    return jnp.where(n < max_exact, n, large)


def _bias_kernel(rb_ref, own_ref, prev_ref):
    h = pl.program_id(0)
    key = lax.broadcasted_iota(jnp.int32, (MOBA_BLOCK, MOBA_BLOCK), 0)
    qry = lax.broadcasted_iota(jnp.int32, (MOBA_BLOCK, MOBA_BLOCK), 1)
    rel = qry - key
    b_own = _t5_bucket(jnp.maximum(rel, 0))
    b_prev = _t5_bucket(rel + MOBA_BLOCK)
    own = jnp.zeros((MOBA_BLOCK, MOBA_BLOCK), F32)
    prev = jnp.zeros((MOBA_BLOCK, MOBA_BLOCK), F32)
    for bk in range(REL_BUCKETS):
        val = rb_ref[bk, h] * LOG2E
        own = jnp.where(b_own == bk, val, own)
        prev = jnp.where(b_prev == bk, val, prev)
    own_ref[0] = jnp.where(rel >= 0, own, NEG)
    prev_ref[0] = prev


def _bias_tiles(rel_bias):
    return pl.pallas_call(
        _bias_kernel,
        grid=(A_HEADS,),
        in_specs=[pl.BlockSpec(memory_space=pltpu.SMEM)],
        out_specs=[pl.BlockSpec((1, MOBA_BLOCK, MOBA_BLOCK), lambda h: (h, 0, 0))] * 2,
        out_shape=[jax.ShapeDtypeStruct((A_HEADS, MOBA_BLOCK, MOBA_BLOCK), F32)] * 2,
        name="t5bias",
    )(rel_bias)


_ROW_COLS = (("qa", A_WIDTH), ("ka", A_WIDTH), ("va", A_WIDTH), ("ga", A_WIDTH),
             ("qb", B_KEY_WIDTH), ("kb", B_KEY_WIDTH), ("vb", B_WIDTH), ("gb", B_WIDTH),
             ("qc", C_WIDTH), ("gc", C_WIDTH), ("zb", ZB_PAD))
_ROW_OFF = {}
_off = 0
for _name, _width in _ROW_COLS:
    _ROW_OFF[_name] = (_off, _off + _width)
    _off += _width
ROW_COLS_TOTAL = _off


W_IN_ALIGNED = 4 * A_WIDTH + 2 * B_KEY_WIDTH + 2 * B_WIDTH


def _proj_kernel(x_ref, nw_ref, w_ref, wa_ref, ba_ref, gnw_ref,
                 qaT_ref, vaT_ref, ka_ref, ga_ref, qc_ref, gc_ref, mb_ref,
                 wr_ref, qb_ref, kb_ref, vb_ref, gb_ref, g_ref, state_scr, att_scr):
    @pl.when(pl.program_id(1) == 0)
    def _():
        state_scr[...] = jnp.zeros_like(state_scr)

    @pl.when((pl.program_id(0) == 0) & (pl.program_id(1) == 0))
    def _():
        chunk = 4 * LANES
        for lo in range(0, W_IN_ALIGNED, chunk):
            wr_ref[lo:lo + chunk, :] = w_ref[lo:lo + chunk, :].astype(BF16)
        tail = W_IN_ALIGNED + GLA_RANK
        for lo in range(0, 2 * C_WIDTH, chunk):
            wr_ref[W_IN_ALIGNED + lo:W_IN_ALIGNED + lo + chunk, :] = (
                w_ref[tail + lo:tail + lo + chunk, :].astype(BF16))
        zb_lo = _ROW_OFF["zb"][0]
        wr_ref[zb_lo:zb_lo + GLA_RANK, :] = w_ref[W_IN_ALIGNED:tail, :].astype(BF16)
        wr_ref[zb_lo + GLA_RANK:, :] = jnp.zeros((ZB_PAD - GLA_RANK, wr_ref.shape[1]), BF16)

    u = _rmsnorm_rows(x_ref[0], nw_ref[...]).astype(BF16)
    tm = u.shape[0]

    def row(name, rows=slice(None)):
        lo, hi = _ROW_OFF[name]
        return _dot_nt(u[rows], wr_ref[lo:hi, :])

    def col(name, rows=slice(None)):
        lo, hi = _ROW_OFF[name]
        return _dot_nt(wr_ref[lo:hi, :], u[rows])

    zb = row("zb")
    zb_hi = zb.astype(BF16)
    zb_lo = (zb - zb_hi.astype(F32)).astype(BF16)
    wa = wa_ref[...]
    wa_hi = wa.astype(BF16)
    wa_lo = (wa - wa_hi.astype(F32)).astype(BF16)
    z = _dot(jnp.concatenate([zb_hi, zb_lo, zb_hi], axis=1),
             jnp.concatenate([wa_hi, wa_hi, wa_lo], axis=0)) + ba_ref[...]
    g_ref[...] = (jnp.minimum(z, 0.0) - jnp.log1p(jnp.exp(-jnp.abs(z)))) * (LOG2E / GLA_TAU)
    qb_ref[...] = (row("qb") * (B_KEY_DIM ** -0.5)).astype(BF16)
    kb_ref[...] = row("kb").astype(BF16)
    vb_ref[...] = row("vb").astype(BF16)
    gb_ref[...] = _silu(row("gb")).astype(BF16)

    def emit_qaT(rows):
        qaT_ref[0, :, rows] = col("qa", rows)

    def emit_vaT(rows):
        vaT = col("va", rows).astype(BF16)
        ones = jnp.ones((PV_ROWS - A_HEAD_DIM, vaT.shape[1]), BF16)
        pieces = []
        for h in range(A_HEADS):
            pieces += [vaT[h * A_HEAD_DIM:(h + 1) * A_HEAD_DIM], ones]
        vaT_ref[0, :, rows] = jnp.concatenate(pieces, axis=0)

    def emit_ka(rows):
        ka = row("ka", rows)
        n = ka.shape[0]
        nb = MOBA_MAX_BLOCKS
        key_pos = (pl.program_id(1) * tm + rows.start
                   + lax.broadcasted_iota(jnp.int32, (n, LANES - A_HEAD_DIM), 0))
        blk_id = jnp.right_shift(key_pos, MOBA_BLOCK.bit_length() - 1)
        lane = lax.broadcasted_iota(jnp.int32, (n, LANES - A_HEAD_DIM), 1)
        onehot = jnp.where((lane < 2 * nb) & ((lane & (nb - 1)) == blk_id), 1.0, 0.0)
        pieces = []
        for h in range(A_HEADS):
            pieces += [ka[:, h * A_HEAD_DIM:(h + 1) * A_HEAD_DIM], onehot]
        ka_ref[0, rows, :] = jnp.concatenate(pieces, axis=1).astype(BF16)

    def emit_ga(rows):
        ga_ref[0, rows, :] = _silu(row("ga", rows)).astype(BF16)

    def emit_qc(rows):
        qc_ref[0, rows, :] = (row("qc", rows) * (C_HEAD_DIM ** -0.5)).astype(BF16)

    def emit_gc(rows):
        gc_ref[0, rows, :] = _silu(row("gc", rows)).astype(BF16)

    chunks = [slice(lo, lo + GLA_CHUNK) for lo in range(0, tm, GLA_CHUNK)]
    pending = [functools.partial(emit, rows) for emit in
               (emit_qaT, emit_vaT, emit_ka, emit_ga, emit_qc, emit_gc) for rows in chunks]

    def fill():
        if pending:
            pending.pop(0)()

    nw_gla = gnw_ref[...]
    for rows in chunks:
        def store_out(cols, val, rows=rows):
            mb_ref[0, rows, cols] = val

        _gla_chunk(qb_ref[rows, :], kb_ref[rows, :], vb_ref[rows, :], g_ref[rows, :], gb_ref[rows, :],
                   nw_gla, state_scr, att_scr, store_out, fill)
    while pending:
        fill()


def _proj(x, nw, w_in, wa, ba, gnw):
    B, S, D = x.shape
    tm = PROJ_TM
    row_spec = lambda width: pl.BlockSpec((1, tm, width), lambda b, i: (b, i, 0))
    col_spec = lambda rows: pl.BlockSpec((1, rows, tm), lambda b, i: (b, 0, i))
    const = lambda shape: pl.BlockSpec(shape, lambda b, i: (0,) * len(shape))
    sds = jax.ShapeDtypeStruct
    return pl.pallas_call(
        _proj_kernel,
        grid=(B, S // tm),
        in_specs=[
            pl.BlockSpec((1, tm, D), lambda b, i: (b, i, 0)),
            const((1, D)),
            pl.BlockSpec(w_in.shape, lambda b, i: (0, 0), pipeline_mode=pl.Buffered(1)),
            const((ZB_PAD, B_KEY_WIDTH)),
            const((1, B_KEY_WIDTH)),
            const((1, B_VAL_DIM)),
        ],
        out_specs=[col_spec(A_WIDTH), col_spec(A_HEADS * PV_ROWS), row_spec(A_HEADS * LANES),
                   row_spec(A_WIDTH), row_spec(C_WIDTH), row_spec(C_WIDTH), row_spec(B_WIDTH)],
        out_shape=[sds((B, A_WIDTH, S), F32), sds((B, A_HEADS * PV_ROWS, S), BF16),
                   sds((B, S, A_HEADS * LANES), BF16), sds((B, S, A_WIDTH), BF16),
                   sds((B, S, C_WIDTH), BF16), sds((B, S, C_WIDTH), BF16),
                   sds((B, S, B_WIDTH), BF16)],
        scratch_shapes=[pltpu.VMEM((ROW_COLS_TOTAL, D), BF16),
                        pltpu.VMEM((tm, B_KEY_WIDTH), BF16), pltpu.VMEM((tm, B_KEY_WIDTH), BF16),
                        pltpu.VMEM((tm, B_WIDTH), BF16), pltpu.VMEM((tm, B_WIDTH), BF16),
                        pltpu.VMEM((tm, B_KEY_WIDTH), F32),
                        pltpu.VMEM((B_KEY_WIDTH, B_WIDTH), F32),
                        pltpu.VMEM((B_HEADS, GLA_CHUNK, GLA_CHUNK), F32)],
        compiler_params=pltpu.CompilerParams(
            dimension_semantics=("arbitrary", "arbitrary"), vmem_limit_bytes=VMEM_LIMIT),
        name="proj_gla",
    )(x, nw, w_in, wa, ba, gnw)


def _moba_kernel(rb_ref, qT_ref, k_ref, vT_ref, own_ref, prev_ref, sg_ref, o_ref,
                 kmean_scr, kmbd_scr, qo_scr, qh_scr, m_scr, acc_scr, s_scr):
    i = pl.program_id(1)
    nb = kmean_scr.shape[0]
    blk = MOBA_BLOCK
    dh = A_HEAD_DIM
    n_tiles = A_HEADS // HEADS_PER_TILE

    @pl.when(i == 0)
    def _():
        for n in range(nb):
            kb = jnp.concatenate(
                [k_ref[0, n * blk:(n + 1) * blk, h * LANES:h * LANES + dh] for h in range(A_HEADS)],
                axis=1).astype(F32)
            kmean_scr[n:n + 1, :] = jnp.mean(kb, axis=0, keepdims=True)
        lane_head = jnp.right_shift(lax.broadcasted_iota(jnp.int32, (nb, A_WIDTH), 1),
                                    dh.bit_length() - 1)
        for h in range(A_HEADS):
            km = jnp.where(lane_head == h, kmean_scr[...], 0.0)
            km_hi = km.astype(BF16)
            km_lo = (km - km_hi.astype(F32)).astype(BF16)
            kmbd_scr[h * nb:(h + 1) * nb, :] = jnp.concatenate([km_hi, km_lo, km_hi], axis=1)

    q_f32 = qT_ref[0]
    q_hi = q_f32.astype(BF16)
    q_lo = (q_f32 - q_hi.astype(F32)).astype(BF16)
    gate_all = _dot(kmbd_scr[...], jnp.concatenate([q_hi, q_hi, q_lo], axis=0))
    brow = lax.broadcasted_iota(jnp.int32, (nb, blk), 0)
    browf = brow.astype(F32)
    zeros = jnp.zeros((LANES - dh - 2 * nb, blk), F32)

    for h in range(A_HEADS):
        q_h = qT_ref[0, h * dh:(h + 1) * dh, :] * (dh ** -0.5 * LOG2E)
        qo_scr[h] = jnp.concatenate([q_h, jnp.zeros((LANES - dh, blk), F32)], axis=0).astype(BF16)

    for h in range(A_HEADS):
        g = jnp.where(brow < i, gate_all[h * nb:(h + 1) * nb], NEG)
        sel = brow < 0
        for _ in range(MOBA_TOPK):
            m = jnp.max(g, axis=0, keepdims=True)
            idx = jnp.min(jnp.where(g == m, browf, float(nb)), axis=0, keepdims=True)
            pick = browf == idx
            sel = sel | pick
            g = jnp.where(pick, -jnp.inf, g)
        valid = sel & (brow < i)
        far_bias = rb_ref[REL_BUCKETS - 1, h] * LOG2E
        pen = jnp.where(brow < i - 1, jnp.where(valid, far_bias, NEG),
                        jnp.where(brow == i - 1, jnp.where(valid, 0.0, NEG), 0.0))
        pen_hi = pen.astype(BF16).astype(F32)
        q_h = qT_ref[0, h * dh:(h + 1) * dh, :] * (dh ** -0.5 * LOG2E)
        qh_scr[h] = jnp.concatenate([q_h, pen_hi, pen - pen_hi, zeros], axis=0).astype(BF16)

    no_prev = jnp.where(i == 0, NEG, 0.0)

    def logits(j, h, kind):
        rows = pl.ds(pl.multiple_of(j * blk, blk), blk)
        keys = k_ref[0, rows, h * LANES:(h + 1) * LANES]
        if kind == "own":
            return _dot(keys, qo_scr[h]) + own_ref[h]
        s = _dot(keys, qh_scr[h])
        if kind == "prev":
            return s + (prev_ref[h] + no_prev)
        return s

    def fold(j, h, s, first):
        rows = pl.ds(pl.multiple_of(j * blk, blk), blk)
        bm = jnp.max(s, axis=0, keepdims=True)
        vj = vT_ref[0, h * PV_ROWS:(h + 1) * PV_ROWS, rows]
        if first:
            m_scr[h] = bm
            acc_scr[h] = _dot(vj, jnp.exp2(s - bm).astype(BF16))
        else:
            m_old = m_scr[h]
            m_new = jnp.maximum(m_old, bm)
            m_scr[h] = m_new
            acc_scr[h] = (jnp.exp2(m_old - m_new) * acc_scr[h]
                          + _dot(vj, jnp.exp2(s - m_new).astype(BF16)))

    n_far = i - 1
    j_prev = jnp.maximum(i - 1, 0)
    items = ([(i, "own", h) for h in range(A_HEADS)] + [(j_prev, "prev", h) for h in range(A_HEADS)]
             + [(0, "far", h) for h in range(QK_LOOKAHEAD)])
    n_fold = 2 * A_HEADS
    pending = []

    def issue(n):
        j, kind, h = items[n]
        s = logits(j, h, kind)
        if n < n_fold:
            pending.append(s)
        else:
            s_scr[h] = s

    for n in range(QK_LOOKAHEAD):
        issue(n)
    for n in range(n_fold):
        issue(n + QK_LOOKAHEAD)
        j, kind, h = items[n]
        fold(j, h, pending.pop(0), kind == "own")

    def far_group(j0, n_blocks):
        j_after = jnp.minimum(j0 + n_blocks, n_far - 1)
        stream = ([(j0 + t, h) for t in range(n_blocks) for h in range(A_HEADS)]
                  + [(j_after, h) for h in range(QK_LOOKAHEAD)])
        n_items = n_blocks * A_HEADS
        pending = [s_scr[h] for h in range(QK_LOOKAHEAD)]
        for n in range(n_items):
            j, h = stream[n + QK_LOOKAHEAD]
            if n + QK_LOOKAHEAD < n_items:
                pending.append(logits(j, h, "far"))
            else:
                s_scr[h] = logits(j, h, "far")
            j, h = stream[n]
            fold(j, h, pending.pop(0), False)

    def far_groups(g, carry):
        far_group(g * FAR_UNROLL, FAR_UNROLL)
        return carry

    n_groups = jnp.maximum(n_far, 0) // FAR_UNROLL
    lax.fori_loop(0, n_groups, far_groups, 0)
    for rest in range(1, FAR_UNROLL):
        @pl.when(n_far - n_groups * FAR_UNROLL == rest)
        def _():
            far_group(n_groups * FAR_UNROLL, rest)

    for t in range(n_tiles):
        oT = []
        for hh in range(HEADS_PER_TILE):
            acc = acc_scr[t * HEADS_PER_TILE + hh]
            oT.append(acc[:dh] / acc[dh:dh + 1])
        cols = slice(t * LANES, (t + 1) * LANES)
        o_ref[0, :, cols] = (jnp.concatenate(oT, axis=0).T * sg_ref[0, :, cols]).astype(BF16)


def _moba(rel_bias, qaT, ka, vaT, bias_own, bias_prev, sga):
    B, S, _ = ka.shape
    nb = S // MOBA_BLOCK
    assert nb == MOBA_MAX_BLOCKS
    bias_spec = pl.BlockSpec((A_HEADS, MOBA_BLOCK, MOBA_BLOCK), lambda b, i: (0, 0, 0))
    return pl.pallas_call(
        _moba_kernel,
        grid=(B, nb),
        in_specs=[
            pl.BlockSpec(memory_space=pltpu.SMEM),
            pl.BlockSpec((1, A_WIDTH, MOBA_BLOCK), lambda b, i: (b, 0, i)),
            pl.BlockSpec((1, S, A_HEADS * LANES), lambda b, i: (b, 0, 0)),
            pl.BlockSpec((1, A_HEADS * PV_ROWS, S), lambda b, i: (b, 0, 0)),
            bias_spec, bias_spec,
            pl.BlockSpec((1, MOBA_BLOCK, A_WIDTH), lambda b, i: (b, i, 0)),
        ],
        out_specs=pl.BlockSpec((1, MOBA_BLOCK, A_WIDTH), lambda b, i: (b, i, 0)),
        out_shape=jax.ShapeDtypeStruct((B, S, A_WIDTH), BF16),
        scratch_shapes=[pltpu.VMEM((nb, A_WIDTH), F32),
                        pltpu.VMEM((A_HEADS * nb, 3 * A_WIDTH), BF16),
                        pltpu.VMEM((A_HEADS, LANES, MOBA_BLOCK), BF16),
                        pltpu.VMEM((A_HEADS, LANES, MOBA_BLOCK), BF16),
                        pltpu.VMEM((A_HEADS, 1, MOBA_BLOCK), F32),
                        pltpu.VMEM((A_HEADS, PV_ROWS, MOBA_BLOCK), F32),
                        pltpu.VMEM((QK_LOOKAHEAD, MOBA_BLOCK, MOBA_BLOCK), F32)],
        compiler_params=pltpu.CompilerParams(
            dimension_semantics=("arbitrary", "arbitrary"),
            vmem_limit_bytes=VMEM_LIMIT),
        name="moba",
    )(rel_bias, qaT, ka, vaT, bias_own, bias_prev, sga)


def _gla_chunk(q_bf, k_bf, v, g, sg, nw, state_scr, att_scr, store_out, fill):
    C = GLA_CHUNK
    dk, dv, H = B_KEY_DIM, B_VAL_DIM, B_HEADS
    n_levels = C.bit_length() - 1

    q = q_bf.astype(F32)
    k = k_bf.astype(F32)

    row = lax.broadcasted_iota(jnp.int32, (C, H * dk), 0)
    tt = lax.broadcasted_iota(jnp.int32, (C, C), 0)
    ss = lax.broadcasted_iota(jnp.int32, (C, C), 1)

    g_hi = g.astype(BF16)
    g_mid = (g - g_hi.astype(F32)).astype(BF16)
    g_lo = (g - g_hi.astype(F32) - g_mid.astype(F32)).astype(BF16)
    tril = jnp.where(ss <= tt, 1.0, 0.0).astype(BF16)
    b3 = _dot(tril, jnp.concatenate([g_hi, g_mid, g_lo], axis=1))
    b = b3[:, :H * dk] + b3[:, H * dk:2 * H * dk] + b3[:, 2 * H * dk:]

    def roll_rows(x, shift):
        if shift % C < SUBLANES:
            x3 = x.reshape(C // SUBLANES, SUBLANES, x.shape[1])
            return pltpu.roll(x3, shift % C, 1).reshape(x.shape)
        if (C - shift % C) < SUBLANES:
            x3 = x.reshape(C // SUBLANES, SUBLANES, x.shape[1])
            return pltpu.roll(x3, SUBLANES - (C - shift % C), 1).reshape(x.shape)
        return pltpu.roll(x, shift, 0)

    txs = tt ^ ss
    level = jnp.full((C, C), -1, jnp.int32)
    for p in range(n_levels):
        level = level + (txs >= (1 << p)).astype(jnp.int32)
    level = jnp.where(ss <= tt, level, -2)

    heads_per_tile = LANES // dk
    lane = lax.broadcasted_iota(jnp.int32, (C, H * dk), 1)
    head_in_tile = jnp.right_shift(lane, dk.bit_length() - 1) & (heads_per_tile - 1)

    def split_heads(kk):
        return [jnp.where(head_in_tile == hh, kk, 0.0).astype(BF16) for hh in range(heads_per_tile)]

    def scores(q_bf, k_split, h):
        t, hh = divmod(h, heads_per_tile)
        cols = slice(t * LANES, (t + 1) * LANES)
        return _dot_nt(q_bf[:, cols], k_split[hh][:, cols])

    on_diag = level == -1
    k_split = split_heads(k)
    for h in range(H):
        att_scr[h] = jnp.where(on_diag, scores(q_bf, k_split, h), 0.0)
    fill()

    block_end = b
    for p in range(n_levels):
        m = 1 << p
        second_half = (row & m) != 0
        r = jnp.where(second_half, roll_rows(block_end, m), block_end)
        decay = jnp.exp2(-jnp.abs(b - r))
        kd_split = split_heads(k * decay)
        if m >= SUBLANES:
            blocks = [slice(lo, lo + m) for lo in range(m, C, 2 * m)]
            take = lambda x: jnp.concatenate([x[rows] for rows in blocks], axis=0)
            qd_bf = (take(q) * take(decay)).astype(BF16)
            for h in range(H):
                s_lvl = scores(qd_bf, kd_split, h)
                for n, rows in enumerate(blocks):
                    keys = slice(rows.start - m, rows.start)
                    att_scr[h, rows, keys] = s_lvl[n * m:(n + 1) * m, keys]
        else:
            qd_bf = (q * decay).astype(BF16)
            at_level = level == p
            for h in range(H):
                att_scr[h] = jnp.where(at_level, scores(qd_bf, kd_split, h), att_scr[h])
        fill()
        if p + 1 < n_levels:
            block_end = jnp.where(second_half, block_end, roll_rows(block_end, C - m))

    state = state_scr[...]
    o_inter = _dot((q * jnp.exp2(b)).astype(BF16), state.astype(BF16))

    for h in range(H):
        cols = slice(h * dv, (h + 1) * dv)
        o_h = o_inter[:, cols] + _dot(att_scr[h].astype(BF16), v[:, cols])
        store_out(cols, (_rmsnorm_rows(o_h, nw) * sg[:, cols]).astype(BF16))

    bT = b.T
    kT = k.T
    b_last = bT[:, C - 1:C]
    kdT = (kT * jnp.exp2(b_last - bT)).astype(BF16)
    ds = _dot(kdT, v)
    srow = lax.broadcasted_iota(jnp.int32, (H * dk, H * dv), 0) // dk
    scol = lax.broadcasted_iota(jnp.int32, (H * dk, H * dv), 1) // dv
    state_scr[...] = jnp.exp2(b_last) * state + jnp.where(srow == scol, ds, 0.0)


def _out_kernel(x_ref, ma_ref, mb_ref, qc_ref, sgc_ref, km_ref, vm_ref, w_ref, fw_ref, o_ref,
                wo_ref):
    @pl.when((pl.program_id(0) == 0) & (pl.program_id(1) == 0))
    def _():
        for lo in range(0, MIX_WIDTH, 4 * LANES):
            wo_ref[lo:lo + 4 * LANES, :] = w_ref[0, lo:lo + 4 * LANES, :].astype(BF16)

    dh = C_HEAD_DIM
    qc = qc_ref[0]
    km = km_ref[0]
    vm = vm_ref[0]
    heads = [slice(h * dh, (h + 1) * dh) for h in range(C_HEADS)]
    scores = [_dot_nt(qc[:, sl], km[:, sl]) for sl in heads]
    h_ab = (_dot(ma_ref[0], wo_ref[0:A_WIDTH, :])
            + _dot(mb_ref[0], wo_ref[A_WIDTH:A_WIDTH + B_WIDTH, :]))
    oc = []
    for s, sl in zip(scores, heads):
        p = jnp.exp(s - jnp.max(s, axis=-1, keepdims=True))
        l = jnp.sum(p, axis=-1, keepdims=True)
        oc.append(_dot(p.astype(BF16), vm[:, sl]) / l)
    mc = (jnp.concatenate(oc, axis=-1) * sgc_ref[0]).astype(BF16)
    h_new = x_ref[0] + h_ab + _dot(mc, wo_ref[A_WIDTH + B_WIDTH:, :])
    o_ref[0] = _rmsnorm_rows(h_new, fw_ref[...])


def _out(x, ma, mb, qc, sgc, km, vm, w_out, fw):
    B, S, D = x.shape
    tm = OUT_TM
    row = lambda width: pl.BlockSpec((1, tm, width), lambda b, i: (b, i, 0))
    mem = pl.BlockSpec((1, MEM_LEN, C_WIDTH), lambda b, i: (b, 0, 0))
    return pl.pallas_call(
        _out_kernel,
        grid=(B, S // tm),
        in_specs=[row(D), row(A_WIDTH), row(B_WIDTH), row(C_WIDTH), row(C_WIDTH), mem, mem,
                  pl.BlockSpec(w_out.shape, lambda b, i: (0, 0, 0), pipeline_mode=pl.Buffered(1)),
                  pl.BlockSpec((1, D), lambda b, i: (0, 0))],
        out_specs=row(D),
        out_shape=jax.ShapeDtypeStruct((B, S, D), F32),
        scratch_shapes=[pltpu.VMEM((MIX_WIDTH, D), BF16)],
        compiler_params=pltpu.CompilerParams(
            dimension_semantics=("arbitrary", "arbitrary"), vmem_limit_bytes=VMEM_LIMIT),
        name="out",
    )(x, ma, mb, qc, sgc, km, vm, w_out, fw)


def kernel(x, mem, norm_w, w_in, w_alpha2, b_alpha, gla_norm_w, mem_norm_w, w_mem_kv, w_out,
           rel_bias, final_norm_w):
    assert norm_w.shape[0] == 1, "single layer"
    assert w_in.shape[2] == W_IN_ALIGNED + GLA_RANK + 2 * C_WIDTH
    wa = jnp.pad(w_alpha2[0], ((0, ZB_PAD - GLA_RANK), (0, 0)))

    km, vm = _memkv(mem, mem_norm_w[0][None, :], w_mem_kv)
    bias_own, bias_prev = _bias_tiles(rel_bias)
    qaT, vaT, ka, sga, qc, sgc, mb = _proj(x, norm_w[0][None, :], w_in[0].T, wa, b_alpha[0][None, :],
                                           gla_norm_w[0][None, :])
    ma = _moba(rel_bias, qaT, ka, vaT, bias_own, bias_prev, sga)
    return _out(x, ma, mb, qc, sgc, km, vm, w_out, final_norm_w[None, :])
```

```python
import functools
import math

import jax
import jax.numpy as jnp
from jax import lax
from jax.experimental import pallas as pl
from jax.experimental.pallas import tpu as pltpu

F32 = jnp.float32
BF16 = jnp.bfloat16

D_MODEL = 1024
MEM_LEN = 256
A_HEADS = 8
A_HEAD_DIM = 64
A_WIDTH = 512
MOBA_BLOCK = 256
MOBA_TOPK = 3
B_HEADS = 4
B_KEY_DIM = 64
B_VAL_DIM = 128
B_KEY_WIDTH = 256
B_WIDTH = 512
GLA_RANK = 16
GLA_TAU = 16.0
C_HEADS = 4
C_HEAD_DIM = 128
C_WIDTH = 512
MIX_WIDTH = A_WIDTH + B_WIDTH + C_WIDTH
REL_BUCKETS = 32
REL_MAX_DIST = 128
RMS_EPS = 1e-6
NEG = -1e30

LANES = 128
SUBLANES = 8
ZB_PAD = LANES
PROJ_TM = 512
OUT_TM = 512
GLA_CHUNK = 256
VMEM_LIMIT = 56 * 1024 * 1024
LOG2E = 1.0 / math.log(2.0)
HEADS_PER_TILE = LANES // A_HEAD_DIM
PV_ROWS = A_HEAD_DIM + 16
MOBA_MAX_BLOCKS = 16
QK_LOOKAHEAD = 5
FAR_UNROLL = 4


def _dot(a, b):
    return jnp.dot(a, b, preferred_element_type=F32)


def _dot_nt(a, b):
    return lax.dot_general(a, b, (((1,), (1,)), ((), ())), preferred_element_type=F32)


def _rmsnorm_rows(x, w):
    return x * lax.rsqrt(jnp.mean(x * x, axis=-1, keepdims=True) + RMS_EPS) * w


def _silu(x):
    return x * jax.nn.sigmoid(x)


def _memkv_kernel(mem_ref, nw_ref, w_ref, km_ref, vm_ref):
    u = _rmsnorm_rows(mem_ref[0], nw_ref[...]).astype(BF16)
    km_ref[0] = _dot(u, w_ref[0, :, :C_WIDTH].astype(BF16)).astype(BF16)
    vm_ref[0] = _dot(u, w_ref[0, :, C_WIDTH:].astype(BF16)).astype(BF16)


def _memkv(mem, nw, w_mem_kv):
    B = mem.shape[0]
    return pl.pallas_call(
        _memkv_kernel,
        grid=(B,),
        in_specs=[
            pl.BlockSpec((1, MEM_LEN, D_MODEL), lambda b: (b, 0, 0)),
            pl.BlockSpec((1, D_MODEL), lambda b: (0, 0)),
            pl.BlockSpec(w_mem_kv.shape, lambda b: (0, 0, 0)),
        ],
        out_specs=[
            pl.BlockSpec((1, MEM_LEN, C_WIDTH), lambda b: (b, 0, 0)),
            pl.BlockSpec((1, MEM_LEN, C_WIDTH), lambda b: (b, 0, 0)),
        ],
        out_shape=[jax.ShapeDtypeStruct((B, MEM_LEN, C_WIDTH), BF16)] * 2,
        name="memkv",
    )(mem, nw, w_mem_kv)


def _t5_bucket(n):
    max_exact = REL_BUCKETS // 2
    nf = jnp.maximum(n, max_exact).astype(F32)
    large = max_exact + jnp.floor(jnp.log(nf / max_exact) / math.log(REL_MAX_DIST / max_exact)
                                  * (REL_BUCKETS - max_exact)).astype(jnp.int32)
    large = jnp.minimum(large, REL_BUCKETS - 1)
    return jnp.where(n < max_exact, n, large)


def _bias_kernel(rb_ref, own_ref, prev_ref):
    h = pl.program_id(0)
    key = lax.broadcasted_iota(jnp.int32, (MOBA_BLOCK, MOBA_BLOCK), 0)
    qry = lax.broadcasted_iota(jnp.int32, (MOBA_BLOCK, MOBA_BLOCK), 1)
    rel = qry - key
    b_own = _t5_bucket(jnp.maximum(rel, 0))
    b_prev = _t5_bucket(rel + MOBA_BLOCK)
    own = jnp.zeros((MOBA_BLOCK, MOBA_BLOCK), F32)
    prev = jnp.zeros((MOBA_BLOCK, MOBA_BLOCK), F32)
    for bk in range(REL_BUCKETS):
        val = rb_ref[bk, h] * LOG2E
        own = jnp.where(b_own == bk, val, own)
        prev = jnp.where(b_prev == bk, val, prev)
    own_ref[0] = jnp.where(rel >= 0, own, NEG)
    prev_ref[0] = prev


def _bias_tiles(rel_bias):
    return pl.pallas_call(
        _bias_kernel,
        grid=(A_HEADS,),
        in_specs=[pl.BlockSpec(memory_space=pltpu.SMEM)],
        out_specs=[pl.BlockSpec((1, MOBA_BLOCK, MOBA_BLOCK), lambda h: (h, 0, 0))] * 2,
        out_shape=[jax.ShapeDtypeStruct((A_HEADS, MOBA_BLOCK, MOBA_BLOCK), F32)] * 2,
        name="t5bias",
    )(rel_bias)


_ROW_COLS = (("qa", A_WIDTH), ("ka", A_WIDTH), ("va", A_WIDTH), ("ga", A_WIDTH),
             ("qb", B_KEY_WIDTH), ("kb", B_KEY_WIDTH), ("vb", B_WIDTH), ("gb", B_WIDTH),
             ("qc", C_WIDTH), ("gc", C_WIDTH), ("zb", ZB_PAD))
_ROW_OFF = {}
_off = 0
for _name, _width in _ROW_COLS:
    _ROW_OFF[_name] = (_off, _off + _width)
    _off += _width
ROW_COLS_TOTAL = _off


W_IN_ALIGNED = 4 * A_WIDTH + 2 * B_KEY_WIDTH + 2 * B_WIDTH


def _proj_kernel(x_ref, nw_ref, w_ref, wa_ref, ba_ref, gnw_ref,
                 qaT_ref, vaT_ref, ka_ref, ga_ref, qc_ref, gc_ref, mb_ref,
                 wr_ref, qb_ref, kb_ref, vb_ref, gb_ref, g_ref, state_scr, att_scr):
    @pl.when(pl.program_id(1) == 0)
    def _():
        state_scr[...] = jnp.zeros_like(state_scr)

    @pl.when((pl.program_id(0) == 0) & (pl.program_id(1) == 0))
    def _():
        chunk = 4 * LANES
        for lo in range(0, W_IN_ALIGNED, chunk):
            wr_ref[lo:lo + chunk, :] = w_ref[lo:lo + chunk, :].astype(BF16)
        tail = W_IN_ALIGNED + GLA_RANK
        for lo in range(0, 2 * C_WIDTH, chunk):
            wr_ref[W_IN_ALIGNED + lo:W_IN_ALIGNED + lo + chunk, :] = (
                w_ref[tail + lo:tail + lo + chunk, :].astype(BF16))
        zb_lo = _ROW_OFF["zb"][0]
        wr_ref[zb_lo:zb_lo + GLA_RANK, :] = w_ref[W_IN_ALIGNED:tail, :].astype(BF16)
        wr_ref[zb_lo + GLA_RANK:, :] = jnp.zeros((ZB_PAD - GLA_RANK, wr_ref.shape[1]), BF16)

    u = _rmsnorm_rows(x_ref[0], nw_ref[...]).astype(BF16)
    tm = u.shape[0]

    def row(name, rows=slice(None)):
        lo, hi = _ROW_OFF[name]
        return _dot_nt(u[rows], wr_ref[lo:hi, :])

    def col(name, rows=slice(None)):
        lo, hi = _ROW_OFF[name]
        return _dot_nt(wr_ref[lo:hi, :], u[rows])

    zb = row("zb")
    zb_hi = zb.astype(BF16)
    zb_lo = (zb - zb_hi.astype(F32)).astype(BF16)
    wa = wa_ref[...]
    wa_hi = wa.astype(BF16)
    wa_lo = (wa - wa_hi.astype(F32)).astype(BF16)
    z = _dot(jnp.concatenate([zb_hi, zb_lo, zb_hi], axis=1),
             jnp.concatenate([wa_hi, wa_hi, wa_lo], axis=0)) + ba_ref[...]
    g_ref[...] = (jnp.minimum(z, 0.0) - jnp.log1p(jnp.exp(-jnp.abs(z)))) * (LOG2E / GLA_TAU)
    qb_ref[...] = (row("qb") * (B_KEY_DIM ** -0.5)).astype(BF16)
    kb_ref[...] = row("kb").astype(BF16)
    vb_ref[...] = row("vb").astype(BF16)
    gb_ref[...] = _silu(row("gb")).astype(BF16)

    def emit_qaT(rows):
        qaT_ref[0, :, rows] = col("qa", rows)

    def emit_vaT(rows):
        vaT = col("va", rows).astype(BF16)
        ones = jnp.ones((PV_ROWS - A_HEAD_DIM, vaT.shape[1]), BF16)
        pieces = []
        for h in range(A_HEADS):
            pieces += [vaT[h * A_HEAD_DIM:(h + 1) * A_HEAD_DIM], ones]
        vaT_ref[0, :, rows] = jnp.concatenate(pieces, axis=0)

    def emit_ka(rows):
        ka = row("ka", rows)
        n = ka.shape[0]
        nb = MOBA_MAX_BLOCKS
        key_pos = (pl.program_id(1) * tm + rows.start
                   + lax.broadcasted_iota(jnp.int32, (n, LANES - A_HEAD_DIM), 0))
        blk_id = jnp.right_shift(key_pos, MOBA_BLOCK.bit_length() - 1)
        lane = lax.broadcasted_iota(jnp.int32, (n, LANES - A_HEAD_DIM), 1)
        onehot = jnp.where((lane < 2 * nb) & ((lane & (nb - 1)) == blk_id), 1.0, 0.0)
        pieces = []
        for h in range(A_HEADS):
            pieces += [ka[:, h * A_HEAD_DIM:(h + 1) * A_HEAD_DIM], onehot]
        ka_ref[0, rows, :] = jnp.concatenate(pieces, axis=1).astype(BF16)

    def emit_ga(rows):
        ga_ref[0, rows, :] = _silu(row("ga", rows)).astype(BF16)

    def emit_qc(rows):
        qc_ref[0, rows, :] = (row("qc", rows) * (C_HEAD_DIM ** -0.5)).astype(BF16)

    def emit_gc(rows):
        gc_ref[0, rows, :] = _silu(row("gc", rows)).astype(BF16)

    chunks = [slice(lo, lo + GLA_CHUNK) for lo in range(0, tm, GLA_CHUNK)]
    pending = [functools.partial(emit, rows) for emit in
               (emit_qaT, emit_vaT, emit_ka, emit_ga, emit_qc, emit_gc) for rows in chunks]

    def fill():
        if pending:
            pending.pop(0)()

    nw_gla = gnw_ref[...]
    for rows in chunks:
        def store_out(cols, val, rows=rows):
            mb_ref[0, rows, cols] = val

        _gla_chunk(qb_ref[rows, :], kb_ref[rows, :], vb_ref[rows, :], g_ref[rows, :], gb_ref[rows, :],
                   nw_gla, state_scr, att_scr, store_out, fill)
    while pending:
        fill()


def _proj(x, nw, w_in, wa, ba, gnw):
    B, S, D = x.shape
    tm = PROJ_TM
    row_spec = lambda width: pl.BlockSpec((1, tm, width), lambda b, i: (b, i, 0))
    col_spec = lambda rows: pl.BlockSpec((1, rows, tm), lambda b, i: (b, 0, i))
    const = lambda shape: pl.BlockSpec(shape, lambda b, i: (0,) * len(shape))
    sds = jax.ShapeDtypeStruct
    return pl.pallas_call(
        _proj_kernel,
        grid=(B, S // tm),
        in_specs=[
            pl.BlockSpec((1, tm, D), lambda b, i: (b, i, 0)),
            const((1, D)),
            pl.BlockSpec(w_in.shape, lambda b, i: (0, 0), pipeline_mode=pl.Buffered(1)),
            const((ZB_PAD, B_KEY_WIDTH)),
            const((1, B_KEY_WIDTH)),
            const((1, B_VAL_DIM)),
        ],
        out_specs=[col_spec(A_WIDTH), col_spec(A_HEADS * PV_ROWS), row_spec(A_HEADS * LANES),
                   row_spec(A_WIDTH), row_spec(C_WIDTH), row_spec(C_WIDTH), row_spec(B_WIDTH)],
        out_shape=[sds((B, A_WIDTH, S), F32), sds((B, A_HEADS * PV_ROWS, S), BF16),
                   sds((B, S, A_HEADS * LANES), BF16), sds((B, S, A_WIDTH), BF16),
                   sds((B, S, C_WIDTH), BF16), sds((B, S, C_WIDTH), BF16),
                   sds((B, S, B_WIDTH), BF16)],
        scratch_shapes=[pltpu.VMEM((ROW_COLS_TOTAL, D), BF16),
                        pltpu.VMEM((tm, B_KEY_WIDTH), BF16), pltpu.VMEM((tm, B_KEY_WIDTH), BF16),
                        pltpu.VMEM((tm, B_WIDTH), BF16), pltpu.VMEM((tm, B_WIDTH), BF16),
                        pltpu.VMEM((tm, B_KEY_WIDTH), F32),
                        pltpu.VMEM((B_KEY_WIDTH, B_WIDTH), F32),
                        pltpu.VMEM((B_HEADS, GLA_CHUNK, GLA_CHUNK), F32)],
        compiler_params=pltpu.CompilerParams(
            dimension_semantics=("arbitrary", "arbitrary"), vmem_limit_bytes=VMEM_LIMIT),
        name="proj_gla",
    )(x, nw, w_in, wa, ba, gnw)


def _moba_kernel(rb_ref, qT_ref, k_ref, vT_ref, own_ref, prev_ref, sg_ref, o_ref,
                 kmean_scr, kmbd_scr, qo_scr, qh_scr, m_scr, acc_scr, s_scr):
    i = pl.program_id(1)
    nb = kmean_scr.shape[0]
    blk = MOBA_BLOCK
    dh = A_HEAD_DIM
    n_tiles = A_HEADS // HEADS_PER_TILE

    @pl.when(i == 0)
    def _():
        for n in range(nb):
            kb = jnp.concatenate(
                [k_ref[0, n * blk:(n + 1) * blk, h * LANES:h * LANES + dh] for h in range(A_HEADS)],
                axis=1).astype(F32)
            kmean_scr[n:n + 1, :] = jnp.mean(kb, axis=0, keepdims=True)
        lane_head = jnp.right_shift(lax.broadcasted_iota(jnp.int32, (nb, A_WIDTH), 1),
                                    dh.bit_length() - 1)
        for h in range(A_HEADS):
            km = jnp.where(lane_head == h, kmean_scr[...], 0.0)
            km_hi = km.astype(BF16)
            km_lo = (km - km_hi.astype(F32)).astype(BF16)
            kmbd_scr[h * nb:(h + 1) * nb, :] = jnp.concatenate([km_hi, km_lo, km_hi], axis=1)

    q_f32 = qT_ref[0]
    q_hi = q_f32.astype(BF16)
    q_lo = (q_f32 - q_hi.astype(F32)).astype(BF16)
    gate_all = _dot(kmbd_scr[...], jnp.concatenate([q_hi, q_hi, q_lo], axis=0))
    brow = lax.broadcasted_iota(jnp.int32, (nb, blk), 0)
    browf = brow.astype(F32)
    zeros = jnp.zeros((LANES - dh - 2 * nb, blk), F32)

    for h in range(A_HEADS):
        q_h = qT_ref[0, h * dh:(h + 1) * dh, :] * (dh ** -0.5 * LOG2E)
        qo_scr[h] = jnp.concatenate([q_h, jnp.zeros((LANES - dh, blk), F32)], axis=0).astype(BF16)

    for h in range(A_HEADS):
        g = jnp.where(brow < i, gate_all[h * nb:(h + 1) * nb], NEG)
        sel = brow < 0
        for _ in range(MOBA_TOPK):
            m = jnp.max(g, axis=0, keepdims=True)
            idx = jnp.min(jnp.where(g == m, browf, float(nb)), axis=0, keepdims=True)
            pick = browf == idx
            sel = sel | pick
            g = jnp.where(pick, -jnp.inf, g)
        valid = sel & (brow < i)
        far_bias = rb_ref[REL_BUCKETS - 1, h] * LOG2E
        pen = jnp.where(brow < i - 1, jnp.where(valid, far_bias, NEG),
                        jnp.where(brow == i - 1, jnp.where(valid, 0.0, NEG), 0.0))
        pen_hi = pen.astype(BF16).astype(F32)
        q_h = qT_ref[0, h * dh:(h + 1) * dh, :] * (dh ** -0.5 * LOG2E)
        qh_scr[h] = jnp.concatenate([q_h, pen_hi, pen - pen_hi, zeros], axis=0).astype(BF16)

    no_prev = jnp.where(i == 0, NEG, 0.0)

    def logits(j, h, kind):
        rows = pl.ds(pl.multiple_of(j * blk, blk), blk)
        keys = k_ref[0, rows, h * LANES:(h + 1) * LANES]
        if kind == "own":
            return _dot(keys, qo_scr[h]) + own_ref[h]
        s = _dot(keys, qh_scr[h])
        if kind == "prev":
            return s + (prev_ref[h] + no_prev)
        return s

    def fold(j, h, s, first):
        rows = pl.ds(pl.multiple_of(j * blk, blk), blk)
        bm = jnp.max(s, axis=0, keepdims=True)
        vj = vT_ref[0, h * PV_ROWS:(h + 1) * PV_ROWS, rows]
        if first:
            m_scr[h] = bm
            acc_scr[h] = _dot(vj, jnp.exp2(s - bm).astype(BF16))
        else:
            m_old = m_scr[h]
            m_new = jnp.maximum(m_old, bm)
            m_scr[h] = m_new
            acc_scr[h] = (jnp.exp2(m_old - m_new) * acc_scr[h]
                          + _dot(vj, jnp.exp2(s - m_new).astype(BF16)))

    n_far = i - 1
    j_prev = jnp.maximum(i - 1, 0)
    items = ([(i, "own", h) for h in range(A_HEADS)] + [(j_prev, "prev", h) for h in range(A_HEADS)]
             + [(0, "far", h) for h in range(QK_LOOKAHEAD)])
    n_fold = 2 * A_HEADS
    pending = []

    def issue(n):
        j, kind, h = items[n]
        s = logits(j, h, kind)
        if n < n_fold:
            pending.append(s)
        else:
            s_scr[h] = s

    for n in range(QK_LOOKAHEAD):
        issue(n)
    for n in range(n_fold):
        issue(n + QK_LOOKAHEAD)
        j, kind, h = items[n]
        fold(j, h, pending.pop(0), kind == "own")

    def far_group(j0, n_blocks):
        j_after = jnp.minimum(j0 + n_blocks, n_far - 1)
        stream = ([(j0 + t, h) for t in range(n_blocks) for h in range(A_HEADS)]
                  + [(j_after, h) for h in range(QK_LOOKAHEAD)])
        n_items = n_blocks * A_HEADS
        pending = [s_scr[h] for h in range(QK_LOOKAHEAD)]
        for n in range(n_items):
            j, h = stream[n + QK_LOOKAHEAD]
            if n + QK_LOOKAHEAD < n_items:
                pending.append(logits(j, h, "far"))
            else:
                s_scr[h] = logits(j, h, "far")
            j, h = stream[n]
            fold(j, h, pending.pop(0), False)

    def far_groups(g, carry):
        far_group(g * FAR_UNROLL, FAR_UNROLL)
        return carry

    n_groups = jnp.maximum(n_far, 0) // FAR_UNROLL
    lax.fori_loop(0, n_groups, far_groups, 0)
    for rest in range(1, FAR_UNROLL):
        @pl.when(n_far - n_groups * FAR_UNROLL == rest)
        def _():
            far_group(n_groups * FAR_UNROLL, rest)

    for t in range(n_tiles):
        oT = []
        for hh in range(HEADS_PER_TILE):
            acc = acc_scr[t * HEADS_PER_TILE + hh]
            oT.append(acc[:dh] / acc[dh:dh + 1])
        cols = slice(t * LANES, (t + 1) * LANES)
        o_ref[0, :, cols] = (jnp.concatenate(oT, axis=0).T * sg_ref[0, :, cols]).astype(BF16)


def _moba(rel_bias, qaT, ka, vaT, bias_own, bias_prev, sga):
    B, S, _ = ka.shape
    nb = S // MOBA_BLOCK
    assert nb == MOBA_MAX_BLOCKS
    bias_spec = pl.BlockSpec((A_HEADS, MOBA_BLOCK, MOBA_BLOCK), lambda b, i: (0, 0, 0))
    return pl.pallas_call(
        _moba_kernel,
        grid=(B, nb),
        in_specs=[
            pl.BlockSpec(memory_space=pltpu.SMEM),
            pl.BlockSpec((1, A_WIDTH, MOBA_BLOCK), lambda b, i: (b, 0, i)),
            pl.BlockSpec((1, S, A_HEADS * LANES), lambda b, i: (b, 0, 0)),
            pl.BlockSpec((1, A_HEADS * PV_ROWS, S), lambda b, i: (b, 0, 0)),
            bias_spec, bias_spec,
            pl.BlockSpec((1, MOBA_BLOCK, A_WIDTH), lambda b, i: (b, i, 0)),
        ],
        out_specs=pl.BlockSpec((1, MOBA_BLOCK, A_WIDTH), lambda b, i: (b, i, 0)),
        out_shape=jax.ShapeDtypeStruct((B, S, A_WIDTH), BF16),
        scratch_shapes=[pltpu.VMEM((nb, A_WIDTH), F32),
                        pltpu.VMEM((A_HEADS * nb, 3 * A_WIDTH), BF16),
                        pltpu.VMEM((A_HEADS, LANES, MOBA_BLOCK), BF16),
                        pltpu.VMEM((A_HEADS, LANES, MOBA_BLOCK), BF16),
                        pltpu.VMEM((A_HEADS, 1, MOBA_BLOCK), F32),
                        pltpu.VMEM((A_HEADS, PV_ROWS, MOBA_BLOCK), F32),
                        pltpu.VMEM((QK_LOOKAHEAD, MOBA_BLOCK, MOBA_BLOCK), F32)],
        compiler_params=pltpu.CompilerParams(
            dimension_semantics=("arbitrary", "arbitrary"),
            vmem_limit_bytes=VMEM_LIMIT),
        name="moba",
    )(rel_bias, qaT, ka, vaT, bias_own, bias_prev, sga)


def _gla_chunk(q_bf, k_bf, v, g, sg, nw, state_scr, att_scr, store_out, fill):
    C = GLA_CHUNK
    dk, dv, H = B_KEY_DIM, B_VAL_DIM, B_HEADS
    n_levels = C.bit_length() - 1

    q = q_bf.astype(F32)
    k = k_bf.astype(F32)

    row = lax.broadcasted_iota(jnp.int32, (C, H * dk), 0)
    tt = lax.broadcasted_iota(jnp.int32, (C, C), 0)
    ss = lax.broadcasted_iota(jnp.int32, (C, C), 1)

    g_hi = g.astype(BF16)
    g_mid = (g - g_hi.astype(F32)).astype(BF16)
    g_lo = (g - g_hi.astype(F32) - g_mid.astype(F32)).astype(BF16)
    tril = jnp.where(ss <= tt, 1.0, 0.0).astype(BF16)
    b3 = _dot(tril, jnp.concatenate([g_hi, g_mid, g_lo], axis=1))
    b = b3[:, :H * dk] + b3[:, H * dk:2 * H * dk] + b3[:, 2 * H * dk:]

    def roll_rows(x, shift):
        if shift % C < SUBLANES:
            x3 = x.reshape(C // SUBLANES, SUBLANES, x.shape[1])
            return pltpu.roll(x3, shift % C, 1).reshape(x.shape)
        if (C - shift % C) < SUBLANES:
            x3 = x.reshape(C // SUBLANES, SUBLANES, x.shape[1])
            return pltpu.roll(x3, SUBLANES - (C - shift % C), 1).reshape(x.shape)
        return pltpu.roll(x, shift, 0)

    txs = tt ^ ss
    level = jnp.full((C, C), -1, jnp.int32)
    for p in range(n_levels):
        level = level + (txs >= (1 << p)).astype(jnp.int32)
    level = jnp.where(ss <= tt, level, -2)

    heads_per_tile = LANES // dk
    lane = lax.broadcasted_iota(jnp.int32, (C, H * dk), 1)
    head_in_tile = jnp.right_shift(lane, dk.bit_length() - 1) & (heads_per_tile - 1)

    def split_heads(kk):
        return [jnp.where(head_in_tile == hh, kk, 0.0).astype(BF16) for hh in range(heads_per_tile)]

    def scores(q_bf, k_split, h):
        t, hh = divmod(h, heads_per_tile)
        cols = slice(t * LANES, (t + 1) * LANES)
        return _dot_nt(q_bf[:, cols], k_split[hh][:, cols])

    on_diag = level == -1
    k_split = split_heads(k)
    for h in range(H):
        att_scr[h] = jnp.where(on_diag, scores(q_bf, k_split, h), 0.0)
    fill()

    block_end = b
    for p in range(n_levels):
        m = 1 << p
        second_half = (row & m) != 0
        r = jnp.where(second_half, roll_rows(block_end, m), block_end)
        decay = jnp.exp2(-jnp.abs(b - r))
        kd_split = split_heads(k * decay)
        if m >= SUBLANES:
            blocks = [slice(lo, lo + m) for lo in range(m, C, 2 * m)]
            take = lambda x: jnp.concatenate([x[rows] for rows in blocks], axis=0)
            qd_bf = (take(q) * take(decay)).astype(BF16)
            for h in range(H):
                s_lvl = scores(qd_bf, kd_split, h)
                for n, rows in enumerate(blocks):
                    keys = slice(rows.start - m, rows.start)
                    att_scr[h, rows, keys] = s_lvl[n * m:(n + 1) * m, keys]
        else:
            qd_bf = (q * decay).astype(BF16)
            at_level = level == p
            for h in range(H):
                att_scr[h] = jnp.where(at_level, scores(qd_bf, kd_split, h), att_scr[h])
        fill()
        if p + 1 < n_levels:
            block_end = jnp.where(second_half, block_end, roll_rows(block_end, C - m))

    state = state_scr[...]
    o_inter = _dot((q * jnp.exp2(b)).astype(BF16), state.astype(BF16))

    for h in range(H):
        cols = slice(h * dv, (h + 1) * dv)
        o_h = o_inter[:, cols] + _dot(att_scr[h].astype(BF16), v[:, cols])
        store_out(cols, (_rmsnorm_rows(o_h, nw) * sg[:, cols]).astype(BF16))

    bT = b.T
    kT = k.T
    b_last = bT[:, C - 1:C]
    kdT = (kT * jnp.exp2(b_last - bT)).astype(BF16)
    ds = _dot(kdT, v)
    srow = lax.broadcasted_iota(jnp.int32, (H * dk, H * dv), 0) // dk
    scol = lax.broadcasted_iota(jnp.int32, (H * dk, H * dv), 1) // dv
    state_scr[...] = jnp.exp2(b_last) * state + jnp.where(srow == scol, ds, 0.0)


def _out_kernel(x_ref, ma_ref, mb_ref, qc_ref, sgc_ref, km_ref, vm_ref, w_ref, fw_ref, o_ref,
                wo_ref):
    @pl.when((pl.program_id(0) == 0) & (pl.program_id(1) == 0))
    def _():
        for lo in range(0, MIX_WIDTH, 4 * LANES):
            wo_ref[lo:lo + 4 * LANES, :] = w_ref[0, lo:lo + 4 * LANES, :].astype(BF16)

    dh = C_HEAD_DIM
    qc = qc_ref[0]
    km = km_ref[0]
    vm = vm_ref[0]
    heads = [slice(h * dh, (h + 1) * dh) for h in range(C_HEADS)]
    scores = [_dot_nt(qc[:, sl], km[:, sl]) for sl in heads]
    h_ab = (_dot(ma_ref[0], wo_ref[0:A_WIDTH, :])
            + _dot(mb_ref[0], wo_ref[A_WIDTH:A_WIDTH + B_WIDTH, :]))
    oc = []
    for s, sl in zip(scores, heads):
        p = jnp.exp(s - jnp.max(s, axis=-1, keepdims=True))
        l = jnp.sum(p, axis=-1, keepdims=True)
        oc.append(_dot(p.astype(BF16), vm[:, sl]) / l)
    mc = (jnp.concatenate(oc, axis=-1) * sgc_ref[0]).astype(BF16)
    h_new = x_ref[0] + h_ab + _dot(mc, wo_ref[A_WIDTH + B_WIDTH:, :])
    o_ref[0] = _rmsnorm_rows(h_new, fw_ref[...])


def _out(x, ma, mb, qc, sgc, km, vm, w_out, fw):
    B, S, D = x.shape
    tm = OUT_TM
    row = lambda width: pl.BlockSpec((1, tm, width), lambda b, i: (b, i, 0))
    mem = pl.BlockSpec((1, MEM_LEN, C_WIDTH), lambda b, i: (b, 0, 0))
    return pl.pallas_call(
        _out_kernel,
        grid=(B, S // tm),
        in_specs=[row(D), row(A_WIDTH), row(B_WIDTH), row(C_WIDTH), row(C_WIDTH), mem, mem,
                  pl.BlockSpec(w_out.shape, lambda b, i: (0, 0, 0), pipeline_mode=pl.Buffered(1)),
                  pl.BlockSpec((1, D), lambda b, i: (0, 0))],
        out_specs=row(D),
        out_shape=jax.ShapeDtypeStruct((B, S, D), F32),
        scratch_shapes=[pltpu.VMEM((MIX_WIDTH, D), BF16)],
        compiler_params=pltpu.CompilerParams(
            dimension_semantics=("arbitrary", "arbitrary"), vmem_limit_bytes=VMEM_LIMIT),
        name="out",
    )(x, ma, mb, qc, sgc, km, vm, w_out, fw)


def kernel(x, mem, norm_w, w_in, w_alpha2, b_alpha, gla_norm_w, mem_norm_w, w_mem_kv, w_out,
           rel_bias, final_norm_w):
    assert norm_w.shape[0] == 1, "single layer"
    assert w_in.shape[2] == W_IN_ALIGNED + GLA_RANK + 2 * C_WIDTH
    wa = jnp.pad(w_alpha2[0], ((0, ZB_PAD - GLA_RANK), (0, 0)))

    km, vm = _memkv(mem, mem_norm_w[0][None, :], w_mem_kv)
    bias_own, bias_prev = _bias_tiles(rel_bias)
    qaT, vaT, ka, sga, qc, sgc, mb = _proj(x, norm_w[0][None, :], w_in[0].T, wa, b_alpha[0][None, :],
                                           gla_norm_w[0][None, :])
    ma = _moba(rel_bias, qaT, ka, vaT, bias_own, bias_prev, sga)
    return _out(x, ma, mb, qc, sgc, km, vm, w_out, final_norm_w[None, :])
```

```python
import functools
import math

import jax
import jax.numpy as jnp
from jax import lax
from jax.experimental import pallas as pl
from jax.experimental.pallas import tpu as pltpu

F32 = jnp.float32
BF16 = jnp.bfloat16

D_MODEL = 1024
MEM_LEN = 256
A_HEADS = 8
A_HEAD_DIM = 64
A_WIDTH = 512
MOBA_BLOCK = 256
MOBA_TOPK = 3
B_HEADS = 4
B_KEY_DIM = 64
B_VAL_DIM = 128
B_KEY_WIDTH = 256
B_WIDTH = 512
GLA_RANK = 16
GLA_TAU = 16.0
C_HEADS = 4
C_HEAD_DIM = 128
C_WIDTH = 512
MIX_WIDTH = A_WIDTH + B_WIDTH + C_WIDTH
REL_BUCKETS = 32
REL_MAX_DIST = 128
RMS_EPS = 1e-6
NEG = -1e30

LANES = 128
SUBLANES = 8
ZB_PAD = LANES
PROJ_TM = 512
OUT_TM = 512
GLA_CHUNK = 256
VMEM_LIMIT = 56 * 1024 * 1024
LOG2E = 1.0 / math.log(2.0)
HEADS_PER_TILE = LANES // A_HEAD_DIM
PV_ROWS = A_HEAD_DIM + 16
MOBA_MAX_BLOCKS = 16
QK_LOOKAHEAD = 5
FAR_UNROLL = 4


def _dot(a, b):
    return jnp.dot(a, b, preferred_element_type=F32)


def _dot_nt(a, b):
    return lax.dot_general(a, b, (((1,), (1,)), ((), ())), preferred_element_type=F32)


def _rmsnorm_rows(x, w):
    return x * lax.rsqrt(jnp.mean(x * x, axis=-1, keepdims=True) + RMS_EPS) * w


def _silu(x):
    return x * jax.nn.sigmoid(x)


def _t5_bucket(n):
    max_exact = REL_BUCKETS // 2
    nf = jnp.maximum(n, max_exact).astype(F32)
    large = max_exact + jnp.floor(jnp.log(nf / max_exact) / math.log(REL_MAX_DIST / max_exact)
                                  * (REL_BUCKETS - max_exact)).astype(jnp.int32)
    large = jnp.minimum(large, REL_BUCKETS - 1)
    return jnp.where(n < max_exact, n, large)


def _prep_kernel(n_batches, rb_ref, mem_ref, mnw_ref, w_ref, own_ref, prev_ref, km_ref, vm_ref):
    h = pl.program_id(0)

    @pl.when(h < n_batches)
    def _():
        u = _rmsnorm_rows(mem_ref[0], mnw_ref[...]).astype(BF16)
        km_ref[0] = _dot(u, w_ref[0, :, :C_WIDTH].astype(BF16)).astype(BF16)
        vm_ref[0] = _dot(u, w_ref[0, :, C_WIDTH:].astype(BF16)).astype(BF16)

    key = lax.broadcasted_iota(jnp.int32, (MOBA_BLOCK, MOBA_BLOCK), 0)
    qry = lax.broadcasted_iota(jnp.int32, (MOBA_BLOCK, MOBA_BLOCK), 1)
    rel = qry - key
    b_own = _t5_bucket(jnp.maximum(rel, 0))
    b_prev = _t5_bucket(rel + MOBA_BLOCK)
    own = jnp.zeros((MOBA_BLOCK, MOBA_BLOCK), F32)
    prev = jnp.zeros((MOBA_BLOCK, MOBA_BLOCK), F32)
    for bk in range(REL_BUCKETS):
        val = rb_ref[bk, h] * LOG2E
        own = jnp.where(b_own == bk, val, own)
        prev = jnp.where(b_prev == bk, val, prev)
    own_ref[0] = jnp.where(rel >= 0, own, NEG)
    prev_ref[0] = prev


def _prep(rel_bias, mem, mnw, w_mem_kv):
    B = mem.shape[0]
    assert B <= A_HEADS
    batch = lambda h: (jnp.minimum(h, B - 1), 0, 0)
    tile = pl.BlockSpec((1, MOBA_BLOCK, MOBA_BLOCK), lambda h: (h, 0, 0))
    kv = pl.BlockSpec((1, MEM_LEN, C_WIDTH), batch)
    return pl.pallas_call(
        functools.partial(_prep_kernel, B),
        grid=(A_HEADS,),
        in_specs=[pl.BlockSpec(memory_space=pltpu.SMEM),
                  pl.BlockSpec((1, MEM_LEN, D_MODEL), batch),
                  pl.BlockSpec((1, D_MODEL), lambda h: (0, 0)),
                  pl.BlockSpec(w_mem_kv.shape, lambda h: (0, 0, 0))],
        out_specs=[tile, tile, kv, kv],
        out_shape=[jax.ShapeDtypeStruct((A_HEADS, MOBA_BLOCK, MOBA_BLOCK), F32)] * 2
        + [jax.ShapeDtypeStruct((B, MEM_LEN, C_WIDTH), BF16)] * 2,
        compiler_params=pltpu.CompilerParams(dimension_semantics=("arbitrary",)),
        name="prep",
    )(rel_bias, mem, mnw, w_mem_kv)


_ROW_COLS = (("qa", A_WIDTH), ("ka", A_WIDTH), ("va", A_WIDTH), ("ga", A_WIDTH),
             ("qb", B_KEY_WIDTH), ("kb", B_KEY_WIDTH), ("vb", B_WIDTH), ("gb", B_WIDTH),
             ("qc", C_WIDTH), ("gc", C_WIDTH), ("zb", ZB_PAD))
_ROW_OFF = {}
_off = 0
for _name, _width in _ROW_COLS:
    _ROW_OFF[_name] = (_off, _off + _width)
    _off += _width
ROW_COLS_TOTAL = _off


W_IN_ALIGNED = 4 * A_WIDTH + 2 * B_KEY_WIDTH + 2 * B_WIDTH


def _proj_kernel(x_ref, nw_ref, w_ref, wa_ref, ba_ref, gnw_ref,
                 qaT_ref, vaT_ref, ka_ref, ga_ref, qc_ref, gc_ref, mb_ref,
                 wr_ref, qb_ref, kb_ref, vb_ref, gb_ref, g_ref, state_scr, att_scr):
    @pl.when(pl.program_id(1) == 0)
    def _():
        state_scr[...] = jnp.zeros_like(state_scr)

    @pl.when((pl.program_id(0) == 0) & (pl.program_id(1) == 0))
    def _():
        chunk = 4 * LANES
        for lo in range(0, W_IN_ALIGNED, chunk):
            wr_ref[lo:lo + chunk, :] = w_ref[lo:lo + chunk, :].astype(BF16)
        tail = W_IN_ALIGNED + GLA_RANK
        for lo in range(0, 2 * C_WIDTH, chunk):
            wr_ref[W_IN_ALIGNED + lo:W_IN_ALIGNED + lo + chunk, :] = (
                w_ref[tail + lo:tail + lo + chunk, :].astype(BF16))
        zb_lo = _ROW_OFF["zb"][0]
        wr_ref[zb_lo:zb_lo + GLA_RANK, :] = w_ref[W_IN_ALIGNED:tail, :].astype(BF16)
        wr_ref[zb_lo + GLA_RANK:, :] = jnp.zeros((ZB_PAD - GLA_RANK, wr_ref.shape[1]), BF16)

    u = _rmsnorm_rows(x_ref[0], nw_ref[...]).astype(BF16)
    tm = u.shape[0]

    def row(name, rows=slice(None)):
        lo, hi = _ROW_OFF[name]
        return _dot_nt(u[rows], wr_ref[lo:hi, :])

    def col(name, rows=slice(None)):
        lo, hi = _ROW_OFF[name]
        return _dot_nt(wr_ref[lo:hi, :], u[rows])

    zb = row("zb")
    zb_hi = zb.astype(BF16)
    zb_lo = (zb - zb_hi.astype(F32)).astype(BF16)
    wa = jnp.concatenate(
        [wa_ref[...], jnp.zeros((ZB_PAD - GLA_RANK, B_KEY_WIDTH), F32)], axis=0)
    wa_hi = wa.astype(BF16)
    wa_lo = (wa - wa_hi.astype(F32)).astype(BF16)
    z = _dot(jnp.concatenate([zb_hi, zb_lo, zb_hi], axis=1),
             jnp.concatenate([wa_hi, wa_hi, wa_lo], axis=0)) + ba_ref[...]
    g_ref[...] = (jnp.minimum(z, 0.0) - jnp.log1p(jnp.exp(-jnp.abs(z)))) * (LOG2E / GLA_TAU)
    qb_ref[...] = (row("qb") * (B_KEY_DIM ** -0.5)).astype(BF16)
    kb_ref[...] = row("kb").astype(BF16)
    vb_ref[...] = row("vb").astype(BF16)
    gb_ref[...] = _silu(row("gb")).astype(BF16)

    def emit_qaT(rows):
        qaT_ref[0, :, rows] = col("qa", rows)

    def emit_vaT(rows):
        vaT = col("va", rows).astype(BF16)
        ones = jnp.ones((PV_ROWS - A_HEAD_DIM, vaT.shape[1]), BF16)
        pieces = []
        for h in range(A_HEADS):
            pieces += [vaT[h * A_HEAD_DIM:(h + 1) * A_HEAD_DIM], ones]
        vaT_ref[0, :, rows] = jnp.concatenate(pieces, axis=0)

    def emit_ka(rows):
        ka = row("ka", rows)
        n = ka.shape[0]
        nb = MOBA_MAX_BLOCKS
        key_pos = (pl.program_id(1) * tm + rows.start
                   + lax.broadcasted_iota(jnp.int32, (n, LANES - A_HEAD_DIM), 0))
        blk_id = jnp.right_shift(key_pos, MOBA_BLOCK.bit_length() - 1)
        lane = lax.broadcasted_iota(jnp.int32, (n, LANES - A_HEAD_DIM), 1)
        onehot = jnp.where((lane < 2 * nb) & ((lane & (nb - 1)) == blk_id), 1.0, 0.0)
        pieces = []
        for h in range(A_HEADS):
            pieces += [ka[:, h * A_HEAD_DIM:(h + 1) * A_HEAD_DIM], onehot]
        ka_ref[0, rows, :] = jnp.concatenate(pieces, axis=1).astype(BF16)

    def emit_ga(rows):
        ga_ref[0, rows, :] = _silu(row("ga", rows)).astype(BF16)

    def emit_qc(rows):
        qc_ref[0, rows, :] = (row("qc", rows) * (C_HEAD_DIM ** -0.5)).astype(BF16)

    def emit_gc(rows):
        gc_ref[0, rows, :] = _silu(row("gc", rows)).astype(BF16)

    chunks = [slice(lo, lo + GLA_CHUNK) for lo in range(0, tm, GLA_CHUNK)]
    pending = [functools.partial(emit, rows) for emit in
               (emit_qaT, emit_vaT, emit_ka, emit_ga, emit_qc, emit_gc) for rows in chunks]

    def fill():
        if pending:
            pending.pop(0)()

    nw_gla = gnw_ref[...]
    for rows in chunks:
        def store_out(cols, val, rows=rows):
            mb_ref[0, rows, cols] = val

        _gla_chunk(qb_ref[rows, :], kb_ref[rows, :], vb_ref[rows, :], g_ref[rows, :], gb_ref[rows, :],
                   nw_gla, state_scr, att_scr, store_out, fill)
    while pending:
        fill()


def _proj(x, nw, w_in, wa, ba, gnw):
    B, S, D = x.shape
    tm = PROJ_TM
    row_spec = lambda width: pl.BlockSpec((1, tm, width), lambda b, i: (b, i, 0))
    col_spec = lambda rows: pl.BlockSpec((1, rows, tm), lambda b, i: (b, 0, i))
    const = lambda shape: pl.BlockSpec(shape, lambda b, i: (0,) * len(shape))
    sds = jax.ShapeDtypeStruct
    return pl.pallas_call(
        _proj_kernel,
        grid=(B, S // tm),
        in_specs=[
            pl.BlockSpec((1, tm, D), lambda b, i: (b, i, 0)),
            const((1, D)),
            pl.BlockSpec(w_in.shape, lambda b, i: (0, 0), pipeline_mode=pl.Buffered(1)),
            const((GLA_RANK, B_KEY_WIDTH)),
            const((1, B_KEY_WIDTH)),
            const((1, B_VAL_DIM)),
        ],
        out_specs=[col_spec(A_WIDTH), col_spec(A_HEADS * PV_ROWS), row_spec(A_HEADS * LANES),
                   row_spec(A_WIDTH), row_spec(C_WIDTH), row_spec(C_WIDTH), row_spec(B_WIDTH)],
        out_shape=[sds((B, A_WIDTH, S), F32), sds((B, A_HEADS * PV_ROWS, S), BF16),
                   sds((B, S, A_HEADS * LANES), BF16), sds((B, S, A_WIDTH), BF16),
                   sds((B, S, C_WIDTH), BF16), sds((B, S, C_WIDTH), BF16),
                   sds((B, S, B_WIDTH), BF16)],
        scratch_shapes=[pltpu.VMEM((ROW_COLS_TOTAL, D), BF16),
                        pltpu.VMEM((tm, B_KEY_WIDTH), BF16), pltpu.VMEM((tm, B_KEY_WIDTH), BF16),
                        pltpu.VMEM((tm, B_WIDTH), BF16), pltpu.VMEM((tm, B_WIDTH), BF16),
                        pltpu.VMEM((tm, B_KEY_WIDTH), F32),
                        pltpu.VMEM((B_KEY_WIDTH, B_WIDTH), F32),
                        pltpu.VMEM((B_HEADS, GLA_CHUNK, GLA_CHUNK), F32)],
        compiler_params=pltpu.CompilerParams(
            dimension_semantics=("arbitrary", "arbitrary"), vmem_limit_bytes=VMEM_LIMIT),
        name="proj_gla",
    )(x, nw, w_in, wa, ba, gnw)


def _moba_kernel(rb_ref, qT_ref, k_ref, vT_ref, own_ref, prev_ref, sg_ref, o_ref,
                 kmean_scr, kmbd_scr, qo_scr, qh_scr, m_scr, acc_scr, s_scr):
    i = pl.program_id(1)
    nb = kmean_scr.shape[0]
    blk = MOBA_BLOCK
    dh = A_HEAD_DIM
    n_tiles = A_HEADS // HEADS_PER_TILE

    @pl.when(i == 0)
    def _():
        for n in range(nb):
            kb = jnp.concatenate(
                [k_ref[0, n * blk:(n + 1) * blk, h * LANES:h * LANES + dh] for h in range(A_HEADS)],
                axis=1).astype(F32)
            kmean_scr[n:n + 1, :] = jnp.mean(kb, axis=0, keepdims=True)
        lane_head = jnp.right_shift(lax.broadcasted_iota(jnp.int32, (nb, A_WIDTH), 1),
                                    dh.bit_length() - 1)
        for h in range(A_HEADS):
            km = jnp.where(lane_head == h, kmean_scr[...], 0.0)
            km_hi = km.astype(BF16)
            km_lo = (km - km_hi.astype(F32)).astype(BF16)
            kmbd_scr[h * nb:(h + 1) * nb, :] = jnp.concatenate([km_hi, km_lo, km_hi], axis=1)

    q_f32 = qT_ref[0]
    q_hi = q_f32.astype(BF16)
    q_lo = (q_f32 - q_hi.astype(F32)).astype(BF16)
    gate_all = _dot(kmbd_scr[...], jnp.concatenate([q_hi, q_hi, q_lo], axis=0))
    brow = lax.broadcasted_iota(jnp.int32, (nb, blk), 0)
    browf = brow.astype(F32)
    zeros = jnp.zeros((LANES - dh - 2 * nb, blk), F32)

    for h in range(A_HEADS):
        q_h = qT_ref[0, h * dh:(h + 1) * dh, :] * (dh ** -0.5 * LOG2E)
        qo_scr[h] = jnp.concatenate([q_h, jnp.zeros((LANES - dh, blk), F32)], axis=0).astype(BF16)

    for h in range(A_HEADS):
        g = jnp.where(brow < i, gate_all[h * nb:(h + 1) * nb], NEG)
        sel = brow < 0
        for _ in range(MOBA_TOPK):
            m = jnp.max(g, axis=0, keepdims=True)
            idx = jnp.min(jnp.where(g == m, browf, float(nb)), axis=0, keepdims=True)
            pick = browf == idx
            sel = sel | pick
            g = jnp.where(pick, -jnp.inf, g)
        valid = sel & (brow < i)
        far_bias = rb_ref[REL_BUCKETS - 1, h] * LOG2E
        pen = jnp.where(brow < i - 1, jnp.where(valid, far_bias, NEG),
                        jnp.where(brow == i - 1, jnp.where(valid, 0.0, NEG), 0.0))
        pen_hi = pen.astype(BF16).astype(F32)
        q_h = qT_ref[0, h * dh:(h + 1) * dh, :] * (dh ** -0.5 * LOG2E)
        qh_scr[h] = jnp.concatenate([q_h, pen_hi, pen - pen_hi, zeros], axis=0).astype(BF16)

    no_prev = jnp.where(i == 0, NEG, 0.0)

    def logits(j, h, kind):
        rows = pl.ds(pl.multiple_of(j * blk, blk), blk)
        keys = k_ref[0, rows, h * LANES:(h + 1) * LANES]
        if kind == "own":
            return _dot(keys, qo_scr[h]) + own_ref[h]
        s = _dot(keys, qh_scr[h])
        if kind == "prev":
            return s + (prev_ref[h] + no_prev)
        return s

    def fold(j, h, s, first):
        rows = pl.ds(pl.multiple_of(j * blk, blk), blk)
        bm = jnp.max(s, axis=0, keepdims=True)
        vj = vT_ref[0, h * PV_ROWS:(h + 1) * PV_ROWS, rows]
        if first:
            m_scr[h] = bm
            acc_scr[h] = _dot(vj, jnp.exp2(s - bm).astype(BF16))
        else:
            m_old = m_scr[h]
            m_new = jnp.maximum(m_old, bm)
            m_scr[h] = m_new
            acc_scr[h] = (jnp.exp2(m_old - m_new) * acc_scr[h]
                          + _dot(vj, jnp.exp2(s - m_new).astype(BF16)))

    n_far = i - 1
    j_prev = jnp.maximum(i - 1, 0)
    items = ([(i, "own", h) for h in range(A_HEADS)] + [(j_prev, "prev", h) for h in range(A_HEADS)]
             + [(0, "far", h) for h in range(QK_LOOKAHEAD)])
    n_fold = 2 * A_HEADS
    pending = []

    def issue(n):
        j, kind, h = items[n]
        s = logits(j, h, kind)
        if n < n_fold:
            pending.append(s)
        else:
            s_scr[h] = s

    for n in range(QK_LOOKAHEAD):
        issue(n)
    for n in range(n_fold):
        issue(n + QK_LOOKAHEAD)
        j, kind, h = items[n]
        fold(j, h, pending.pop(0), kind == "own")

    def far_group(j0, n_blocks):
        j_after = jnp.minimum(j0 + n_blocks, n_far - 1)
        stream = ([(j0 + t, h) for t in range(n_blocks) for h in range(A_HEADS)]
                  + [(j_after, h) for h in range(QK_LOOKAHEAD)])
        n_items = n_blocks * A_HEADS
        pending = [s_scr[h] for h in range(QK_LOOKAHEAD)]
        for n in range(n_items):
            j, h = stream[n + QK_LOOKAHEAD]
            if n + QK_LOOKAHEAD < n_items:
                pending.append(logits(j, h, "far"))
            else:
                s_scr[h] = logits(j, h, "far")
            j, h = stream[n]
            fold(j, h, pending.pop(0), False)

    def far_groups(g, carry):
        far_group(g * FAR_UNROLL, FAR_UNROLL)
        return carry

    n_groups = jnp.maximum(n_far, 0) // FAR_UNROLL
    lax.fori_loop(0, n_groups, far_groups, 0)
    for rest in range(1, FAR_UNROLL):
        @pl.when(n_far - n_groups * FAR_UNROLL == rest)
        def _():
            far_group(n_groups * FAR_UNROLL, rest)

    for t in range(n_tiles):
        oT = []
        for hh in range(HEADS_PER_TILE):
            acc = acc_scr[t * HEADS_PER_TILE + hh]
            oT.append(acc[:dh] / acc[dh:dh + 1])
        cols = slice(t * LANES, (t + 1) * LANES)
        o_ref[0, :, cols] = (jnp.concatenate(oT, axis=0).T * sg_ref[0, :, cols]).astype(BF16)


def _moba(rel_bias, qaT, ka, vaT, bias_own, bias_prev, sga):
    B, S, _ = ka.shape
    nb = S // MOBA_BLOCK
    assert nb == MOBA_MAX_BLOCKS
    bias_spec = pl.BlockSpec((A_HEADS, MOBA_BLOCK, MOBA_BLOCK), lambda b, i: (0, 0, 0))
    return pl.pallas_call(
        _moba_kernel,
        grid=(B, nb),
        in_specs=[
            pl.BlockSpec(memory_space=pltpu.SMEM),
            pl.BlockSpec((1, A_WIDTH, MOBA_BLOCK), lambda b, i: (b, 0, i)),
            pl.BlockSpec((1, S, A_HEADS * LANES), lambda b, i: (b, 0, 0)),
            pl.BlockSpec((1, A_HEADS * PV_ROWS, S), lambda b, i: (b, 0, 0)),
            bias_spec, bias_spec,
            pl.BlockSpec((1, MOBA_BLOCK, A_WIDTH), lambda b, i: (b, i, 0)),
        ],
        out_specs=pl.BlockSpec((1, MOBA_BLOCK, A_WIDTH), lambda b, i: (b, i, 0)),
        out_shape=jax.ShapeDtypeStruct((B, S, A_WIDTH), BF16),
        scratch_shapes=[pltpu.VMEM((nb, A_WIDTH), F32),
                        pltpu.VMEM((A_HEADS * nb, 3 * A_WIDTH), BF16),
                        pltpu.VMEM((A_HEADS, LANES, MOBA_BLOCK), BF16),
                        pltpu.VMEM((A_HEADS, LANES, MOBA_BLOCK), BF16),
                        pltpu.VMEM((A_HEADS, 1, MOBA_BLOCK), F32),
                        pltpu.VMEM((A_HEADS, PV_ROWS, MOBA_BLOCK), F32),
                        pltpu.VMEM((QK_LOOKAHEAD, MOBA_BLOCK, MOBA_BLOCK), F32)],
        compiler_params=pltpu.CompilerParams(
            dimension_semantics=("arbitrary", "arbitrary"),
            vmem_limit_bytes=VMEM_LIMIT),
        name="moba",
    )(rel_bias, qaT, ka, vaT, bias_own, bias_prev, sga)


def _gla_chunk(q_bf, k_bf, v, g, sg, nw, state_scr, att_scr, store_out, fill):
    C = GLA_CHUNK
    dk, dv, H = B_KEY_DIM, B_VAL_DIM, B_HEADS
    n_levels = C.bit_length() - 1

    q = q_bf.astype(F32)
    k = k_bf.astype(F32)

    row = lax.broadcasted_iota(jnp.int32, (C, H * dk), 0)
    tt = lax.broadcasted_iota(jnp.int32, (C, C), 0)
    ss = lax.broadcasted_iota(jnp.int32, (C, C), 1)

    g_hi = g.astype(BF16)
    g_mid = (g - g_hi.astype(F32)).astype(BF16)
    g_lo = (g - g_hi.astype(F32) - g_mid.astype(F32)).astype(BF16)
    tril = jnp.where(ss <= tt, 1.0, 0.0).astype(BF16)
    b3 = _dot(tril, jnp.concatenate([g_hi, g_mid, g_lo], axis=1))
    b = b3[:, :H * dk] + b3[:, H * dk:2 * H * dk] + b3[:, 2 * H * dk:]

    def roll_rows(x, shift):
        if shift % C < SUBLANES:
            x3 = x.reshape(C // SUBLANES, SUBLANES, x.shape[1])
            return pltpu.roll(x3, shift % C, 1).reshape(x.shape)
        if (C - shift % C) < SUBLANES:
            x3 = x.reshape(C // SUBLANES, SUBLANES, x.shape[1])
            return pltpu.roll(x3, SUBLANES - (C - shift % C), 1).reshape(x.shape)
        return pltpu.roll(x, shift, 0)

    txs = tt ^ ss
    level = jnp.full((C, C), -1, jnp.int32)
    for p in range(n_levels):
        level = level + (txs >= (1 << p)).astype(jnp.int32)
    level = jnp.where(ss <= tt, level, -2)

    heads_per_tile = LANES // dk
    lane = lax.broadcasted_iota(jnp.int32, (C, H * dk), 1)
    head_in_tile = jnp.right_shift(lane, dk.bit_length() - 1) & (heads_per_tile - 1)

    def split_heads(kk):
        return [jnp.where(head_in_tile == hh, kk, 0.0).astype(BF16) for hh in range(heads_per_tile)]

    def scores(q_bf, k_split, h):
        t, hh = divmod(h, heads_per_tile)
        cols = slice(t * LANES, (t + 1) * LANES)
        return _dot_nt(q_bf[:, cols], k_split[hh][:, cols])

    on_diag = level == -1
    k_split = split_heads(k)
    for h in range(H):
        att_scr[h] = jnp.where(on_diag, scores(q_bf, k_split, h), 0.0)
    fill()

    block_end = b
    for p in range(n_levels):
        m = 1 << p
        second_half = (row & m) != 0
        r = jnp.where(second_half, roll_rows(block_end, m), block_end)
        decay = jnp.exp2(-jnp.abs(b - r))
        kd_split = split_heads(k * decay)
        if m >= SUBLANES:
            blocks = [slice(lo, lo + m) for lo in range(m, C, 2 * m)]
            take = lambda x: jnp.concatenate([x[rows] for rows in blocks], axis=0)
            qd_bf = (take(q) * take(decay)).astype(BF16)
            for h in range(H):
                s_lvl = scores(qd_bf, kd_split, h)
                for n, rows in enumerate(blocks):
                    keys = slice(rows.start - m, rows.start)
                    att_scr[h, rows, keys] = s_lvl[n * m:(n + 1) * m, keys]
        else:
            qd_bf = (q * decay).astype(BF16)
            at_level = level == p
            for h in range(H):
                att_scr[h] = jnp.where(at_level, scores(qd_bf, kd_split, h), att_scr[h])
        fill()
        if p + 1 < n_levels:
            block_end = jnp.where(second_half, block_end, roll_rows(block_end, C - m))

    state = state_scr[...]
    o_inter = _dot((q * jnp.exp2(b)).astype(BF16), state.astype(BF16))

    for h in range(H):
        cols = slice(h * dv, (h + 1) * dv)
        o_h = o_inter[:, cols] + _dot(att_scr[h].astype(BF16), v[:, cols])
        store_out(cols, (_rmsnorm_rows(o_h, nw) * sg[:, cols]).astype(BF16))

    bT = b.T
    kT = k.T
    b_last = bT[:, C - 1:C]
    kdT = (kT * jnp.exp2(b_last - bT)).astype(BF16)
    ds = _dot(kdT, v)
    srow = lax.broadcasted_iota(jnp.int32, (H * dk, H * dv), 0) // dk
    scol = lax.broadcasted_iota(jnp.int32, (H * dk, H * dv), 1) // dv
    state_scr[...] = jnp.exp2(b_last) * state + jnp.where(srow == scol, ds, 0.0)


def _out_kernel(x_ref, ma_ref, mb_ref, qc_ref, sgc_ref, km_ref, vm_ref, w_ref, fw_ref, o_ref,
                wo_ref):
    @pl.when((pl.program_id(0) == 0) & (pl.program_id(1) == 0))
    def _():
        for lo in range(0, MIX_WIDTH, 4 * LANES):
            wo_ref[lo:lo + 4 * LANES, :] = w_ref[0, lo:lo + 4 * LANES, :].astype(BF16)

    dh = C_HEAD_DIM
    qc = qc_ref[0]
    km = km_ref[0]
    vm = vm_ref[0]
    heads = [slice(h * dh, (h + 1) * dh) for h in range(C_HEADS)]
    scores = [_dot_nt(qc[:, sl], km[:, sl]) for sl in heads]
    h_ab = (_dot(ma_ref[0], wo_ref[0:A_WIDTH, :])
            + _dot(mb_ref[0], wo_ref[A_WIDTH:A_WIDTH + B_WIDTH, :]))
    oc = []
    for s, sl in zip(scores, heads):
        p = jnp.exp(s - jnp.max(s, axis=-1, keepdims=True))
        l = jnp.sum(p, axis=-1, keepdims=True)
        oc.append(_dot(p.astype(BF16), vm[:, sl]) / l)
    mc = (jnp.concatenate(oc, axis=-1) * sgc_ref[0]).astype(BF16)
    h_new = x_ref[0] + h_ab + _dot(mc, wo_ref[A_WIDTH + B_WIDTH:, :])
    o_ref[0] = _rmsnorm_rows(h_new, fw_ref[...])


def _out(x, ma, mb, qc, sgc, km, vm, w_out, fw):
    B, S, D = x.shape
    tm = OUT_TM
    row = lambda width: pl.BlockSpec((1, tm, width), lambda b, i: (b, i, 0))
    mem = pl.BlockSpec((1, MEM_LEN, C_WIDTH), lambda b, i: (b, 0, 0))
    return pl.pallas_call(
        _out_kernel,
        grid=(B, S // tm),
        in_specs=[row(D), row(A_WIDTH), row(B_WIDTH), row(C_WIDTH), row(C_WIDTH), mem, mem,
                  pl.BlockSpec(w_out.shape, lambda b, i: (0, 0, 0), pipeline_mode=pl.Buffered(1)),
                  pl.BlockSpec((1, D), lambda b, i: (0, 0))],
        out_specs=row(D),
        out_shape=jax.ShapeDtypeStruct((B, S, D), F32),
        scratch_shapes=[pltpu.VMEM((MIX_WIDTH, D), BF16)],
        compiler_params=pltpu.CompilerParams(
            dimension_semantics=("arbitrary", "arbitrary"), vmem_limit_bytes=VMEM_LIMIT),
        name="out",
    )(x, ma, mb, qc, sgc, km, vm, w_out, fw)


def kernel(x, mem, norm_w, w_in, w_alpha2, b_alpha, gla_norm_w, mem_norm_w, w_mem_kv, w_out,
           rel_bias, final_norm_w):
    assert norm_w.shape[0] == 1, "single layer"
    assert w_in.shape[2] == W_IN_ALIGNED + GLA_RANK + 2 * C_WIDTH

    bias_own, bias_prev, km, vm = _prep(rel_bias, mem, mem_norm_w[0][None, :], w_mem_kv)
    qaT, vaT, ka, sga, qc, sgc, mb = _proj(x, norm_w[0][None, :], w_in[0].T, w_alpha2[0], b_alpha[0][None, :],
                                           gla_norm_w[0][None, :])
    ma = _moba(rel_bias, qaT, ka, vaT, bias_own, bias_prev, sga)
    return _out(x, ma, mb, qc, sgc, km, vm, w_out, final_norm_w[None, :])
```

```python
import functools
import math

import jax
import jax.numpy as jnp
from jax import lax
from jax.experimental import pallas as pl
from jax.experimental.pallas import tpu as pltpu

F32 = jnp.float32
BF16 = jnp.bfloat16

D_MODEL = 1024
MEM_LEN = 256
A_HEADS = 8
A_HEAD_DIM = 64
A_WIDTH = 512
MOBA_BLOCK = 256
MOBA_TOPK = 3
B_HEADS = 4
B_KEY_DIM = 64
B_VAL_DIM = 128
B_KEY_WIDTH = 256
B_WIDTH = 512
GLA_RANK = 16
GLA_TAU = 16.0
C_HEADS = 4
C_HEAD_DIM = 128
C_WIDTH = 512
MIX_WIDTH = A_WIDTH + B_WIDTH + C_WIDTH
REL_BUCKETS = 32
REL_MAX_DIST = 128
RMS_EPS = 1e-6
NEG = -1e30

LANES = 128
SUBLANES = 8
ZB_PAD = LANES
PROJ_TM = 512
OUT_TM = 512
GLA_CHUNK = 256
VMEM_LIMIT = 56 * 1024 * 1024
LOG2E = 1.0 / math.log(2.0)
HEADS_PER_TILE = LANES // A_HEAD_DIM
PV_ROWS = A_HEAD_DIM + 16
MOBA_MAX_BLOCKS = 16
QK_LOOKAHEAD = 5
FAR_UNROLL = 4


def _dot(a, b):
    return jnp.dot(a, b, preferred_element_type=F32)


def _dot_nt(a, b):
    return lax.dot_general(a, b, (((1,), (1,)), ((), ())), preferred_element_type=F32)


def _rmsnorm_rows(x, w):
    return x * lax.rsqrt(jnp.mean(x * x, axis=-1, keepdims=True) + RMS_EPS) * w


def _silu(x):
    return x * jax.nn.sigmoid(x)


def _t5_bucket(n):
    max_exact = REL_BUCKETS // 2
    nf = jnp.maximum(n, max_exact).astype(F32)
    large = max_exact + jnp.floor(jnp.log(nf / max_exact) / math.log(REL_MAX_DIST / max_exact)
                                  * (REL_BUCKETS - max_exact)).astype(jnp.int32)
    large = jnp.minimum(large, REL_BUCKETS - 1)
    return jnp.where(n < max_exact, n, large)


def _prep_kernel(n_batches, rb_ref, mem_ref, mnw_ref, w_ref, own_ref, prev_ref, km_ref, vm_ref):
    h = pl.program_id(0)

    @pl.when(h < n_batches)
    def _():
        u = _rmsnorm_rows(mem_ref[0], mnw_ref[...]).astype(BF16)
        km_ref[0] = _dot(u, w_ref[0, :, :C_WIDTH].astype(BF16)).astype(BF16)
        vm_ref[0] = _dot(u, w_ref[0, :, C_WIDTH:].astype(BF16)).astype(BF16)

    key = lax.broadcasted_iota(jnp.int32, (MOBA_BLOCK, MOBA_BLOCK), 0)
    qry = lax.broadcasted_iota(jnp.int32, (MOBA_BLOCK, MOBA_BLOCK), 1)
    rel = qry - key
    b_own = _t5_bucket(jnp.maximum(rel, 0))
    half = MOBA_BLOCK // 2
    assert REL_MAX_DIST <= half
    corner = (lax.broadcasted_iota(jnp.int32, (half, half), 1)
              - lax.broadcasted_iota(jnp.int32, (half, half), 0) + half)
    b_prev = _t5_bucket(corner)
    own = jnp.zeros((MOBA_BLOCK, MOBA_BLOCK), F32)
    prev = jnp.zeros((half, half), F32)
    for bk in range(REL_BUCKETS):
        val = rb_ref[bk, h] * LOG2E
        own = jnp.where(b_own == bk, val, own)
        prev = jnp.where(b_prev == bk, val, prev)
    own_ref[0] = jnp.where(rel >= 0, own, NEG)
    prev_ref[0] = prev - rb_ref[REL_BUCKETS - 1, h] * LOG2E


def _prep(rel_bias, mem, mnw, w_mem_kv):
    B = mem.shape[0]
    assert B <= A_HEADS
    batch = lambda h: (jnp.minimum(h, B - 1), 0, 0)
    half = MOBA_BLOCK // 2
    tile = pl.BlockSpec((1, MOBA_BLOCK, MOBA_BLOCK), lambda h: (h, 0, 0))
    corner = pl.BlockSpec((1, half, half), lambda h: (h, 0, 0))
    kv = pl.BlockSpec((1, MEM_LEN, C_WIDTH), batch)
    return pl.pallas_call(
        functools.partial(_prep_kernel, B),
        grid=(A_HEADS,),
        in_specs=[pl.BlockSpec(memory_space=pltpu.SMEM),
                  pl.BlockSpec((1, MEM_LEN, D_MODEL), batch),
                  pl.BlockSpec((1, D_MODEL), lambda h: (0, 0)),
                  pl.BlockSpec(w_mem_kv.shape, lambda h: (0, 0, 0))],
        out_specs=[tile, corner, kv, kv],
        out_shape=[jax.ShapeDtypeStruct((A_HEADS, MOBA_BLOCK, MOBA_BLOCK), F32),
                   jax.ShapeDtypeStruct((A_HEADS, half, half), F32)]
        + [jax.ShapeDtypeStruct((B, MEM_LEN, C_WIDTH), BF16)] * 2,
        compiler_params=pltpu.CompilerParams(dimension_semantics=("arbitrary",)),
        name="prep",
    )(rel_bias, mem, mnw, w_mem_kv)


_ROW_COLS = (("qa", A_WIDTH), ("ka", A_WIDTH), ("va", A_WIDTH), ("ga", A_WIDTH),
             ("qb", B_KEY_WIDTH), ("kb", B_KEY_WIDTH), ("vb", B_WIDTH), ("gb", B_WIDTH),
             ("qc", C_WIDTH), ("gc", C_WIDTH), ("zb", ZB_PAD))
_ROW_OFF = {}
_off = 0
for _name, _width in _ROW_COLS:
    _ROW_OFF[_name] = (_off, _off + _width)
    _off += _width
ROW_COLS_TOTAL = _off


W_IN_ALIGNED = 4 * A_WIDTH + 2 * B_KEY_WIDTH + 2 * B_WIDTH


def _proj_kernel(x_ref, nw_ref, w_ref, wa_ref, ba_ref, gnw_ref,
                 qaT_ref, vaT_ref, ka_ref, ga_ref, qc_ref, gc_ref, mb_ref,
                 wr_ref, qb_ref, kb_ref, vb_ref, gb_ref, g_ref, state_scr, att_scr):
    @pl.when(pl.program_id(1) == 0)
    def _():
        state_scr[...] = jnp.zeros_like(state_scr)

    @pl.when((pl.program_id(0) == 0) & (pl.program_id(1) == 0))
    def _():
        chunk = 4 * LANES
        for lo in range(0, W_IN_ALIGNED, chunk):
            wr_ref[lo:lo + chunk, :] = w_ref[lo:lo + chunk, :].astype(BF16)
        tail = W_IN_ALIGNED + GLA_RANK
        for lo in range(0, 2 * C_WIDTH, chunk):
            wr_ref[W_IN_ALIGNED + lo:W_IN_ALIGNED + lo + chunk, :] = (
                w_ref[tail + lo:tail + lo + chunk, :].astype(BF16))
        zb_lo = _ROW_OFF["zb"][0]
        wr_ref[zb_lo:zb_lo + GLA_RANK, :] = w_ref[W_IN_ALIGNED:tail, :].astype(BF16)
        wr_ref[zb_lo + GLA_RANK:, :] = jnp.zeros((ZB_PAD - GLA_RANK, wr_ref.shape[1]), BF16)

    u = _rmsnorm_rows(x_ref[0], nw_ref[...]).astype(BF16)
    tm = u.shape[0]

    def row(name, rows=slice(None)):
        lo, hi = _ROW_OFF[name]
        return _dot_nt(u[rows], wr_ref[lo:hi, :])

    def col(name, rows=slice(None)):
        lo, hi = _ROW_OFF[name]
        return _dot_nt(wr_ref[lo:hi, :], u[rows])

    zb = row("zb")
    zb_hi = zb.astype(BF16)
    zb_lo = (zb - zb_hi.astype(F32)).astype(BF16)
    wa = jnp.concatenate(
        [wa_ref[...], jnp.zeros((ZB_PAD - GLA_RANK, B_KEY_WIDTH), F32)], axis=0)
    wa_hi = wa.astype(BF16)
    wa_lo = (wa - wa_hi.astype(F32)).astype(BF16)
    z = _dot(jnp.concatenate([zb_hi, zb_lo, zb_hi], axis=1),
             jnp.concatenate([wa_hi, wa_hi, wa_lo], axis=0)) + ba_ref[...]
    g_ref[...] = (jnp.minimum(z, 0.0) - jnp.log1p(jnp.exp(-jnp.abs(z)))) * (LOG2E / GLA_TAU)
    qb_ref[...] = (row("qb") * (B_KEY_DIM ** -0.5)).astype(BF16)
    kb_ref[...] = row("kb").astype(BF16)
    vb_ref[...] = row("vb").astype(BF16)
    gb_ref[...] = _silu(row("gb")).astype(BF16)

    def emit_qaT(rows):
        qaT_ref[0, :, rows] = col("qa", rows)

    def emit_vaT(rows):
        vaT = col("va", rows).astype(BF16)
        ones = jnp.ones((PV_ROWS - A_HEAD_DIM, vaT.shape[1]), BF16)
        pieces = []
        for h in range(A_HEADS):
            pieces += [vaT[h * A_HEAD_DIM:(h + 1) * A_HEAD_DIM], ones]
        vaT_ref[0, :, rows] = jnp.concatenate(pieces, axis=0)

    def emit_ka(rows):
        ka = row("ka", rows)
        n = ka.shape[0]
        nb = MOBA_MAX_BLOCKS
        key_pos = (pl.program_id(1) * tm + rows.start
                   + lax.broadcasted_iota(jnp.int32, (n, LANES - A_HEAD_DIM), 0))
        blk_id = jnp.right_shift(key_pos, MOBA_BLOCK.bit_length() - 1)
        lane = lax.broadcasted_iota(jnp.int32, (n, LANES - A_HEAD_DIM), 1)
        onehot = jnp.where((lane < 2 * nb) & ((lane & (nb - 1)) == blk_id), 1.0, 0.0)
        pieces = []
        for h in range(A_HEADS):
            pieces += [ka[:, h * A_HEAD_DIM:(h + 1) * A_HEAD_DIM], onehot]
        ka_ref[0, rows, :] = jnp.concatenate(pieces, axis=1).astype(BF16)

    def emit_ga(rows):
        ga_ref[0, rows, :] = _silu(row("ga", rows)).astype(BF16)

    def emit_qc(rows):
        qc_ref[0, rows, :] = (row("qc", rows) * (C_HEAD_DIM ** -0.5)).astype(BF16)

    def emit_gc(rows):
        gc_ref[0, rows, :] = _silu(row("gc", rows)).astype(BF16)

    chunks = [slice(lo, lo + GLA_CHUNK) for lo in range(0, tm, GLA_CHUNK)]
    pending = [functools.partial(emit, rows) for emit in
               (emit_qaT, emit_vaT, emit_ka, emit_ga, emit_qc, emit_gc) for rows in chunks]

    def fill():
        if pending:
            pending.pop(0)()

    nw_gla = gnw_ref[...]
    for rows in chunks:
        def store_out(cols, val, rows=rows):
            mb_ref[0, rows, cols] = val

        _gla_chunk(qb_ref[rows, :], kb_ref[rows, :], vb_ref[rows, :], g_ref[rows, :], gb_ref[rows, :],
                   nw_gla, state_scr, att_scr, store_out, fill)
    while pending:
        fill()


def _proj(x, nw, w_in, wa, ba, gnw):
    B, S, D = x.shape
    tm = PROJ_TM
    row_spec = lambda width: pl.BlockSpec((1, tm, width), lambda b, i: (b, i, 0))
    col_spec = lambda rows: pl.BlockSpec((1, rows, tm), lambda b, i: (b, 0, i))
    const = lambda shape: pl.BlockSpec(shape, lambda b, i: (0,) * len(shape))
    sds = jax.ShapeDtypeStruct
    return pl.pallas_call(
        _proj_kernel,
        grid=(B, S // tm),
        in_specs=[
            pl.BlockSpec((1, tm, D), lambda b, i: (b, i, 0)),
            const((1, D)),
            pl.BlockSpec(w_in.shape, lambda b, i: (0, 0), pipeline_mode=pl.Buffered(1)),
            const((GLA_RANK, B_KEY_WIDTH)),
            const((1, B_KEY_WIDTH)),
            const((1, B_VAL_DIM)),
        ],
        out_specs=[col_spec(A_WIDTH), col_spec(A_HEADS * PV_ROWS), row_spec(A_HEADS * LANES),
                   row_spec(A_WIDTH), row_spec(C_WIDTH), row_spec(C_WIDTH), row_spec(B_WIDTH)],
        out_shape=[sds((B, A_WIDTH, S), F32), sds((B, A_HEADS * PV_ROWS, S), BF16),
                   sds((B, S, A_HEADS * LANES), BF16), sds((B, S, A_WIDTH), BF16),
                   sds((B, S, C_WIDTH), BF16), sds((B, S, C_WIDTH), BF16),
                   sds((B, S, B_WIDTH), BF16)],
        scratch_shapes=[pltpu.VMEM((ROW_COLS_TOTAL, D), BF16),
                        pltpu.VMEM((tm, B_KEY_WIDTH), BF16), pltpu.VMEM((tm, B_KEY_WIDTH), BF16),
                        pltpu.VMEM((tm, B_WIDTH), BF16), pltpu.VMEM((tm, B_WIDTH), BF16),
                        pltpu.VMEM((tm, B_KEY_WIDTH), F32),
                        pltpu.VMEM((B_KEY_WIDTH, B_WIDTH), F32),
                        pltpu.VMEM((B_HEADS, GLA_CHUNK, GLA_CHUNK), F32)],
        compiler_params=pltpu.CompilerParams(
            dimension_semantics=("arbitrary", "arbitrary"), vmem_limit_bytes=VMEM_LIMIT),
        name="proj_gla",
    )(x, nw, w_in, wa, ba, gnw)


def _moba_kernel(rb_ref, qT_ref, k_ref, vT_ref, own_ref, prev_ref, sg_ref, o_ref,
                 kmean_scr, kmbd_scr, qo_scr, qh_scr, m_scr, acc_scr, s_scr):
    i = pl.program_id(1)
    nb = kmean_scr.shape[0]
    blk = MOBA_BLOCK
    dh = A_HEAD_DIM
    n_tiles = A_HEADS // HEADS_PER_TILE

    @pl.when(i == 0)
    def _():
        for n in range(nb):
            kb = jnp.concatenate(
                [k_ref[0, n * blk:(n + 1) * blk, h * LANES:h * LANES + dh] for h in range(A_HEADS)],
                axis=1).astype(F32)
            kmean_scr[n:n + 1, :] = jnp.mean(kb, axis=0, keepdims=True)
        lane_head = jnp.right_shift(lax.broadcasted_iota(jnp.int32, (nb, A_WIDTH), 1),
                                    dh.bit_length() - 1)
        for h in range(A_HEADS):
            km = jnp.where(lane_head == h, kmean_scr[...], 0.0)
            km_hi = km.astype(BF16)
            km_lo = (km - km_hi.astype(F32)).astype(BF16)
            kmbd_scr[h * nb:(h + 1) * nb, :] = jnp.concatenate([km_hi, km_lo, km_hi], axis=1)

    q_f32 = qT_ref[0]
    q_hi = q_f32.astype(BF16)
    q_lo = (q_f32 - q_hi.astype(F32)).astype(BF16)
    gate_all = _dot(kmbd_scr[...], jnp.concatenate([q_hi, q_hi, q_lo], axis=0))
    brow = lax.broadcasted_iota(jnp.int32, (nb, blk), 0)
    browf = brow.astype(F32)
    zeros = jnp.zeros((LANES - dh - 2 * nb, blk), F32)

    for h in range(A_HEADS):
        q_h = qT_ref[0, h * dh:(h + 1) * dh, :] * (dh ** -0.5 * LOG2E)
        qo_scr[h] = jnp.concatenate([q_h, jnp.zeros((LANES - dh, blk), F32)], axis=0).astype(BF16)

    for h in range(A_HEADS):
        g = jnp.where(brow < i, gate_all[h * nb:(h + 1) * nb], NEG)
        sel = brow < 0
        for _ in range(MOBA_TOPK):
            m = jnp.max(g, axis=0, keepdims=True)
            idx = jnp.min(jnp.where(g == m, browf, float(nb)), axis=0, keepdims=True)
            pick = browf == idx
            sel = sel | pick
            g = jnp.where(pick, -jnp.inf, g)
        valid = sel & (brow < i)
        far_bias = rb_ref[REL_BUCKETS - 1, h] * LOG2E
        pen = jnp.where(valid, far_bias, NEG)
        pen_hi = pen.astype(BF16).astype(F32)
        q_h = qT_ref[0, h * dh:(h + 1) * dh, :] * (dh ** -0.5 * LOG2E)
        qh_scr[h] = jnp.concatenate([q_h, pen_hi, pen - pen_hi, zeros], axis=0).astype(BF16)

    half = blk // 2

    def logits(j, h, kind):
        rows = pl.ds(pl.multiple_of(j * blk, blk), blk)
        keys = k_ref[0, rows, h * LANES:(h + 1) * LANES]
        if kind == "own":
            return _dot(keys, qo_scr[h]) + own_ref[h]
        s = _dot(keys, qh_scr[h])
        if kind == "prev":
            near = s[half:, :half] + prev_ref[h]
            s = jnp.concatenate(
                [s[:half], jnp.concatenate([near, s[half:, half:]], axis=1)], axis=0)
        return s

    def fold(j, h, s, first):
        rows = pl.ds(pl.multiple_of(j * blk, blk), blk)
        bm = jnp.max(s, axis=0, keepdims=True)
        vj = vT_ref[0, h * PV_ROWS:(h + 1) * PV_ROWS, rows]
        if first:
            m_scr[h] = bm
            acc_scr[h] = _dot(vj, jnp.exp2(s - bm).astype(BF16))
        else:
            m_old = m_scr[h]
            m_new = jnp.maximum(m_old, bm)
            m_scr[h] = m_new
            acc_scr[h] = (jnp.exp2(m_old - m_new) * acc_scr[h]
                          + _dot(vj, jnp.exp2(s - m_new).astype(BF16)))

    n_far = i - 1
    j_prev = jnp.maximum(i - 1, 0)
    items = ([(i, "own", h) for h in range(A_HEADS)] + [(j_prev, "prev", h) for h in range(A_HEADS)]
             + [(0, "far", h) for h in range(QK_LOOKAHEAD)])
    n_fold = 2 * A_HEADS
    pending = []

    def issue(n):
        j, kind, h = items[n]
        s = logits(j, h, kind)
        if n < n_fold:
            pending.append(s)
        else:
            s_scr[h] = s

    for n in range(QK_LOOKAHEAD):
        issue(n)
    for n in range(n_fold):
        issue(n + QK_LOOKAHEAD)
        j, kind, h = items[n]
        fold(j, h, pending.pop(0), kind == "own")

    def far_group(j0, n_blocks):
        j_after = jnp.minimum(j0 + n_blocks, n_far - 1)
        stream = ([(j0 + t, h) for t in range(n_blocks) for h in range(A_HEADS)]
                  + [(j_after, h) for h in range(QK_LOOKAHEAD)])
        n_items = n_blocks * A_HEADS
        pending = [s_scr[h] for h in range(QK_LOOKAHEAD)]
        for n in range(n_items):
            j, h = stream[n + QK_LOOKAHEAD]
            if n + QK_LOOKAHEAD < n_items:
                pending.append(logits(j, h, "far"))
            else:
                s_scr[h] = logits(j, h, "far")
            j, h = stream[n]
            fold(j, h, pending.pop(0), False)

    def far_groups(g, carry):
        far_group(g * FAR_UNROLL, FAR_UNROLL)
        return carry

    n_groups = jnp.maximum(n_far, 0) // FAR_UNROLL
    lax.fori_loop(0, n_groups, far_groups, 0)
    for rest in range(1, FAR_UNROLL):
        @pl.when(n_far - n_groups * FAR_UNROLL == rest)
        def _():
            far_group(n_groups * FAR_UNROLL, rest)

    for t in range(n_tiles):
        oT = []
        for hh in range(HEADS_PER_TILE):
            acc = acc_scr[t * HEADS_PER_TILE + hh]
            oT.append(acc[:dh] / acc[dh:dh + 1])
        cols = slice(t * LANES, (t + 1) * LANES)
        o_ref[0, :, cols] = (jnp.concatenate(oT, axis=0).T * sg_ref[0, :, cols]).astype(BF16)


def _moba(rel_bias, qaT, ka, vaT, bias_own, bias_prev, sga):
    B, S, _ = ka.shape
    nb = S // MOBA_BLOCK
    assert nb == MOBA_MAX_BLOCKS
    bias_spec = pl.BlockSpec((A_HEADS, MOBA_BLOCK, MOBA_BLOCK), lambda b, i: (0, 0, 0))
    corner_spec = pl.BlockSpec((A_HEADS, MOBA_BLOCK // 2, MOBA_BLOCK // 2), lambda b, i: (0, 0, 0))
    return pl.pallas_call(
        _moba_kernel,
        grid=(B, nb),
        in_specs=[
            pl.BlockSpec(memory_space=pltpu.SMEM),
            pl.BlockSpec((1, A_WIDTH, MOBA_BLOCK), lambda b, i: (b, 0, i)),
            pl.BlockSpec((1, S, A_HEADS * LANES), lambda b, i: (b, 0, 0)),
            pl.BlockSpec((1, A_HEADS * PV_ROWS, S), lambda b, i: (b, 0, 0)),
            bias_spec, corner_spec,
            pl.BlockSpec((1, MOBA_BLOCK, A_WIDTH), lambda b, i: (b, i, 0)),
        ],
        out_specs=pl.BlockSpec((1, MOBA_BLOCK, A_WIDTH), lambda b, i: (b, i, 0)),
        out_shape=jax.ShapeDtypeStruct((B, S, A_WIDTH), BF16),
        scratch_shapes=[pltpu.VMEM((nb, A_WIDTH), F32),
                        pltpu.VMEM((A_HEADS * nb, 3 * A_WIDTH), BF16),
                        pltpu.VMEM((A_HEADS, LANES, MOBA_BLOCK), BF16),
                        pltpu.VMEM((A_HEADS, LANES, MOBA_BLOCK), BF16),
                        pltpu.VMEM((A_HEADS, 1, MOBA_BLOCK), F32),
                        pltpu.VMEM((A_HEADS, PV_ROWS, MOBA_BLOCK), F32),
                        pltpu.VMEM((QK_LOOKAHEAD, MOBA_BLOCK, MOBA_BLOCK), F32)],
        compiler_params=pltpu.CompilerParams(
            dimension_semantics=("arbitrary", "arbitrary"),
            vmem_limit_bytes=VMEM_LIMIT),
        name="moba",
    )(rel_bias, qaT, ka, vaT, bias_own, bias_prev, sga)


def _gla_chunk(q_bf, k_bf, v, g, sg, nw, state_scr, att_scr, store_out, fill):
    C = GLA_CHUNK
    dk, dv, H = B_KEY_DIM, B_VAL_DIM, B_HEADS
    n_levels = C.bit_length() - 1

    q = q_bf.astype(F32)
    k = k_bf.astype(F32)

    row = lax.broadcasted_iota(jnp.int32, (C, H * dk), 0)
    tt = lax.broadcasted_iota(jnp.int32, (C, C), 0)
    ss = lax.broadcasted_iota(jnp.int32, (C, C), 1)

    g_hi = g.astype(BF16)
    g_mid = (g - g_hi.astype(F32)).astype(BF16)
    g_lo = (g - g_hi.astype(F32) - g_mid.astype(F32)).astype(BF16)
    tril = jnp.where(ss <= tt, 1.0, 0.0).astype(BF16)
    b3 = _dot(tril, jnp.concatenate([g_hi, g_mid, g_lo], axis=1))
    b = b3[:, :H * dk] + b3[:, H * dk:2 * H * dk] + b3[:, 2 * H * dk:]

    def roll_rows(x, shift):
        if shift % C < SUBLANES:
            x3 = x.reshape(C // SUBLANES, SUBLANES, x.shape[1])
            return pltpu.roll(x3, shift % C, 1).reshape(x.shape)
        if (C - shift % C) < SUBLANES:
            x3 = x.reshape(C // SUBLANES, SUBLANES, x.shape[1])
            return pltpu.roll(x3, SUBLANES - (C - shift % C), 1).reshape(x.shape)
        return pltpu.roll(x, shift, 0)

    txs = tt ^ ss
    level = jnp.full((C, C), -1, jnp.int32)
    for p in range(n_levels):
        level = level + (txs >= (1 << p)).astype(jnp.int32)
    level = jnp.where(ss <= tt, level, -2)

    heads_per_tile = LANES // dk
    lane = lax.broadcasted_iota(jnp.int32, (C, H * dk), 1)
    head_in_tile = jnp.right_shift(lane, dk.bit_length() - 1) & (heads_per_tile - 1)

    def split_heads(kk):
        return [jnp.where(head_in_tile == hh, kk, 0.0).astype(BF16) for hh in range(heads_per_tile)]

    def scores(q_bf, k_split, h):
        t, hh = divmod(h, heads_per_tile)
        cols = slice(t * LANES, (t + 1) * LANES)
        return _dot_nt(q_bf[:, cols], k_split[hh][:, cols])

    on_diag = level == -1
    k_split = split_heads(k)
    for h in range(H):
        att_scr[h] = jnp.where(on_diag, scores(q_bf, k_split, h), 0.0)
    fill()

    block_end = b
    for p in range(n_levels):
        m = 1 << p
        second_half = (row & m) != 0
        r = jnp.where(second_half, roll_rows(block_end, m), block_end)
        decay = jnp.exp2(-jnp.abs(b - r))
        kd_split = split_heads(k * decay)
        if m >= SUBLANES:
            blocks = [slice(lo, lo + m) for lo in range(m, C, 2 * m)]
            take = lambda x: jnp.concatenate([x[rows] for rows in blocks], axis=0)
            qd_bf = (take(q) * take(decay)).astype(BF16)
            for h in range(H):
                s_lvl = scores(qd_bf, kd_split, h)
                for n, rows in enumerate(blocks):
                    keys = slice(rows.start - m, rows.start)
                    att_scr[h, rows, keys] = s_lvl[n * m:(n + 1) * m, keys]
        else:
            qd_bf = (q * decay).astype(BF16)
            at_level = level == p
            for h in range(H):
                att_scr[h] = jnp.where(at_level, scores(qd_bf, kd_split, h), att_scr[h])
        fill()
        if p + 1 < n_levels:
            block_end = jnp.where(second_half, block_end, roll_rows(block_end, C - m))

    state = state_scr[...]
    o_inter = _dot((q * jnp.exp2(b)).astype(BF16), state.astype(BF16))

    for h in range(H):
        cols = slice(h * dv, (h + 1) * dv)
        o_h = o_inter[:, cols] + _dot(att_scr[h].astype(BF16), v[:, cols])
        store_out(cols, (_rmsnorm_rows(o_h, nw) * sg[:, cols]).astype(BF16))

    bT = b.T
    kT = k.T
    b_last = bT[:, C - 1:C]
    kdT = (kT * jnp.exp2(b_last - bT)).astype(BF16)
    ds = _dot(kdT, v)
    srow = lax.broadcasted_iota(jnp.int32, (H * dk, H * dv), 0) // dk
    scol = lax.broadcasted_iota(jnp.int32, (H * dk, H * dv), 1) // dv
    state_scr[...] = jnp.exp2(b_last) * state + jnp.where(srow == scol, ds, 0.0)


def _out_kernel(x_ref, ma_ref, mb_ref, qc_ref, sgc_ref, km_ref, vm_ref, w_ref, fw_ref, o_ref,
                wo_ref):
    @pl.when((pl.program_id(0) == 0) & (pl.program_id(1) == 0))
    def _():
        for lo in range(0, MIX_WIDTH, 4 * LANES):
            wo_ref[lo:lo + 4 * LANES, :] = w_ref[0, lo:lo + 4 * LANES, :].astype(BF16)

    dh = C_HEAD_DIM
    qc = qc_ref[0]
    km = km_ref[0]
    vm = vm_ref[0]
    heads = [slice(h * dh, (h + 1) * dh) for h in range(C_HEADS)]
    scores = [_dot_nt(qc[:, sl], km[:, sl]) for sl in heads]
    h_ab = (_dot(ma_ref[0], wo_ref[0:A_WIDTH, :])
            + _dot(mb_ref[0], wo_ref[A_WIDTH:A_WIDTH + B_WIDTH, :]))
    oc = []
    for s, sl in zip(scores, heads):
        p = jnp.exp(s - jnp.max(s, axis=-1, keepdims=True))
        l = jnp.sum(p, axis=-1, keepdims=True)
        oc.append(_dot(p.astype(BF16), vm[:, sl]) / l)
    mc = (jnp.concatenate(oc, axis=-1) * sgc_ref[0]).astype(BF16)
    h_new = x_ref[0] + h_ab + _dot(mc, wo_ref[A_WIDTH + B_WIDTH:, :])
    o_ref[0] = _rmsnorm_rows(h_new, fw_ref[...])


def _out(x, ma, mb, qc, sgc, km, vm, w_out, fw):
    B, S, D = x.shape
    tm = OUT_TM
    row = lambda width: pl.BlockSpec((1, tm, width), lambda b, i: (b, i, 0))
    mem = pl.BlockSpec((1, MEM_LEN, C_WIDTH), lambda b, i: (b, 0, 0))
    return pl.pallas_call(
        _out_kernel,
        grid=(B, S // tm),
        in_specs=[row(D), row(A_WIDTH), row(B_WIDTH), row(C_WIDTH), row(C_WIDTH), mem, mem,
                  pl.BlockSpec(w_out.shape, lambda b, i: (0, 0, 0), pipeline_mode=pl.Buffered(1)),
                  pl.BlockSpec((1, D), lambda b, i: (0, 0))],
        out_specs=row(D),
        out_shape=jax.ShapeDtypeStruct((B, S, D), F32),
        scratch_shapes=[pltpu.VMEM((MIX_WIDTH, D), BF16)],
        compiler_params=pltpu.CompilerParams(
            dimension_semantics=("arbitrary", "arbitrary"), vmem_limit_bytes=VMEM_LIMIT),
        name="out",
    )(x, ma, mb, qc, sgc, km, vm, w_out, fw)


def kernel(x, mem, norm_w, w_in, w_alpha2, b_alpha, gla_norm_w, mem_norm_w, w_mem_kv, w_out,
           rel_bias, final_norm_w):
    assert norm_w.shape[0] == 1, "single layer"
    assert w_in.shape[2] == W_IN_ALIGNED + GLA_RANK + 2 * C_WIDTH

    bias_own, bias_prev, km, vm = _prep(rel_bias, mem, mem_norm_w[0][None, :], w_mem_kv)
    qaT, vaT, ka, sga, qc, sgc, mb = _proj(x, norm_w[0][None, :], w_in[0].T, w_alpha2[0], b_alpha[0][None, :],
                                           gla_norm_w[0][None, :])
    ma = _moba(rel_bias, qaT, ka, vaT, bias_own, bias_prev, sga)
    return _out(x, ma, mb, qc, sgc, km, vm, w_out, final_norm_w[None, :])
```

```python
import functools
import math

import jax
import jax.numpy as jnp
from jax import lax
from jax.experimental import pallas as pl
from jax.experimental.pallas import tpu as pltpu

F32 = jnp.float32
BF16 = jnp.bfloat16

D_MODEL = 1024
MEM_LEN = 256
A_HEADS = 8
A_HEAD_DIM = 64
A_WIDTH = 512
MOBA_BLOCK = 256
MOBA_TOPK = 3
B_HEADS = 4
B_KEY_DIM = 64
B_VAL_DIM = 128
B_KEY_WIDTH = 256
B_WIDTH = 512
GLA_RANK = 16
GLA_TAU = 16.0
C_HEADS = 4
C_HEAD_DIM = 128
C_WIDTH = 512
MIX_WIDTH = A_WIDTH + B_WIDTH + C_WIDTH
REL_BUCKETS = 32
REL_MAX_DIST = 128
RMS_EPS = 1e-6
NEG = -1e30

LANES = 128
SUBLANES = 8
PROJ_TM = 512
OUT_TM = 512
GLA_CHUNK = 256
VMEM_LIMIT = 56 * 1024 * 1024
LOG2E = 1.0 / math.log(2.0)
HEADS_PER_TILE = LANES // A_HEAD_DIM
PV_ROWS = A_HEAD_DIM + 16
MOBA_MAX_BLOCKS = 16
QK_LOOKAHEAD = 5
FAR_UNROLL = 4


def _dot(a, b):
    return jnp.dot(a, b, preferred_element_type=F32)


def _dot_nt(a, b):
    return lax.dot_general(a, b, (((1,), (1,)), ((), ())), preferred_element_type=F32)


def _rmsnorm_rows(x, w):
    return x * lax.rsqrt(jnp.mean(x * x, axis=-1, keepdims=True) + RMS_EPS) * w


def _silu(x):
    return x * jax.nn.sigmoid(x)


def _t5_bucket(n):
    max_exact = REL_BUCKETS // 2
    nf = jnp.maximum(n, max_exact).astype(F32)
    large = max_exact + jnp.floor(jnp.log(nf / max_exact) / math.log(REL_MAX_DIST / max_exact)
                                  * (REL_BUCKETS - max_exact)).astype(jnp.int32)
    large = jnp.minimum(large, REL_BUCKETS - 1)
    return jnp.where(n < max_exact, n, large)


def _prep_kernel(n_batches, rb_ref, mem_ref, mnw_ref, w_ref, own_ref, prev_ref, km_ref, vm_ref):
    h = pl.program_id(0)

    @pl.when(h < n_batches)
    def _():
        u = _rmsnorm_rows(mem_ref[0], mnw_ref[...]).astype(BF16)
        km_ref[0] = _dot(u, w_ref[0, :, :C_WIDTH].astype(BF16)).astype(BF16)
        vm_ref[0] = _dot(u, w_ref[0, :, C_WIDTH:].astype(BF16)).astype(BF16)

    key = lax.broadcasted_iota(jnp.int32, (MOBA_BLOCK, MOBA_BLOCK), 0)
    qry = lax.broadcasted_iota(jnp.int32, (MOBA_BLOCK, MOBA_BLOCK), 1)
    rel = qry - key
    b_own = _t5_bucket(jnp.maximum(rel, 0))
    half = MOBA_BLOCK // 2
    assert REL_MAX_DIST <= half
    corner = (lax.broadcasted_iota(jnp.int32, (half, half), 1)
              - lax.broadcasted_iota(jnp.int32, (half, half), 0) + half)
    b_prev = _t5_bucket(corner)
    own = jnp.zeros((MOBA_BLOCK, MOBA_BLOCK), F32)
    prev = jnp.zeros((half, half), F32)
    for bk in range(REL_BUCKETS):
        val = rb_ref[bk, h] * LOG2E
        own = jnp.where(b_own == bk, val, own)
        prev = jnp.where(b_prev == bk, val, prev)
    own_ref[0] = jnp.where(rel >= 0, own, NEG)
    prev_ref[0] = prev - rb_ref[REL_BUCKETS - 1, h] * LOG2E


def _prep(rel_bias, mem, mnw, w_mem_kv):
    B = mem.shape[0]
    assert B <= A_HEADS
    batch = lambda h: (jnp.minimum(h, B - 1), 0, 0)
    half = MOBA_BLOCK // 2
    tile = pl.BlockSpec((1, MOBA_BLOCK, MOBA_BLOCK), lambda h: (h, 0, 0))
    corner = pl.BlockSpec((1, half, half), lambda h: (h, 0, 0))
    kv = pl.BlockSpec((1, MEM_LEN, C_WIDTH), batch)
    return pl.pallas_call(
        functools.partial(_prep_kernel, B),
        grid=(A_HEADS,),
        in_specs=[pl.BlockSpec(memory_space=pltpu.SMEM),
                  pl.BlockSpec((1, MEM_LEN, D_MODEL), batch),
                  pl.BlockSpec((1, D_MODEL), lambda h: (0, 0)),
                  pl.BlockSpec(w_mem_kv.shape, lambda h: (0, 0, 0))],
        out_specs=[tile, corner, kv, kv],
        out_shape=[jax.ShapeDtypeStruct((A_HEADS, MOBA_BLOCK, MOBA_BLOCK), F32),
                   jax.ShapeDtypeStruct((A_HEADS, half, half), F32)]
        + [jax.ShapeDtypeStruct((B, MEM_LEN, C_WIDTH), BF16)] * 2,
        compiler_params=pltpu.CompilerParams(dimension_semantics=("arbitrary",)),
        name="prep",
    )(rel_bias, mem, mnw, w_mem_kv)


_ROW_COLS = (("qa", A_WIDTH), ("ka", A_WIDTH), ("va", A_WIDTH), ("ga", A_WIDTH),
             ("qb", B_KEY_WIDTH), ("kb", B_KEY_WIDTH), ("vb", B_WIDTH), ("gb", B_WIDTH),
             ("qc", C_WIDTH), ("gc", C_WIDTH), ("z", B_KEY_WIDTH))
_ROW_OFF = {}
_off = 0
for _name, _width in _ROW_COLS:
    _ROW_OFF[_name] = (_off, _off + _width)
    _off += _width
ROW_COLS_TOTAL = _off


W_IN_ALIGNED = 4 * A_WIDTH + 2 * B_KEY_WIDTH + 2 * B_WIDTH


def _proj_kernel(x_ref, nw_ref, w_ref, wa_ref, ba_ref, gnw_ref,
                 qaT_ref, vaT_ref, ka_ref, ga_ref, qc_ref, gc_ref, mb_ref,
                 wr_ref, qb_ref, kb_ref, vb_ref, gb_ref, g_ref, state_scr, att_scr):
    @pl.when(pl.program_id(1) == 0)
    def _():
        state_scr[...] = jnp.zeros_like(state_scr)

    @pl.when((pl.program_id(0) == 0) & (pl.program_id(1) == 0))
    def _():
        chunk = 4 * LANES
        for lo in range(0, W_IN_ALIGNED, chunk):
            wr_ref[lo:lo + chunk, :] = w_ref[lo:lo + chunk, :].astype(BF16)
        tail = W_IN_ALIGNED + GLA_RANK
        for lo in range(0, 2 * C_WIDTH, chunk):
            wr_ref[W_IN_ALIGNED + lo:W_IN_ALIGNED + lo + chunk, :] = (
                w_ref[tail + lo:tail + lo + chunk, :].astype(BF16))
        z_lo, z_hi = _ROW_OFF["z"]
        wr_ref[z_lo:z_hi, :] = jnp.dot(wa_ref[...].T, w_ref[W_IN_ALIGNED:tail, :],
                                       preferred_element_type=F32,
                                       precision=lax.Precision.HIGHEST).astype(BF16)

    u = _rmsnorm_rows(x_ref[0], nw_ref[...]).astype(BF16)
    tm = u.shape[0]

    def row(name, rows=slice(None)):
        lo, hi = _ROW_OFF[name]
        return _dot_nt(u[rows], wr_ref[lo:hi, :])

    def col(name, rows=slice(None)):
        lo, hi = _ROW_OFF[name]
        return _dot_nt(wr_ref[lo:hi, :], u[rows])

    z = row("z") + ba_ref[...]
    g_ref[...] = (jnp.minimum(z, 0.0) - jnp.log1p(jnp.exp(-jnp.abs(z)))) * (LOG2E / GLA_TAU)
    qb_ref[...] = (row("qb") * (B_KEY_DIM ** -0.5)).astype(BF16)
    kb_ref[...] = row("kb").astype(BF16)
    vb_ref[...] = row("vb").astype(BF16)
    gb_ref[...] = _silu(row("gb")).astype(BF16)

    def emit_qaT(rows):
        qaT_ref[0, :, rows] = col("qa", rows)

    def emit_vaT(rows):
        vaT = col("va", rows).astype(BF16)
        ones = jnp.ones((PV_ROWS - A_HEAD_DIM, vaT.shape[1]), BF16)
        pieces = []
        for h in range(A_HEADS):
            pieces += [vaT[h * A_HEAD_DIM:(h + 1) * A_HEAD_DIM], ones]
        vaT_ref[0, :, rows] = jnp.concatenate(pieces, axis=0)

    def emit_ka(rows):
        ka = row("ka", rows)
        n = ka.shape[0]
        nb = MOBA_MAX_BLOCKS
        key_pos = (pl.program_id(1) * tm + rows.start
                   + lax.broadcasted_iota(jnp.int32, (n, LANES - A_HEAD_DIM), 0))
        blk_id = jnp.right_shift(key_pos, MOBA_BLOCK.bit_length() - 1)
        lane = lax.broadcasted_iota(jnp.int32, (n, LANES - A_HEAD_DIM), 1)
        onehot = jnp.where((lane < 2 * nb) & ((lane & (nb - 1)) == blk_id), 1.0, 0.0)
        pieces = []
        for h in range(A_HEADS):
            pieces += [ka[:, h * A_HEAD_DIM:(h + 1) * A_HEAD_DIM], onehot]
        ka_ref[0, rows, :] = jnp.concatenate(pieces, axis=1).astype(BF16)

    def emit_ga(rows):
        ga_ref[0, rows, :] = _silu(row("ga", rows)).astype(BF16)

    def emit_qc(rows):
        qc_ref[0, rows, :] = (row("qc", rows) * (C_HEAD_DIM ** -0.5)).astype(BF16)

    def emit_gc(rows):
        gc_ref[0, rows, :] = _silu(row("gc", rows)).astype(BF16)

    chunks = [slice(lo, lo + GLA_CHUNK) for lo in range(0, tm, GLA_CHUNK)]
    pending = [functools.partial(emit, rows) for emit in
               (emit_qaT, emit_vaT, emit_ka, emit_ga, emit_qc, emit_gc) for rows in chunks]

    def fill():
        if pending:
            pending.pop(0)()

    nw_gla = gnw_ref[...]
    for rows in chunks:
        def store_out(cols, val, rows=rows):
            mb_ref[0, rows, cols] = val

        _gla_chunk(qb_ref[rows, :], kb_ref[rows, :], vb_ref[rows, :], g_ref[rows, :], gb_ref[rows, :],
                   nw_gla, state_scr, att_scr, store_out, fill)
    while pending:
        fill()


def _proj(x, nw, w_in, wa, ba, gnw):
    B, S, D = x.shape
    tm = PROJ_TM
    row_spec = lambda width: pl.BlockSpec((1, tm, width), lambda b, i: (b, i, 0))
    col_spec = lambda rows: pl.BlockSpec((1, rows, tm), lambda b, i: (b, 0, i))
    const = lambda shape: pl.BlockSpec(shape, lambda b, i: (0,) * len(shape))
    sds = jax.ShapeDtypeStruct
    return pl.pallas_call(
        _proj_kernel,
        grid=(B, S // tm),
        in_specs=[
            pl.BlockSpec((1, tm, D), lambda b, i: (b, i, 0)),
            const((1, D)),
            pl.BlockSpec(w_in.shape, lambda b, i: (0, 0), pipeline_mode=pl.Buffered(1)),
            const((GLA_RANK, B_KEY_WIDTH)),
            const((1, B_KEY_WIDTH)),
            const((1, B_VAL_DIM)),
        ],
        out_specs=[col_spec(A_WIDTH), col_spec(A_HEADS * PV_ROWS), row_spec(A_HEADS * LANES),
                   row_spec(A_WIDTH), row_spec(C_WIDTH), row_spec(C_WIDTH), row_spec(B_WIDTH)],
        out_shape=[sds((B, A_WIDTH, S), F32), sds((B, A_HEADS * PV_ROWS, S), BF16),
                   sds((B, S, A_HEADS * LANES), BF16), sds((B, S, A_WIDTH), BF16),
                   sds((B, S, C_WIDTH), BF16), sds((B, S, C_WIDTH), BF16),
                   sds((B, S, B_WIDTH), BF16)],
        scratch_shapes=[pltpu.VMEM((ROW_COLS_TOTAL, D), BF16),
                        pltpu.VMEM((tm, B_KEY_WIDTH), BF16), pltpu.VMEM((tm, B_KEY_WIDTH), BF16),
                        pltpu.VMEM((tm, B_WIDTH), BF16), pltpu.VMEM((tm, B_WIDTH), BF16),
                        pltpu.VMEM((tm, B_KEY_WIDTH), F32),
                        pltpu.VMEM((B_KEY_WIDTH, B_WIDTH), F32),
                        pltpu.VMEM((B_HEADS, GLA_CHUNK, GLA_CHUNK), F32)],
        compiler_params=pltpu.CompilerParams(
            dimension_semantics=("arbitrary", "arbitrary"), vmem_limit_bytes=VMEM_LIMIT),
        name="proj_gla",
    )(x, nw, w_in, wa, ba, gnw)


def _moba_kernel(rb_ref, qT_ref, k_ref, vT_ref, own_ref, prev_ref, sg_ref, o_ref,
                 kmean_scr, kmbd_scr, qo_scr, qh_scr, m_scr, acc_scr, s_scr):
    i = pl.program_id(1)
    nb = kmean_scr.shape[0]
    blk = MOBA_BLOCK
    dh = A_HEAD_DIM
    n_tiles = A_HEADS // HEADS_PER_TILE

    @pl.when(i == 0)
    def _():
        for n in range(nb):
            kb = jnp.concatenate(
                [k_ref[0, n * blk:(n + 1) * blk, h * LANES:h * LANES + dh] for h in range(A_HEADS)],
                axis=1).astype(F32)
            kmean_scr[n:n + 1, :] = jnp.mean(kb, axis=0, keepdims=True)
        lane_head = jnp.right_shift(lax.broadcasted_iota(jnp.int32, (nb, A_WIDTH), 1),
                                    dh.bit_length() - 1)
        for h in range(A_HEADS):
            km = jnp.where(lane_head == h, kmean_scr[...], 0.0)
            km_hi = km.astype(BF16)
            km_lo = (km - km_hi.astype(F32)).astype(BF16)
            kmbd_scr[h * nb:(h + 1) * nb, :] = jnp.concatenate([km_hi, km_lo, km_hi], axis=1)

    q_f32 = qT_ref[0]
    q_hi = q_f32.astype(BF16)
    q_lo = (q_f32 - q_hi.astype(F32)).astype(BF16)
    gate_all = _dot(kmbd_scr[...], jnp.concatenate([q_hi, q_hi, q_lo], axis=0))
    brow = lax.broadcasted_iota(jnp.int32, (nb, blk), 0)
    browf = brow.astype(F32)
    zeros = jnp.zeros((LANES - dh - 2 * nb, blk), F32)

    for h in range(A_HEADS):
        q_h = qT_ref[0, h * dh:(h + 1) * dh, :] * (dh ** -0.5 * LOG2E)
        qo_scr[h] = jnp.concatenate([q_h, jnp.zeros((LANES - dh, blk), F32)], axis=0).astype(BF16)

    for h in range(A_HEADS):
        g = jnp.where(brow < i, gate_all[h * nb:(h + 1) * nb], NEG)
        sel = brow < 0
        for _ in range(MOBA_TOPK):
            m = jnp.max(g, axis=0, keepdims=True)
            idx = jnp.min(jnp.where(g == m, browf, float(nb)), axis=0, keepdims=True)
            pick = browf == idx
            sel = sel | pick
            g = jnp.where(pick, -jnp.inf, g)
        valid = sel & (brow < i)
        far_bias = rb_ref[REL_BUCKETS - 1, h] * LOG2E
        pen = jnp.where(valid, far_bias, NEG)
        pen_hi = pen.astype(BF16).astype(F32)
        q_h = qT_ref[0, h * dh:(h + 1) * dh, :] * (dh ** -0.5 * LOG2E)
        qh_scr[h] = jnp.concatenate([q_h, pen_hi, pen - pen_hi, zeros], axis=0).astype(BF16)

    half = blk // 2

    def logits(j, h, kind):
        rows = pl.ds(pl.multiple_of(j * blk, blk), blk)
        keys = k_ref[0, rows, h * LANES:(h + 1) * LANES]
        if kind == "own":
            return _dot(keys, qo_scr[h]) + own_ref[h]
        s = _dot(keys, qh_scr[h])
        if kind == "prev":
            near = s[half:, :half] + prev_ref[h]
            s = jnp.concatenate(
                [s[:half], jnp.concatenate([near, s[half:, half:]], axis=1)], axis=0)
        return s

    def fold(j, h, s, first):
        rows = pl.ds(pl.multiple_of(j * blk, blk), blk)
        bm = jnp.max(s, axis=0, keepdims=True)
        vj = vT_ref[0, h * PV_ROWS:(h + 1) * PV_ROWS, rows]
        if first:
            m_scr[h] = bm
            acc_scr[h] = _dot(vj, jnp.exp2(s - bm).astype(BF16))
        else:
            m_old = m_scr[h]
            m_new = jnp.maximum(m_old, bm)
            m_scr[h] = m_new
            acc_scr[h] = (jnp.exp2(m_old - m_new) * acc_scr[h]
                          + _dot(vj, jnp.exp2(s - m_new).astype(BF16)))

    n_far = i - 1
    j_prev = jnp.maximum(i - 1, 0)
    items = ([(i, "own", h) for h in range(A_HEADS)] + [(j_prev, "prev", h) for h in range(A_HEADS)]
             + [(0, "far", h) for h in range(QK_LOOKAHEAD)])
    n_fold = 2 * A_HEADS
    pending = []

    def issue(n):
        j, kind, h = items[n]
        s = logits(j, h, kind)
        if n < n_fold:
            pending.append(s)
        else:
            s_scr[h] = s

    for n in range(QK_LOOKAHEAD):
        issue(n)
    for n in range(n_fold):
        issue(n + QK_LOOKAHEAD)
        j, kind, h = items[n]
        fold(j, h, pending.pop(0), kind == "own")

    def far_group(j0, n_blocks):
        j_after = jnp.minimum(j0 + n_blocks, n_far - 1)
        stream = ([(j0 + t, h) for t in range(n_blocks) for h in range(A_HEADS)]
                  + [(j_after, h) for h in range(QK_LOOKAHEAD)])
        n_items = n_blocks * A_HEADS
        pending = [s_scr[h] for h in range(QK_LOOKAHEAD)]
        for n in range(n_items):
            j, h = stream[n + QK_LOOKAHEAD]
            if n + QK_LOOKAHEAD < n_items:
                pending.append(logits(j, h, "far"))
            else:
                s_scr[h] = logits(j, h, "far")
            j, h = stream[n]
            fold(j, h, pending.pop(0), False)

    def far_groups(g, carry):
        far_group(g * FAR_UNROLL, FAR_UNROLL)
        return carry

    n_groups = jnp.maximum(n_far, 0) // FAR_UNROLL
    lax.fori_loop(0, n_groups, far_groups, 0)
    for rest in range(1, FAR_UNROLL):
        @pl.when(n_far - n_groups * FAR_UNROLL == rest)
        def _():
            far_group(n_groups * FAR_UNROLL, rest)

    for t in range(n_tiles):
        oT = []
        for hh in range(HEADS_PER_TILE):
            acc = acc_scr[t * HEADS_PER_TILE + hh]
            oT.append(acc[:dh] / acc[dh:dh + 1])
        cols = slice(t * LANES, (t + 1) * LANES)
        o_ref[0, :, cols] = (jnp.concatenate(oT, axis=0).T * sg_ref[0, :, cols]).astype(BF16)


def _moba(rel_bias, qaT, ka, vaT, bias_own, bias_prev, sga):
    B, S, _ = ka.shape
    nb = S // MOBA_BLOCK
    assert nb == MOBA_MAX_BLOCKS
    bias_spec = pl.BlockSpec((A_HEADS, MOBA_BLOCK, MOBA_BLOCK), lambda b, i: (0, 0, 0))
    corner_spec = pl.BlockSpec((A_HEADS, MOBA_BLOCK // 2, MOBA_BLOCK // 2), lambda b, i: (0, 0, 0))
    return pl.pallas_call(
        _moba_kernel,
        grid=(B, nb),
        in_specs=[
            pl.BlockSpec(memory_space=pltpu.SMEM),
            pl.BlockSpec((1, A_WIDTH, MOBA_BLOCK), lambda b, i: (b, 0, i)),
            pl.BlockSpec((1, S, A_HEADS * LANES), lambda b, i: (b, 0, 0)),
            pl.BlockSpec((1, A_HEADS * PV_ROWS, S), lambda b, i: (b, 0, 0)),
            bias_spec, corner_spec,
            pl.BlockSpec((1, MOBA_BLOCK, A_WIDTH), lambda b, i: (b, i, 0)),
        ],
        out_specs=pl.BlockSpec((1, MOBA_BLOCK, A_WIDTH), lambda b, i: (b, i, 0)),
        out_shape=jax.ShapeDtypeStruct((B, S, A_WIDTH), BF16),
        scratch_shapes=[pltpu.VMEM((nb, A_WIDTH), F32),
                        pltpu.VMEM((A_HEADS * nb, 3 * A_WIDTH), BF16),
                        pltpu.VMEM((A_HEADS, LANES, MOBA_BLOCK), BF16),
                        pltpu.VMEM((A_HEADS, LANES, MOBA_BLOCK), BF16),
                        pltpu.VMEM((A_HEADS, 1, MOBA_BLOCK), F32),
                        pltpu.VMEM((A_HEADS, PV_ROWS, MOBA_BLOCK), F32),
                        pltpu.VMEM((QK_LOOKAHEAD, MOBA_BLOCK, MOBA_BLOCK), F32)],
        compiler_params=pltpu.CompilerParams(
            dimension_semantics=("arbitrary", "arbitrary"),
            vmem_limit_bytes=VMEM_LIMIT),
        name="moba",
    )(rel_bias, qaT, ka, vaT, bias_own, bias_prev, sga)


def _gla_chunk(q_bf, k_bf, v, g, sg, nw, state_scr, att_scr, store_out, fill):
    C = GLA_CHUNK
    dk, dv, H = B_KEY_DIM, B_VAL_DIM, B_HEADS
    n_levels = C.bit_length() - 1

    q = q_bf.astype(F32)
    k = k_bf.astype(F32)

    row = lax.broadcasted_iota(jnp.int32, (C, H * dk), 0)
    tt = lax.broadcasted_iota(jnp.int32, (C, C), 0)
    ss = lax.broadcasted_iota(jnp.int32, (C, C), 1)

    g_hi = g.astype(BF16)
    g_mid = (g - g_hi.astype(F32)).astype(BF16)
    g_lo = (g - g_hi.astype(F32) - g_mid.astype(F32)).astype(BF16)
    tril = jnp.where(ss <= tt, 1.0, 0.0).astype(BF16)
    b3 = _dot(tril, jnp.concatenate([g_hi, g_mid, g_lo], axis=1))
    b = b3[:, :H * dk] + b3[:, H * dk:2 * H * dk] + b3[:, 2 * H * dk:]

    def roll_rows(x, shift):
        if shift % C < SUBLANES:
            x3 = x.reshape(C // SUBLANES, SUBLANES, x.shape[1])
            return pltpu.roll(x3, shift % C, 1).reshape(x.shape)
        if (C - shift % C) < SUBLANES:
            x3 = x.reshape(C // SUBLANES, SUBLANES, x.shape[1])
            return pltpu.roll(x3, SUBLANES - (C - shift % C), 1).reshape(x.shape)
        return pltpu.roll(x, shift, 0)

    txs = tt ^ ss
    level = jnp.full((C, C), -1, jnp.int32)
    for p in range(n_levels):
        level = level + (txs >= (1 << p)).astype(jnp.int32)
    level = jnp.where(ss <= tt, level, -2)

    heads_per_tile = LANES // dk
    lane = lax.broadcasted_iota(jnp.int32, (C, H * dk), 1)
    head_in_tile = jnp.right_shift(lane, dk.bit_length() - 1) & (heads_per_tile - 1)

    def split_heads(kk):
        return [jnp.where(head_in_tile == hh, kk, 0.0).astype(BF16) for hh in range(heads_per_tile)]

    def scores(q_bf, k_split, h):
        t, hh = divmod(h, heads_per_tile)
        cols = slice(t * LANES, (t + 1) * LANES)
        return _dot_nt(q_bf[:, cols], k_split[hh][:, cols])

    on_diag = level == -1
    k_split = split_heads(k)
    for h in range(H):
        att_scr[h] = jnp.where(on_diag, scores(q_bf, k_split, h), 0.0)
    fill()

    block_end = b
    for p in range(n_levels):
        m = 1 << p
        second_half = (row & m) != 0
        r = jnp.where(second_half, roll_rows(block_end, m), block_end)
        decay = jnp.exp2(-jnp.abs(b - r))
        kd_split = split_heads(k * decay)
        if m >= SUBLANES:
            blocks = [slice(lo, lo + m) for lo in range(m, C, 2 * m)]
            take = lambda x: jnp.concatenate([x[rows] for rows in blocks], axis=0)
            qd_bf = (take(q) * take(decay)).astype(BF16)
            for h in range(H):
                s_lvl = scores(qd_bf, kd_split, h)
                for n, rows in enumerate(blocks):
                    keys = slice(rows.start - m, rows.start)
                    att_scr[h, rows, keys] = s_lvl[n * m:(n + 1) * m, keys]
        else:
            qd_bf = (q * decay).astype(BF16)
            at_level = level == p
            for h in range(H):
                att_scr[h] = jnp.where(at_level, scores(qd_bf, kd_split, h), att_scr[h])
        fill()
        if p + 1 < n_levels:
            block_end = jnp.where(second_half, block_end, roll_rows(block_end, C - m))

    state = state_scr[...]
    o_inter = _dot((q * jnp.exp2(b)).astype(BF16), state.astype(BF16))

    for h in range(H):
        cols = slice(h * dv, (h + 1) * dv)
        o_h = o_inter[:, cols] + _dot(att_scr[h].astype(BF16), v[:, cols])
        store_out(cols, (_rmsnorm_rows(o_h, nw) * sg[:, cols]).astype(BF16))

    bT = b.T
    kT = k.T
    b_last = bT[:, C - 1:C]
    kdT = (kT * jnp.exp2(b_last - bT)).astype(BF16)
    ds = _dot(kdT, v)
    srow = lax.broadcasted_iota(jnp.int32, (H * dk, H * dv), 0) // dk
    scol = lax.broadcasted_iota(jnp.int32, (H * dk, H * dv), 1) // dv
    state_scr[...] = jnp.exp2(b_last) * state + jnp.where(srow == scol, ds, 0.0)


def _out_kernel(x_ref, ma_ref, mb_ref, qc_ref, sgc_ref, km_ref, vm_ref, w_ref, fw_ref, o_ref,
                wo_ref):
    @pl.when((pl.program_id(0) == 0) & (pl.program_id(1) == 0))
    def _():
        for lo in range(0, MIX_WIDTH, 4 * LANES):
            wo_ref[lo:lo + 4 * LANES, :] = w_ref[0, lo:lo + 4 * LANES, :].astype(BF16)

    dh = C_HEAD_DIM
    qc = qc_ref[0]
    km = km_ref[0]
    vm = vm_ref[0]
    heads = [slice(h * dh, (h + 1) * dh) for h in range(C_HEADS)]
    scores = [_dot_nt(qc[:, sl], km[:, sl]) for sl in heads]
    h_ab = (_dot(ma_ref[0], wo_ref[0:A_WIDTH, :])
            + _dot(mb_ref[0], wo_ref[A_WIDTH:A_WIDTH + B_WIDTH, :]))
    oc = []
    for s, sl in zip(scores, heads):
        p = jnp.exp(s - jnp.max(s, axis=-1, keepdims=True))
        l = jnp.sum(p, axis=-1, keepdims=True)
        oc.append(_dot(p.astype(BF16), vm[:, sl]) / l)
    mc = (jnp.concatenate(oc, axis=-1) * sgc_ref[0]).astype(BF16)
    h_new = x_ref[0] + h_ab + _dot(mc, wo_ref[A_WIDTH + B_WIDTH:, :])
    o_ref[0] = _rmsnorm_rows(h_new, fw_ref[...])


def _out(x, ma, mb, qc, sgc, km, vm, w_out, fw):
    B, S, D = x.shape
    tm = OUT_TM
    row = lambda width: pl.BlockSpec((1, tm, width), lambda b, i: (b, i, 0))
    mem = pl.BlockSpec((1, MEM_LEN, C_WIDTH), lambda b, i: (b, 0, 0))
    return pl.pallas_call(
        _out_kernel,
        grid=(B, S // tm),
        in_specs=[row(D), row(A_WIDTH), row(B_WIDTH), row(C_WIDTH), row(C_WIDTH), mem, mem,
                  pl.BlockSpec(w_out.shape, lambda b, i: (0, 0, 0), pipeline_mode=pl.Buffered(1)),
                  pl.BlockSpec((1, D), lambda b, i: (0, 0))],
        out_specs=row(D),
        out_shape=jax.ShapeDtypeStruct((B, S, D), F32),
        scratch_shapes=[pltpu.VMEM((MIX_WIDTH, D), BF16)],
        compiler_params=pltpu.CompilerParams(
            dimension_semantics=("arbitrary", "arbitrary"), vmem_limit_bytes=VMEM_LIMIT),
        name="out",
    )(x, ma, mb, qc, sgc, km, vm, w_out, fw)


def kernel(x, mem, norm_w, w_in, w_alpha2, b_alpha, gla_norm_w, mem_norm_w, w_mem_kv, w_out,
           rel_bias, final_norm_w):
    assert norm_w.shape[0] == 1, "single layer"
    assert w_in.shape[2] == W_IN_ALIGNED + GLA_RANK + 2 * C_WIDTH

    bias_own, bias_prev, km, vm = _prep(rel_bias, mem, mem_norm_w[0][None, :], w_mem_kv)
    qaT, vaT, ka, sga, qc, sgc, mb = _proj(x, norm_w[0][None, :], w_in[0].T, w_alpha2[0], b_alpha[0][None, :],
                                           gla_norm_w[0][None, :])
    ma = _moba(rel_bias, qaT, ka, vaT, bias_own, bias_prev, sga)
    return _out(x, ma, mb, qc, sgc, km, vm, w_out, final_norm_w[None, :])
```

```python
import functools
import math

import jax
import jax.numpy as jnp
from jax import lax
from jax.experimental import pallas as pl
from jax.experimental.pallas import tpu as pltpu

F32 = jnp.float32
BF16 = jnp.bfloat16

D_MODEL = 1024
MEM_LEN = 256
A_HEADS = 8
A_HEAD_DIM = 64
A_WIDTH = 512
MOBA_BLOCK = 256
MOBA_TOPK = 3
B_HEADS = 4
B_KEY_DIM = 64
B_VAL_DIM = 128
B_KEY_WIDTH = 256
B_WIDTH = 512
GLA_RANK = 16
GLA_TAU = 16.0
C_HEADS = 4
C_HEAD_DIM = 128
C_WIDTH = 512
MIX_WIDTH = A_WIDTH + B_WIDTH + C_WIDTH
REL_BUCKETS = 32
REL_MAX_DIST = 128
RMS_EPS = 1e-6
NEG = -1e30

LANES = 128
SUBLANES = 8
PROJ_TM = 512
OUT_TM = 512
GLA_CHUNK = 256
VMEM_LIMIT = 56 * 1024 * 1024
LOG2E = 1.0 / math.log(2.0)
HEADS_PER_TILE = LANES // A_HEAD_DIM
PV_ROWS = A_HEAD_DIM + 16
MOBA_MAX_BLOCKS = 16
QK_LOOKAHEAD = 5
FAR_UNROLL = 4


def _dot(a, b):
    return jnp.dot(a, b, preferred_element_type=F32)


def _dot_nt(a, b):
    return lax.dot_general(a, b, (((1,), (1,)), ((), ())), preferred_element_type=F32)


def _rmsnorm_rows(x, w):
    return x * lax.rsqrt(jnp.mean(x * x, axis=-1, keepdims=True) + RMS_EPS) * w


def _silu(x):
    return x * jax.nn.sigmoid(x)


def _t5_bucket(n):
    max_exact = REL_BUCKETS // 2
    nf = jnp.maximum(n, max_exact).astype(F32)
    large = max_exact + jnp.floor(jnp.log(nf / max_exact) / math.log(REL_MAX_DIST / max_exact)
                                  * (REL_BUCKETS - max_exact)).astype(jnp.int32)
    large = jnp.minimum(large, REL_BUCKETS - 1)
    return jnp.where(n < max_exact, n, large)


def _prep_kernel(n_batches, rb_ref, mem_ref, mnw_ref, w_ref, own_ref, prev_ref, km_ref, vm_ref):
    h = pl.program_id(0)

    @pl.when(h < n_batches)
    def _():
        u = _rmsnorm_rows(mem_ref[0], mnw_ref[...]).astype(BF16)
        km_ref[0] = _dot(u, w_ref[0, :, :C_WIDTH].astype(BF16)).astype(BF16)
        vm_ref[0] = _dot(u, w_ref[0, :, C_WIDTH:].astype(BF16)).astype(BF16)

    key = lax.broadcasted_iota(jnp.int32, (MOBA_BLOCK, MOBA_BLOCK), 0)
    qry = lax.broadcasted_iota(jnp.int32, (MOBA_BLOCK, MOBA_BLOCK), 1)
    rel = qry - key
    b_own = _t5_bucket(jnp.maximum(rel, 0))
    half = MOBA_BLOCK // 2
    assert REL_MAX_DIST <= half
    corner = (lax.broadcasted_iota(jnp.int32, (half, half), 1)
              - lax.broadcasted_iota(jnp.int32, (half, half), 0) + half)
    b_prev = _t5_bucket(corner)
    own = jnp.zeros((MOBA_BLOCK, MOBA_BLOCK), F32)
    prev = jnp.zeros((half, half), F32)
    for bk in range(REL_BUCKETS):
        val = rb_ref[bk, h] * LOG2E
        own = jnp.where(b_own == bk, val, own)
        prev = jnp.where(b_prev == bk, val, prev)
    own_ref[0] = jnp.where(rel >= 0, own, NEG)
    prev_ref[0] = prev - rb_ref[REL_BUCKETS - 1, h] * LOG2E


def _prep(rel_bias, mem, mnw, w_mem_kv):
    B = mem.shape[0]
    assert B <= A_HEADS
    batch = lambda h: (jnp.minimum(h, B - 1), 0, 0)
    half = MOBA_BLOCK // 2
    tile = pl.BlockSpec((1, MOBA_BLOCK, MOBA_BLOCK), lambda h: (h, 0, 0))
    corner = pl.BlockSpec((1, half, half), lambda h: (h, 0, 0))
    kv = pl.BlockSpec((1, MEM_LEN, C_WIDTH), batch)
    return pl.pallas_call(
        functools.partial(_prep_kernel, B),
        grid=(A_HEADS,),
        in_specs=[pl.BlockSpec(memory_space=pltpu.SMEM),
                  pl.BlockSpec((1, MEM_LEN, D_MODEL), batch),
                  pl.BlockSpec((1, D_MODEL), lambda h: (0, 0)),
                  pl.BlockSpec(w_mem_kv.shape, lambda h: (0, 0, 0))],
        out_specs=[tile, corner, kv, kv],
        out_shape=[jax.ShapeDtypeStruct((A_HEADS, MOBA_BLOCK, MOBA_BLOCK), F32),
                   jax.ShapeDtypeStruct((A_HEADS, half, half), F32)]
        + [jax.ShapeDtypeStruct((B, MEM_LEN, C_WIDTH), BF16)] * 2,
        compiler_params=pltpu.CompilerParams(dimension_semantics=("arbitrary",)),
        name="prep",
    )(rel_bias, mem, mnw, w_mem_kv)


_ROW_COLS = (("qa", A_WIDTH), ("ka", A_WIDTH), ("va", A_WIDTH), ("ga", A_WIDTH),
             ("qb", B_KEY_WIDTH), ("kb", B_KEY_WIDTH), ("vb", B_WIDTH), ("gb", B_WIDTH),
             ("qc", C_WIDTH), ("gc", C_WIDTH), ("z", B_KEY_WIDTH))
_ROW_OFF = {}
_off = 0
for _name, _width in _ROW_COLS:
    _ROW_OFF[_name] = (_off, _off + _width)
    _off += _width
ROW_COLS_TOTAL = _off


W_IN_ALIGNED = 4 * A_WIDTH + 2 * B_KEY_WIDTH + 2 * B_WIDTH


def _proj_kernel(x_ref, nw_ref, w_ref, wa_ref, ba_ref, gnw_ref,
                 qaT_ref, vaT_ref, ka_ref, ga_ref, qc_ref, gc_ref, mb_ref,
                 wr_ref, qb_ref, kb_ref, vb_ref, gb_ref, g_ref, state_scr, att_scr):
    @pl.when(pl.program_id(1) == 0)
    def _():
        state_scr[...] = jnp.zeros_like(state_scr)

    @pl.when((pl.program_id(0) == 0) & (pl.program_id(1) == 0))
    def _():
        chunk = 4 * LANES
        for lo in range(0, W_IN_ALIGNED, chunk):
            wr_ref[lo:lo + chunk, :] = w_ref[lo:lo + chunk, :].astype(BF16)
        tail = W_IN_ALIGNED + GLA_RANK
        for lo in range(0, 2 * C_WIDTH, chunk):
            wr_ref[W_IN_ALIGNED + lo:W_IN_ALIGNED + lo + chunk, :] = (
                w_ref[tail + lo:tail + lo + chunk, :].astype(BF16))
        z_lo, z_hi = _ROW_OFF["z"]
        wr_ref[z_lo:z_hi, :] = jnp.dot(wa_ref[...].T, w_ref[W_IN_ALIGNED:tail, :],
                                       preferred_element_type=F32,
                                       precision=lax.Precision.HIGHEST).astype(BF16)

    u = _rmsnorm_rows(x_ref[0], nw_ref[...]).astype(BF16)
    tm = u.shape[0]

    def row(name, rows=slice(None)):
        lo, hi = _ROW_OFF[name]
        return _dot_nt(u[rows], wr_ref[lo:hi, :])

    def col(name, rows=slice(None)):
        lo, hi = _ROW_OFF[name]
        return _dot_nt(wr_ref[lo:hi, :], u[rows])

    z = row("z") + ba_ref[...]
    g_ref[...] = (jnp.minimum(z, 0.0) - jnp.log1p(jnp.exp(-jnp.abs(z)))) * (LOG2E / GLA_TAU)
    qb_ref[...] = (row("qb") * (B_KEY_DIM ** -0.5)).astype(BF16)
    kb_ref[...] = row("kb").astype(BF16)
    vb_ref[...] = row("vb").astype(BF16)
    gb_ref[...] = _silu(row("gb")).astype(BF16)

    def emit_qaT(rows):
        qaT_ref[0, :, rows] = col("qa", rows)

    def emit_vaT(rows):
        vaT = col("va", rows).astype(BF16)
        ones = jnp.ones((PV_ROWS - A_HEAD_DIM, vaT.shape[1]), BF16)
        pieces = []
        for h in range(A_HEADS):
            pieces += [vaT[h * A_HEAD_DIM:(h + 1) * A_HEAD_DIM], ones]
        vaT_ref[0, :, rows] = jnp.concatenate(pieces, axis=0)

    def emit_ka(rows):
        ka = row("ka", rows)
        n = ka.shape[0]
        nb = MOBA_MAX_BLOCKS
        key_pos = (pl.program_id(1) * tm + rows.start
                   + lax.broadcasted_iota(jnp.int32, (n, LANES - A_HEAD_DIM), 0))
        blk_id = jnp.right_shift(key_pos, MOBA_BLOCK.bit_length() - 1)
        lane = lax.broadcasted_iota(jnp.int32, (n, LANES - A_HEAD_DIM), 1)
        onehot = jnp.where((lane < 2 * nb) & ((lane & (nb - 1)) == blk_id), 1.0, 0.0)
        pieces = []
        for h in range(A_HEADS):
            pieces += [ka[:, h * A_HEAD_DIM:(h + 1) * A_HEAD_DIM], onehot]
        ka_ref[0, rows, :] = jnp.concatenate(pieces, axis=1).astype(BF16)

    def emit_ga(rows):
        ga_ref[0, :, rows] = _silu(col("ga", rows)).astype(BF16)

    def emit_qc(rows):
        qc_ref[0, rows, :] = (row("qc", rows) * (C_HEAD_DIM ** -0.5)).astype(BF16)

    def emit_gc(rows):
        gc_ref[0, rows, :] = _silu(row("gc", rows)).astype(BF16)

    chunks = [slice(lo, lo + GLA_CHUNK) for lo in range(0, tm, GLA_CHUNK)]
    pending = [functools.partial(emit, rows) for emit in
               (emit_qaT, emit_vaT, emit_ka, emit_ga, emit_qc, emit_gc) for rows in chunks]

    def fill():
        if pending:
            pending.pop(0)()

    nw_gla = gnw_ref[...]
    for rows in chunks:
        def store_out(cols, val, rows=rows):
            mb_ref[0, rows, cols] = val

        _gla_chunk(qb_ref[rows, :], kb_ref[rows, :], vb_ref[rows, :], g_ref[rows, :], gb_ref[rows, :],
                   nw_gla, state_scr, att_scr, store_out, fill)
    while pending:
        fill()


def _proj(x, nw, w_in, wa, ba, gnw):
    B, S, D = x.shape
    tm = PROJ_TM
    row_spec = lambda width: pl.BlockSpec((1, tm, width), lambda b, i: (b, i, 0))
    col_spec = lambda rows: pl.BlockSpec((1, rows, tm), lambda b, i: (b, 0, i))
    const = lambda shape: pl.BlockSpec(shape, lambda b, i: (0,) * len(shape))
    sds = jax.ShapeDtypeStruct
    return pl.pallas_call(
        _proj_kernel,
        grid=(B, S // tm),
        in_specs=[
            pl.BlockSpec((1, tm, D), lambda b, i: (b, i, 0)),
            const((1, D)),
            pl.BlockSpec(w_in.shape, lambda b, i: (0, 0), pipeline_mode=pl.Buffered(1)),
            const((GLA_RANK, B_KEY_WIDTH)),
            const((1, B_KEY_WIDTH)),
            const((1, B_VAL_DIM)),
        ],
        out_specs=[col_spec(A_WIDTH), col_spec(A_HEADS * PV_ROWS), row_spec(A_HEADS * LANES),
                   col_spec(A_WIDTH), row_spec(C_WIDTH), row_spec(C_WIDTH), row_spec(B_WIDTH)],
        out_shape=[sds((B, A_WIDTH, S), F32), sds((B, A_HEADS * PV_ROWS, S), BF16),
                   sds((B, S, A_HEADS * LANES), BF16), sds((B, A_WIDTH, S), BF16),
                   sds((B, S, C_WIDTH), BF16), sds((B, S, C_WIDTH), BF16),
                   sds((B, S, B_WIDTH), BF16)],
        scratch_shapes=[pltpu.VMEM((ROW_COLS_TOTAL, D), BF16),
                        pltpu.VMEM((tm, B_KEY_WIDTH), BF16), pltpu.VMEM((tm, B_KEY_WIDTH), BF16),
                        pltpu.VMEM((tm, B_WIDTH), BF16), pltpu.VMEM((tm, B_WIDTH), BF16),
                        pltpu.VMEM((tm, B_KEY_WIDTH), F32),
                        pltpu.VMEM((B_KEY_WIDTH, B_WIDTH), F32),
                        pltpu.VMEM((B_HEADS, GLA_CHUNK, GLA_CHUNK), F32)],
        compiler_params=pltpu.CompilerParams(
            dimension_semantics=("arbitrary", "arbitrary"), vmem_limit_bytes=VMEM_LIMIT),
        name="proj_gla",
    )(x, nw, w_in, wa, ba, gnw)


def _moba_kernel(rb_ref, qT_ref, k_ref, vT_ref, own_ref, prev_ref, sg_ref, o_ref,
                 kmean_scr, kmbd_scr, qo_scr, qh_scr, m_scr, acc_scr, s_scr):
    i = pl.program_id(1)
    nb = kmean_scr.shape[0]
    blk = MOBA_BLOCK
    dh = A_HEAD_DIM
    n_tiles = A_HEADS // HEADS_PER_TILE

    @pl.when(i == 0)
    def _():
        for n in range(nb):
            kb = jnp.concatenate(
                [k_ref[0, n * blk:(n + 1) * blk, h * LANES:h * LANES + dh] for h in range(A_HEADS)],
                axis=1).astype(F32)
            kmean_scr[n:n + 1, :] = jnp.mean(kb, axis=0, keepdims=True)
        lane_head = jnp.right_shift(lax.broadcasted_iota(jnp.int32, (nb, A_WIDTH), 1),
                                    dh.bit_length() - 1)
        for h in range(A_HEADS):
            km = jnp.where(lane_head == h, kmean_scr[...], 0.0)
            km_hi = km.astype(BF16)
            km_lo = (km - km_hi.astype(F32)).astype(BF16)
            kmbd_scr[h * nb:(h + 1) * nb, :] = jnp.concatenate([km_hi, km_lo, km_hi], axis=1)

    q_f32 = qT_ref[0]
    q_hi = q_f32.astype(BF16)
    q_lo = (q_f32 - q_hi.astype(F32)).astype(BF16)
    gate_all = _dot(kmbd_scr[...], jnp.concatenate([q_hi, q_hi, q_lo], axis=0))
    brow = lax.broadcasted_iota(jnp.int32, (nb, blk), 0)
    browf = brow.astype(F32)
    zeros = jnp.zeros((LANES - dh - 2 * nb, blk), F32)

    for h in range(A_HEADS):
        q_h = qT_ref[0, h * dh:(h + 1) * dh, :] * (dh ** -0.5 * LOG2E)
        qo_scr[h] = jnp.concatenate([q_h, jnp.zeros((LANES - dh, blk), F32)], axis=0).astype(BF16)

    for h in range(A_HEADS):
        g = jnp.where(brow < i, gate_all[h * nb:(h + 1) * nb], NEG)
        sel = brow < 0
        for _ in range(MOBA_TOPK):
            m = jnp.max(g, axis=0, keepdims=True)
            idx = jnp.min(jnp.where(g == m, browf, float(nb)), axis=0, keepdims=True)
            pick = browf == idx
            sel = sel | pick
            g = jnp.where(pick, -jnp.inf, g)
        valid = sel & (brow < i)
        far_bias = rb_ref[REL_BUCKETS - 1, h] * LOG2E
        pen = jnp.where(valid, far_bias, NEG)
        pen_hi = pen.astype(BF16).astype(F32)
        q_h = qT_ref[0, h * dh:(h + 1) * dh, :] * (dh ** -0.5 * LOG2E)
        qh_scr[h] = jnp.concatenate([q_h, pen_hi, pen - pen_hi, zeros], axis=0).astype(BF16)

    half = blk // 2

    def logits(j, h, kind):
        rows = pl.ds(pl.multiple_of(j * blk, blk), blk)
        keys = k_ref[0, rows, h * LANES:(h + 1) * LANES]
        if kind == "own":
            return _dot(keys, qo_scr[h]) + own_ref[h]
        s = _dot(keys, qh_scr[h])
        if kind == "prev":
            near = s[half:, :half] + prev_ref[h]
            s = jnp.concatenate(
                [s[:half], jnp.concatenate([near, s[half:, half:]], axis=1)], axis=0)
        return s

    def fold(j, h, s, first):
        rows = pl.ds(pl.multiple_of(j * blk, blk), blk)
        bm = jnp.max(s, axis=0, keepdims=True)
        vj = vT_ref[0, h * PV_ROWS:(h + 1) * PV_ROWS, rows]
        if first:
            m_scr[h] = bm
            acc_scr[h] = _dot(vj, jnp.exp2(s - bm).astype(BF16))
        else:
            m_old = m_scr[h]
            m_new = jnp.maximum(m_old, bm)
            m_scr[h] = m_new
            acc_scr[h] = (jnp.exp2(m_old - m_new) * acc_scr[h]
                          + _dot(vj, jnp.exp2(s - m_new).astype(BF16)))

    n_far = i - 1
    j_prev = jnp.maximum(i - 1, 0)
    items = ([(i, "own", h) for h in range(A_HEADS)] + [(j_prev, "prev", h) for h in range(A_HEADS)]
             + [(0, "far", h) for h in range(QK_LOOKAHEAD)])
    n_fold = 2 * A_HEADS
    pending = []

    def issue(n):
        j, kind, h = items[n]
        s = logits(j, h, kind)
        if n < n_fold:
            pending.append(s)
        else:
            s_scr[h] = s

    for n in range(QK_LOOKAHEAD):
        issue(n)
    for n in range(n_fold):
        issue(n + QK_LOOKAHEAD)
        j, kind, h = items[n]
        fold(j, h, pending.pop(0), kind == "own")

    def far_group(j0, n_blocks):
        j_after = jnp.minimum(j0 + n_blocks, n_far - 1)
        stream = ([(j0 + t, h) for t in range(n_blocks) for h in range(A_HEADS)]
                  + [(j_after, h) for h in range(QK_LOOKAHEAD)])
        n_items = n_blocks * A_HEADS
        pending = [s_scr[h] for h in range(QK_LOOKAHEAD)]
        for n in range(n_items):
            j, h = stream[n + QK_LOOKAHEAD]
            if n + QK_LOOKAHEAD < n_items:
                pending.append(logits(j, h, "far"))
            else:
                s_scr[h] = logits(j, h, "far")
            j, h = stream[n]
            fold(j, h, pending.pop(0), False)

    def far_groups(g, carry):
        far_group(g * FAR_UNROLL, FAR_UNROLL)
        return carry

    n_groups = jnp.maximum(n_far, 0) // FAR_UNROLL
    lax.fori_loop(0, n_groups, far_groups, 0)
    for rest in range(1, FAR_UNROLL):
        @pl.when(n_far - n_groups * FAR_UNROLL == rest)
        def _():
            far_group(n_groups * FAR_UNROLL, rest)

    for t in range(n_tiles):
        oT = []
        for hh in range(HEADS_PER_TILE):
            acc = acc_scr[t * HEADS_PER_TILE + hh]
            oT.append(acc[:dh] / acc[dh:dh + 1])
        rows = slice(t * LANES, (t + 1) * LANES)
        o_ref[0, rows, :] = (jnp.concatenate(oT, axis=0) * sg_ref[0, rows, :]).astype(BF16)


def _moba(rel_bias, qaT, ka, vaT, bias_own, bias_prev, sga):
    B, S, _ = ka.shape
    nb = S // MOBA_BLOCK
    assert nb == MOBA_MAX_BLOCKS
    bias_spec = pl.BlockSpec((A_HEADS, MOBA_BLOCK, MOBA_BLOCK), lambda b, i: (0, 0, 0))
    corner_spec = pl.BlockSpec((A_HEADS, MOBA_BLOCK // 2, MOBA_BLOCK // 2), lambda b, i: (0, 0, 0))
    return pl.pallas_call(
        _moba_kernel,
        grid=(B, nb),
        in_specs=[
            pl.BlockSpec(memory_space=pltpu.SMEM),
            pl.BlockSpec((1, A_WIDTH, MOBA_BLOCK), lambda b, i: (b, 0, i)),
            pl.BlockSpec((1, S, A_HEADS * LANES), lambda b, i: (b, 0, 0)),
            pl.BlockSpec((1, A_HEADS * PV_ROWS, S), lambda b, i: (b, 0, 0)),
            bias_spec, corner_spec,
            pl.BlockSpec((1, A_WIDTH, MOBA_BLOCK), lambda b, i: (b, 0, i)),
        ],
        out_specs=pl.BlockSpec((1, A_WIDTH, MOBA_BLOCK), lambda b, i: (b, 0, i)),
        out_shape=jax.ShapeDtypeStruct((B, A_WIDTH, S), BF16),
        scratch_shapes=[pltpu.VMEM((nb, A_WIDTH), F32),
                        pltpu.VMEM((A_HEADS * nb, 3 * A_WIDTH), BF16),
                        pltpu.VMEM((A_HEADS, LANES, MOBA_BLOCK), BF16),
                        pltpu.VMEM((A_HEADS, LANES, MOBA_BLOCK), BF16),
                        pltpu.VMEM((A_HEADS, 1, MOBA_BLOCK), F32),
                        pltpu.VMEM((A_HEADS, PV_ROWS, MOBA_BLOCK), F32),
                        pltpu.VMEM((QK_LOOKAHEAD, MOBA_BLOCK, MOBA_BLOCK), F32)],
        compiler_params=pltpu.CompilerParams(
            dimension_semantics=("arbitrary", "arbitrary"),
            vmem_limit_bytes=VMEM_LIMIT),
        name="moba",
    )(rel_bias, qaT, ka, vaT, bias_own, bias_prev, sga)


def _gla_chunk(q_bf, k_bf, v, g, sg, nw, state_scr, att_scr, store_out, fill):
    C = GLA_CHUNK
    dk, dv, H = B_KEY_DIM, B_VAL_DIM, B_HEADS
    n_levels = C.bit_length() - 1

    q = q_bf.astype(F32)
    k = k_bf.astype(F32)

    row = lax.broadcasted_iota(jnp.int32, (C, H * dk), 0)
    tt = lax.broadcasted_iota(jnp.int32, (C, C), 0)
    ss = lax.broadcasted_iota(jnp.int32, (C, C), 1)

    g_hi = g.astype(BF16)
    g_mid = (g - g_hi.astype(F32)).astype(BF16)
    g_lo = (g - g_hi.astype(F32) - g_mid.astype(F32)).astype(BF16)
    tril = jnp.where(ss <= tt, 1.0, 0.0).astype(BF16)
    b3 = _dot(tril, jnp.concatenate([g_hi, g_mid, g_lo], axis=1))
    b = b3[:, :H * dk] + b3[:, H * dk:2 * H * dk] + b3[:, 2 * H * dk:]

    def roll_rows(x, shift):
        if shift % C < SUBLANES:
            x3 = x.reshape(C // SUBLANES, SUBLANES, x.shape[1])
            return pltpu.roll(x3, shift % C, 1).reshape(x.shape)
        if (C - shift % C) < SUBLANES:
            x3 = x.reshape(C // SUBLANES, SUBLANES, x.shape[1])
            return pltpu.roll(x3, SUBLANES - (C - shift % C), 1).reshape(x.shape)
        return pltpu.roll(x, shift, 0)

    txs = tt ^ ss
    level = jnp.full((C, C), -1, jnp.int32)
    for p in range(n_levels):
        level = level + (txs >= (1 << p)).astype(jnp.int32)
    level = jnp.where(ss <= tt, level, -2)

    heads_per_tile = LANES // dk
    lane = lax.broadcasted_iota(jnp.int32, (C, H * dk), 1)
    head_in_tile = jnp.right_shift(lane, dk.bit_length() - 1) & (heads_per_tile - 1)

    def split_heads(kk):
        return [jnp.where(head_in_tile == hh, kk, 0.0).astype(BF16) for hh in range(heads_per_tile)]

    def scores(q_bf, k_split, h):
        t, hh = divmod(h, heads_per_tile)
        cols = slice(t * LANES, (t + 1) * LANES)
        return _dot_nt(q_bf[:, cols], k_split[hh][:, cols])

    on_diag = level == -1
    k_split = split_heads(k)
    for h in range(H):
        att_scr[h] = jnp.where(on_diag, scores(q_bf, k_split, h), 0.0)
    fill()

    block_end = b
    for p in range(n_levels):
        m = 1 << p
        second_half = (row & m) != 0
        r = jnp.where(second_half, roll_rows(block_end, m), block_end)
        decay = jnp.exp2(-jnp.abs(b - r))
        kd_split = split_heads(k * decay)
        if m >= SUBLANES:
            blocks = [slice(lo, lo + m) for lo in range(m, C, 2 * m)]
            take = lambda x: jnp.concatenate([x[rows] for rows in blocks], axis=0)
            qd_bf = (take(q) * take(decay)).astype(BF16)
            for h in range(H):
                s_lvl = scores(qd_bf, kd_split, h)
                for n, rows in enumerate(blocks):
                    keys = slice(rows.start - m, rows.start)
                    att_scr[h, rows, keys] = s_lvl[n * m:(n + 1) * m, keys]
        else:
            qd_bf = (q * decay).astype(BF16)
            at_level = level == p
            for h in range(H):
                att_scr[h] = jnp.where(at_level, scores(qd_bf, kd_split, h), att_scr[h])
        fill()
        if p + 1 < n_levels:
            block_end = jnp.where(second_half, block_end, roll_rows(block_end, C - m))

    state = state_scr[...]
    o_inter = _dot((q * jnp.exp2(b)).astype(BF16), state.astype(BF16))

    for h in range(H):
        cols = slice(h * dv, (h + 1) * dv)
        o_h = o_inter[:, cols] + _dot(att_scr[h].astype(BF16), v[:, cols])
        store_out(cols, (_rmsnorm_rows(o_h, nw) * sg[:, cols]).astype(BF16))

    bT = b.T
    kT = k.T
    b_last = bT[:, C - 1:C]
    kdT = (kT * jnp.exp2(b_last - bT)).astype(BF16)
    ds = _dot(kdT, v)
    srow = lax.broadcasted_iota(jnp.int32, (H * dk, H * dv), 0) // dk
    scol = lax.broadcasted_iota(jnp.int32, (H * dk, H * dv), 1) // dv
    state_scr[...] = jnp.exp2(b_last) * state + jnp.where(srow == scol, ds, 0.0)


def _out_kernel(x_ref, maT_ref, mb_ref, qc_ref, sgc_ref, km_ref, vm_ref, w_ref, fw_ref, o_ref,
                wo_ref):
    @pl.when((pl.program_id(0) == 0) & (pl.program_id(1) == 0))
    def _():
        for lo in range(0, MIX_WIDTH, 4 * LANES):
            wo_ref[lo:lo + 4 * LANES, :] = w_ref[0, lo:lo + 4 * LANES, :].astype(BF16)

    dh = C_HEAD_DIM
    qc = qc_ref[0]
    km = km_ref[0]
    vm = vm_ref[0]
    heads = [slice(h * dh, (h + 1) * dh) for h in range(C_HEADS)]
    scores = [_dot_nt(qc[:, sl], km[:, sl]) for sl in heads]
    h_ab = (lax.dot_general(maT_ref[0], wo_ref[0:A_WIDTH, :], (((0,), (0,)), ((), ())),
                            preferred_element_type=F32)
            + _dot(mb_ref[0], wo_ref[A_WIDTH:A_WIDTH + B_WIDTH, :]))
    oc = []
    for s, sl in zip(scores, heads):
        p = jnp.exp(s - jnp.max(s, axis=-1, keepdims=True))
        l = jnp.sum(p, axis=-1, keepdims=True)
        oc.append(_dot(p.astype(BF16), vm[:, sl]) / l)
    mc = (jnp.concatenate(oc, axis=-1) * sgc_ref[0]).astype(BF16)
    h_new = x_ref[0] + h_ab + _dot(mc, wo_ref[A_WIDTH + B_WIDTH:, :])
    o_ref[0] = _rmsnorm_rows(h_new, fw_ref[...])


def _out(x, ma, mb, qc, sgc, km, vm, w_out, fw):
    B, S, D = x.shape
    tm = OUT_TM
    row = lambda width: pl.BlockSpec((1, tm, width), lambda b, i: (b, i, 0))
    mem = pl.BlockSpec((1, MEM_LEN, C_WIDTH), lambda b, i: (b, 0, 0))
    return pl.pallas_call(
        _out_kernel,
        grid=(B, S // tm),
        in_specs=[row(D), pl.BlockSpec((1, A_WIDTH, tm), lambda b, i: (b, 0, i)),
                  row(B_WIDTH), row(C_WIDTH), row(C_WIDTH), mem, mem,
                  pl.BlockSpec(w_out.shape, lambda b, i: (0, 0, 0), pipeline_mode=pl.Buffered(1)),
                  pl.BlockSpec((1, D), lambda b, i: (0, 0))],
        out_specs=row(D),
        out_shape=jax.ShapeDtypeStruct((B, S, D), F32),
        scratch_shapes=[pltpu.VMEM((MIX_WIDTH, D), BF16)],
        compiler_params=pltpu.CompilerParams(
            dimension_semantics=("arbitrary", "arbitrary"), vmem_limit_bytes=VMEM_LIMIT),
        name="out",
    )(x, ma, mb, qc, sgc, km, vm, w_out, fw)


def kernel(x, mem, norm_w, w_in, w_alpha2, b_alpha, gla_norm_w, mem_norm_w, w_mem_kv, w_out,
           rel_bias, final_norm_w):
    assert norm_w.shape[0] == 1, "single layer"
    assert w_in.shape[2] == W_IN_ALIGNED + GLA_RANK + 2 * C_WIDTH

    bias_own, bias_prev, km, vm = _prep(rel_bias, mem, mem_norm_w[0][None, :], w_mem_kv)
    qaT, vaT, ka, sga, qc, sgc, mb = _proj(x, norm_w[0][None, :], w_in[0].T, w_alpha2[0], b_alpha[0][None, :],
                                           gla_norm_w[0][None, :])
    ma = _moba(rel_bias, qaT, ka, vaT, bias_own, bias_prev, sga)
    return _out(x, ma, mb, qc, sgc, km, vm, w_out, final_norm_w[None, :])
```

```python
import functools
import math

import jax
import jax.numpy as jnp
from jax import lax
from jax.experimental import pallas as pl
from jax.experimental.pallas import tpu as pltpu

F32 = jnp.float32
BF16 = jnp.bfloat16

D_MODEL = 1024
MEM_LEN = 256
A_HEADS = 8
A_HEAD_DIM = 64
A_WIDTH = 512
MOBA_BLOCK = 256
MOBA_TOPK = 3
B_HEADS = 4
B_KEY_DIM = 64
B_VAL_DIM = 128
B_KEY_WIDTH = 256
B_WIDTH = 512
GLA_RANK = 16
GLA_TAU = 16.0
C_HEADS = 4
C_HEAD_DIM = 128
C_WIDTH = 512
MIX_WIDTH = A_WIDTH + B_WIDTH + C_WIDTH
REL_BUCKETS = 32
REL_MAX_DIST = 128
RMS_EPS = 1e-6
NEG = -1e30

LANES = 128
SUBLANES = 8
PROJ_TM = 512
OUT_TM = 512
GLA_CHUNK = 256
VMEM_LIMIT = 56 * 1024 * 1024
LOG2E = 1.0 / math.log(2.0)
HEADS_PER_TILE = LANES // A_HEAD_DIM
PV_ROWS = A_HEAD_DIM + 16
MOBA_MAX_BLOCKS = 16
QK_LOOKAHEAD = 5
FAR_UNROLL = 7


def _dot(a, b):
    return jnp.dot(a, b, preferred_element_type=F32)


def _dot_nt(a, b):
    return lax.dot_general(a, b, (((1,), (1,)), ((), ())), preferred_element_type=F32)


def _rmsnorm_rows(x, w):
    return x * lax.rsqrt(jnp.mean(x * x, axis=-1, keepdims=True) + RMS_EPS) * w


def _silu(x):
    return x * jax.nn.sigmoid(x)


def _t5_bucket(n):
    max_exact = REL_BUCKETS // 2
    nf = jnp.maximum(n, max_exact).astype(F32)
    large = max_exact + jnp.floor(jnp.log(nf / max_exact) / math.log(REL_MAX_DIST / max_exact)
                                  * (REL_BUCKETS - max_exact)).astype(jnp.int32)
    large = jnp.minimum(large, REL_BUCKETS - 1)
    return jnp.where(n < max_exact, n, large)


def _prep_kernel(n_batches, rb_ref, mem_ref, mnw_ref, w_ref, own_ref, prev_ref, km_ref, vm_ref):
    h = pl.program_id(0)

    @pl.when(h < n_batches)
    def _():
        u = _rmsnorm_rows(mem_ref[0], mnw_ref[...]).astype(BF16)
        km_ref[0] = _dot(u, w_ref[0, :, :C_WIDTH].astype(BF16)).astype(BF16)
        vm_ref[0] = _dot(u, w_ref[0, :, C_WIDTH:].astype(BF16)).astype(BF16)

    key = lax.broadcasted_iota(jnp.int32, (MOBA_BLOCK, MOBA_BLOCK), 0)
    qry = lax.broadcasted_iota(jnp.int32, (MOBA_BLOCK, MOBA_BLOCK), 1)
    rel = qry - key
    b_own = _t5_bucket(jnp.maximum(rel, 0))
    half = MOBA_BLOCK // 2
    assert REL_MAX_DIST <= half
    corner = (lax.broadcasted_iota(jnp.int32, (half, half), 1)
              - lax.broadcasted_iota(jnp.int32, (half, half), 0) + half)
    b_prev = _t5_bucket(corner)
    own = jnp.zeros((MOBA_BLOCK, MOBA_BLOCK), F32)
    prev = jnp.zeros((half, half), F32)
    for bk in range(REL_BUCKETS):
        val = rb_ref[bk, h] * LOG2E
        own = jnp.where(b_own == bk, val, own)
        prev = jnp.where(b_prev == bk, val, prev)
    own_ref[0] = jnp.where(rel >= 0, own, NEG)
    prev_ref[0] = prev - rb_ref[REL_BUCKETS - 1, h] * LOG2E


def _prep(rel_bias, mem, mnw, w_mem_kv):
    B = mem.shape[0]
    assert B <= A_HEADS
    batch = lambda h: (jnp.minimum(h, B - 1), 0, 0)
    half = MOBA_BLOCK // 2
    tile = pl.BlockSpec((1, MOBA_BLOCK, MOBA_BLOCK), lambda h: (h, 0, 0))
    corner = pl.BlockSpec((1, half, half), lambda h: (h, 0, 0))
    kv = pl.BlockSpec((1, MEM_LEN, C_WIDTH), batch)
    return pl.pallas_call(
        functools.partial(_prep_kernel, B),
        grid=(A_HEADS,),
        in_specs=[pl.BlockSpec(memory_space=pltpu.SMEM),
                  pl.BlockSpec((1, MEM_LEN, D_MODEL), batch),
                  pl.BlockSpec((1, D_MODEL), lambda h: (0, 0)),
                  pl.BlockSpec(w_mem_kv.shape, lambda h: (0, 0, 0))],
        out_specs=[tile, corner, kv, kv],
        out_shape=[jax.ShapeDtypeStruct((A_HEADS, MOBA_BLOCK, MOBA_BLOCK), F32),
                   jax.ShapeDtypeStruct((A_HEADS, half, half), F32)]
        + [jax.ShapeDtypeStruct((B, MEM_LEN, C_WIDTH), BF16)] * 2,
        compiler_params=pltpu.CompilerParams(dimension_semantics=("arbitrary",)),
        name="prep",
    )(rel_bias, mem, mnw, w_mem_kv)


_ROW_COLS = (("qa", A_WIDTH), ("ka", A_WIDTH), ("va", A_WIDTH), ("ga", A_WIDTH),
             ("qb", B_KEY_WIDTH), ("kb", B_KEY_WIDTH), ("vb", B_WIDTH), ("gb", B_WIDTH),
             ("qc", C_WIDTH), ("gc", C_WIDTH), ("z", B_KEY_WIDTH))
_ROW_OFF = {}
_off = 0
for _name, _width in _ROW_COLS:
    _ROW_OFF[_name] = (_off, _off + _width)
    _off += _width
ROW_COLS_TOTAL = _off


W_IN_ALIGNED = 4 * A_WIDTH + 2 * B_KEY_WIDTH + 2 * B_WIDTH


def _proj_kernel(x_ref, nw_ref, w_ref, wa_ref, ba_ref, gnw_ref,
                 qaT_ref, vaT_ref, ka_ref, ga_ref, qc_ref, gc_ref, mb_ref,
                 wr_ref, qb_ref, kb_ref, vb_ref, gb_ref, g_ref, state_scr, att_scr):
    @pl.when(pl.program_id(1) == 0)
    def _():
        state_scr[...] = jnp.zeros_like(state_scr)

    @pl.when((pl.program_id(0) == 0) & (pl.program_id(1) == 0))
    def _():
        chunk = 4 * LANES
        for lo in range(0, W_IN_ALIGNED, chunk):
            wr_ref[lo:lo + chunk, :] = w_ref[lo:lo + chunk, :].astype(BF16)
        tail = W_IN_ALIGNED + GLA_RANK
        for lo in range(0, 2 * C_WIDTH, chunk):
            wr_ref[W_IN_ALIGNED + lo:W_IN_ALIGNED + lo + chunk, :] = (
                w_ref[tail + lo:tail + lo + chunk, :].astype(BF16))
        z_lo, z_hi = _ROW_OFF["z"]
        wr_ref[z_lo:z_hi, :] = jnp.dot(wa_ref[...].T, w_ref[W_IN_ALIGNED:tail, :],
                                       preferred_element_type=F32,
                                       precision=lax.Precision.HIGHEST).astype(BF16)

    u = _rmsnorm_rows(x_ref[0], nw_ref[...]).astype(BF16)
    tm = u.shape[0]

    def row(name, rows=slice(None)):
        lo, hi = _ROW_OFF[name]
        return _dot_nt(u[rows], wr_ref[lo:hi, :])

    def col(name, rows=slice(None)):
        lo, hi = _ROW_OFF[name]
        return _dot_nt(wr_ref[lo:hi, :], u[rows])

    z = row("z") + ba_ref[...]
    g_ref[...] = (jnp.minimum(z, 0.0) - jnp.log1p(jnp.exp(-jnp.abs(z)))) * (LOG2E / GLA_TAU)
    qb_ref[...] = (row("qb") * (B_KEY_DIM ** -0.5)).astype(BF16)
    kb_ref[...] = row("kb").astype(BF16)
    vb_ref[...] = row("vb").astype(BF16)
    gb_ref[...] = _silu(row("gb")).astype(BF16)

    def emit_qaT(rows):
        qaT_ref[0, :, rows] = col("qa", rows)

    def emit_vaT(rows):
        vaT = col("va", rows).astype(BF16)
        ones = jnp.ones((PV_ROWS - A_HEAD_DIM, vaT.shape[1]), BF16)
        pieces = []
        for h in range(A_HEADS):
            pieces += [vaT[h * A_HEAD_DIM:(h + 1) * A_HEAD_DIM], ones]
        vaT_ref[0, :, rows] = jnp.concatenate(pieces, axis=0)

    def emit_ka(rows):
        ka = row("ka", rows)
        n = ka.shape[0]
        nb = MOBA_MAX_BLOCKS
        key_pos = (pl.program_id(1) * tm + rows.start
                   + lax.broadcasted_iota(jnp.int32, (n, LANES - A_HEAD_DIM), 0))
        blk_id = jnp.right_shift(key_pos, MOBA_BLOCK.bit_length() - 1)
        lane = lax.broadcasted_iota(jnp.int32, (n, LANES - A_HEAD_DIM), 1)
        onehot = jnp.where((lane < 2 * nb) & ((lane & (nb - 1)) == blk_id), 1.0, 0.0)
        pieces = []
        for h in range(A_HEADS):
            pieces += [ka[:, h * A_HEAD_DIM:(h + 1) * A_HEAD_DIM], onehot]
        ka_ref[0, rows, :] = jnp.concatenate(pieces, axis=1).astype(BF16)

    def emit_ga(rows):
        ga_ref[0, :, rows] = _silu(col("ga", rows)).astype(BF16)

    def emit_qc(rows):
        qc_ref[0, rows, :] = (row("qc", rows) * (C_HEAD_DIM ** -0.5)).astype(BF16)

    def emit_gc(rows):
        gc_ref[0, rows, :] = _silu(row("gc", rows)).astype(BF16)

    chunks = [slice(lo, lo + GLA_CHUNK) for lo in range(0, tm, GLA_CHUNK)]
    pending = [functools.partial(emit, rows) for emit in
               (emit_qaT, emit_vaT, emit_ka, emit_ga, emit_qc, emit_gc) for rows in chunks]

    def fill():
        if pending:
            pending.pop(0)()

    nw_gla = gnw_ref[...]
    for rows in chunks:
        def store_out(cols, val, rows=rows):
            mb_ref[0, rows, cols] = val

        _gla_chunk(qb_ref[rows, :], kb_ref[rows, :], vb_ref[rows, :], g_ref[rows, :], gb_ref[rows, :],
                   nw_gla, state_scr, att_scr, store_out, fill)
    while pending:
        fill()


def _proj(x, nw, w_in, wa, ba, gnw):
    B, S, D = x.shape
    tm = PROJ_TM
    row_spec = lambda width: pl.BlockSpec((1, tm, width), lambda b, i: (b, i, 0))
    col_spec = lambda rows: pl.BlockSpec((1, rows, tm), lambda b, i: (b, 0, i))
    const = lambda shape: pl.BlockSpec(shape, lambda b, i: (0,) * len(shape))
    sds = jax.ShapeDtypeStruct
    return pl.pallas_call(
        _proj_kernel,
        grid=(B, S // tm),
        in_specs=[
            pl.BlockSpec((1, tm, D), lambda b, i: (b, i, 0)),
            const((1, D)),
            pl.BlockSpec(w_in.shape, lambda b, i: (0, 0), pipeline_mode=pl.Buffered(1)),
            const((GLA_RANK, B_KEY_WIDTH)),
            const((1, B_KEY_WIDTH)),
            const((1, B_VAL_DIM)),
        ],
        out_specs=[col_spec(A_WIDTH), col_spec(A_HEADS * PV_ROWS), row_spec(A_HEADS * LANES),
                   col_spec(A_WIDTH), row_spec(C_WIDTH), row_spec(C_WIDTH), row_spec(B_WIDTH)],
        out_shape=[sds((B, A_WIDTH, S), F32), sds((B, A_HEADS * PV_ROWS, S), BF16),
                   sds((B, S, A_HEADS * LANES), BF16), sds((B, A_WIDTH, S), BF16),
                   sds((B, S, C_WIDTH), BF16), sds((B, S, C_WIDTH), BF16),
                   sds((B, S, B_WIDTH), BF16)],
        scratch_shapes=[pltpu.VMEM((ROW_COLS_TOTAL, D), BF16),
                        pltpu.VMEM((tm, B_KEY_WIDTH), BF16), pltpu.VMEM((tm, B_KEY_WIDTH), BF16),
                        pltpu.VMEM((tm, B_WIDTH), BF16), pltpu.VMEM((tm, B_WIDTH), BF16),
                        pltpu.VMEM((tm, B_KEY_WIDTH), F32),
                        pltpu.VMEM((B_KEY_WIDTH, B_WIDTH), F32),
                        pltpu.VMEM((B_HEADS, GLA_CHUNK, GLA_CHUNK), F32)],
        compiler_params=pltpu.CompilerParams(
            dimension_semantics=("arbitrary", "arbitrary"), vmem_limit_bytes=VMEM_LIMIT),
        name="proj_gla",
    )(x, nw, w_in, wa, ba, gnw)


def _moba_kernel(rb_ref, qT_ref, k_ref, vT_ref, own_ref, prev_ref, sg_ref, o_ref,
                 kmean_scr, kmbd_scr, qo_scr, qh_scr, m_scr, acc_scr, s_scr):
    i = pl.program_id(1)
    nb = kmean_scr.shape[0]
    blk = MOBA_BLOCK
    dh = A_HEAD_DIM
    n_tiles = A_HEADS // HEADS_PER_TILE

    @pl.when(i == 0)
    def _():
        for n in range(nb):
            kb = jnp.concatenate(
                [k_ref[0, n * blk:(n + 1) * blk, h * LANES:h * LANES + dh] for h in range(A_HEADS)],
                axis=1).astype(F32)
            kmean_scr[n:n + 1, :] = jnp.mean(kb, axis=0, keepdims=True)
        lane_head = jnp.right_shift(lax.broadcasted_iota(jnp.int32, (nb, A_WIDTH), 1),
                                    dh.bit_length() - 1)
        for h in range(A_HEADS):
            km = jnp.where(lane_head == h, kmean_scr[...], 0.0)
            km_hi = km.astype(BF16)
            km_lo = (km - km_hi.astype(F32)).astype(BF16)
            kmbd_scr[h * nb:(h + 1) * nb, :] = jnp.concatenate([km_hi, km_lo, km_hi], axis=1)

    q_f32 = qT_ref[0]
    q_hi = q_f32.astype(BF16)
    q_lo = (q_f32 - q_hi.astype(F32)).astype(BF16)
    gate_all = _dot(kmbd_scr[...], jnp.concatenate([q_hi, q_hi, q_lo], axis=0))
    brow = lax.broadcasted_iota(jnp.int32, (nb, blk), 0)
    browf = brow.astype(F32)
    zeros = jnp.zeros((LANES - dh - 2 * nb, blk), F32)

    for h in range(A_HEADS):
        q_h = qT_ref[0, h * dh:(h + 1) * dh, :] * (dh ** -0.5 * LOG2E)
        qo_scr[h] = jnp.concatenate([q_h, jnp.zeros((LANES - dh, blk), F32)], axis=0).astype(BF16)

    for h in range(A_HEADS):
        g = jnp.where(brow < i, gate_all[h * nb:(h + 1) * nb], NEG)
        sel = brow < 0
        for _ in range(MOBA_TOPK):
            m = jnp.max(g, axis=0, keepdims=True)
            idx = jnp.min(jnp.where(g == m, browf, float(nb)), axis=0, keepdims=True)
            pick = browf == idx
            sel = sel | pick
            g = jnp.where(pick, -jnp.inf, g)
        valid = sel & (brow < i)
        far_bias = rb_ref[REL_BUCKETS - 1, h] * LOG2E
        pen = jnp.where(valid, far_bias, NEG)
        pen_hi = pen.astype(BF16).astype(F32)
        q_h = qT_ref[0, h * dh:(h + 1) * dh, :] * (dh ** -0.5 * LOG2E)
        qh_scr[h] = jnp.concatenate([q_h, pen_hi, pen - pen_hi, zeros], axis=0).astype(BF16)

    half = blk // 2

    def logits(j, h, kind):
        rows = pl.ds(pl.multiple_of(j * blk, blk), blk)
        keys = k_ref[0, rows, h * LANES:(h + 1) * LANES]
        if kind == "own":
            return _dot(keys, qo_scr[h]) + own_ref[h]
        s = _dot(keys, qh_scr[h])
        if kind == "prev":
            near = s[half:, :half] + prev_ref[h]
            s = jnp.concatenate(
                [s[:half], jnp.concatenate([near, s[half:, half:]], axis=1)], axis=0)
        return s

    def fold(j, h, s, first):
        rows = pl.ds(pl.multiple_of(j * blk, blk), blk)
        bm = jnp.max(s, axis=0, keepdims=True)
        vj = vT_ref[0, h * PV_ROWS:(h + 1) * PV_ROWS, rows]
        if first:
            m_scr[h] = bm
            acc_scr[h] = _dot(vj, jnp.exp2(s - bm).astype(BF16))
        else:
            m_old = m_scr[h]
            m_new = jnp.maximum(m_old, bm)
            m_scr[h] = m_new
            acc_scr[h] = (jnp.exp2(m_old - m_new) * acc_scr[h]
                          + _dot(vj, jnp.exp2(s - m_new).astype(BF16)))

    n_far = i - 1
    j_prev = jnp.maximum(i - 1, 0)
    items = ([(i, "own", h) for h in range(A_HEADS)] + [(j_prev, "prev", h) for h in range(A_HEADS)]
             + [(0, "far", h) for h in range(QK_LOOKAHEAD)])
    n_fold = 2 * A_HEADS
    pending = []

    def issue(n):
        j, kind, h = items[n]
        s = logits(j, h, kind)
        if n < n_fold:
            pending.append(s)
        else:
            s_scr[h] = s

    for n in range(QK_LOOKAHEAD):
        issue(n)
    for n in range(n_fold):
        issue(n + QK_LOOKAHEAD)
        j, kind, h = items[n]
        fold(j, h, pending.pop(0), kind == "own")

    def far_group(j0, n_blocks):
        j_after = jnp.minimum(j0 + n_blocks, n_far - 1)
        stream = ([(j0 + t, h) for t in range(n_blocks) for h in range(A_HEADS)]
                  + [(j_after, h) for h in range(QK_LOOKAHEAD)])
        n_items = n_blocks * A_HEADS
        pending = [s_scr[h] for h in range(QK_LOOKAHEAD)]
        for n in range(n_items):
            j, h = stream[n + QK_LOOKAHEAD]
            if n + QK_LOOKAHEAD < n_items:
                pending.append(logits(j, h, "far"))
            else:
                s_scr[h] = logits(j, h, "far")
            j, h = stream[n]
            fold(j, h, pending.pop(0), False)

    def far_groups(g, carry):
        far_group(g * FAR_UNROLL, FAR_UNROLL)
        return carry

    n_groups = jnp.maximum(n_far, 0) // FAR_UNROLL
    lax.fori_loop(0, n_groups, far_groups, 0)
    for rest in range(1, FAR_UNROLL):
        @pl.when(n_far - n_groups * FAR_UNROLL == rest)
        def _():
            far_group(n_groups * FAR_UNROLL, rest)

    for t in range(n_tiles):
        oT = []
        for hh in range(HEADS_PER_TILE):
            acc = acc_scr[t * HEADS_PER_TILE + hh]
            oT.append(acc[:dh] / acc[dh:dh + 1])
        rows = slice(t * LANES, (t + 1) * LANES)
        o_ref[0, rows, :] = (jnp.concatenate(oT, axis=0) * sg_ref[0, rows, :]).astype(BF16)


def _moba(rel_bias, qaT, ka, vaT, bias_own, bias_prev, sga):
    B, S, _ = ka.shape
    nb = S // MOBA_BLOCK
    assert nb == MOBA_MAX_BLOCKS
    bias_spec = pl.BlockSpec((A_HEADS, MOBA_BLOCK, MOBA_BLOCK), lambda b, i: (0, 0, 0))
    corner_spec = pl.BlockSpec((A_HEADS, MOBA_BLOCK // 2, MOBA_BLOCK // 2), lambda b, i: (0, 0, 0))
    return pl.pallas_call(
        _moba_kernel,
        grid=(B, nb),
        in_specs=[
            pl.BlockSpec(memory_space=pltpu.SMEM),
            pl.BlockSpec((1, A_WIDTH, MOBA_BLOCK), lambda b, i: (b, 0, i)),
            pl.BlockSpec((1, S, A_HEADS * LANES), lambda b, i: (b, 0, 0)),
            pl.BlockSpec((1, A_HEADS * PV_ROWS, S), lambda b, i: (b, 0, 0)),
            bias_spec, corner_spec,
            pl.BlockSpec((1, A_WIDTH, MOBA_BLOCK), lambda b, i: (b, 0, i)),
        ],
        out_specs=pl.BlockSpec((1, A_WIDTH, MOBA_BLOCK), lambda b, i: (b, 0, i)),
        out_shape=jax.ShapeDtypeStruct((B, A_WIDTH, S), BF16),
        scratch_shapes=[pltpu.VMEM((nb, A_WIDTH), F32),
                        pltpu.VMEM((A_HEADS * nb, 3 * A_WIDTH), BF16),
                        pltpu.VMEM((A_HEADS, LANES, MOBA_BLOCK), BF16),
                        pltpu.VMEM((A_HEADS, LANES, MOBA_BLOCK), BF16),
                        pltpu.VMEM((A_HEADS, 1, MOBA_BLOCK), F32),
                        pltpu.VMEM((A_HEADS, PV_ROWS, MOBA_BLOCK), F32),
                        pltpu.VMEM((QK_LOOKAHEAD, MOBA_BLOCK, MOBA_BLOCK), F32)],
        compiler_params=pltpu.CompilerParams(
            dimension_semantics=("arbitrary", "arbitrary"),
            vmem_limit_bytes=VMEM_LIMIT),
        name="moba",
    )(rel_bias, qaT, ka, vaT, bias_own, bias_prev, sga)


def _gla_chunk(q_bf, k_bf, v, g, sg, nw, state_scr, att_scr, store_out, fill):
    C = GLA_CHUNK
    dk, dv, H = B_KEY_DIM, B_VAL_DIM, B_HEADS
    n_levels = C.bit_length() - 1

    q = q_bf.astype(F32)
    k = k_bf.astype(F32)

    row = lax.broadcasted_iota(jnp.int32, (C, H * dk), 0)
    tt = lax.broadcasted_iota(jnp.int32, (C, C), 0)
    ss = lax.broadcasted_iota(jnp.int32, (C, C), 1)

    g_hi = g.astype(BF16)
    g_mid = (g - g_hi.astype(F32)).astype(BF16)
    g_lo = (g - g_hi.astype(F32) - g_mid.astype(F32)).astype(BF16)
    tril = jnp.where(ss <= tt, 1.0, 0.0).astype(BF16)
    b3 = _dot(tril, jnp.concatenate([g_hi, g_mid, g_lo], axis=1))
    b = b3[:, :H * dk] + b3[:, H * dk:2 * H * dk] + b3[:, 2 * H * dk:]

    def roll_rows(x, shift):
        if shift % C < SUBLANES:
            x3 = x.reshape(C // SUBLANES, SUBLANES, x.shape[1])
            return pltpu.roll(x3, shift % C, 1).reshape(x.shape)
        if (C - shift % C) < SUBLANES:
            x3 = x.reshape(C // SUBLANES, SUBLANES, x.shape[1])
            return pltpu.roll(x3, SUBLANES - (C - shift % C), 1).reshape(x.shape)
        return pltpu.roll(x, shift, 0)

    txs = tt ^ ss
    level = jnp.full((C, C), -1, jnp.int32)
    for p in range(n_levels):
        level = level + (txs >= (1 << p)).astype(jnp.int32)
    level = jnp.where(ss <= tt, level, -2)

    heads_per_tile = LANES // dk
    lane = lax.broadcasted_iota(jnp.int32, (C, H * dk), 1)
    head_in_tile = jnp.right_shift(lane, dk.bit_length() - 1) & (heads_per_tile - 1)

    def split_heads(kk):
        return [jnp.where(head_in_tile == hh, kk, 0.0).astype(BF16) for hh in range(heads_per_tile)]

    def scores(q_bf, k_split, h):
        t, hh = divmod(h, heads_per_tile)
        cols = slice(t * LANES, (t + 1) * LANES)
        return _dot_nt(q_bf[:, cols], k_split[hh][:, cols])

    on_diag = level == -1
    k_split = split_heads(k)
    for h in range(H):
        att_scr[h] = jnp.where(on_diag, scores(q_bf, k_split, h), 0.0)
    fill()

    block_end = b
    for p in range(n_levels):
        m = 1 << p
        second_half = (row & m) != 0
        r = jnp.where(second_half, roll_rows(block_end, m), block_end)
        decay = jnp.exp2(-jnp.abs(b - r))
        kd_split = split_heads(k * decay)
        if m >= SUBLANES:
            blocks = [slice(lo, lo + m) for lo in range(m, C, 2 * m)]
            take = lambda x: jnp.concatenate([x[rows] for rows in blocks], axis=0)
            qd_bf = (take(q) * take(decay)).astype(BF16)
            for h in range(H):
                s_lvl = scores(qd_bf, kd_split, h)
                for n, rows in enumerate(blocks):
                    keys = slice(rows.start - m, rows.start)
                    att_scr[h, rows, keys] = s_lvl[n * m:(n + 1) * m, keys]
        else:
            qd_bf = (q * decay).astype(BF16)
            at_level = level == p
            for h in range(H):
                att_scr[h] = jnp.where(at_level, scores(qd_bf, kd_split, h), att_scr[h])
        fill()
        if p + 1 < n_levels:
            block_end = jnp.where(second_half, block_end, roll_rows(block_end, C - m))

    state = state_scr[...]
    o_inter = _dot((q * jnp.exp2(b)).astype(BF16), state.astype(BF16))

    for h in range(H):
        cols = slice(h * dv, (h + 1) * dv)
        o_h = o_inter[:, cols] + _dot(att_scr[h].astype(BF16), v[:, cols])
        store_out(cols, (_rmsnorm_rows(o_h, nw) * sg[:, cols]).astype(BF16))

    bT = b.T
    kT = k.T
    b_last = bT[:, C - 1:C]
    kdT = (kT * jnp.exp2(b_last - bT)).astype(BF16)
    ds = _dot(kdT, v)
    srow = lax.broadcasted_iota(jnp.int32, (H * dk, H * dv), 0) // dk
    scol = lax.broadcasted_iota(jnp.int32, (H * dk, H * dv), 1) // dv
    state_scr[...] = jnp.exp2(b_last) * state + jnp.where(srow == scol, ds, 0.0)


def _out_kernel(x_ref, maT_ref, mb_ref, qc_ref, sgc_ref, km_ref, vm_ref, w_ref, fw_ref, o_ref,
                wo_ref):
    @pl.when((pl.program_id(0) == 0) & (pl.program_id(1) == 0))
    def _():
        for lo in range(0, MIX_WIDTH, 4 * LANES):
            wo_ref[lo:lo + 4 * LANES, :] = w_ref[0, lo:lo + 4 * LANES, :].astype(BF16)

    dh = C_HEAD_DIM
    qc = qc_ref[0]
    km = km_ref[0]
    vm = vm_ref[0]
    heads = [slice(h * dh, (h + 1) * dh) for h in range(C_HEADS)]
    scores = [_dot_nt(qc[:, sl], km[:, sl]) for sl in heads]
    h_ab = (lax.dot_general(maT_ref[0], wo_ref[0:A_WIDTH, :], (((0,), (0,)), ((), ())),
                            preferred_element_type=F32)
            + _dot(mb_ref[0], wo_ref[A_WIDTH:A_WIDTH + B_WIDTH, :]))
    oc = []
    for s, sl in zip(scores, heads):
        p = jnp.exp(s - jnp.max(s, axis=-1, keepdims=True))
        l = jnp.sum(p, axis=-1, keepdims=True)
        oc.append(_dot(p.astype(BF16), vm[:, sl]) / l)
    mc = (jnp.concatenate(oc, axis=-1) * sgc_ref[0]).astype(BF16)
    h_new = x_ref[0] + h_ab + _dot(mc, wo_ref[A_WIDTH + B_WIDTH:, :])
    o_ref[0] = _rmsnorm_rows(h_new, fw_ref[...])


def _out(x, ma, mb, qc, sgc, km, vm, w_out, fw):
    B, S, D = x.shape
    tm = OUT_TM
    row = lambda width: pl.BlockSpec((1, tm, width), lambda b, i: (b, i, 0))
    mem = pl.BlockSpec((1, MEM_LEN, C_WIDTH), lambda b, i: (b, 0, 0))
    return pl.pallas_call(
        _out_kernel,
        grid=(B, S // tm),
        in_specs=[row(D), pl.BlockSpec((1, A_WIDTH, tm), lambda b, i: (b, 0, i)),
                  row(B_WIDTH), row(C_WIDTH), row(C_WIDTH), mem, mem,
                  pl.BlockSpec(w_out.shape, lambda b, i: (0, 0, 0), pipeline_mode=pl.Buffered(1)),
                  pl.BlockSpec((1, D), lambda b, i: (0, 0))],
        out_specs=row(D),
        out_shape=jax.ShapeDtypeStruct((B, S, D), F32),
        scratch_shapes=[pltpu.VMEM((MIX_WIDTH, D), BF16)],
        compiler_params=pltpu.CompilerParams(
            dimension_semantics=("arbitrary", "arbitrary"), vmem_limit_bytes=VMEM_LIMIT),
        name="out",
    )(x, ma, mb, qc, sgc, km, vm, w_out, fw)


def kernel(x, mem, norm_w, w_in, w_alpha2, b_alpha, gla_norm_w, mem_norm_w, w_mem_kv, w_out,
           rel_bias, final_norm_w):
    assert norm_w.shape[0] == 1, "single layer"
    assert w_in.shape[2] == W_IN_ALIGNED + GLA_RANK + 2 * C_WIDTH

    bias_own, bias_prev, km, vm = _prep(rel_bias, mem, mem_norm_w[0][None, :], w_mem_kv)
    qaT, vaT, ka, sga, qc, sgc, mb = _proj(x, norm_w[0][None, :], w_in[0].T, w_alpha2[0], b_alpha[0][None, :],
                                           gla_norm_w[0][None, :])
    ma = _moba(rel_bias, qaT, ka, vaT, bias_own, bias_prev, sga)
    return _out(x, ma, mb, qc, sgc, km, vm, w_out, final_norm_w[None, :])
```

```python
import functools
import math

import jax
import jax.numpy as jnp
from jax import lax
from jax.experimental import pallas as pl
from jax.experimental.pallas import tpu as pltpu

F32 = jnp.float32
BF16 = jnp.bfloat16

D_MODEL = 1024
MEM_LEN = 256
A_HEADS = 8
A_HEAD_DIM = 64
A_WIDTH = 512
MOBA_BLOCK = 256
MOBA_TOPK = 3
B_HEADS = 4
B_KEY_DIM = 64
B_VAL_DIM = 128
B_KEY_WIDTH = 256
B_WIDTH = 512
GLA_RANK = 16
GLA_TAU = 16.0
C_HEADS = 4
C_HEAD_DIM = 128
C_WIDTH = 512
MIX_WIDTH = A_WIDTH + B_WIDTH + C_WIDTH
REL_BUCKETS = 32
REL_MAX_DIST = 128
RMS_EPS = 1e-6
NEG = -1e30

LANES = 128
SUBLANES = 8
PROJ_TM = 512
OUT_TM = 512
GLA_CHUNK = 256
VMEM_LIMIT = 56 * 1024 * 1024
LOG2E = 1.0 / math.log(2.0)
HEADS_PER_TILE = LANES // A_HEAD_DIM
PV_ROWS = A_HEAD_DIM + 16
MOBA_MAX_BLOCKS = 16
QK_LOOKAHEAD = 5
FAR_UNROLL = 7


def _dot(a, b):
    return jnp.dot(a, b, preferred_element_type=F32)


def _dot_nt(a, b):
    return lax.dot_general(a, b, (((1,), (1,)), ((), ())), preferred_element_type=F32)


def _rmsnorm_rows(x, w):
    return x * lax.rsqrt(jnp.mean(x * x, axis=-1, keepdims=True) + RMS_EPS) * w


def _silu(x):
    return x * jax.nn.sigmoid(x)


def _t5_bucket(n):
    max_exact = REL_BUCKETS // 2
    nf = jnp.maximum(n, max_exact).astype(F32)
    large = max_exact + jnp.floor(jnp.log(nf / max_exact) / math.log(REL_MAX_DIST / max_exact)
                                  * (REL_BUCKETS - max_exact)).astype(jnp.int32)
    large = jnp.minimum(large, REL_BUCKETS - 1)
    return jnp.where(n < max_exact, n, large)


def _prep_kernel(n_batches, rb_ref, mem_ref, mnw_ref, w_ref, own_ref, prev_ref, km_ref, vm_ref):
    h = pl.program_id(0)

    @pl.when(h < n_batches)
    def _():
        u = _rmsnorm_rows(mem_ref[0], mnw_ref[...]).astype(BF16)
        km_ref[0] = _dot(u, w_ref[0, :, :C_WIDTH].astype(BF16)).astype(BF16)
        vm_ref[0] = lax.dot_general(w_ref[0, :, C_WIDTH:].astype(BF16), u, (((0,), (1,)), ((), ())),
                                    preferred_element_type=F32).astype(BF16)

    key = lax.broadcasted_iota(jnp.int32, (MOBA_BLOCK, MOBA_BLOCK), 0)
    qry = lax.broadcasted_iota(jnp.int32, (MOBA_BLOCK, MOBA_BLOCK), 1)
    rel = qry - key
    b_own = _t5_bucket(jnp.maximum(rel, 0))
    half = MOBA_BLOCK // 2
    assert REL_MAX_DIST <= half
    corner = (lax.broadcasted_iota(jnp.int32, (half, half), 1)
              - lax.broadcasted_iota(jnp.int32, (half, half), 0) + half)
    b_prev = _t5_bucket(corner)
    own = jnp.zeros((MOBA_BLOCK, MOBA_BLOCK), F32)
    prev = jnp.zeros((half, half), F32)
    for bk in range(REL_BUCKETS):
        val = rb_ref[bk, h] * LOG2E
        own = jnp.where(b_own == bk, val, own)
        prev = jnp.where(b_prev == bk, val, prev)
    own_ref[0] = jnp.where(rel >= 0, own, NEG)
    prev_ref[0] = prev - rb_ref[REL_BUCKETS - 1, h] * LOG2E


def _prep(rel_bias, mem, mnw, w_mem_kv):
    B = mem.shape[0]
    assert B <= A_HEADS
    batch = lambda h: (jnp.minimum(h, B - 1), 0, 0)
    half = MOBA_BLOCK // 2
    tile = pl.BlockSpec((1, MOBA_BLOCK, MOBA_BLOCK), lambda h: (h, 0, 0))
    corner = pl.BlockSpec((1, half, half), lambda h: (h, 0, 0))
    kv = pl.BlockSpec((1, MEM_LEN, C_WIDTH), batch)
    return pl.pallas_call(
        functools.partial(_prep_kernel, B),
        grid=(A_HEADS,),
        in_specs=[pl.BlockSpec(memory_space=pltpu.SMEM),
                  pl.BlockSpec((1, MEM_LEN, D_MODEL), batch),
                  pl.BlockSpec((1, D_MODEL), lambda h: (0, 0)),
                  pl.BlockSpec(w_mem_kv.shape, lambda h: (0, 0, 0))],
        out_specs=[tile, corner, kv, pl.BlockSpec((1, C_WIDTH, MEM_LEN), batch)],
        out_shape=[jax.ShapeDtypeStruct((A_HEADS, MOBA_BLOCK, MOBA_BLOCK), F32),
                   jax.ShapeDtypeStruct((A_HEADS, half, half), F32),
                   jax.ShapeDtypeStruct((B, MEM_LEN, C_WIDTH), BF16),
                   jax.ShapeDtypeStruct((B, C_WIDTH, MEM_LEN), BF16)],
        compiler_params=pltpu.CompilerParams(dimension_semantics=("arbitrary",)),
        name="prep",
    )(rel_bias, mem, mnw, w_mem_kv)


_ROW_COLS = (("qa", A_WIDTH), ("ka", A_WIDTH), ("va", A_WIDTH), ("ga", A_WIDTH),
             ("qb", B_KEY_WIDTH), ("kb", B_KEY_WIDTH), ("vb", B_WIDTH), ("gb", B_WIDTH),
             ("qc", C_WIDTH), ("gc", C_WIDTH), ("z", B_KEY_WIDTH))
_ROW_OFF = {}
_off = 0
for _name, _width in _ROW_COLS:
    _ROW_OFF[_name] = (_off, _off + _width)
    _off += _width
ROW_COLS_TOTAL = _off


W_IN_ALIGNED = 4 * A_WIDTH + 2 * B_KEY_WIDTH + 2 * B_WIDTH


def _proj_kernel(x_ref, nw_ref, w_ref, wa_ref, ba_ref, gnw_ref,
                 qaT_ref, vaT_ref, ka_ref, ga_ref, qc_ref, gc_ref, mb_ref,
                 wr_ref, qb_ref, kb_ref, vb_ref, gb_ref, g_ref, state_scr, att_scr):
    @pl.when(pl.program_id(1) == 0)
    def _():
        state_scr[...] = jnp.zeros_like(state_scr)

    @pl.when((pl.program_id(0) == 0) & (pl.program_id(1) == 0))
    def _():
        chunk = 4 * LANES
        for lo in range(0, W_IN_ALIGNED, chunk):
            wr_ref[lo:lo + chunk, :] = w_ref[lo:lo + chunk, :].astype(BF16)
        tail = W_IN_ALIGNED + GLA_RANK
        for lo in range(0, 2 * C_WIDTH, chunk):
            wr_ref[W_IN_ALIGNED + lo:W_IN_ALIGNED + lo + chunk, :] = (
                w_ref[tail + lo:tail + lo + chunk, :].astype(BF16))
        z_lo, z_hi = _ROW_OFF["z"]
        wr_ref[z_lo:z_hi, :] = jnp.dot(wa_ref[...].T, w_ref[W_IN_ALIGNED:tail, :],
                                       preferred_element_type=F32,
                                       precision=lax.Precision.HIGHEST).astype(BF16)

    u = _rmsnorm_rows(x_ref[0], nw_ref[...]).astype(BF16)
    tm = u.shape[0]

    def row(name, rows=slice(None)):
        lo, hi = _ROW_OFF[name]
        return _dot_nt(u[rows], wr_ref[lo:hi, :])

    def col(name, rows=slice(None)):
        lo, hi = _ROW_OFF[name]
        return _dot_nt(wr_ref[lo:hi, :], u[rows])

    z = row("z") + ba_ref[...]
    g_ref[...] = (jnp.minimum(z, 0.0) - jnp.log1p(jnp.exp(-jnp.abs(z)))) * (LOG2E / GLA_TAU)
    qb_ref[...] = (row("qb") * (B_KEY_DIM ** -0.5)).astype(BF16)
    kb_ref[...] = row("kb").astype(BF16)
    vb_ref[...] = row("vb").astype(BF16)
    gb_ref[...] = _silu(row("gb")).astype(BF16)

    def emit_qaT(rows):
        qaT_ref[0, :, rows] = col("qa", rows)

    def emit_vaT(rows):
        vaT = col("va", rows).astype(BF16)
        ones = jnp.ones((PV_ROWS - A_HEAD_DIM, vaT.shape[1]), BF16)
        pieces = []
        for h in range(A_HEADS):
            pieces += [vaT[h * A_HEAD_DIM:(h + 1) * A_HEAD_DIM], ones]
        vaT_ref[0, :, rows] = jnp.concatenate(pieces, axis=0)

    def emit_ka(rows):
        ka = row("ka", rows)
        n = ka.shape[0]
        nb = MOBA_MAX_BLOCKS
        key_pos = (pl.program_id(1) * tm + rows.start
                   + lax.broadcasted_iota(jnp.int32, (n, LANES - A_HEAD_DIM), 0))
        blk_id = jnp.right_shift(key_pos, MOBA_BLOCK.bit_length() - 1)
        lane = lax.broadcasted_iota(jnp.int32, (n, LANES - A_HEAD_DIM), 1)
        onehot = jnp.where((lane < 2 * nb) & ((lane & (nb - 1)) == blk_id), 1.0, 0.0)
        pieces = []
        for h in range(A_HEADS):
            pieces += [ka[:, h * A_HEAD_DIM:(h + 1) * A_HEAD_DIM], onehot]
        ka_ref[0, rows, :] = jnp.concatenate(pieces, axis=1).astype(BF16)

    def emit_ga(rows):
        ga_ref[0, :, rows] = _silu(col("ga", rows)).astype(BF16)

    def emit_qc(rows):
        qc_ref[0, :, rows] = (col("qc", rows) * (C_HEAD_DIM ** -0.5)).astype(BF16)

    def emit_gc(rows):
        gc_ref[0, :, rows] = _silu(col("gc", rows)).astype(BF16)

    chunks = [slice(lo, lo + GLA_CHUNK) for lo in range(0, tm, GLA_CHUNK)]
    pending = [functools.partial(emit, rows) for emit in
               (emit_qaT, emit_vaT, emit_ka, emit_ga, emit_qc, emit_gc) for rows in chunks]

    def fill():
        if pending:
            pending.pop(0)()

    nw_gla = gnw_ref[...]
    for rows in chunks:
        def store_out(cols, val, rows=rows):
            mb_ref[0, rows, cols] = val

        _gla_chunk(qb_ref[rows, :], kb_ref[rows, :], vb_ref[rows, :], g_ref[rows, :], gb_ref[rows, :],
                   nw_gla, state_scr, att_scr, store_out, fill)
    while pending:
        fill()


def _proj(x, nw, w_in, wa, ba, gnw):
    B, S, D = x.shape
    tm = PROJ_TM
    row_spec = lambda width: pl.BlockSpec((1, tm, width), lambda b, i: (b, i, 0))
    col_spec = lambda rows: pl.BlockSpec((1, rows, tm), lambda b, i: (b, 0, i))
    const = lambda shape: pl.BlockSpec(shape, lambda b, i: (0,) * len(shape))
    sds = jax.ShapeDtypeStruct
    return pl.pallas_call(
        _proj_kernel,
        grid=(B, S // tm),
        in_specs=[
            pl.BlockSpec((1, tm, D), lambda b, i: (b, i, 0)),
            const((1, D)),
            pl.BlockSpec(w_in.shape, lambda b, i: (0, 0), pipeline_mode=pl.Buffered(1)),
            const((GLA_RANK, B_KEY_WIDTH)),
            const((1, B_KEY_WIDTH)),
            const((1, B_VAL_DIM)),
        ],
        out_specs=[col_spec(A_WIDTH), col_spec(A_HEADS * PV_ROWS), row_spec(A_HEADS * LANES),
                   col_spec(A_WIDTH), col_spec(C_WIDTH), col_spec(C_WIDTH), row_spec(B_WIDTH)],
        out_shape=[sds((B, A_WIDTH, S), F32), sds((B, A_HEADS * PV_ROWS, S), BF16),
                   sds((B, S, A_HEADS * LANES), BF16), sds((B, A_WIDTH, S), BF16),
                   sds((B, C_WIDTH, S), BF16), sds((B, C_WIDTH, S), BF16),
                   sds((B, S, B_WIDTH), BF16)],
        scratch_shapes=[pltpu.VMEM((ROW_COLS_TOTAL, D), BF16),
                        pltpu.VMEM((tm, B_KEY_WIDTH), BF16), pltpu.VMEM((tm, B_KEY_WIDTH), BF16),
                        pltpu.VMEM((tm, B_WIDTH), BF16), pltpu.VMEM((tm, B_WIDTH), BF16),
                        pltpu.VMEM((tm, B_KEY_WIDTH), F32),
                        pltpu.VMEM((B_KEY_WIDTH, B_WIDTH), F32),
                        pltpu.VMEM((B_HEADS, GLA_CHUNK, GLA_CHUNK), F32)],
        compiler_params=pltpu.CompilerParams(
            dimension_semantics=("arbitrary", "arbitrary"), vmem_limit_bytes=VMEM_LIMIT),
        name="proj_gla",
    )(x, nw, w_in, wa, ba, gnw)


def _moba_kernel(rb_ref, qT_ref, k_ref, vT_ref, own_ref, prev_ref, sg_ref, o_ref,
                 kmean_scr, kmbd_scr, qo_scr, qh_scr, m_scr, acc_scr, s_scr):
    i = pl.program_id(1)
    nb = kmean_scr.shape[0]
    blk = MOBA_BLOCK
    dh = A_HEAD_DIM
    n_tiles = A_HEADS // HEADS_PER_TILE

    @pl.when(i == 0)
    def _():
        for n in range(nb):
            kb = jnp.concatenate(
                [k_ref[0, n * blk:(n + 1) * blk, h * LANES:h * LANES + dh] for h in range(A_HEADS)],
                axis=1).astype(F32)
            kmean_scr[n:n + 1, :] = jnp.mean(kb, axis=0, keepdims=True)
        lane_head = jnp.right_shift(lax.broadcasted_iota(jnp.int32, (nb, A_WIDTH), 1),
                                    dh.bit_length() - 1)
        for h in range(A_HEADS):
            km = jnp.where(lane_head == h, kmean_scr[...], 0.0)
            km_hi = km.astype(BF16)
            km_lo = (km - km_hi.astype(F32)).astype(BF16)
            kmbd_scr[h * nb:(h + 1) * nb, :] = jnp.concatenate([km_hi, km_lo, km_hi], axis=1)

    q_f32 = qT_ref[0]
    q_hi = q_f32.astype(BF16)
    q_lo = (q_f32 - q_hi.astype(F32)).astype(BF16)
    gate_all = _dot(kmbd_scr[...], jnp.concatenate([q_hi, q_hi, q_lo], axis=0))
    brow = lax.broadcasted_iota(jnp.int32, (nb, blk), 0)
    browf = brow.astype(F32)
    zeros = jnp.zeros((LANES - dh - 2 * nb, blk), F32)

    for h in range(A_HEADS):
        q_h = qT_ref[0, h * dh:(h + 1) * dh, :] * (dh ** -0.5 * LOG2E)
        qo_scr[h] = jnp.concatenate([q_h, jnp.zeros((LANES - dh, blk), F32)], axis=0).astype(BF16)

    for h in range(A_HEADS):
        g = jnp.where(brow < i, gate_all[h * nb:(h + 1) * nb], NEG)
        sel = brow < 0
        for _ in range(MOBA_TOPK):
            m = jnp.max(g, axis=0, keepdims=True)
            idx = jnp.min(jnp.where(g == m, browf, float(nb)), axis=0, keepdims=True)
            pick = browf == idx
            sel = sel | pick
            g = jnp.where(pick, -jnp.inf, g)
        valid = sel & (brow < i)
        far_bias = rb_ref[REL_BUCKETS - 1, h] * LOG2E
        pen = jnp.where(valid, far_bias, NEG)
        pen_hi = pen.astype(BF16).astype(F32)
        q_h = qT_ref[0, h * dh:(h + 1) * dh, :] * (dh ** -0.5 * LOG2E)
        qh_scr[h] = jnp.concatenate([q_h, pen_hi, pen - pen_hi, zeros], axis=0).astype(BF16)

    half = blk // 2

    def logits(j, h, kind):
        rows = pl.ds(pl.multiple_of(j * blk, blk), blk)
        keys = k_ref[0, rows, h * LANES:(h + 1) * LANES]
        if kind == "own":
            return _dot(keys, qo_scr[h]) + own_ref[h]
        s = _dot(keys, qh_scr[h])
        if kind == "prev":
            near = s[half:, :half] + prev_ref[h]
            s = jnp.concatenate(
                [s[:half], jnp.concatenate([near, s[half:, half:]], axis=1)], axis=0)
        return s

    def fold(j, h, s, first):
        rows = pl.ds(pl.multiple_of(j * blk, blk), blk)
        bm = jnp.max(s, axis=0, keepdims=True)
        vj = vT_ref[0, h * PV_ROWS:(h + 1) * PV_ROWS, rows]
        if first:
            m_scr[h] = bm
            acc_scr[h] = _dot(vj, jnp.exp2(s - bm).astype(BF16))
        else:
            m_old = m_scr[h]
            m_new = jnp.maximum(m_old, bm)
            m_scr[h] = m_new
            acc_scr[h] = (jnp.exp2(m_old - m_new) * acc_scr[h]
                          + _dot(vj, jnp.exp2(s - m_new).astype(BF16)))

    n_far = i - 1
    j_prev = jnp.maximum(i - 1, 0)
    items = ([(i, "own", h) for h in range(A_HEADS)] + [(j_prev, "prev", h) for h in range(A_HEADS)]
             + [(0, "far", h) for h in range(QK_LOOKAHEAD)])
    n_fold = 2 * A_HEADS
    pending = []

    def issue(n):
        j, kind, h = items[n]
        s = logits(j, h, kind)
        if n < n_fold:
            pending.append(s)
        else:
            s_scr[h] = s

    for n in range(QK_LOOKAHEAD):
        issue(n)
    for n in range(n_fold):
        issue(n + QK_LOOKAHEAD)
        j, kind, h = items[n]
        fold(j, h, pending.pop(0), kind == "own")

    def far_group(j0, n_blocks):
        j_after = jnp.minimum(j0 + n_blocks, n_far - 1)
        stream = ([(j0 + t, h) for t in range(n_blocks) for h in range(A_HEADS)]
                  + [(j_after, h) for h in range(QK_LOOKAHEAD)])
        n_items = n_blocks * A_HEADS
        pending = [s_scr[h] for h in range(QK_LOOKAHEAD)]
        for n in range(n_items):
            j, h = stream[n + QK_LOOKAHEAD]
            if n + QK_LOOKAHEAD < n_items:
                pending.append(logits(j, h, "far"))
            else:
                s_scr[h] = logits(j, h, "far")
            j, h = stream[n]
            fold(j, h, pending.pop(0), False)

    def far_groups(g, carry):
        far_group(g * FAR_UNROLL, FAR_UNROLL)
        return carry

    n_groups = jnp.maximum(n_far, 0) // FAR_UNROLL
    lax.fori_loop(0, n_groups, far_groups, 0)
    for rest in range(1, FAR_UNROLL):
        @pl.when(n_far - n_groups * FAR_UNROLL == rest)
        def _():
            far_group(n_groups * FAR_UNROLL, rest)

    for t in range(n_tiles):
        oT = []
        for hh in range(HEADS_PER_TILE):
            acc = acc_scr[t * HEADS_PER_TILE + hh]
            oT.append(acc[:dh] / acc[dh:dh + 1])
        rows = slice(t * LANES, (t + 1) * LANES)
        o_ref[0, rows, :] = (jnp.concatenate(oT, axis=0) * sg_ref[0, rows, :]).astype(BF16)


def _moba(rel_bias, qaT, ka, vaT, bias_own, bias_prev, sga):
    B, S, _ = ka.shape
    nb = S // MOBA_BLOCK
    assert nb == MOBA_MAX_BLOCKS
    bias_spec = pl.BlockSpec((A_HEADS, MOBA_BLOCK, MOBA_BLOCK), lambda b, i: (0, 0, 0))
    corner_spec = pl.BlockSpec((A_HEADS, MOBA_BLOCK // 2, MOBA_BLOCK // 2), lambda b, i: (0, 0, 0))
    return pl.pallas_call(
        _moba_kernel,
        grid=(B, nb),
        in_specs=[
            pl.BlockSpec(memory_space=pltpu.SMEM),
            pl.BlockSpec((1, A_WIDTH, MOBA_BLOCK), lambda b, i: (b, 0, i)),
            pl.BlockSpec((1, S, A_HEADS * LANES), lambda b, i: (b, 0, 0)),
            pl.BlockSpec((1, A_HEADS * PV_ROWS, S), lambda b, i: (b, 0, 0)),
            bias_spec, corner_spec,
            pl.BlockSpec((1, A_WIDTH, MOBA_BLOCK), lambda b, i: (b, 0, i)),
        ],
        out_specs=pl.BlockSpec((1, A_WIDTH, MOBA_BLOCK), lambda b, i: (b, 0, i)),
        out_shape=jax.ShapeDtypeStruct((B, A_WIDTH, S), BF16),
        scratch_shapes=[pltpu.VMEM((nb, A_WIDTH), F32),
                        pltpu.VMEM((A_HEADS * nb, 3 * A_WIDTH), BF16),
                        pltpu.VMEM((A_HEADS, LANES, MOBA_BLOCK), BF16),
                        pltpu.VMEM((A_HEADS, LANES, MOBA_BLOCK), BF16),
                        pltpu.VMEM((A_HEADS, 1, MOBA_BLOCK), F32),
                        pltpu.VMEM((A_HEADS, PV_ROWS, MOBA_BLOCK), F32),
                        pltpu.VMEM((QK_LOOKAHEAD, MOBA_BLOCK, MOBA_BLOCK), F32)],
        compiler_params=pltpu.CompilerParams(
            dimension_semantics=("arbitrary", "arbitrary"),
            vmem_limit_bytes=VMEM_LIMIT),
        name="moba",
    )(rel_bias, qaT, ka, vaT, bias_own, bias_prev, sga)


def _gla_chunk(q_bf, k_bf, v, g, sg, nw, state_scr, att_scr, store_out, fill):
    C = GLA_CHUNK
    dk, dv, H = B_KEY_DIM, B_VAL_DIM, B_HEADS
    n_levels = C.bit_length() - 1

    q = q_bf.astype(F32)
    k = k_bf.astype(F32)

    row = lax.broadcasted_iota(jnp.int32, (C, H * dk), 0)
    tt = lax.broadcasted_iota(jnp.int32, (C, C), 0)
    ss = lax.broadcasted_iota(jnp.int32, (C, C), 1)

    g_hi = g.astype(BF16)
    g_mid = (g - g_hi.astype(F32)).astype(BF16)
    g_lo = (g - g_hi.astype(F32) - g_mid.astype(F32)).astype(BF16)
    tril = jnp.where(ss <= tt, 1.0, 0.0).astype(BF16)
    b3 = _dot(tril, jnp.concatenate([g_hi, g_mid, g_lo], axis=1))
    b = b3[:, :H * dk] + b3[:, H * dk:2 * H * dk] + b3[:, 2 * H * dk:]

    def roll_rows(x, shift):
        if shift % C < SUBLANES:
            x3 = x.reshape(C // SUBLANES, SUBLANES, x.shape[1])
            return pltpu.roll(x3, shift % C, 1).reshape(x.shape)
        if (C - shift % C) < SUBLANES:
            x3 = x.reshape(C // SUBLANES, SUBLANES, x.shape[1])
            return pltpu.roll(x3, SUBLANES - (C - shift % C), 1).reshape(x.shape)
        return pltpu.roll(x, shift, 0)

    txs = tt ^ ss
    level = jnp.full((C, C), -1, jnp.int32)
    for p in range(n_levels):
        level = level + (txs >= (1 << p)).astype(jnp.int32)
    level = jnp.where(ss <= tt, level, -2)

    heads_per_tile = LANES // dk
    lane = lax.broadcasted_iota(jnp.int32, (C, H * dk), 1)
    head_in_tile = jnp.right_shift(lane, dk.bit_length() - 1) & (heads_per_tile - 1)

    def split_heads(kk):
        return [jnp.where(head_in_tile == hh, kk, 0.0).astype(BF16) for hh in range(heads_per_tile)]

    def scores(q_bf, k_split, h):
        t, hh = divmod(h, heads_per_tile)
        cols = slice(t * LANES, (t + 1) * LANES)
        return _dot_nt(q_bf[:, cols], k_split[hh][:, cols])

    on_diag = level == -1
    k_split = split_heads(k)
    for h in range(H):
        att_scr[h] = jnp.where(on_diag, scores(q_bf, k_split, h), 0.0)
    fill()

    block_end = b
    for p in range(n_levels):
        m = 1 << p
        second_half = (row & m) != 0
        r = jnp.where(second_half, roll_rows(block_end, m), block_end)
        decay = jnp.exp2(-jnp.abs(b - r))
        kd_split = split_heads(k * decay)
        if m >= SUBLANES:
            blocks = [slice(lo, lo + m) for lo in range(m, C, 2 * m)]
            take = lambda x: jnp.concatenate([x[rows] for rows in blocks], axis=0)
            qd_bf = (take(q) * take(decay)).astype(BF16)
            for h in range(H):
                s_lvl = scores(qd_bf, kd_split, h)
                for n, rows in enumerate(blocks):
                    keys = slice(rows.start - m, rows.start)
                    att_scr[h, rows, keys] = s_lvl[n * m:(n + 1) * m, keys]
        else:
            qd_bf = (q * decay).astype(BF16)
            at_level = level == p
            for h in range(H):
                att_scr[h] = jnp.where(at_level, scores(qd_bf, kd_split, h), att_scr[h])
        fill()
        if p + 1 < n_levels:
            block_end = jnp.where(second_half, block_end, roll_rows(block_end, C - m))

    state = state_scr[...]
    o_inter = _dot((q * jnp.exp2(b)).astype(BF16), state.astype(BF16))

    for h in range(H):
        cols = slice(h * dv, (h + 1) * dv)
        o_h = o_inter[:, cols] + _dot(att_scr[h].astype(BF16), v[:, cols])
        store_out(cols, (_rmsnorm_rows(o_h, nw) * sg[:, cols]).astype(BF16))

    bT = b.T
    kT = k.T
    b_last = bT[:, C - 1:C]
    kdT = (kT * jnp.exp2(b_last - bT)).astype(BF16)
    ds = _dot(kdT, v)
    srow = lax.broadcasted_iota(jnp.int32, (H * dk, H * dv), 0) // dk
    scol = lax.broadcasted_iota(jnp.int32, (H * dk, H * dv), 1) // dv
    state_scr[...] = jnp.exp2(b_last) * state + jnp.where(srow == scol, ds, 0.0)


def _tn_dot(aT, b):
    return lax.dot_general(aT, b, (((0,), (0,)), ((), ())), preferred_element_type=F32)


def _out_kernel(x_ref, maT_ref, mb_ref, qcT_ref, sgcT_ref, km_ref, vmT_ref, w_ref, fw_ref, o_ref,
                wo_ref):
    @pl.when((pl.program_id(0) == 0) & (pl.program_id(1) == 0))
    def _():
        for lo in range(0, MIX_WIDTH, 4 * LANES):
            wo_ref[lo:lo + 4 * LANES, :] = w_ref[0, lo:lo + 4 * LANES, :].astype(BF16)

    dh = C_HEAD_DIM
    qcT = qcT_ref[0]
    km = km_ref[0]
    vmT = vmT_ref[0]
    heads = [slice(h * dh, (h + 1) * dh) for h in range(C_HEADS)]
    scores = [_dot(km[:, sl], qcT[sl, :]) for sl in heads]
    h_ab = (_tn_dot(maT_ref[0], wo_ref[0:A_WIDTH, :])
            + _dot(mb_ref[0], wo_ref[A_WIDTH:A_WIDTH + B_WIDTH, :]))
    ocT = []
    for s, sl in zip(scores, heads):
        p = jnp.exp(s - jnp.max(s, axis=0, keepdims=True))
        l = jnp.sum(p, axis=0, keepdims=True)
        ocT.append(_dot(vmT[sl, :], p.astype(BF16)) / l)
    mcT = (jnp.concatenate(ocT, axis=0) * sgcT_ref[0]).astype(BF16)
    h_new = x_ref[0] + h_ab + _tn_dot(mcT, wo_ref[A_WIDTH + B_WIDTH:, :])
    o_ref[0] = _rmsnorm_rows(h_new, fw_ref[...])


def _out(x, ma, mb, qc, sgc, km, vm, w_out, fw):
    B, S, D = x.shape
    tm = OUT_TM
    row = lambda width: pl.BlockSpec((1, tm, width), lambda b, i: (b, i, 0))
    col = lambda width: pl.BlockSpec((1, width, tm), lambda b, i: (b, 0, i))
    return pl.pallas_call(
        _out_kernel,
        grid=(B, S // tm),
        in_specs=[row(D), col(A_WIDTH), row(B_WIDTH), col(C_WIDTH), col(C_WIDTH),
                  pl.BlockSpec((1, MEM_LEN, C_WIDTH), lambda b, i: (b, 0, 0)),
                  pl.BlockSpec((1, C_WIDTH, MEM_LEN), lambda b, i: (b, 0, 0)),
                  pl.BlockSpec(w_out.shape, lambda b, i: (0, 0, 0), pipeline_mode=pl.Buffered(1)),
                  pl.BlockSpec((1, D), lambda b, i: (0, 0))],
        out_specs=row(D),
        out_shape=jax.ShapeDtypeStruct((B, S, D), F32),
        scratch_shapes=[pltpu.VMEM((MIX_WIDTH, D), BF16)],
        compiler_params=pltpu.CompilerParams(
            dimension_semantics=("arbitrary", "arbitrary"), vmem_limit_bytes=VMEM_LIMIT),
        name="out",
    )(x, ma, mb, qc, sgc, km, vm, w_out, fw)


def kernel(x, mem, norm_w, w_in, w_alpha2, b_alpha, gla_norm_w, mem_norm_w, w_mem_kv, w_out,
           rel_bias, final_norm_w):
    assert norm_w.shape[0] == 1, "single layer"
    assert w_in.shape[2] == W_IN_ALIGNED + GLA_RANK + 2 * C_WIDTH

    bias_own, bias_prev, km, vm = _prep(rel_bias, mem, mem_norm_w[0][None, :], w_mem_kv)
    qaT, vaT, ka, sga, qc, sgc, mb = _proj(x, norm_w[0][None, :], w_in[0].T, w_alpha2[0], b_alpha[0][None, :],
                                           gla_norm_w[0][None, :])
    ma = _moba(rel_bias, qaT, ka, vaT, bias_own, bias_prev, sga)
    return _out(x, ma, mb, qc, sgc, km, vm, w_out, final_norm_w[None, :])
```

```python
import functools
import math

import jax
import jax.numpy as jnp
from jax import lax
from jax.experimental import pallas as pl
from jax.experimental.pallas import tpu as pltpu

F32 = jnp.float32
BF16 = jnp.bfloat16

D_MODEL = 1024
MEM_LEN = 256
A_HEADS = 8
A_HEAD_DIM = 64
A_WIDTH = 512
MOBA_BLOCK = 256
MOBA_TOPK = 3
B_HEADS = 4
B_KEY_DIM = 64
B_VAL_DIM = 128
B_KEY_WIDTH = 256
B_WIDTH = 512
GLA_RANK = 16
GLA_TAU = 16.0
C_HEADS = 4
C_HEAD_DIM = 128
C_WIDTH = 512
MIX_WIDTH = A_WIDTH + B_WIDTH + C_WIDTH
REL_BUCKETS = 32
REL_MAX_DIST = 128
RMS_EPS = 1e-6
NEG = -1e30

LANES = 128
SUBLANES = 8
PROJ_TM = 512
OUT_TM = 512
GLA_CHUNK = 256
VMEM_LIMIT = 56 * 1024 * 1024
LOG2E = 1.0 / math.log(2.0)
HEADS_PER_TILE = LANES // A_HEAD_DIM
PV_ROWS = A_HEAD_DIM + 16
MOBA_MAX_BLOCKS = 16
QK_LOOKAHEAD = 5
FAR_UNROLL = 7


def _dot(a, b):
    return jnp.dot(a, b, preferred_element_type=F32)


def _dot_nt(a, b):
    return lax.dot_general(a, b, (((1,), (1,)), ((), ())), preferred_element_type=F32)


def _rmsnorm_rows(x, w):
    return x * lax.rsqrt(jnp.mean(x * x, axis=-1, keepdims=True) + RMS_EPS) * w


def _silu(x):
    return x * jax.nn.sigmoid(x)


def _t5_bucket(n):
    max_exact = REL_BUCKETS // 2
    nf = jnp.maximum(n, max_exact).astype(F32)
    large = max_exact + jnp.floor(jnp.log(nf / max_exact) / math.log(REL_MAX_DIST / max_exact)
                                  * (REL_BUCKETS - max_exact)).astype(jnp.int32)
    large = jnp.minimum(large, REL_BUCKETS - 1)
    return jnp.where(n < max_exact, n, large)


def _prep_kernel(n_batches, rb_ref, mem_ref, mnw_ref, w_ref, own_ref, prev_ref, km_ref, vm_ref):
    h = pl.program_id(0)

    @pl.when(h < n_batches)
    def _():
        u = _rmsnorm_rows(mem_ref[0], mnw_ref[...]).astype(BF16)
        km_ref[0] = _dot(u, w_ref[0, :, :C_WIDTH].astype(BF16)).astype(BF16)
        vm_ref[0] = lax.dot_general(w_ref[0, :, C_WIDTH:].astype(BF16), u, (((0,), (1,)), ((), ())),
                                    preferred_element_type=F32).astype(BF16)

    key = lax.broadcasted_iota(jnp.int32, (MOBA_BLOCK, MOBA_BLOCK), 0)
    qry = lax.broadcasted_iota(jnp.int32, (MOBA_BLOCK, MOBA_BLOCK), 1)
    rel = qry - key
    b_own = _t5_bucket(jnp.maximum(rel, 0))
    half = MOBA_BLOCK // 2
    assert REL_MAX_DIST <= half
    corner = (lax.broadcasted_iota(jnp.int32, (half, half), 1)
              - lax.broadcasted_iota(jnp.int32, (half, half), 0) + half)
    b_prev = _t5_bucket(corner)
    own = jnp.zeros((MOBA_BLOCK, MOBA_BLOCK), F32)
    prev = jnp.zeros((half, half), F32)
    for bk in range(REL_BUCKETS):
        val = rb_ref[bk, h] * LOG2E
        own = jnp.where(b_own == bk, val, own)
        prev = jnp.where(b_prev == bk, val, prev)
    own_ref[0] = jnp.where(rel >= 0, own, NEG)
    prev_ref[0] = prev - rb_ref[REL_BUCKETS - 1, h] * LOG2E


def _prep(rel_bias, mem, mnw, w_mem_kv):
    B = mem.shape[0]
    assert B <= A_HEADS
    batch = lambda h: (jnp.minimum(h, B - 1), 0, 0)
    half = MOBA_BLOCK // 2
    tile = pl.BlockSpec((1, MOBA_BLOCK, MOBA_BLOCK), lambda h: (h, 0, 0))
    corner = pl.BlockSpec((1, half, half), lambda h: (h, 0, 0))
    kv = pl.BlockSpec((1, MEM_LEN, C_WIDTH), batch)
    return pl.pallas_call(
        functools.partial(_prep_kernel, B),
        grid=(A_HEADS,),
        in_specs=[pl.BlockSpec(memory_space=pltpu.SMEM),
                  pl.BlockSpec((1, MEM_LEN, D_MODEL), batch),
                  pl.BlockSpec((1, D_MODEL), lambda h: (0, 0)),
                  pl.BlockSpec(w_mem_kv.shape, lambda h: (0, 0, 0))],
        out_specs=[tile, corner, kv, pl.BlockSpec((1, C_WIDTH, MEM_LEN), batch)],
        out_shape=[jax.ShapeDtypeStruct((A_HEADS, MOBA_BLOCK, MOBA_BLOCK), F32),
                   jax.ShapeDtypeStruct((A_HEADS, half, half), F32),
                   jax.ShapeDtypeStruct((B, MEM_LEN, C_WIDTH), BF16),
                   jax.ShapeDtypeStruct((B, C_WIDTH, MEM_LEN), BF16)],
        compiler_params=pltpu.CompilerParams(dimension_semantics=("arbitrary",)),
        name="prep",
    )(rel_bias, mem, mnw, w_mem_kv)


_ROW_COLS = (("qa", A_WIDTH), ("ka", A_WIDTH), ("va", A_WIDTH), ("ga", A_WIDTH),
             ("qb", B_KEY_WIDTH), ("kb", B_KEY_WIDTH), ("vb", B_WIDTH), ("gb", B_WIDTH),
             ("qc", C_WIDTH), ("gc", C_WIDTH), ("z", B_KEY_WIDTH))
_ROW_OFF = {}
_off = 0
for _name, _width in _ROW_COLS:
    _ROW_OFF[_name] = (_off, _off + _width)
    _off += _width
ROW_COLS_TOTAL = _off


W_IN_ALIGNED = 4 * A_WIDTH + 2 * B_KEY_WIDTH + 2 * B_WIDTH


def _proj_kernel(x_ref, nw_ref, w_ref, wa_ref, ba_ref, gnw_ref,
                 qaT_ref, vaT_ref, ka_ref, ga_ref, qc_ref, gc_ref, mb_ref,
                 wr_ref, qb_ref, kb_ref, vb_ref, gb_ref, g_ref, state_scr, att_scr):
    @pl.when(pl.program_id(1) == 0)
    def _():
        state_scr[...] = jnp.zeros_like(state_scr)

    @pl.when((pl.program_id(0) == 0) & (pl.program_id(1) == 0))
    def _():
        chunk = 4 * LANES
        for lo in range(0, W_IN_ALIGNED, chunk):
            wr_ref[lo:lo + chunk, :] = w_ref[lo:lo + chunk, :].astype(BF16)
        tail = W_IN_ALIGNED + GLA_RANK
        for lo in range(0, 2 * C_WIDTH, chunk):
            wr_ref[W_IN_ALIGNED + lo:W_IN_ALIGNED + lo + chunk, :] = (
                w_ref[tail + lo:tail + lo + chunk, :].astype(BF16))
        z_lo, z_hi = _ROW_OFF["z"]
        wr_ref[z_lo:z_hi, :] = jnp.dot(wa_ref[...].T, w_ref[W_IN_ALIGNED:tail, :],
                                       preferred_element_type=F32,
                                       precision=lax.Precision.HIGHEST).astype(BF16)

    u = _rmsnorm_rows(x_ref[0], nw_ref[...]).astype(BF16)
    tm = u.shape[0]

    def row(name, rows=slice(None)):
        lo, hi = _ROW_OFF[name]
        return _dot_nt(u[rows], wr_ref[lo:hi, :])

    def col(name, rows=slice(None)):
        lo, hi = _ROW_OFF[name]
        return _dot_nt(wr_ref[lo:hi, :], u[rows])

    z = row("z") + ba_ref[...]
    g_ref[...] = (jnp.minimum(z, 0.0) - jnp.log1p(jnp.exp(-jnp.abs(z)))) * (LOG2E / GLA_TAU)
    qb_ref[...] = (row("qb") * (B_KEY_DIM ** -0.5)).astype(BF16)
    kb_ref[...] = row("kb").astype(BF16)
    vb_ref[...] = row("vb").astype(BF16)
    gb_ref[...] = _silu(row("gb")).astype(BF16)

    def emit_qaT(rows):
        qaT_ref[0, :, rows] = col("qa", rows)

    def emit_vaT(rows):
        vaT = col("va", rows).astype(BF16)
        ones = jnp.ones((PV_ROWS - A_HEAD_DIM, vaT.shape[1]), BF16)
        pieces = []
        for h in range(A_HEADS):
            pieces += [vaT[h * A_HEAD_DIM:(h + 1) * A_HEAD_DIM], ones]
        vaT_ref[0, :, rows] = jnp.concatenate(pieces, axis=0)

    def emit_ka(rows):
        ka = row("ka", rows)
        n = ka.shape[0]
        nb = MOBA_MAX_BLOCKS
        key_pos = (pl.program_id(1) * tm + rows.start
                   + lax.broadcasted_iota(jnp.int32, (n, LANES - A_HEAD_DIM), 0))
        blk_id = jnp.right_shift(key_pos, MOBA_BLOCK.bit_length() - 1)
        lane = lax.broadcasted_iota(jnp.int32, (n, LANES - A_HEAD_DIM), 1)
        onehot = jnp.where((lane < 2 * nb) & ((lane & (nb - 1)) == blk_id), 1.0, 0.0)
        pieces = []
        for h in range(A_HEADS):
            pieces += [ka[:, h * A_HEAD_DIM:(h + 1) * A_HEAD_DIM], onehot]
        ka_ref[0, rows, :] = jnp.concatenate(pieces, axis=1).astype(BF16)

    def emit_ga(rows):
        ga_ref[0, :, rows] = _silu(col("ga", rows)).astype(BF16)

    def emit_qc(rows):
        qc_ref[0, :, rows] = (col("qc", rows) * (C_HEAD_DIM ** -0.5)).astype(BF16)

    def emit_gc(rows):
        gc_ref[0, :, rows] = _silu(col("gc", rows)).astype(BF16)

    chunks = [slice(lo, lo + GLA_CHUNK) for lo in range(0, tm, GLA_CHUNK)]
    pending = [functools.partial(emit, rows) for emit in
               (emit_qaT, emit_vaT, emit_ka, emit_ga, emit_qc, emit_gc) for rows in chunks]

    def fill():
        if pending:
            pending.pop(0)()

    nw_gla = gnw_ref[...]
    for rows in chunks:
        def store_out(cols, val, rows=rows):
            mb_ref[0, rows, cols] = val

        _gla_chunk(qb_ref[rows, :], kb_ref[rows, :], vb_ref[rows, :], g_ref[rows, :], gb_ref[rows, :],
                   nw_gla, state_scr, att_scr, store_out, fill)
    while pending:
        fill()


def _proj(x, nw, w_in, wa, ba, gnw):
    B, S, D = x.shape
    tm = PROJ_TM
    row_spec = lambda width: pl.BlockSpec((1, tm, width), lambda b, i: (b, i, 0))
    col_spec = lambda rows: pl.BlockSpec((1, rows, tm), lambda b, i: (b, 0, i))
    const = lambda shape: pl.BlockSpec(shape, lambda b, i: (0,) * len(shape))
    sds = jax.ShapeDtypeStruct
    return pl.pallas_call(
        _proj_kernel,
        grid=(B, S // tm),
        in_specs=[
            pl.BlockSpec((1, tm, D), lambda b, i: (b, i, 0)),
            const((1, D)),
            pl.BlockSpec(w_in.shape, lambda b, i: (0, 0), pipeline_mode=pl.Buffered(1)),
            const((GLA_RANK, B_KEY_WIDTH)),
            const((1, B_KEY_WIDTH)),
            const((1, B_VAL_DIM)),
        ],
        out_specs=[col_spec(A_WIDTH), col_spec(A_HEADS * PV_ROWS), row_spec(A_HEADS * LANES),
                   col_spec(A_WIDTH), col_spec(C_WIDTH), col_spec(C_WIDTH), row_spec(B_WIDTH)],
        out_shape=[sds((B, A_WIDTH, S), F32), sds((B, A_HEADS * PV_ROWS, S), BF16),
                   sds((B, S, A_HEADS * LANES), BF16), sds((B, A_WIDTH, S), BF16),
                   sds((B, C_WIDTH, S), BF16), sds((B, C_WIDTH, S), BF16),
                   sds((B, S, B_WIDTH), BF16)],
        scratch_shapes=[pltpu.VMEM((ROW_COLS_TOTAL, D), BF16),
                        pltpu.VMEM((tm, B_KEY_WIDTH), BF16), pltpu.VMEM((tm, B_KEY_WIDTH), BF16),
                        pltpu.VMEM((tm, B_WIDTH), BF16), pltpu.VMEM((tm, B_WIDTH), BF16),
                        pltpu.VMEM((tm, B_KEY_WIDTH), F32),
                        pltpu.VMEM((B_KEY_WIDTH, B_WIDTH), F32),
                        pltpu.VMEM((B_HEADS, GLA_CHUNK, GLA_CHUNK), F32)],
        compiler_params=pltpu.CompilerParams(
            dimension_semantics=("arbitrary", "arbitrary"), vmem_limit_bytes=VMEM_LIMIT),
        name="proj_gla",
    )(x, nw, w_in, wa, ba, gnw)


def _moba_kernel(rb_ref, qT_ref, k_ref, vT_ref, own_ref, prev_ref, sg_ref, o_ref,
                 kmean_scr, kmbd_scr, qo_scr, qh_scr, m_scr, acc_scr, s_scr):
    i = pl.program_id(1)
    nb = kmean_scr.shape[0]
    blk = MOBA_BLOCK
    dh = A_HEAD_DIM
    n_tiles = A_HEADS // HEADS_PER_TILE

    @pl.when(i == 0)
    def _():
        for n in range(nb):
            kb = jnp.concatenate(
                [k_ref[0, n * blk:(n + 1) * blk, h * LANES:h * LANES + dh] for h in range(A_HEADS)],
                axis=1).astype(F32)
            kmean_scr[n:n + 1, :] = jnp.mean(kb, axis=0, keepdims=True)
        lane_head = jnp.right_shift(lax.broadcasted_iota(jnp.int32, (nb, A_WIDTH), 1),
                                    dh.bit_length() - 1)
        for h in range(A_HEADS):
            km = jnp.where(lane_head == h, kmean_scr[...], 0.0)
            km_hi = km.astype(BF16)
            km_lo = (km - km_hi.astype(F32)).astype(BF16)
            kmbd_scr[h * nb:(h + 1) * nb, :] = jnp.concatenate([km_hi, km_lo, km_hi], axis=1)

    q_f32 = qT_ref[0]
    q_hi = q_f32.astype(BF16)
    q_lo = (q_f32 - q_hi.astype(F32)).astype(BF16)
    gate_all = _dot(kmbd_scr[...], jnp.concatenate([q_hi, q_hi, q_lo], axis=0))
    brow = lax.broadcasted_iota(jnp.int32, (nb, blk), 0)
    browf = brow.astype(F32)
    zeros = jnp.zeros((LANES - dh - 2 * nb, blk), F32)

    for h in range(A_HEADS):
        q_h = qT_ref[0, h * dh:(h + 1) * dh, :] * (dh ** -0.5 * LOG2E)
        qo_scr[h] = jnp.concatenate([q_h, jnp.zeros((LANES - dh, blk), F32)], axis=0).astype(BF16)

    for h in range(A_HEADS):
        g = jnp.where(brow < i, gate_all[h * nb:(h + 1) * nb], NEG)
        sel = brow < 0
        for _ in range(MOBA_TOPK):
            m = jnp.max(g, axis=0, keepdims=True)
            idx = jnp.min(jnp.where(g == m, browf, float(nb)), axis=0, keepdims=True)
            pick = browf == idx
            sel = sel | pick
            g = jnp.where(pick, -jnp.inf, g)
        valid = sel & (brow < i)
        far_bias = rb_ref[REL_BUCKETS - 1, h] * LOG2E
        pen = jnp.where(valid, far_bias, NEG)
        pen_hi = pen.astype(BF16).astype(F32)
        q_h = qT_ref[0, h * dh:(h + 1) * dh, :] * (dh ** -0.5 * LOG2E)
        qh_scr[h] = jnp.concatenate([q_h, pen_hi, pen - pen_hi, zeros], axis=0).astype(BF16)

    half = blk // 2

    def logits(j, h, kind):
        rows = pl.ds(pl.multiple_of(j * blk, blk), blk)
        keys = k_ref[0, rows, h * LANES:(h + 1) * LANES]
        if kind == "own":
            return _dot(keys, qo_scr[h]) + own_ref[h]
        s = _dot(keys, qh_scr[h])
        if kind == "prev":
            near = s[half:, :half] + prev_ref[h]
            s = jnp.concatenate(
                [s[:half], jnp.concatenate([near, s[half:, half:]], axis=1)], axis=0)
        return s

    def fold(j, h, s, first):
        rows = pl.ds(pl.multiple_of(j * blk, blk), blk)
        bm = jnp.max(s, axis=0, keepdims=True)
        vj = vT_ref[0, h * PV_ROWS:(h + 1) * PV_ROWS, rows]
        if first:
            m_scr[h] = bm
            acc_scr[h] = _dot(vj, jnp.exp2(s - bm).astype(BF16))
        else:
            m_old = m_scr[h]
            m_new = jnp.maximum(m_old, bm)
            m_scr[h] = m_new
            acc_scr[h] = (jnp.exp2(m_old - m_new) * acc_scr[h]
                          + _dot(vj, jnp.exp2(s - m_new).astype(BF16)))

    n_far = i - 1
    j_prev = jnp.maximum(i - 1, 0)
    items = ([(i, "own", h) for h in range(A_HEADS)] + [(j_prev, "prev", h) for h in range(A_HEADS)]
             + [(0, "far", h) for h in range(QK_LOOKAHEAD)])
    n_fold = 2 * A_HEADS
    pending = []

    def issue(n):
        j, kind, h = items[n]
        s = logits(j, h, kind)
        if n < n_fold:
            pending.append(s)
        else:
            s_scr[h] = s

    for n in range(QK_LOOKAHEAD):
        issue(n)
    for n in range(n_fold):
        issue(n + QK_LOOKAHEAD)
        j, kind, h = items[n]
        fold(j, h, pending.pop(0), kind == "own")

    def far_group(j0, n_blocks):
        j_after = jnp.minimum(j0 + n_blocks, n_far - 1)
        stream = ([(j0 + t, h) for t in range(n_blocks) for h in range(A_HEADS)]
                  + [(j_after, h) for h in range(QK_LOOKAHEAD)])
        n_items = n_blocks * A_HEADS
        pending = [s_scr[h] for h in range(QK_LOOKAHEAD)]
        for n in range(n_items):
            j, h = stream[n + QK_LOOKAHEAD]
            if n + QK_LOOKAHEAD < n_items:
                pending.append(logits(j, h, "far"))
            else:
                s_scr[h] = logits(j, h, "far")
            j, h = stream[n]
            fold(j, h, pending.pop(0), False)

    def far_groups(g, carry):
        far_group(g * FAR_UNROLL, FAR_UNROLL)
        return carry

    n_groups = jnp.maximum(n_far, 0) // FAR_UNROLL
    lax.fori_loop(0, n_groups, far_groups, 0)
    for rest in range(1, FAR_UNROLL):
        @pl.when(n_far - n_groups * FAR_UNROLL == rest)
        def _():
            far_group(n_groups * FAR_UNROLL, rest)

    for t in range(n_tiles):
        oT = []
        for hh in range(HEADS_PER_TILE):
            acc = acc_scr[t * HEADS_PER_TILE + hh]
            oT.append(acc[:dh] / acc[dh:dh + 1])
        rows = slice(t * LANES, (t + 1) * LANES)
        o_ref[0, rows, :] = (jnp.concatenate(oT, axis=0) * sg_ref[0, rows, :]).astype(BF16)


def _moba(rel_bias, qaT, ka, vaT, bias_own, bias_prev, sga):
    B, S, _ = ka.shape
    nb = S // MOBA_BLOCK
    assert nb == MOBA_MAX_BLOCKS
    bias_spec = pl.BlockSpec((A_HEADS, MOBA_BLOCK, MOBA_BLOCK), lambda b, i: (0, 0, 0))
    corner_spec = pl.BlockSpec((A_HEADS, MOBA_BLOCK // 2, MOBA_BLOCK // 2), lambda b, i: (0, 0, 0))
    return pl.pallas_call(
        _moba_kernel,
        grid=(B, nb),
        in_specs=[
            pl.BlockSpec(memory_space=pltpu.SMEM),
            pl.BlockSpec((1, A_WIDTH, MOBA_BLOCK), lambda b, i: (b, 0, i)),
            pl.BlockSpec((1, S, A_HEADS * LANES), lambda b, i: (b, 0, 0)),
            pl.BlockSpec((1, A_HEADS * PV_ROWS, S), lambda b, i: (b, 0, 0)),
            bias_spec, corner_spec,
            pl.BlockSpec((1, A_WIDTH, MOBA_BLOCK), lambda b, i: (b, 0, i)),
        ],
        out_specs=pl.BlockSpec((1, A_WIDTH, MOBA_BLOCK), lambda b, i: (b, 0, i)),
        out_shape=jax.ShapeDtypeStruct((B, A_WIDTH, S), BF16),
        scratch_shapes=[pltpu.VMEM((nb, A_WIDTH), F32),
                        pltpu.VMEM((A_HEADS * nb, 3 * A_WIDTH), BF16),
                        pltpu.VMEM((A_HEADS, LANES, MOBA_BLOCK), BF16),
                        pltpu.VMEM((A_HEADS, LANES, MOBA_BLOCK), BF16),
                        pltpu.VMEM((A_HEADS, 1, MOBA_BLOCK), F32),
                        pltpu.VMEM((A_HEADS, PV_ROWS, MOBA_BLOCK), F32),
                        pltpu.VMEM((QK_LOOKAHEAD, MOBA_BLOCK, MOBA_BLOCK), F32)],
        compiler_params=pltpu.CompilerParams(
            dimension_semantics=("arbitrary", "arbitrary"),
            vmem_limit_bytes=VMEM_LIMIT),
        name="moba",
    )(rel_bias, qaT, ka, vaT, bias_own, bias_prev, sga)


def _gla_chunk(q_bf, k_bf, v, g, sg, nw, state_scr, att_scr, store_out, fill):
    C = GLA_CHUNK
    dk, dv, H = B_KEY_DIM, B_VAL_DIM, B_HEADS
    n_levels = C.bit_length() - 1

    q = q_bf.astype(F32)
    k = k_bf.astype(F32)

    row = lax.broadcasted_iota(jnp.int32, (C, H * dk), 0)
    tt = lax.broadcasted_iota(jnp.int32, (C, C), 0)
    ss = lax.broadcasted_iota(jnp.int32, (C, C), 1)

    g_hi = g.astype(BF16)
    g_mid = (g - g_hi.astype(F32)).astype(BF16)
    g_lo = (g - g_hi.astype(F32) - g_mid.astype(F32)).astype(BF16)
    tril = jnp.where(ss <= tt, 1.0, 0.0).astype(BF16)
    b3 = _dot(tril, jnp.concatenate([g_hi, g_mid, g_lo], axis=1))
    b = b3[:, :H * dk] + b3[:, H * dk:2 * H * dk] + b3[:, 2 * H * dk:]

    def roll_rows(x, shift):
        if shift % C < SUBLANES:
            x3 = x.reshape(C // SUBLANES, SUBLANES, x.shape[1])
            return pltpu.roll(x3, shift % C, 1).reshape(x.shape)
        if (C - shift % C) < SUBLANES:
            x3 = x.reshape(C // SUBLANES, SUBLANES, x.shape[1])
            return pltpu.roll(x3, SUBLANES - (C - shift % C), 1).reshape(x.shape)
        return pltpu.roll(x, shift, 0)

    txs = tt ^ ss
    level = jnp.full((C, C), -1, jnp.int32)
    for p in range(n_levels):
        level = level + (txs >= (1 << p)).astype(jnp.int32)
    level = jnp.where(ss <= tt, level, -2)

    heads_per_tile = LANES // dk
    lane = lax.broadcasted_iota(jnp.int32, (C, H * dk), 1)
    head_in_tile = jnp.right_shift(lane, dk.bit_length() - 1) & (heads_per_tile - 1)

    def split_heads(kk):
        return [jnp.where(head_in_tile == hh, kk, 0.0).astype(BF16) for hh in range(heads_per_tile)]

    def scores(q_bf, k_split, h):
        t, hh = divmod(h, heads_per_tile)
        cols = slice(t * LANES, (t + 1) * LANES)
        return _dot_nt(q_bf[:, cols], k_split[hh][:, cols])

    on_diag = level == -1
    k_split = split_heads(k)
    for h in range(H):
        att_scr[h] = jnp.where(on_diag, scores(q_bf, k_split, h), 0.0)
    fill()

    block_end = b
    for p in range(n_levels):
        m = 1 << p
        second_half = (row & m) != 0
        r = jnp.where(second_half, roll_rows(block_end, m), block_end)
        decay = jnp.exp2(-jnp.abs(b - r))
        kd_split = split_heads(k * decay)
        if m >= SUBLANES:
            blocks = [slice(lo, lo + m) for lo in range(m, C, 2 * m)]
            take = lambda x: jnp.concatenate([x[rows] for rows in blocks], axis=0)
            qd_bf = (take(q) * take(decay)).astype(BF16)
            for h in range(H):
                s_lvl = scores(qd_bf, kd_split, h)
                for n, rows in enumerate(blocks):
                    keys = slice(rows.start - m, rows.start)
                    att_scr[h, rows, keys] = s_lvl[n * m:(n + 1) * m, keys]
        else:
            qd_bf = (q * decay).astype(BF16)
            at_level = level == p
            for h in range(H):
                att_scr[h] = jnp.where(at_level, scores(qd_bf, kd_split, h), att_scr[h])
        fill()
        if p + 1 < n_levels:
            block_end = jnp.where(second_half, block_end, roll_rows(block_end, C - m))

    state = state_scr[...]
    o_inter = _dot((q * jnp.exp2(b)).astype(BF16), state.astype(BF16))

    for h in range(H):
        cols = slice(h * dv, (h + 1) * dv)
        o_h = o_inter[:, cols] + _dot(att_scr[h].astype(BF16), v[:, cols])
        store_out(cols, (_rmsnorm_rows(o_h, nw) * sg[:, cols]).astype(BF16))

    bT = b.T
    kT = k.T
    b_last = bT[:, C - 1:C]
    kdT = (kT * jnp.exp2(b_last - bT)).astype(BF16)
    ds = _dot(kdT, v)
    srow = lax.broadcasted_iota(jnp.int32, (H * dk, H * dv), 0) // dk
    scol = lax.broadcasted_iota(jnp.int32, (H * dk, H * dv), 1) // dv
    state_scr[...] = jnp.exp2(b_last) * state + jnp.where(srow == scol, ds, 0.0)


def _tn_dot(aT, b):
    return lax.dot_general(aT, b, (((0,), (0,)), ((), ())), preferred_element_type=F32)


def _out_kernel(x_ref, maT_ref, mb_ref, qcT_ref, sgcT_ref, km_ref, vmT_ref, w_ref, fw_ref, o_ref,
                wo_ref):
    @pl.when((pl.program_id(0) == 0) & (pl.program_id(1) == 0))
    def _():
        for lo in range(0, MIX_WIDTH, 4 * LANES):
            wo_ref[lo:lo + 4 * LANES, :] = w_ref[0, lo:lo + 4 * LANES, :].astype(BF16)

    dh = C_HEAD_DIM
    qcT = qcT_ref[0]
    km = km_ref[0]
    vmT = vmT_ref[0]
    heads = [slice(h * dh, (h + 1) * dh) for h in range(C_HEADS)]
    scores = [_dot(km[:, sl], qcT[sl, :]) for sl in heads]
    h_ab = (_tn_dot(maT_ref[0], wo_ref[0:A_WIDTH, :])
            + _dot(mb_ref[0], wo_ref[A_WIDTH:A_WIDTH + B_WIDTH, :]))
    ocT = []
    for s, sl in zip(scores, heads):
        p = jnp.exp(s - jnp.max(s, axis=0, keepdims=True))
        l = jnp.sum(p, axis=0, keepdims=True)
        ocT.append(_dot(vmT[sl, :], p.astype(BF16)) / l)
    mcT = (jnp.concatenate(ocT, axis=0) * sgcT_ref[0]).astype(BF16)
    tm = mcT.shape[1]
    for rows in (slice(0, tm // 2), slice(tm // 2, tm)):
        h_new = x_ref[0, rows, :] + h_ab[rows] + _tn_dot(mcT[:, rows], wo_ref[A_WIDTH + B_WIDTH:, :])
        o_ref[0, rows, :] = _rmsnorm_rows(h_new, fw_ref[...])


def _out(x, ma, mb, qc, sgc, km, vm, w_out, fw):
    B, S, D = x.shape
    tm = OUT_TM
    row = lambda width: pl.BlockSpec((1, tm, width), lambda b, i: (b, i, 0))
    col = lambda width: pl.BlockSpec((1, width, tm), lambda b, i: (b, 0, i))
    return pl.pallas_call(
        _out_kernel,
        grid=(B, S // tm),
        in_specs=[row(D), col(A_WIDTH), row(B_WIDTH), col(C_WIDTH), col(C_WIDTH),
                  pl.BlockSpec((1, MEM_LEN, C_WIDTH), lambda b, i: (b, 0, 0)),
                  pl.BlockSpec((1, C_WIDTH, MEM_LEN), lambda b, i: (b, 0, 0)),
                  pl.BlockSpec(w_out.shape, lambda b, i: (0, 0, 0), pipeline_mode=pl.Buffered(1)),
                  pl.BlockSpec((1, D), lambda b, i: (0, 0))],
        out_specs=row(D),
        out_shape=jax.ShapeDtypeStruct((B, S, D), F32),
        scratch_shapes=[pltpu.VMEM((MIX_WIDTH, D), BF16)],
        compiler_params=pltpu.CompilerParams(
            dimension_semantics=("arbitrary", "arbitrary"), vmem_limit_bytes=VMEM_LIMIT),
        name="out",
    )(x, ma, mb, qc, sgc, km, vm, w_out, fw)


def kernel(x, mem, norm_w, w_in, w_alpha2, b_alpha, gla_norm_w, mem_norm_w, w_mem_kv, w_out,
           rel_bias, final_norm_w):
    assert norm_w.shape[0] == 1, "single layer"
    assert w_in.shape[2] == W_IN_ALIGNED + GLA_RANK + 2 * C_WIDTH

    bias_own, bias_prev, km, vm = _prep(rel_bias, mem, mem_norm_w[0][None, :], w_mem_kv)
    qaT, vaT, ka, sga, qc, sgc, mb = _proj(x, norm_w[0][None, :], w_in[0].T, w_alpha2[0], b_alpha[0][None, :],
                                           gla_norm_w[0][None, :])
    ma = _moba(rel_bias, qaT, ka, vaT, bias_own, bias_prev, sga)
    return _out(x, ma, mb, qc, sgc, km, vm, w_out, final_norm_w[None, :])
```

```python
import functools
import math

import jax
import jax.numpy as jnp
from jax import lax
from jax.experimental import pallas as pl
from jax.experimental.pallas import tpu as pltpu

F32 = jnp.float32
BF16 = jnp.bfloat16

D_MODEL = 1024
MEM_LEN = 256
A_HEADS = 8
A_HEAD_DIM = 64
A_WIDTH = 512
MOBA_BLOCK = 256
MOBA_TOPK = 3
B_HEADS = 4
B_KEY_DIM = 64
B_VAL_DIM = 128
B_KEY_WIDTH = 256
B_WIDTH = 512
GLA_RANK = 16
GLA_TAU = 16.0
C_HEADS = 4
C_HEAD_DIM = 128
C_WIDTH = 512
MIX_WIDTH = A_WIDTH + B_WIDTH + C_WIDTH
REL_BUCKETS = 32
REL_MAX_DIST = 128
RMS_EPS = 1e-6
NEG = -1e30

LANES = 128
SUBLANES = 8
PROJ_TM = 512
OUT_TM = 1024
GLA_CHUNK = 256
VMEM_LIMIT = 56 * 1024 * 1024
LOG2E = 1.0 / math.log(2.0)
HEADS_PER_TILE = LANES // A_HEAD_DIM
PV_ROWS = A_HEAD_DIM + 16
MOBA_MAX_BLOCKS = 16
QK_LOOKAHEAD = 5
FAR_UNROLL = 7


def _dot(a, b):
    return jnp.dot(a, b, preferred_element_type=F32)


def _dot_nt(a, b):
    return lax.dot_general(a, b, (((1,), (1,)), ((), ())), preferred_element_type=F32)


def _rmsnorm_rows(x, w):
    return x * lax.rsqrt(jnp.mean(x * x, axis=-1, keepdims=True) + RMS_EPS) * w


def _silu(x):
    return x * jax.nn.sigmoid(x)


def _t5_bucket(n):
    max_exact = REL_BUCKETS // 2
    nf = jnp.maximum(n, max_exact).astype(F32)
    large = max_exact + jnp.floor(jnp.log(nf / max_exact) / math.log(REL_MAX_DIST / max_exact)
                                  * (REL_BUCKETS - max_exact)).astype(jnp.int32)
    large = jnp.minimum(large, REL_BUCKETS - 1)
    return jnp.where(n < max_exact, n, large)


def _prep_kernel(n_batches, rb_ref, mem_ref, mnw_ref, w_ref, own_ref, prev_ref, km_ref, vm_ref):
    h = pl.program_id(0)

    @pl.when(h < n_batches)
    def _():
        u = _rmsnorm_rows(mem_ref[0], mnw_ref[...]).astype(BF16)
        km_ref[0] = _dot(u, w_ref[0, :, :C_WIDTH].astype(BF16)).astype(BF16)
        vm_ref[0] = lax.dot_general(w_ref[0, :, C_WIDTH:].astype(BF16), u, (((0,), (1,)), ((), ())),
                                    preferred_element_type=F32).astype(BF16)

    key = lax.broadcasted_iota(jnp.int32, (MOBA_BLOCK, MOBA_BLOCK), 0)
    qry = lax.broadcasted_iota(jnp.int32, (MOBA_BLOCK, MOBA_BLOCK), 1)
    rel = qry - key
    b_own = _t5_bucket(jnp.maximum(rel, 0))
    half = MOBA_BLOCK // 2
    assert REL_MAX_DIST <= half
    corner = (lax.broadcasted_iota(jnp.int32, (half, half), 1)
              - lax.broadcasted_iota(jnp.int32, (half, half), 0) + half)
    b_prev = _t5_bucket(corner)
    own = jnp.zeros((MOBA_BLOCK, MOBA_BLOCK), F32)
    prev = jnp.zeros((half, half), F32)
    for bk in range(REL_BUCKETS):
        val = rb_ref[bk, h] * LOG2E
        own = jnp.where(b_own == bk, val, own)
        prev = jnp.where(b_prev == bk, val, prev)
    own_ref[0] = jnp.where(rel >= 0, own, NEG)
    prev_ref[0] = prev - rb_ref[REL_BUCKETS - 1, h] * LOG2E


def _prep(rel_bias, mem, mnw, w_mem_kv):
    B = mem.shape[0]
    assert B <= A_HEADS
    batch = lambda h: (jnp.minimum(h, B - 1), 0, 0)
    half = MOBA_BLOCK // 2
    tile = pl.BlockSpec((1, MOBA_BLOCK, MOBA_BLOCK), lambda h: (h, 0, 0))
    corner = pl.BlockSpec((1, half, half), lambda h: (h, 0, 0))
    kv = pl.BlockSpec((1, MEM_LEN, C_WIDTH), batch)
    return pl.pallas_call(
        functools.partial(_prep_kernel, B),
        grid=(A_HEADS,),
        in_specs=[pl.BlockSpec(memory_space=pltpu.SMEM),
                  pl.BlockSpec((1, MEM_LEN, D_MODEL), batch),
                  pl.BlockSpec((1, D_MODEL), lambda h: (0, 0)),
                  pl.BlockSpec(w_mem_kv.shape, lambda h: (0, 0, 0))],
        out_specs=[tile, corner, kv, pl.BlockSpec((1, C_WIDTH, MEM_LEN), batch)],
        out_shape=[jax.ShapeDtypeStruct((A_HEADS, MOBA_BLOCK, MOBA_BLOCK), F32),
                   jax.ShapeDtypeStruct((A_HEADS, half, half), F32),
                   jax.ShapeDtypeStruct((B, MEM_LEN, C_WIDTH), BF16),
                   jax.ShapeDtypeStruct((B, C_WIDTH, MEM_LEN), BF16)],
        compiler_params=pltpu.CompilerParams(dimension_semantics=("arbitrary",)),
        name="prep",
    )(rel_bias, mem, mnw, w_mem_kv)


_ROW_COLS = (("qa", A_WIDTH), ("ka", A_WIDTH), ("va", A_WIDTH), ("ga", A_WIDTH),
             ("qb", B_KEY_WIDTH), ("kb", B_KEY_WIDTH), ("vb", B_WIDTH), ("gb", B_WIDTH),
             ("qc", C_WIDTH), ("gc", C_WIDTH), ("z", B_KEY_WIDTH))
_ROW_OFF = {}
_off = 0
for _name, _width in _ROW_COLS:
    _ROW_OFF[_name] = (_off, _off + _width)
    _off += _width
ROW_COLS_TOTAL = _off


W_IN_ALIGNED = 4 * A_WIDTH + 2 * B_KEY_WIDTH + 2 * B_WIDTH


def _proj_kernel(x_ref, nw_ref, w_ref, wa_ref, ba_ref, gnw_ref,
                 qaT_ref, vaT_ref, ka_ref, ga_ref, qc_ref, gc_ref, mb_ref,
                 wr_ref, qb_ref, kb_ref, vb_ref, gb_ref, g_ref, state_scr, att_scr):
    @pl.when(pl.program_id(1) == 0)
    def _():
        state_scr[...] = jnp.zeros_like(state_scr)

    @pl.when((pl.program_id(0) == 0) & (pl.program_id(1) == 0))
    def _():
        chunk = 4 * LANES
        for lo in range(0, W_IN_ALIGNED, chunk):
            wr_ref[lo:lo + chunk, :] = w_ref[lo:lo + chunk, :].astype(BF16)
        tail = W_IN_ALIGNED + GLA_RANK
        for lo in range(0, 2 * C_WIDTH, chunk):
            wr_ref[W_IN_ALIGNED + lo:W_IN_ALIGNED + lo + chunk, :] = (
                w_ref[tail + lo:tail + lo + chunk, :].astype(BF16))
        z_lo, z_hi = _ROW_OFF["z"]
        wr_ref[z_lo:z_hi, :] = jnp.dot(wa_ref[...].T, w_ref[W_IN_ALIGNED:tail, :],
                                       preferred_element_type=F32,
                                       precision=lax.Precision.HIGHEST).astype(BF16)

    u = _rmsnorm_rows(x_ref[0], nw_ref[...]).astype(BF16)
    tm = u.shape[0]

    def row(name, rows=slice(None)):
        lo, hi = _ROW_OFF[name]
        return _dot_nt(u[rows], wr_ref[lo:hi, :])

    def col(name, rows=slice(None)):
        lo, hi = _ROW_OFF[name]
        return _dot_nt(wr_ref[lo:hi, :], u[rows])

    z = row("z") + ba_ref[...]
    g_ref[...] = (jnp.minimum(z, 0.0) - jnp.log1p(jnp.exp(-jnp.abs(z)))) * (LOG2E / GLA_TAU)
    qb_ref[...] = (row("qb") * (B_KEY_DIM ** -0.5)).astype(BF16)
    kb_ref[...] = row("kb").astype(BF16)
    vb_ref[...] = row("vb").astype(BF16)
    gb_ref[...] = _silu(row("gb")).astype(BF16)

    def emit_qaT(rows):
        qaT_ref[0, :, rows] = col("qa", rows)

    def emit_vaT(rows):
        vaT = col("va", rows).astype(BF16)
        ones = jnp.ones((PV_ROWS - A_HEAD_DIM, vaT.shape[1]), BF16)
        pieces = []
        for h in range(A_HEADS):
            pieces += [vaT[h * A_HEAD_DIM:(h + 1) * A_HEAD_DIM], ones]
        vaT_ref[0, :, rows] = jnp.concatenate(pieces, axis=0)

    def emit_ka(rows):
        ka = row("ka", rows)
        n = ka.shape[0]
        nb = MOBA_MAX_BLOCKS
        key_pos = (pl.program_id(1) * tm + rows.start
                   + lax.broadcasted_iota(jnp.int32, (n, LANES - A_HEAD_DIM), 0))
        blk_id = jnp.right_shift(key_pos, MOBA_BLOCK.bit_length() - 1)
        lane = lax.broadcasted_iota(jnp.int32, (n, LANES - A_HEAD_DIM), 1)
        onehot = jnp.where((lane < 2 * nb) & ((lane & (nb - 1)) == blk_id), 1.0, 0.0)
        pieces = []
        for h in range(A_HEADS):
            pieces += [ka[:, h * A_HEAD_DIM:(h + 1) * A_HEAD_DIM], onehot]
        ka_ref[0, rows, :] = jnp.concatenate(pieces, axis=1).astype(BF16)

    def emit_ga(rows):
        ga_ref[0, :, rows] = _silu(col("ga", rows)).astype(BF16)

    def emit_qc(rows):
        qc_ref[0, :, rows] = (col("qc", rows) * (C_HEAD_DIM ** -0.5)).astype(BF16)

    def emit_gc(rows):
        gc_ref[0, :, rows] = _silu(col("gc", rows)).astype(BF16)

    chunks = [slice(lo, lo + GLA_CHUNK) for lo in range(0, tm, GLA_CHUNK)]
    pending = [functools.partial(emit, rows) for emit in
               (emit_qaT, emit_vaT, emit_ka, emit_ga, emit_qc, emit_gc) for rows in chunks]

    def fill():
        if pending:
            pending.pop(0)()

    nw_gla = gnw_ref[...]
    for rows in chunks:
        def store_out(cols, val, rows=rows):
            mb_ref[0, rows, cols] = val

        _gla_chunk(qb_ref[rows, :], kb_ref[rows, :], vb_ref[rows, :], g_ref[rows, :], gb_ref[rows, :],
                   nw_gla, state_scr, att_scr, store_out, fill)
    while pending:
        fill()


def _proj(x, nw, w_in, wa, ba, gnw):
    B, S, D = x.shape
    tm = PROJ_TM
    row_spec = lambda width: pl.BlockSpec((1, tm, width), lambda b, i: (b, i, 0))
    col_spec = lambda rows: pl.BlockSpec((1, rows, tm), lambda b, i: (b, 0, i))
    const = lambda shape: pl.BlockSpec(shape, lambda b, i: (0,) * len(shape))
    sds = jax.ShapeDtypeStruct
    return pl.pallas_call(
        _proj_kernel,
        grid=(B, S // tm),
        in_specs=[
            pl.BlockSpec((1, tm, D), lambda b, i: (b, i, 0)),
            const((1, D)),
            pl.BlockSpec(w_in.shape, lambda b, i: (0, 0), pipeline_mode=pl.Buffered(1)),
            const((GLA_RANK, B_KEY_WIDTH)),
            const((1, B_KEY_WIDTH)),
            const((1, B_VAL_DIM)),
        ],
        out_specs=[col_spec(A_WIDTH), col_spec(A_HEADS * PV_ROWS), row_spec(A_HEADS * LANES),
                   col_spec(A_WIDTH), col_spec(C_WIDTH), col_spec(C_WIDTH), row_spec(B_WIDTH)],
        out_shape=[sds((B, A_WIDTH, S), F32), sds((B, A_HEADS * PV_ROWS, S), BF16),
                   sds((B, S, A_HEADS * LANES), BF16), sds((B, A_WIDTH, S), BF16),
                   sds((B, C_WIDTH, S), BF16), sds((B, C_WIDTH, S), BF16),
                   sds((B, S, B_WIDTH), BF16)],
        scratch_shapes=[pltpu.VMEM((ROW_COLS_TOTAL, D), BF16),
                        pltpu.VMEM((tm, B_KEY_WIDTH), BF16), pltpu.VMEM((tm, B_KEY_WIDTH), BF16),
                        pltpu.VMEM((tm, B_WIDTH), BF16), pltpu.VMEM((tm, B_WIDTH), BF16),
                        pltpu.VMEM((tm, B_KEY_WIDTH), F32),
                        pltpu.VMEM((B_KEY_WIDTH, B_WIDTH), F32),
                        pltpu.VMEM((B_HEADS, GLA_CHUNK, GLA_CHUNK), F32)],
        compiler_params=pltpu.CompilerParams(
            dimension_semantics=("arbitrary", "arbitrary"), vmem_limit_bytes=VMEM_LIMIT),
        name="proj_gla",
    )(x, nw, w_in, wa, ba, gnw)


def _moba_kernel(rb_ref, qT_ref, k_ref, vT_ref, own_ref, prev_ref, sg_ref, o_ref,
                 kmean_scr, kmbd_scr, qo_scr, qh_scr, m_scr, acc_scr, s_scr):
    i = pl.program_id(1)
    nb = kmean_scr.shape[0]
    blk = MOBA_BLOCK
    dh = A_HEAD_DIM
    n_tiles = A_HEADS // HEADS_PER_TILE

    @pl.when(i == 0)
    def _():
        for n in range(nb):
            kb = jnp.concatenate(
                [k_ref[0, n * blk:(n + 1) * blk, h * LANES:h * LANES + dh] for h in range(A_HEADS)],
                axis=1).astype(F32)
            kmean_scr[n:n + 1, :] = jnp.mean(kb, axis=0, keepdims=True)
        lane_head = jnp.right_shift(lax.broadcasted_iota(jnp.int32, (nb, A_WIDTH), 1),
                                    dh.bit_length() - 1)
        for h in range(A_HEADS):
            km = jnp.where(lane_head == h, kmean_scr[...], 0.0)
            km_hi = km.astype(BF16)
            km_lo = (km - km_hi.astype(F32)).astype(BF16)
            kmbd_scr[h * nb:(h + 1) * nb, :] = jnp.concatenate([km_hi, km_lo, km_hi], axis=1)

    q_f32 = qT_ref[0]
    q_hi = q_f32.astype(BF16)
    q_lo = (q_f32 - q_hi.astype(F32)).astype(BF16)
    gate_all = _dot(kmbd_scr[...], jnp.concatenate([q_hi, q_hi, q_lo], axis=0))
    brow = lax.broadcasted_iota(jnp.int32, (nb, blk), 0)
    browf = brow.astype(F32)
    zeros = jnp.zeros((LANES - dh - 2 * nb, blk), F32)

    for h in range(A_HEADS):
        q_h = qT_ref[0, h * dh:(h + 1) * dh, :] * (dh ** -0.5 * LOG2E)
        qo_scr[h] = jnp.concatenate([q_h, jnp.zeros((LANES - dh, blk), F32)], axis=0).astype(BF16)

    for h in range(A_HEADS):
        g = jnp.where(brow < i, gate_all[h * nb:(h + 1) * nb], NEG)
        sel = brow < 0
        for _ in range(MOBA_TOPK):
            m = jnp.max(g, axis=0, keepdims=True)
            idx = jnp.min(jnp.where(g == m, browf, float(nb)), axis=0, keepdims=True)
            pick = browf == idx
            sel = sel | pick
            g = jnp.where(pick, -jnp.inf, g)
        valid = sel & (brow < i)
        far_bias = rb_ref[REL_BUCKETS - 1, h] * LOG2E
        pen = jnp.where(valid, far_bias, NEG)
        pen_hi = pen.astype(BF16).astype(F32)
        q_h = qT_ref[0, h * dh:(h + 1) * dh, :] * (dh ** -0.5 * LOG2E)
        qh_scr[h] = jnp.concatenate([q_h, pen_hi, pen - pen_hi, zeros], axis=0).astype(BF16)

    half = blk // 2

    def logits(j, h, kind):
        rows = pl.ds(pl.multiple_of(j * blk, blk), blk)
        keys = k_ref[0, rows, h * LANES:(h + 1) * LANES]
        if kind == "own":
            return _dot(keys, qo_scr[h]) + own_ref[h]
        s = _dot(keys, qh_scr[h])
        if kind == "prev":
            near = s[half:, :half] + prev_ref[h]
            s = jnp.concatenate(
                [s[:half], jnp.concatenate([near, s[half:, half:]], axis=1)], axis=0)
        return s

    def fold(j, h, s, first):
        rows = pl.ds(pl.multiple_of(j * blk, blk), blk)
        bm = jnp.max(s, axis=0, keepdims=True)
        vj = vT_ref[0, h * PV_ROWS:(h + 1) * PV_ROWS, rows]
        if first:
            m_scr[h] = bm
            acc_scr[h] = _dot(vj, jnp.exp2(s - bm).astype(BF16))
        else:
            m_old = m_scr[h]
            m_new = jnp.maximum(m_old, bm)
            m_scr[h] = m_new
            acc_scr[h] = (jnp.exp2(m_old - m_new) * acc_scr[h]
                          + _dot(vj, jnp.exp2(s - m_new).astype(BF16)))

    n_far = i - 1
    j_prev = jnp.maximum(i - 1, 0)
    items = ([(i, "own", h) for h in range(A_HEADS)] + [(j_prev, "prev", h) for h in range(A_HEADS)]
             + [(0, "far", h) for h in range(QK_LOOKAHEAD)])
    n_fold = 2 * A_HEADS
    pending = []

    def issue(n):
        j, kind, h = items[n]
        s = logits(j, h, kind)
        if n < n_fold:
            pending.append(s)
        else:
            s_scr[h] = s

    for n in range(QK_LOOKAHEAD):
        issue(n)
    for n in range(n_fold):
        issue(n + QK_LOOKAHEAD)
        j, kind, h = items[n]
        fold(j, h, pending.pop(0), kind == "own")

    def far_group(j0, n_blocks):
        j_after = jnp.minimum(j0 + n_blocks, n_far - 1)
        stream = ([(j0 + t, h) for t in range(n_blocks) for h in range(A_HEADS)]
                  + [(j_after, h) for h in range(QK_LOOKAHEAD)])
        n_items = n_blocks * A_HEADS
        pending = [s_scr[h] for h in range(QK_LOOKAHEAD)]
        for n in range(n_items):
            j, h = stream[n + QK_LOOKAHEAD]
            if n + QK_LOOKAHEAD < n_items:
                pending.append(logits(j, h, "far"))
            else:
                s_scr[h] = logits(j, h, "far")
            j, h = stream[n]
            fold(j, h, pending.pop(0), False)

    def far_groups(g, carry):
        far_group(g * FAR_UNROLL, FAR_UNROLL)
        return carry

    n_groups = jnp.maximum(n_far, 0) // FAR_UNROLL
    lax.fori_loop(0, n_groups, far_groups, 0)
    for rest in range(1, FAR_UNROLL):
        @pl.when(n_far - n_groups * FAR_UNROLL == rest)
        def _():
            far_group(n_groups * FAR_UNROLL, rest)

    for t in range(n_tiles):
        oT = []
        for hh in range(HEADS_PER_TILE):
            acc = acc_scr[t * HEADS_PER_TILE + hh]
            oT.append(acc[:dh] / acc[dh:dh + 1])
        rows = slice(t * LANES, (t + 1) * LANES)
        o_ref[0, rows, :] = (jnp.concatenate(oT, axis=0) * sg_ref[0, rows, :]).astype(BF16)


def _moba(rel_bias, qaT, ka, vaT, bias_own, bias_prev, sga):
    B, S, _ = ka.shape
    nb = S // MOBA_BLOCK
    assert nb == MOBA_MAX_BLOCKS
    bias_spec = pl.BlockSpec((A_HEADS, MOBA_BLOCK, MOBA_BLOCK), lambda b, i: (0, 0, 0))
    corner_spec = pl.BlockSpec((A_HEADS, MOBA_BLOCK // 2, MOBA_BLOCK // 2), lambda b, i: (0, 0, 0))
    return pl.pallas_call(
        _moba_kernel,
        grid=(B, nb),
        in_specs=[
            pl.BlockSpec(memory_space=pltpu.SMEM),
            pl.BlockSpec((1, A_WIDTH, MOBA_BLOCK), lambda b, i: (b, 0, i)),
            pl.BlockSpec((1, S, A_HEADS * LANES), lambda b, i: (b, 0, 0)),
            pl.BlockSpec((1, A_HEADS * PV_ROWS, S), lambda b, i: (b, 0, 0)),
            bias_spec, corner_spec,
            pl.BlockSpec((1, A_WIDTH, MOBA_BLOCK), lambda b, i: (b, 0, i)),
        ],
        out_specs=pl.BlockSpec((1, A_WIDTH, MOBA_BLOCK), lambda b, i: (b, 0, i)),
        out_shape=jax.ShapeDtypeStruct((B, A_WIDTH, S), BF16),
        scratch_shapes=[pltpu.VMEM((nb, A_WIDTH), F32),
                        pltpu.VMEM((A_HEADS * nb, 3 * A_WIDTH), BF16),
                        pltpu.VMEM((A_HEADS, LANES, MOBA_BLOCK), BF16),
                        pltpu.VMEM((A_HEADS, LANES, MOBA_BLOCK), BF16),
                        pltpu.VMEM((A_HEADS, 1, MOBA_BLOCK), F32),
                        pltpu.VMEM((A_HEADS, PV_ROWS, MOBA_BLOCK), F32),
                        pltpu.VMEM((QK_LOOKAHEAD, MOBA_BLOCK, MOBA_BLOCK), F32)],
        compiler_params=pltpu.CompilerParams(
            dimension_semantics=("arbitrary", "arbitrary"),
            vmem_limit_bytes=VMEM_LIMIT),
        name="moba",
    )(rel_bias, qaT, ka, vaT, bias_own, bias_prev, sga)


def _gla_chunk(q_bf, k_bf, v, g, sg, nw, state_scr, att_scr, store_out, fill):
    C = GLA_CHUNK
    dk, dv, H = B_KEY_DIM, B_VAL_DIM, B_HEADS
    n_levels = C.bit_length() - 1

    q = q_bf.astype(F32)
    k = k_bf.astype(F32)

    row = lax.broadcasted_iota(jnp.int32, (C, H * dk), 0)
    tt = lax.broadcasted_iota(jnp.int32, (C, C), 0)
    ss = lax.broadcasted_iota(jnp.int32, (C, C), 1)

    g_hi = g.astype(BF16)
    g_mid = (g - g_hi.astype(F32)).astype(BF16)
    g_lo = (g - g_hi.astype(F32) - g_mid.astype(F32)).astype(BF16)
    tril = jnp.where(ss <= tt, 1.0, 0.0).astype(BF16)
    b3 = _dot(tril, jnp.concatenate([g_hi, g_mid, g_lo], axis=1))
    b = b3[:, :H * dk] + b3[:, H * dk:2 * H * dk] + b3[:, 2 * H * dk:]

    def roll_rows(x, shift):
        if shift % C < SUBLANES:
            x3 = x.reshape(C // SUBLANES, SUBLANES, x.shape[1])
            return pltpu.roll(x3, shift % C, 1).reshape(x.shape)
        if (C - shift % C) < SUBLANES:
            x3 = x.reshape(C // SUBLANES, SUBLANES, x.shape[1])
            return pltpu.roll(x3, SUBLANES - (C - shift % C), 1).reshape(x.shape)
        return pltpu.roll(x, shift, 0)

    txs = tt ^ ss
    level = jnp.full((C, C), -1, jnp.int32)
    for p in range(n_levels):
        level = level + (txs >= (1 << p)).astype(jnp.int32)
    level = jnp.where(ss <= tt, level, -2)

    heads_per_tile = LANES // dk
    lane = lax.broadcasted_iota(jnp.int32, (C, H * dk), 1)
    head_in_tile = jnp.right_shift(lane, dk.bit_length() - 1) & (heads_per_tile - 1)

    def split_heads(kk):
        return [jnp.where(head_in_tile == hh, kk, 0.0).astype(BF16) for hh in range(heads_per_tile)]

    def scores(q_bf, k_split, h):
        t, hh = divmod(h, heads_per_tile)
        cols = slice(t * LANES, (t + 1) * LANES)
        return _dot_nt(q_bf[:, cols], k_split[hh][:, cols])

    on_diag = level == -1
    k_split = split_heads(k)
    for h in range(H):
        att_scr[h] = jnp.where(on_diag, scores(q_bf, k_split, h), 0.0)
    fill()

    block_end = b
    for p in range(n_levels):
        m = 1 << p
        second_half = (row & m) != 0
        r = jnp.where(second_half, roll_rows(block_end, m), block_end)
        decay = jnp.exp2(-jnp.abs(b - r))
        kd_split = split_heads(k * decay)
        if m >= SUBLANES:
            blocks = [slice(lo, lo + m) for lo in range(m, C, 2 * m)]
            take = lambda x: jnp.concatenate([x[rows] for rows in blocks], axis=0)
            qd_bf = (take(q) * take(decay)).astype(BF16)
            for h in range(H):
                s_lvl = scores(qd_bf, kd_split, h)
                for n, rows in enumerate(blocks):
                    keys = slice(rows.start - m, rows.start)
                    att_scr[h, rows, keys] = s_lvl[n * m:(n + 1) * m, keys]
        else:
            qd_bf = (q * decay).astype(BF16)
            at_level = level == p
            for h in range(H):
                att_scr[h] = jnp.where(at_level, scores(qd_bf, kd_split, h), att_scr[h])
        fill()
        if p + 1 < n_levels:
            block_end = jnp.where(second_half, block_end, roll_rows(block_end, C - m))

    state = state_scr[...]
    o_inter = _dot((q * jnp.exp2(b)).astype(BF16), state.astype(BF16))

    for h in range(H):
        cols = slice(h * dv, (h + 1) * dv)
        o_h = o_inter[:, cols] + _dot(att_scr[h].astype(BF16), v[:, cols])
        store_out(cols, (_rmsnorm_rows(o_h, nw) * sg[:, cols]).astype(BF16))

    bT = b.T
    kT = k.T
    b_last = bT[:, C - 1:C]
    kdT = (kT * jnp.exp2(b_last - bT)).astype(BF16)
    ds = _dot(kdT, v)
    srow = lax.broadcasted_iota(jnp.int32, (H * dk, H * dv), 0) // dk
    scol = lax.broadcasted_iota(jnp.int32, (H * dk, H * dv), 1) // dv
    state_scr[...] = jnp.exp2(b_last) * state + jnp.where(srow == scol, ds, 0.0)


def _tn_dot(aT, b):
    return lax.dot_general(aT, b, (((0,), (0,)), ((), ())), preferred_element_type=F32)


def _out_kernel(x_ref, maT_ref, mb_ref, qcT_ref, sgcT_ref, km_ref, vmT_ref, w_ref, fw_ref, o_ref,
                wo_ref):
    @pl.when((pl.program_id(0) == 0) & (pl.program_id(1) == 0))
    def _():
        for lo in range(0, MIX_WIDTH, 4 * LANES):
            wo_ref[lo:lo + 4 * LANES, :] = w_ref[0, lo:lo + 4 * LANES, :].astype(BF16)

    dh = C_HEAD_DIM
    qcT = qcT_ref[0]
    km = km_ref[0]
    vmT = vmT_ref[0]
    heads = [slice(h * dh, (h + 1) * dh) for h in range(C_HEADS)]
    scores = [_dot(km[:, sl], qcT[sl, :]) for sl in heads]
    h_ab = (_tn_dot(maT_ref[0], wo_ref[0:A_WIDTH, :])
            + _dot(mb_ref[0], wo_ref[A_WIDTH:A_WIDTH + B_WIDTH, :]))
    ocT = []
    for s, sl in zip(scores, heads):
        p = jnp.exp(s - jnp.max(s, axis=0, keepdims=True))
        l = jnp.sum(p, axis=0, keepdims=True)
        ocT.append(_dot(vmT[sl, :], p.astype(BF16)) / l)
    mcT = (jnp.concatenate(ocT, axis=0) * sgcT_ref[0]).astype(BF16)
    tm = mcT.shape[1]
    for rows in (slice(0, tm // 2), slice(tm // 2, tm)):
        h_new = x_ref[0, rows, :] + h_ab[rows] + _tn_dot(mcT[:, rows], wo_ref[A_WIDTH + B_WIDTH:, :])
        o_ref[0, rows, :] = _rmsnorm_rows(h_new, fw_ref[...])


def _out(x, ma, mb, qc, sgc, km, vm, w_out, fw):
    B, S, D = x.shape
    tm = OUT_TM
    row = lambda width: pl.BlockSpec((1, tm, width), lambda b, i: (b, i, 0))
    col = lambda width: pl.BlockSpec((1, width, tm), lambda b, i: (b, 0, i))
    return pl.pallas_call(
        _out_kernel,
        grid=(B, S // tm),
        in_specs=[row(D), col(A_WIDTH), row(B_WIDTH), col(C_WIDTH), col(C_WIDTH),
                  pl.BlockSpec((1, MEM_LEN, C_WIDTH), lambda b, i: (b, 0, 0)),
                  pl.BlockSpec((1, C_WIDTH, MEM_LEN), lambda b, i: (b, 0, 0)),
                  pl.BlockSpec(w_out.shape, lambda b, i: (0, 0, 0), pipeline_mode=pl.Buffered(1)),
                  pl.BlockSpec((1, D), lambda b, i: (0, 0))],
        out_specs=row(D),
        out_shape=jax.ShapeDtypeStruct((B, S, D), F32),
        scratch_shapes=[pltpu.VMEM((MIX_WIDTH, D), BF16)],
        compiler_params=pltpu.CompilerParams(
            dimension_semantics=("arbitrary", "arbitrary"), vmem_limit_bytes=VMEM_LIMIT),
        name="out",
    )(x, ma, mb, qc, sgc, km, vm, w_out, fw)


def kernel(x, mem, norm_w, w_in, w_alpha2, b_alpha, gla_norm_w, mem_norm_w, w_mem_kv, w_out,
           rel_bias, final_norm_w):
    assert norm_w.shape[0] == 1, "single layer"
    assert w_in.shape[2] == W_IN_ALIGNED + GLA_RANK + 2 * C_WIDTH

    bias_own, bias_prev, km, vm = _prep(rel_bias, mem, mem_norm_w[0][None, :], w_mem_kv)
    qaT, vaT, ka, sga, qc, sgc, mb = _proj(x, norm_w[0][None, :], w_in[0].T, w_alpha2[0], b_alpha[0][None, :],
                                           gla_norm_w[0][None, :])
    ma = _moba(rel_bias, qaT, ka, vaT, bias_own, bias_prev, sga)
    return _out(x, ma, mb, qc, sgc, km, vm, w_out, final_norm_w[None, :])
```

```python
import functools
import math

import jax
import jax.numpy as jnp
from jax import lax
from jax.experimental import pallas as pl
from jax.experimental.pallas import tpu as pltpu

F32 = jnp.float32
BF16 = jnp.bfloat16

D_MODEL = 1024
MEM_LEN = 256
A_HEADS = 8
A_HEAD_DIM = 64
A_WIDTH = 512
MOBA_BLOCK = 256
MOBA_TOPK = 3
B_HEADS = 4
B_KEY_DIM = 64
B_VAL_DIM = 128
B_KEY_WIDTH = 256
B_WIDTH = 512
GLA_RANK = 16
GLA_TAU = 16.0
C_HEADS = 4
C_HEAD_DIM = 128
C_WIDTH = 512
MIX_WIDTH = A_WIDTH + B_WIDTH + C_WIDTH
REL_BUCKETS = 32
REL_MAX_DIST = 128
RMS_EPS = 1e-6
NEG = -1e30

LANES = 128
SUBLANES = 8
PROJ_TM = 512
OUT_TM = 512
GLA_CHUNK = 256
VMEM_LIMIT = 56 * 1024 * 1024
LOG2E = 1.0 / math.log(2.0)
HEADS_PER_TILE = LANES // A_HEAD_DIM
PV_ROWS = A_HEAD_DIM + 16
MOBA_MAX_BLOCKS = 16
QK_LOOKAHEAD = 5
FAR_UNROLL = 7


def _dot(a, b):
    return jnp.dot(a, b, preferred_element_type=F32)


def _dot_nt(a, b):
    return lax.dot_general(a, b, (((1,), (1,)), ((), ())), preferred_element_type=F32)


def _rmsnorm_rows(x, w):
    return x * lax.rsqrt(jnp.mean(x * x, axis=-1, keepdims=True) + RMS_EPS) * w


def _silu(x):
    return x * jax.nn.sigmoid(x)


def _t5_bucket(n):
    max_exact = REL_BUCKETS // 2
    nf = jnp.maximum(n, max_exact).astype(F32)
    large = max_exact + jnp.floor(jnp.log(nf / max_exact) / math.log(REL_MAX_DIST / max_exact)
                                  * (REL_BUCKETS - max_exact)).astype(jnp.int32)
    large = jnp.minimum(large, REL_BUCKETS - 1)
    return jnp.where(n < max_exact, n, large)


def _prep_kernel(n_batches, rb_ref, mem_ref, mnw_ref, w_ref, own_ref, prev_ref, km_ref, vm_ref):
    h = pl.program_id(0)

    @pl.when(h < n_batches)
    def _():
        u = _rmsnorm_rows(mem_ref[0], mnw_ref[...]).astype(BF16)
        km_ref[0] = _dot(u, w_ref[0, :, :C_WIDTH].astype(BF16)).astype(BF16)
        vm_ref[0] = lax.dot_general(w_ref[0, :, C_WIDTH:].astype(BF16), u, (((0,), (1,)), ((), ())),
                                    preferred_element_type=F32).astype(BF16)

    key = lax.broadcasted_iota(jnp.int32, (MOBA_BLOCK, MOBA_BLOCK), 0)
    qry = lax.broadcasted_iota(jnp.int32, (MOBA_BLOCK, MOBA_BLOCK), 1)
    rel = qry - key
    b_own = _t5_bucket(jnp.maximum(rel, 0))
    half = MOBA_BLOCK // 2
    assert REL_MAX_DIST <= half
    corner = (lax.broadcasted_iota(jnp.int32, (half, half), 1)
              - lax.broadcasted_iota(jnp.int32, (half, half), 0) + half)
    b_prev = _t5_bucket(corner)
    own = jnp.zeros((MOBA_BLOCK, MOBA_BLOCK), F32)
    prev = jnp.zeros((half, half), F32)
    for bk in range(REL_BUCKETS):
        val = rb_ref[bk, h] * LOG2E
        own = jnp.where(b_own == bk, val, own)
        prev = jnp.where(b_prev == bk, val, prev)
    own_ref[0] = jnp.where(rel >= 0, own, NEG)
    prev_ref[0] = prev - rb_ref[REL_BUCKETS - 1, h] * LOG2E


def _prep(rel_bias, mem, mnw, w_mem_kv):
    B = mem.shape[0]
    assert B <= A_HEADS
    batch = lambda h: (jnp.minimum(h, B - 1), 0, 0)
    half = MOBA_BLOCK // 2
    tile = pl.BlockSpec((1, MOBA_BLOCK, MOBA_BLOCK), lambda h: (h, 0, 0))
    corner = pl.BlockSpec((1, half, half), lambda h: (h, 0, 0))
    kv = pl.BlockSpec((1, MEM_LEN, C_WIDTH), batch)
    return pl.pallas_call(
        functools.partial(_prep_kernel, B),
        grid=(A_HEADS,),
        in_specs=[pl.BlockSpec(memory_space=pltpu.SMEM),
                  pl.BlockSpec((1, MEM_LEN, D_MODEL), batch),
                  pl.BlockSpec((1, D_MODEL), lambda h: (0, 0)),
                  pl.BlockSpec(w_mem_kv.shape, lambda h: (0, 0, 0))],
        out_specs=[tile, corner, kv, pl.BlockSpec((1, C_WIDTH, MEM_LEN), batch)],
        out_shape=[jax.ShapeDtypeStruct((A_HEADS, MOBA_BLOCK, MOBA_BLOCK), F32),
                   jax.ShapeDtypeStruct((A_HEADS, half, half), F32),
                   jax.ShapeDtypeStruct((B, MEM_LEN, C_WIDTH), BF16),
                   jax.ShapeDtypeStruct((B, C_WIDTH, MEM_LEN), BF16)],
        compiler_params=pltpu.CompilerParams(dimension_semantics=("arbitrary",)),
        name="prep",
    )(rel_bias, mem, mnw, w_mem_kv)


_ROW_COLS = (("qa", A_WIDTH), ("ka", A_WIDTH), ("va", A_WIDTH), ("ga", A_WIDTH),
             ("qb", B_KEY_WIDTH), ("kb", B_KEY_WIDTH), ("vb", B_WIDTH), ("gb", B_WIDTH),
             ("qc", C_WIDTH), ("gc", C_WIDTH), ("z", B_KEY_WIDTH))
_ROW_OFF = {}
_off = 0
for _name, _width in _ROW_COLS:
    _ROW_OFF[_name] = (_off, _off + _width)
    _off += _width
ROW_COLS_TOTAL = _off


W_IN_ALIGNED = 4 * A_WIDTH + 2 * B_KEY_WIDTH + 2 * B_WIDTH


def _proj_kernel(x_ref, nw_ref, w_ref, wa_ref, ba_ref, gnw_ref,
                 qaT_ref, vaT_ref, ka_ref, ga_ref, qc_ref, gc_ref, mb_ref,
                 wr_ref, qb_ref, kb_ref, vb_ref, gb_ref, g_ref, state_scr, att_scr):
    @pl.when(pl.program_id(1) == 0)
    def _():
        state_scr[...] = jnp.zeros_like(state_scr)

    @pl.when((pl.program_id(0) == 0) & (pl.program_id(1) == 0))
    def _():
        chunk = 4 * LANES
        for lo in range(0, W_IN_ALIGNED, chunk):
            wr_ref[lo:lo + chunk, :] = w_ref[lo:lo + chunk, :].astype(BF16)
        tail = W_IN_ALIGNED + GLA_RANK
        for lo in range(0, 2 * C_WIDTH, chunk):
            wr_ref[W_IN_ALIGNED + lo:W_IN_ALIGNED + lo + chunk, :] = (
                w_ref[tail + lo:tail + lo + chunk, :].astype(BF16))
        z_lo, z_hi = _ROW_OFF["z"]
        wr_ref[z_lo:z_hi, :] = jnp.dot(wa_ref[...].T, w_ref[W_IN_ALIGNED:tail, :],
                                       preferred_element_type=F32,
                                       precision=lax.Precision.HIGHEST).astype(BF16)

    u = _rmsnorm_rows(x_ref[0], nw_ref[...]).astype(BF16)
    tm = u.shape[0]

    def row(name, rows=slice(None)):
        lo, hi = _ROW_OFF[name]
        return _dot_nt(u[rows], wr_ref[lo:hi, :])

    def col(name, rows=slice(None)):
        lo, hi = _ROW_OFF[name]
        return _dot_nt(wr_ref[lo:hi, :], u[rows])

    z = row("z") + ba_ref[...]
    g_ref[...] = (jnp.minimum(z, 0.0) - jnp.log1p(jnp.exp(-jnp.abs(z)))) * (LOG2E / GLA_TAU)
    qb_ref[...] = (row("qb") * (B_KEY_DIM ** -0.5)).astype(BF16)
    kb_ref[...] = row("kb").astype(BF16)
    vb_ref[...] = row("vb").astype(BF16)
    gb_ref[...] = _silu(row("gb")).astype(BF16)

    def emit_qaT(rows):
        qaT_ref[0, :, rows] = col("qa", rows)

    def emit_vaT(rows):
        vaT = col("va", rows).astype(BF16)
        ones = jnp.ones((PV_ROWS - A_HEAD_DIM, vaT.shape[1]), BF16)
        pieces = []
        for h in range(A_HEADS):
            pieces += [vaT[h * A_HEAD_DIM:(h + 1) * A_HEAD_DIM], ones]
        vaT_ref[0, :, rows] = jnp.concatenate(pieces, axis=0)

    def emit_ka(rows):
        ka = row("ka", rows)
        n = ka.shape[0]
        nb = MOBA_MAX_BLOCKS
        key_pos = (pl.program_id(1) * tm + rows.start
                   + lax.broadcasted_iota(jnp.int32, (n, LANES - A_HEAD_DIM), 0))
        blk_id = jnp.right_shift(key_pos, MOBA_BLOCK.bit_length() - 1)
        lane = lax.broadcasted_iota(jnp.int32, (n, LANES - A_HEAD_DIM), 1)
        onehot = jnp.where((lane < 2 * nb) & ((lane & (nb - 1)) == blk_id), 1.0, 0.0)
        pieces = []
        for h in range(A_HEADS):
            pieces += [ka[:, h * A_HEAD_DIM:(h + 1) * A_HEAD_DIM], onehot]
        ka_ref[0, rows, :] = jnp.concatenate(pieces, axis=1).astype(BF16)

    def emit_ga(rows):
        ga_ref[0, :, rows] = _silu(col("ga", rows)).astype(BF16)

    def emit_qc(rows):
        qc_ref[0, :, rows] = (col("qc", rows) * (C_HEAD_DIM ** -0.5)).astype(BF16)

    def emit_gc(rows):
        gc_ref[0, :, rows] = _silu(col("gc", rows)).astype(BF16)

    chunks = [slice(lo, lo + GLA_CHUNK) for lo in range(0, tm, GLA_CHUNK)]
    pending = [functools.partial(emit, rows) for emit in
               (emit_qaT, emit_vaT, emit_ka, emit_ga, emit_qc, emit_gc) for rows in chunks]

    def fill():
        if pending:
            pending.pop(0)()

    nw_gla = gnw_ref[...]
    for rows in chunks:
        def store_out(cols, val, rows=rows):
            mb_ref[0, rows, cols] = val

        _gla_chunk(qb_ref[rows, :], kb_ref[rows, :], vb_ref[rows, :], g_ref[rows, :], gb_ref[rows, :],
                   nw_gla, state_scr, att_scr, store_out, fill)
    while pending:
        fill()


def _proj(x, nw, w_in, wa, ba, gnw):
    B, S, D = x.shape
    tm = PROJ_TM
    row_spec = lambda width: pl.BlockSpec((1, tm, width), lambda b, i: (b, i, 0))
    col_spec = lambda rows: pl.BlockSpec((1, rows, tm), lambda b, i: (b, 0, i))
    const = lambda shape: pl.BlockSpec(shape, lambda b, i: (0,) * len(shape))
    sds = jax.ShapeDtypeStruct
    return pl.pallas_call(
        _proj_kernel,
        grid=(B, S // tm),
        in_specs=[
            pl.BlockSpec((1, tm, D), lambda b, i: (b, i, 0)),
            const((1, D)),
            pl.BlockSpec(w_in.shape, lambda b, i: (0, 0), pipeline_mode=pl.Buffered(1)),
            const((GLA_RANK, B_KEY_WIDTH)),
            const((1, B_KEY_WIDTH)),
            const((1, B_VAL_DIM)),
        ],
        out_specs=[col_spec(A_WIDTH), col_spec(A_HEADS * PV_ROWS), row_spec(A_HEADS * LANES),
                   col_spec(A_WIDTH), col_spec(C_WIDTH), col_spec(C_WIDTH), row_spec(B_WIDTH)],
        out_shape=[sds((B, A_WIDTH, S), F32), sds((B, A_HEADS * PV_ROWS, S), BF16),
                   sds((B, S, A_HEADS * LANES), BF16), sds((B, A_WIDTH, S), BF16),
                   sds((B, C_WIDTH, S), BF16), sds((B, C_WIDTH, S), BF16),
                   sds((B, S, B_WIDTH), BF16)],
        scratch_shapes=[pltpu.VMEM((ROW_COLS_TOTAL, D), BF16),
                        pltpu.VMEM((tm, B_KEY_WIDTH), BF16), pltpu.VMEM((tm, B_KEY_WIDTH), BF16),
                        pltpu.VMEM((tm, B_WIDTH), BF16), pltpu.VMEM((tm, B_WIDTH), BF16),
                        pltpu.VMEM((tm, B_KEY_WIDTH), F32),
                        pltpu.VMEM((B_KEY_WIDTH, B_WIDTH), F32),
                        pltpu.VMEM((B_HEADS, GLA_CHUNK, GLA_CHUNK), F32)],
        compiler_params=pltpu.CompilerParams(
            dimension_semantics=("arbitrary", "arbitrary"), vmem_limit_bytes=VMEM_LIMIT),
        name="proj_gla",
    )(x, nw, w_in, wa, ba, gnw)


def _moba_kernel(rb_ref, qT_ref, k_ref, vT_ref, own_ref, prev_ref, sg_ref, o_ref,
                 kmean_scr, kmbd_scr, qo_scr, qh_scr, m_scr, acc_scr, s_scr):
    i = pl.program_id(1)
    nb = kmean_scr.shape[0]
    blk = MOBA_BLOCK
    dh = A_HEAD_DIM
    n_tiles = A_HEADS // HEADS_PER_TILE

    @pl.when(i == 0)
    def _():
        for n in range(nb):
            kb = jnp.concatenate(
                [k_ref[0, n * blk:(n + 1) * blk, h * LANES:h * LANES + dh] for h in range(A_HEADS)],
                axis=1).astype(F32)
            kmean_scr[n:n + 1, :] = jnp.mean(kb, axis=0, keepdims=True)
        lane_head = jnp.right_shift(lax.broadcasted_iota(jnp.int32, (nb, A_WIDTH), 1),
                                    dh.bit_length() - 1)
        for h in range(A_HEADS):
            km = jnp.where(lane_head == h, kmean_scr[...], 0.0)
            km_hi = km.astype(BF16)
            km_lo = (km - km_hi.astype(F32)).astype(BF16)
            kmbd_scr[h * nb:(h + 1) * nb, :] = jnp.concatenate([km_hi, km_lo, km_hi], axis=1)

    q_f32 = qT_ref[0]
    q_hi = q_f32.astype(BF16)
    q_lo = (q_f32 - q_hi.astype(F32)).astype(BF16)
    gate_all = _dot(kmbd_scr[...], jnp.concatenate([q_hi, q_hi, q_lo], axis=0))
    brow = lax.broadcasted_iota(jnp.int32, (nb, blk), 0)
    browf = brow.astype(F32)
    zeros = jnp.zeros((LANES - dh - 2 * nb, blk), F32)

    for h in range(A_HEADS):
        q_h = qT_ref[0, h * dh:(h + 1) * dh, :] * (dh ** -0.5 * LOG2E)
        qo_scr[h] = jnp.concatenate([q_h, jnp.zeros((LANES - dh, blk), F32)], axis=0).astype(BF16)

    for h in range(A_HEADS):
        g = jnp.where(brow < i, gate_all[h * nb:(h + 1) * nb], NEG)
        sel = brow < 0
        for _ in range(MOBA_TOPK):
            m = jnp.max(g, axis=0, keepdims=True)
            idx = jnp.min(jnp.where(g == m, browf, float(nb)), axis=0, keepdims=True)
            pick = browf == idx
            sel = sel | pick
            g = jnp.where(pick, -jnp.inf, g)
        valid = sel & (brow < i)
        far_bias = rb_ref[REL_BUCKETS - 1, h] * LOG2E
        pen = jnp.where(valid, far_bias, NEG)
        pen_hi = pen.astype(BF16).astype(F32)
        q_h = qT_ref[0, h * dh:(h + 1) * dh, :] * (dh ** -0.5 * LOG2E)
        qh_scr[h] = jnp.concatenate([q_h, pen_hi, pen - pen_hi, zeros], axis=0).astype(BF16)

    half = blk // 2

    def logits(j, h, kind):
        rows = pl.ds(pl.multiple_of(j * blk, blk), blk)
        keys = k_ref[0, rows, h * LANES:(h + 1) * LANES]
        if kind == "own":
            return _dot(keys, qo_scr[h]) + own_ref[h]
        s = _dot(keys, qh_scr[h])
        if kind == "prev":
            near = s[half:, :half] + prev_ref[h]
            s = jnp.concatenate(
                [s[:half], jnp.concatenate([near, s[half:, half:]], axis=1)], axis=0)
        return s

    def fold(j, h, s, first):
        rows = pl.ds(pl.multiple_of(j * blk, blk), blk)
        bm = jnp.max(s, axis=0, keepdims=True)
        vj = vT_ref[0, h * PV_ROWS:(h + 1) * PV_ROWS, rows]
        if first:
            m_scr[h] = bm
            acc_scr[h] = _dot(vj, jnp.exp2(s - bm).astype(BF16))
        else:
            m_old = m_scr[h]
            m_new = jnp.maximum(m_old, bm)
            m_scr[h] = m_new
            acc_scr[h] = (jnp.exp2(m_old - m_new) * acc_scr[h]
                          + _dot(vj, jnp.exp2(s - m_new).astype(BF16)))

    n_far = i - 1
    j_prev = jnp.maximum(i - 1, 0)
    items = ([(i, "own", h) for h in range(A_HEADS)] + [(j_prev, "prev", h) for h in range(A_HEADS)]
             + [(0, "far", h) for h in range(QK_LOOKAHEAD)])
    n_fold = 2 * A_HEADS
    pending = []

    def issue(n):
        j, kind, h = items[n]
        s = logits(j, h, kind)
        if n < n_fold:
            pending.append(s)
        else:
            s_scr[h] = s

    for n in range(QK_LOOKAHEAD):
        issue(n)
    for n in range(n_fold):
        issue(n + QK_LOOKAHEAD)
        j, kind, h = items[n]
        fold(j, h, pending.pop(0), kind == "own")

    def far_group(j0, n_blocks):
        j_after = jnp.minimum(j0 + n_blocks, n_far - 1)
        stream = ([(j0 + t, h) for t in range(n_blocks) for h in range(A_HEADS)]
                  + [(j_after, h) for h in range(QK_LOOKAHEAD)])
        n_items = n_blocks * A_HEADS
        pending = [s_scr[h] for h in range(QK_LOOKAHEAD)]
        for n in range(n_items):
            j, h = stream[n + QK_LOOKAHEAD]
            if n + QK_LOOKAHEAD < n_items:
                pending.append(logits(j, h, "far"))
            else:
                s_scr[h] = logits(j, h, "far")
            j, h = stream[n]
            fold(j, h, pending.pop(0), False)

    def far_groups(g, carry):
        far_group(g * FAR_UNROLL, FAR_UNROLL)
        return carry

    n_groups = jnp.maximum(n_far, 0) // FAR_UNROLL
    lax.fori_loop(0, n_groups, far_groups, 0)
    for rest in range(1, FAR_UNROLL):
        @pl.when(n_far - n_groups * FAR_UNROLL == rest)
        def _():
            far_group(n_groups * FAR_UNROLL, rest)

    for t in range(n_tiles):
        oT = []
        for hh in range(HEADS_PER_TILE):
            acc = acc_scr[t * HEADS_PER_TILE + hh]
            oT.append(acc[:dh] / acc[dh:dh + 1])
        rows = slice(t * LANES, (t + 1) * LANES)
        o_ref[0, rows, :] = (jnp.concatenate(oT, axis=0) * sg_ref[0, rows, :]).astype(BF16)


def _moba(rel_bias, qaT, ka, vaT, bias_own, bias_prev, sga):
    B, S, _ = ka.shape
    nb = S // MOBA_BLOCK
    assert nb == MOBA_MAX_BLOCKS
    bias_spec = pl.BlockSpec((A_HEADS, MOBA_BLOCK, MOBA_BLOCK), lambda b, i: (0, 0, 0))
    corner_spec = pl.BlockSpec((A_HEADS, MOBA_BLOCK // 2, MOBA_BLOCK // 2), lambda b, i: (0, 0, 0))
    return pl.pallas_call(
        _moba_kernel,
        grid=(B, nb),
        in_specs=[
            pl.BlockSpec(memory_space=pltpu.SMEM),
            pl.BlockSpec((1, A_WIDTH, MOBA_BLOCK), lambda b, i: (b, 0, i)),
            pl.BlockSpec((1, S, A_HEADS * LANES), lambda b, i: (b, 0, 0)),
            pl.BlockSpec((1, A_HEADS * PV_ROWS, S), lambda b, i: (b, 0, 0)),
            bias_spec, corner_spec,
            pl.BlockSpec((1, A_WIDTH, MOBA_BLOCK), lambda b, i: (b, 0, i)),
        ],
        out_specs=pl.BlockSpec((1, A_WIDTH, MOBA_BLOCK), lambda b, i: (b, 0, i)),
        out_shape=jax.ShapeDtypeStruct((B, A_WIDTH, S), BF16),
        scratch_shapes=[pltpu.VMEM((nb, A_WIDTH), F32),
                        pltpu.VMEM((A_HEADS * nb, 3 * A_WIDTH), BF16),
                        pltpu.VMEM((A_HEADS, LANES, MOBA_BLOCK), BF16),
                        pltpu.VMEM((A_HEADS, LANES, MOBA_BLOCK), BF16),
                        pltpu.VMEM((A_HEADS, 1, MOBA_BLOCK), F32),
                        pltpu.VMEM((A_HEADS, PV_ROWS, MOBA_BLOCK), F32),
                        pltpu.VMEM((QK_LOOKAHEAD, MOBA_BLOCK, MOBA_BLOCK), F32)],
        compiler_params=pltpu.CompilerParams(
            dimension_semantics=("arbitrary", "arbitrary"),
            vmem_limit_bytes=VMEM_LIMIT),
        name="moba",
    )(rel_bias, qaT, ka, vaT, bias_own, bias_prev, sga)


def _gla_chunk(q_bf, k_bf, v, g, sg, nw, state_scr, att_scr, store_out, fill):
    C = GLA_CHUNK
    dk, dv, H = B_KEY_DIM, B_VAL_DIM, B_HEADS
    n_levels = C.bit_length() - 1

    q = q_bf.astype(F32)
    k = k_bf.astype(F32)

    row = lax.broadcasted_iota(jnp.int32, (C, H * dk), 0)
    tt = lax.broadcasted_iota(jnp.int32, (C, C), 0)
    ss = lax.broadcasted_iota(jnp.int32, (C, C), 1)

    n_tiles = C // SUBLANES
    sub = lax.broadcasted_iota(jnp.int32, (n_tiles, SUBLANES, H * dk), 1)
    scan = g.reshape(n_tiles, SUBLANES, H * dk)
    sh = 1
    while sh < SUBLANES:
        scan = scan + jnp.where(sub >= sh, pltpu.roll(scan, sh, 1), 0.0)
        sh *= 2
    totals = scan[:, SUBLANES - 1, :]
    t_hi = totals.astype(BF16)
    t_mid = (totals - t_hi.astype(F32)).astype(BF16)
    t_lo = (totals - t_hi.astype(F32) - t_mid.astype(F32)).astype(BF16)
    before = (lax.broadcasted_iota(jnp.int32, (n_tiles, n_tiles), 1)
              < lax.broadcasted_iota(jnp.int32, (n_tiles, n_tiles), 0))
    p3 = _dot(jnp.where(before, 1.0, 0.0).astype(BF16),
              jnp.concatenate([t_hi, t_mid, t_lo], axis=1))
    prefix = p3[:, :H * dk] + p3[:, H * dk:2 * H * dk] + p3[:, 2 * H * dk:]
    b = (scan + prefix[:, None, :]).reshape(C, H * dk)

    def roll_rows(x, shift):
        if shift % C < SUBLANES:
            x3 = x.reshape(C // SUBLANES, SUBLANES, x.shape[1])
            return pltpu.roll(x3, shift % C, 1).reshape(x.shape)
        if (C - shift % C) < SUBLANES:
            x3 = x.reshape(C // SUBLANES, SUBLANES, x.shape[1])
            return pltpu.roll(x3, SUBLANES - (C - shift % C), 1).reshape(x.shape)
        return pltpu.roll(x, shift, 0)

    txs = tt ^ ss
    level = jnp.full((C, C), -1, jnp.int32)
    for p in range(n_levels):
        level = level + (txs >= (1 << p)).astype(jnp.int32)
    level = jnp.where(ss <= tt, level, -2)

    heads_per_tile = LANES // dk
    lane = lax.broadcasted_iota(jnp.int32, (C, H * dk), 1)
    head_in_tile = jnp.right_shift(lane, dk.bit_length() - 1) & (heads_per_tile - 1)

    def split_heads(kk):
        return [jnp.where(head_in_tile == hh, kk, 0.0).astype(BF16) for hh in range(heads_per_tile)]

    def scores(q_bf, k_split, h):
        t, hh = divmod(h, heads_per_tile)
        cols = slice(t * LANES, (t + 1) * LANES)
        return _dot_nt(q_bf[:, cols], k_split[hh][:, cols])

    on_diag = level == -1
    k_split = split_heads(k)
    for h in range(H):
        att_scr[h] = jnp.where(on_diag, scores(q_bf, k_split, h), 0.0)
    fill()

    block_end = b
    for p in range(n_levels):
        m = 1 << p
        second_half = (row & m) != 0
        r = jnp.where(second_half, roll_rows(block_end, m), block_end)
        decay = jnp.exp2(-jnp.abs(b - r))
        kd_split = split_heads(k * decay)
        if m >= SUBLANES:
            blocks = [slice(lo, lo + m) for lo in range(m, C, 2 * m)]
            take = lambda x: jnp.concatenate([x[rows] for rows in blocks], axis=0)
            qd_bf = (take(q) * take(decay)).astype(BF16)
            for h in range(H):
                s_lvl = scores(qd_bf, kd_split, h)
                for n, rows in enumerate(blocks):
                    keys = slice(rows.start - m, rows.start)
                    att_scr[h, rows, keys] = s_lvl[n * m:(n + 1) * m, keys]
        else:
            qd_bf = (q * decay).astype(BF16)
            at_level = level == p
            for h in range(H):
                att_scr[h] = jnp.where(at_level, scores(qd_bf, kd_split, h), att_scr[h])
        fill()
        if p + 1 < n_levels:
            block_end = jnp.where(second_half, block_end, roll_rows(block_end, C - m))

    state = state_scr[...]
    o_inter = _dot((q * jnp.exp2(b)).astype(BF16), state.astype(BF16))

    for h in range(H):
        cols = slice(h * dv, (h + 1) * dv)
        o_h = o_inter[:, cols] + _dot(att_scr[h].astype(BF16), v[:, cols])
        store_out(cols, (_rmsnorm_rows(o_h, nw) * sg[:, cols]).astype(BF16))

    bT = b.T
    kT = k.T
    b_last = bT[:, C - 1:C]
    kdT = (kT * jnp.exp2(b_last - bT)).astype(BF16)
    ds = _dot(kdT, v)
    srow = lax.broadcasted_iota(jnp.int32, (H * dk, H * dv), 0) // dk
    scol = lax.broadcasted_iota(jnp.int32, (H * dk, H * dv), 1) // dv
    state_scr[...] = jnp.exp2(b_last) * state + jnp.where(srow == scol, ds, 0.0)


def _tn_dot(aT, b):
    return lax.dot_general(aT, b, (((0,), (0,)), ((), ())), preferred_element_type=F32)


def _out_kernel(x_ref, maT_ref, mb_ref, qcT_ref, sgcT_ref, km_ref, vmT_ref, w_ref, fw_ref, o_ref,
                wo_ref):
    @pl.when((pl.program_id(0) == 0) & (pl.program_id(1) == 0))
    def _():
        for lo in range(0, MIX_WIDTH, 4 * LANES):
            wo_ref[lo:lo + 4 * LANES, :] = w_ref[0, lo:lo + 4 * LANES, :].astype(BF16)

    dh = C_HEAD_DIM
    qcT = qcT_ref[0]
    km = km_ref[0]
    vmT = vmT_ref[0]
    heads = [slice(h * dh, (h + 1) * dh) for h in range(C_HEADS)]
    scores = [_dot(km[:, sl], qcT[sl, :]) for sl in heads]
    h_ab = (_tn_dot(maT_ref[0], wo_ref[0:A_WIDTH, :])
            + _dot(mb_ref[0], wo_ref[A_WIDTH:A_WIDTH + B_WIDTH, :]))
    ocT = []
    for s, sl in zip(scores, heads):
        p = jnp.exp(s - jnp.max(s, axis=0, keepdims=True))
        l = jnp.sum(p, axis=0, keepdims=True)
        ocT.append(_dot(vmT[sl, :], p.astype(BF16)) / l)
    mcT = (jnp.concatenate(ocT, axis=0) * sgcT_ref[0]).astype(BF16)
    tm = mcT.shape[1]
    for rows in (slice(0, tm // 2), slice(tm // 2, tm)):
        h_new = x_ref[0, rows, :] + h_ab[rows] + _tn_dot(mcT[:, rows], wo_ref[A_WIDTH + B_WIDTH:, :])
        o_ref[0, rows, :] = _rmsnorm_rows(h_new, fw_ref[...])


def _out(x, ma, mb, qc, sgc, km, vm, w_out, fw):
    B, S, D = x.shape
    tm = OUT_TM
    row = lambda width: pl.BlockSpec((1, tm, width), lambda b, i: (b, i, 0))
    col = lambda width: pl.BlockSpec((1, width, tm), lambda b, i: (b, 0, i))
    return pl.pallas_call(
        _out_kernel,
        grid=(B, S // tm),
        in_specs=[row(D), col(A_WIDTH), row(B_WIDTH), col(C_WIDTH), col(C_WIDTH),
                  pl.BlockSpec((1, MEM_LEN, C_WIDTH), lambda b, i: (b, 0, 0)),
                  pl.BlockSpec((1, C_WIDTH, MEM_LEN), lambda b, i: (b, 0, 0)),
                  pl.BlockSpec(w_out.shape, lambda b, i: (0, 0, 0), pipeline_mode=pl.Buffered(1)),
                  pl.BlockSpec((1, D), lambda b, i: (0, 0))],
        out_specs=row(D),
        out_shape=jax.ShapeDtypeStruct((B, S, D), F32),
        scratch_shapes=[pltpu.VMEM((MIX_WIDTH, D), BF16)],
        compiler_params=pltpu.CompilerParams(
            dimension_semantics=("arbitrary", "arbitrary"), vmem_limit_bytes=VMEM_LIMIT),
        name="out",
    )(x, ma, mb, qc, sgc, km, vm, w_out, fw)


def kernel(x, mem, norm_w, w_in, w_alpha2, b_alpha, gla_norm_w, mem_norm_w, w_mem_kv, w_out,
           rel_bias, final_norm_w):
    assert norm_w.shape[0] == 1, "single layer"
    assert w_in.shape[2] == W_IN_ALIGNED + GLA_RANK + 2 * C_WIDTH

    bias_own, bias_prev, km, vm = _prep(rel_bias, mem, mem_norm_w[0][None, :], w_mem_kv)
    qaT, vaT, ka, sga, qc, sgc, mb = _proj(x, norm_w[0][None, :], w_in[0].T, w_alpha2[0], b_alpha[0][None, :],
                                           gla_norm_w[0][None, :])
    ma = _moba(rel_bias, qaT, ka, vaT, bias_own, bias_prev, sga)
    return _out(x, ma, mb, qc, sgc, km, vm, w_out, final_norm_w[None, :])
```

```python
import functools
import math

import jax
import jax.numpy as jnp
from jax import lax
from jax.experimental import pallas as pl
from jax.experimental.pallas import tpu as pltpu

F32 = jnp.float32
BF16 = jnp.bfloat16

D_MODEL = 1024
MEM_LEN = 256
A_HEADS = 8
A_HEAD_DIM = 64
A_WIDTH = 512
MOBA_BLOCK = 256
MOBA_TOPK = 3
B_HEADS = 4
B_KEY_DIM = 64
B_VAL_DIM = 128
B_KEY_WIDTH = 256
B_WIDTH = 512
GLA_RANK = 16
GLA_TAU = 16.0
C_HEADS = 4
C_HEAD_DIM = 128
C_WIDTH = 512
MIX_WIDTH = A_WIDTH + B_WIDTH + C_WIDTH
REL_BUCKETS = 32
REL_MAX_DIST = 128
RMS_EPS = 1e-6
NEG = -1e30

LANES = 128
SUBLANES = 8
BF16_SUBLANES = 16
PROJ_TM = 512
OUT_TM = 512
GLA_CHUNK = 256
CAST_ROWS = 512
VMEM_LIMIT = 56 * 1024 * 1024
LOG2E = 1.0 / math.log(2.0)
HEADS_PER_TILE = LANES // A_HEAD_DIM
PV_ROWS = A_HEAD_DIM + BF16_SUBLANES
MOBA_MAX_BLOCKS = 16
QK_LOOKAHEAD = 5
FAR_UNROLL = 7


def _dot(a, b):
    return jnp.dot(a, b, preferred_element_type=F32)


def _dot_nt(a, b):
    return lax.dot_general(a, b, (((1,), (1,)), ((), ())), preferred_element_type=F32)


def _rmsnorm_rows(x, w):
    return x * lax.rsqrt(jnp.mean(x * x, axis=-1, keepdims=True) + RMS_EPS) * w


def _silu(x):
    return x * jax.nn.sigmoid(x)


def _t5_bucket(n):
    max_exact = REL_BUCKETS // 2
    nf = jnp.maximum(n, max_exact).astype(F32)
    large = max_exact + jnp.floor(jnp.log(nf / max_exact) / math.log(REL_MAX_DIST / max_exact)
                                  * (REL_BUCKETS - max_exact)).astype(jnp.int32)
    large = jnp.minimum(large, REL_BUCKETS - 1)
    return jnp.where(n < max_exact, n, large)


def _prep_kernel(n_batches, rb_ref, mem_ref, mnw_ref, w_ref, own_ref, prev_ref, km_ref, vm_ref):
    h = pl.program_id(0)

    @pl.when(h < n_batches)
    def _():
        u = _rmsnorm_rows(mem_ref[0], mnw_ref[...]).astype(BF16)
        km_ref[0] = _dot(u, w_ref[0, :, :C_WIDTH].astype(BF16)).astype(BF16)
        vm_ref[0] = lax.dot_general(w_ref[0, :, C_WIDTH:].astype(BF16), u, (((0,), (1,)), ((), ())),
                                    preferred_element_type=F32).astype(BF16)

    half = MOBA_BLOCK // 2
    key = lax.broadcasted_iota(jnp.int32, (half, MOBA_BLOCK), 0)
    qry = lax.broadcasted_iota(jnp.int32, (half, MOBA_BLOCK), 1)
    rel = qry - key
    b_own = _t5_bucket(jnp.maximum(rel, 0))
    assert REL_MAX_DIST <= half
    corner = (lax.broadcasted_iota(jnp.int32, (half, half), 1)
              - lax.broadcasted_iota(jnp.int32, (half, half), 0) + half)
    b_prev = _t5_bucket(corner)
    own = jnp.zeros((half, MOBA_BLOCK), F32)
    prev = jnp.zeros((half, half), F32)
    for bk in range(REL_BUCKETS):
        val = rb_ref[bk, h] * LOG2E
        own = jnp.where(b_own == bk, val, own)
        prev = jnp.where(b_prev == bk, val, prev)
    own = jnp.where(rel >= 0, own, NEG)
    own_ref[0, :half, :] = own
    own_ref[0, half:, :half] = jnp.full((half, half), NEG, F32)
    own_ref[0, half:, half:] = own[:, :half]
    prev_ref[0] = prev - rb_ref[REL_BUCKETS - 1, h] * LOG2E


def _prep(rel_bias, mem, mnw, w_mem_kv):
    B = mem.shape[0]
    assert B <= A_HEADS
    batch = lambda h: (jnp.minimum(h, B - 1), 0, 0)
    half = MOBA_BLOCK // 2
    tile = pl.BlockSpec((1, MOBA_BLOCK, MOBA_BLOCK), lambda h: (h, 0, 0))
    corner = pl.BlockSpec((1, half, half), lambda h: (h, 0, 0))
    kv = pl.BlockSpec((1, MEM_LEN, C_WIDTH), batch)
    return pl.pallas_call(
        functools.partial(_prep_kernel, B),
        grid=(A_HEADS,),
        in_specs=[pl.BlockSpec(memory_space=pltpu.SMEM),
                  pl.BlockSpec((1, MEM_LEN, D_MODEL), batch),
                  pl.BlockSpec((1, D_MODEL), lambda h: (0, 0)),
                  pl.BlockSpec(w_mem_kv.shape, lambda h: (0, 0, 0))],
        out_specs=[tile, corner, kv, pl.BlockSpec((1, C_WIDTH, MEM_LEN), batch)],
        out_shape=[jax.ShapeDtypeStruct((A_HEADS, MOBA_BLOCK, MOBA_BLOCK), F32),
                   jax.ShapeDtypeStruct((A_HEADS, half, half), F32),
                   jax.ShapeDtypeStruct((B, MEM_LEN, C_WIDTH), BF16),
                   jax.ShapeDtypeStruct((B, C_WIDTH, MEM_LEN), BF16)],
        compiler_params=pltpu.CompilerParams(dimension_semantics=("arbitrary",)),
        name="prep",
    )(rel_bias, mem, mnw, w_mem_kv)


_ROW_COLS = (("qa", A_WIDTH), ("ka", A_WIDTH), ("va", A_WIDTH), ("ga", A_WIDTH),
             ("qb", B_KEY_WIDTH), ("kb", B_KEY_WIDTH), ("vb", B_WIDTH), ("gb", B_WIDTH),
             ("qc", C_WIDTH), ("gc", C_WIDTH), ("z", B_KEY_WIDTH))
_ROW_OFF = {}
_off = 0
for _name, _width in _ROW_COLS:
    _ROW_OFF[_name] = (_off, _off + _width)
    _off += _width
ROW_COLS_TOTAL = _off


W_IN_ALIGNED = 4 * A_WIDTH + 2 * B_KEY_WIDTH + 2 * B_WIDTH


def _proj_kernel(x_ref, nw_ref, w_ref, wa_ref, ba_ref, gnw_ref,
                 qaT_ref, vaT_ref, ka_ref, ga_ref, qc_ref, gc_ref, mb_ref,
                 wr_ref, qb_ref, kb_ref, vb_ref, gb_ref, g_ref, state_scr, att_scr):
    @pl.when(pl.program_id(1) == 0)
    def _():
        state_scr[...] = jnp.zeros_like(state_scr)

    @pl.when((pl.program_id(0) == 0) & (pl.program_id(1) == 0))
    def _():
        chunk = CAST_ROWS
        for lo in range(0, W_IN_ALIGNED, chunk):
            wr_ref[lo:lo + chunk, :] = w_ref[lo:lo + chunk, :].astype(BF16)
        tail = W_IN_ALIGNED + GLA_RANK
        for lo in range(0, 2 * C_WIDTH, chunk):
            wr_ref[W_IN_ALIGNED + lo:W_IN_ALIGNED + lo + chunk, :] = (
                w_ref[tail + lo:tail + lo + chunk, :].astype(BF16))
        z_lo, z_hi = _ROW_OFF["z"]
        wr_ref[z_lo:z_hi, :] = jnp.dot(wa_ref[...].T, w_ref[W_IN_ALIGNED:tail, :],
                                       preferred_element_type=F32,
                                       precision=lax.Precision.HIGHEST).astype(BF16)

    u = _rmsnorm_rows(x_ref[0], nw_ref[...]).astype(BF16)
    tm = u.shape[0]

    def row(name, rows=slice(None)):
        lo, hi = _ROW_OFF[name]
        return _dot_nt(u[rows], wr_ref[lo:hi, :])

    def col(name, rows=slice(None)):
        lo, hi = _ROW_OFF[name]
        return _dot_nt(wr_ref[lo:hi, :], u[rows])

    z = row("z") + ba_ref[...]
    g_ref[...] = (jnp.minimum(z, 0.0) - jnp.log1p(jnp.exp(-jnp.abs(z)))) * (LOG2E / GLA_TAU)
    qb_ref[...] = (row("qb") * (B_KEY_DIM ** -0.5)).astype(BF16)
    kb_ref[...] = row("kb").astype(BF16)
    vb_ref[...] = row("vb").astype(BF16)
    gb_ref[...] = _silu(row("gb")).astype(BF16)

    def emit_qaT(rows):
        qaT_ref[0, :, rows] = col("qa", rows)

    def emit_vaT(rows):
        vaT = col("va", rows).astype(BF16)
        ones = jnp.ones((PV_ROWS - A_HEAD_DIM, vaT.shape[1]), BF16)
        pieces = []
        for h in range(A_HEADS):
            pieces += [vaT[h * A_HEAD_DIM:(h + 1) * A_HEAD_DIM], ones]
        vaT_ref[0, :, rows] = jnp.concatenate(pieces, axis=0)

    def emit_ka(rows):
        ka = row("ka", rows)
        n = ka.shape[0]
        nb = MOBA_MAX_BLOCKS
        key_pos = (pl.program_id(1) * tm + rows.start
                   + lax.broadcasted_iota(jnp.int32, (n, LANES - A_HEAD_DIM), 0))
        blk_id = jnp.right_shift(key_pos, MOBA_BLOCK.bit_length() - 1)
        lane = lax.broadcasted_iota(jnp.int32, (n, LANES - A_HEAD_DIM), 1)
        onehot = jnp.where((lane < 2 * nb) & ((lane & (nb - 1)) == blk_id), 1.0, 0.0)
        pieces = []
        for h in range(A_HEADS):
            pieces += [ka[:, h * A_HEAD_DIM:(h + 1) * A_HEAD_DIM], onehot]
        ka_ref[0, rows, :] = jnp.concatenate(pieces, axis=1).astype(BF16)

    def emit_ga(rows):
        ga_ref[0, :, rows] = _silu(col("ga", rows)).astype(BF16)

    def emit_qc(rows):
        qc_ref[0, :, rows] = (col("qc", rows) * (C_HEAD_DIM ** -0.5)).astype(BF16)

    def emit_gc(rows):
        gc_ref[0, :, rows] = _silu(col("gc", rows)).astype(BF16)

    chunks = [slice(lo, lo + GLA_CHUNK) for lo in range(0, tm, GLA_CHUNK)]
    pending = [functools.partial(emit, rows) for emit in
               (emit_qaT, emit_vaT, emit_ka, emit_ga, emit_qc, emit_gc) for rows in chunks]

    def fill():
        if pending:
            pending.pop(0)()

    nw_gla = gnw_ref[...]
    for rows in chunks:
        def store_out(cols, val, rows=rows):
            mb_ref[0, rows, cols] = val

        _gla_chunk(qb_ref[rows, :], kb_ref[rows, :], vb_ref[rows, :], g_ref[rows, :], gb_ref[rows, :],
                   nw_gla, state_scr, att_scr, store_out, fill)
    while pending:
        fill()


def _proj(x, nw, w_in, wa, ba, gnw):
    B, S, D = x.shape
    tm = PROJ_TM
    row_spec = lambda width: pl.BlockSpec((1, tm, width), lambda b, i: (b, i, 0))
    col_spec = lambda rows: pl.BlockSpec((1, rows, tm), lambda b, i: (b, 0, i))
    const = lambda shape: pl.BlockSpec(shape, lambda b, i: (0,) * len(shape))
    sds = jax.ShapeDtypeStruct
    return pl.pallas_call(
        _proj_kernel,
        grid=(B, S // tm),
        in_specs=[
            pl.BlockSpec((1, tm, D), lambda b, i: (b, i, 0)),
            const((1, D)),
            pl.BlockSpec(w_in.shape, lambda b, i: (0, 0), pipeline_mode=pl.Buffered(1)),
            const((GLA_RANK, B_KEY_WIDTH)),
            const((1, B_KEY_WIDTH)),
            const((1, B_VAL_DIM)),
        ],
        out_specs=[col_spec(A_WIDTH), col_spec(A_HEADS * PV_ROWS), row_spec(A_HEADS * LANES),
                   col_spec(A_WIDTH), col_spec(C_WIDTH), col_spec(C_WIDTH), row_spec(B_WIDTH)],
        out_shape=[sds((B, A_WIDTH, S), F32), sds((B, A_HEADS * PV_ROWS, S), BF16),
                   sds((B, S, A_HEADS * LANES), BF16), sds((B, A_WIDTH, S), BF16),
                   sds((B, C_WIDTH, S), BF16), sds((B, C_WIDTH, S), BF16),
                   sds((B, S, B_WIDTH), BF16)],
        scratch_shapes=[pltpu.VMEM((ROW_COLS_TOTAL, D), BF16),
                        pltpu.VMEM((tm, B_KEY_WIDTH), BF16), pltpu.VMEM((tm, B_KEY_WIDTH), BF16),
                        pltpu.VMEM((tm, B_WIDTH), BF16), pltpu.VMEM((tm, B_WIDTH), BF16),
                        pltpu.VMEM((tm, B_KEY_WIDTH), F32),
                        pltpu.VMEM((B_KEY_WIDTH, B_WIDTH), F32),
                        pltpu.VMEM((B_HEADS, GLA_CHUNK, GLA_CHUNK), F32)],
        compiler_params=pltpu.CompilerParams(
            dimension_semantics=("arbitrary", "arbitrary"), vmem_limit_bytes=VMEM_LIMIT),
        name="proj_gla",
    )(x, nw, w_in, wa, ba, gnw)


def _moba_kernel(rb_ref, qT_ref, k_ref, vT_ref, own_ref, prev_ref, sg_ref, o_ref,
                 kmean_scr, kmbd_scr, qo_scr, qh_scr, m_scr, acc_scr, s_scr):
    i = pl.program_id(1)
    nb = kmean_scr.shape[0]
    blk = MOBA_BLOCK
    dh = A_HEAD_DIM
    n_tiles = A_HEADS // HEADS_PER_TILE

    @pl.when(i == 0)
    def _():
        for n in range(nb):
            kb = jnp.concatenate(
                [k_ref[0, n * blk:(n + 1) * blk, h * LANES:h * LANES + dh] for h in range(A_HEADS)],
                axis=1).astype(F32)
            kmean_scr[n:n + 1, :] = jnp.mean(kb, axis=0, keepdims=True)
        lane_head = jnp.right_shift(lax.broadcasted_iota(jnp.int32, (nb, A_WIDTH), 1),
                                    dh.bit_length() - 1)
        for h in range(A_HEADS):
            km = jnp.where(lane_head == h, kmean_scr[...], 0.0)
            km_hi = km.astype(BF16)
            km_lo = (km - km_hi.astype(F32)).astype(BF16)
            kmbd_scr[h * nb:(h + 1) * nb, :] = jnp.concatenate([km_hi, km_lo, km_hi], axis=1)

    q_f32 = qT_ref[0]
    q_hi = q_f32.astype(BF16)
    q_lo = (q_f32 - q_hi.astype(F32)).astype(BF16)
    gate_all = _dot(kmbd_scr[...], jnp.concatenate([q_hi, q_hi, q_lo], axis=0))
    brow = lax.broadcasted_iota(jnp.int32, (nb, blk), 0)
    browf = brow.astype(F32)
    zeros = jnp.zeros((LANES - dh - 2 * nb, blk), F32)

    for h in range(A_HEADS):
        q_h = qT_ref[0, h * dh:(h + 1) * dh, :] * (dh ** -0.5 * LOG2E)
        qo_scr[h] = jnp.concatenate([q_h, jnp.zeros((LANES - dh, blk), F32)], axis=0).astype(BF16)

    for h in range(A_HEADS):
        g = jnp.where(brow < i, gate_all[h * nb:(h + 1) * nb], NEG)
        sel = brow < 0
        for _ in range(MOBA_TOPK):
            m = jnp.max(g, axis=0, keepdims=True)
            idx = jnp.min(jnp.where(g == m, browf, float(nb)), axis=0, keepdims=True)
            pick = browf == idx
            sel = sel | pick
            g = jnp.where(pick, -jnp.inf, g)
        valid = sel & (brow < i)
        far_bias = rb_ref[REL_BUCKETS - 1, h] * LOG2E
        pen = jnp.where(valid, far_bias, NEG)
        pen_hi = pen.astype(BF16).astype(F32)
        q_h = qT_ref[0, h * dh:(h + 1) * dh, :] * (dh ** -0.5 * LOG2E)
        qh_scr[h] = jnp.concatenate([q_h, pen_hi, pen - pen_hi, zeros], axis=0).astype(BF16)

    half = blk // 2

    def logits(j, h, kind):
        rows = pl.ds(pl.multiple_of(j * blk, blk), blk)
        keys = k_ref[0, rows, h * LANES:(h + 1) * LANES]
        if kind == "own":
            return _dot(keys, qo_scr[h]) + own_ref[h]
        s = _dot(keys, qh_scr[h])
        if kind == "prev":
            near = s[half:, :half] + prev_ref[h]
            s = jnp.concatenate(
                [s[:half], jnp.concatenate([near, s[half:, half:]], axis=1)], axis=0)
        return s

    def fold(j, h, s, first):
        rows = pl.ds(pl.multiple_of(j * blk, blk), blk)
        bm = jnp.max(s, axis=0, keepdims=True)
        vj = vT_ref[0, h * PV_ROWS:(h + 1) * PV_ROWS, rows]
        if first:
            m_scr[h] = bm
            acc_scr[h] = _dot(vj, jnp.exp2(s - bm).astype(BF16))
        else:
            m_old = m_scr[h]
            m_new = jnp.maximum(m_old, bm)
            m_scr[h] = m_new
            acc_scr[h] = (jnp.exp2(m_old - m_new) * acc_scr[h]
                          + _dot(vj, jnp.exp2(s - m_new).astype(BF16)))

    n_far = i - 1
    j_prev = jnp.maximum(i - 1, 0)
    items = ([(i, "own", h) for h in range(A_HEADS)] + [(j_prev, "prev", h) for h in range(A_HEADS)]
             + [(0, "far", h) for h in range(QK_LOOKAHEAD)])
    n_fold = 2 * A_HEADS
    pending = []

    def issue(n):
        j, kind, h = items[n]
        s = logits(j, h, kind)
        if n < n_fold:
            pending.append(s)
        else:
            s_scr[h] = s

    for n in range(QK_LOOKAHEAD):
        issue(n)
    for n in range(n_fold):
        issue(n + QK_LOOKAHEAD)
        j, kind, h = items[n]
        fold(j, h, pending.pop(0), kind == "own")

    def far_group(j0, n_blocks):
        j_after = jnp.minimum(j0 + n_blocks, n_far - 1)
        stream = ([(j0 + t, h) for t in range(n_blocks) for h in range(A_HEADS)]
                  + [(j_after, h) for h in range(QK_LOOKAHEAD)])
        n_items = n_blocks * A_HEADS
        pending = [s_scr[h] for h in range(QK_LOOKAHEAD)]
        for n in range(n_items):
            j, h = stream[n + QK_LOOKAHEAD]
            if n + QK_LOOKAHEAD < n_items:
                pending.append(logits(j, h, "far"))
            else:
                s_scr[h] = logits(j, h, "far")
            j, h = stream[n]
            fold(j, h, pending.pop(0), False)

    def far_groups(g, carry):
        far_group(g * FAR_UNROLL, FAR_UNROLL)
        return carry

    n_groups = jnp.maximum(n_far, 0) // FAR_UNROLL
    lax.fori_loop(0, n_groups, far_groups, 0)
    for rest in range(1, FAR_UNROLL):
        @pl.when(n_far - n_groups * FAR_UNROLL == rest)
        def _():
            far_group(n_groups * FAR_UNROLL, rest)

    for t in range(n_tiles):
        oT = []
        for hh in range(HEADS_PER_TILE):
            acc = acc_scr[t * HEADS_PER_TILE + hh]
            oT.append(acc[:dh] / acc[dh:dh + 1])
        rows = slice(t * LANES, (t + 1) * LANES)
        o_ref[0, rows, :] = (jnp.concatenate(oT, axis=0) * sg_ref[0, rows, :]).astype(BF16)


def _moba(rel_bias, qaT, ka, vaT, bias_own, bias_prev, sga):
    B, S, _ = ka.shape
    nb = S // MOBA_BLOCK
    assert nb == MOBA_MAX_BLOCKS
    bias_spec = pl.BlockSpec((A_HEADS, MOBA_BLOCK, MOBA_BLOCK), lambda b, i: (0, 0, 0))
    corner_spec = pl.BlockSpec((A_HEADS, MOBA_BLOCK // 2, MOBA_BLOCK // 2), lambda b, i: (0, 0, 0))
    return pl.pallas_call(
        _moba_kernel,
        grid=(B, nb),
        in_specs=[
            pl.BlockSpec(memory_space=pltpu.SMEM),
            pl.BlockSpec((1, A_WIDTH, MOBA_BLOCK), lambda b, i: (b, 0, i)),
            pl.BlockSpec((1, S, A_HEADS * LANES), lambda b, i: (b, 0, 0)),
            pl.BlockSpec((1, A_HEADS * PV_ROWS, S), lambda b, i: (b, 0, 0)),
            bias_spec, corner_spec,
            pl.BlockSpec((1, A_WIDTH, MOBA_BLOCK), lambda b, i: (b, 0, i)),
        ],
        out_specs=pl.BlockSpec((1, A_WIDTH, MOBA_BLOCK), lambda b, i: (b, 0, i)),
        out_shape=jax.ShapeDtypeStruct((B, A_WIDTH, S), BF16),
        scratch_shapes=[pltpu.VMEM((nb, A_WIDTH), F32),
                        pltpu.VMEM((A_HEADS * nb, 3 * A_WIDTH), BF16),
                        pltpu.VMEM((A_HEADS, LANES, MOBA_BLOCK), BF16),
                        pltpu.VMEM((A_HEADS, LANES, MOBA_BLOCK), BF16),
                        pltpu.VMEM((A_HEADS, 1, MOBA_BLOCK), F32),
                        pltpu.VMEM((A_HEADS, PV_ROWS, MOBA_BLOCK), F32),
                        pltpu.VMEM((QK_LOOKAHEAD, MOBA_BLOCK, MOBA_BLOCK), F32)],
        compiler_params=pltpu.CompilerParams(
            dimension_semantics=("arbitrary", "arbitrary"),
            vmem_limit_bytes=VMEM_LIMIT),
        name="moba",
    )(rel_bias, qaT, ka, vaT, bias_own, bias_prev, sga)


def _gla_chunk(q_bf, k_bf, v, g, sg, nw, state_scr, att_scr, store_out, fill):
    C = GLA_CHUNK
    dk, dv, H = B_KEY_DIM, B_VAL_DIM, B_HEADS
    n_levels = C.bit_length() - 1

    q = q_bf.astype(F32)
    k = k_bf.astype(F32)

    row = lax.broadcasted_iota(jnp.int32, (C, H * dk), 0)
    tt = lax.broadcasted_iota(jnp.int32, (C, C), 0)
    ss = lax.broadcasted_iota(jnp.int32, (C, C), 1)

    n_tiles = C // SUBLANES
    sub = lax.broadcasted_iota(jnp.int32, (n_tiles, SUBLANES, H * dk), 1)
    scan = g.reshape(n_tiles, SUBLANES, H * dk)
    sh = 1
    while sh < SUBLANES:
        scan = scan + jnp.where(sub >= sh, pltpu.roll(scan, sh, 1), 0.0)
        sh *= 2
    totals = scan[:, SUBLANES - 1, :]
    t_hi = totals.astype(BF16)
    t_mid = (totals - t_hi.astype(F32)).astype(BF16)
    t_lo = (totals - t_hi.astype(F32) - t_mid.astype(F32)).astype(BF16)
    before = (lax.broadcasted_iota(jnp.int32, (n_tiles, n_tiles), 1)
              < lax.broadcasted_iota(jnp.int32, (n_tiles, n_tiles), 0))
    p3 = _dot(jnp.where(before, 1.0, 0.0).astype(BF16),
              jnp.concatenate([t_hi, t_mid, t_lo], axis=1))
    prefix = p3[:, :H * dk] + p3[:, H * dk:2 * H * dk] + p3[:, 2 * H * dk:]
    b = (scan + prefix[:, None, :]).reshape(C, H * dk)

    def roll_rows(x, shift):
        if shift % C < SUBLANES:
            x3 = x.reshape(C // SUBLANES, SUBLANES, x.shape[1])
            return pltpu.roll(x3, shift % C, 1).reshape(x.shape)
        if (C - shift % C) < SUBLANES:
            x3 = x.reshape(C // SUBLANES, SUBLANES, x.shape[1])
            return pltpu.roll(x3, SUBLANES - (C - shift % C), 1).reshape(x.shape)
        return pltpu.roll(x, shift, 0)

    txs = tt ^ ss
    level = jnp.full((C, C), -1, jnp.int32)
    for p in range(n_levels):
        level = level + (txs >= (1 << p)).astype(jnp.int32)
    level = jnp.where(ss <= tt, level, -2)

    heads_per_tile = LANES // dk
    lane = lax.broadcasted_iota(jnp.int32, (C, H * dk), 1)
    head_in_tile = jnp.right_shift(lane, dk.bit_length() - 1) & (heads_per_tile - 1)

    def split_heads(kk):
        return [jnp.where(head_in_tile == hh, kk, 0.0).astype(BF16) for hh in range(heads_per_tile)]

    def scores(q_bf, k_split, h):
        t, hh = divmod(h, heads_per_tile)
        cols = slice(t * LANES, (t + 1) * LANES)
        return _dot_nt(q_bf[:, cols], k_split[hh][:, cols])

    on_diag = level == -1
    k_split = split_heads(k)
    for h in range(H):
        att_scr[h] = jnp.where(on_diag, scores(q_bf, k_split, h), 0.0)
    fill()

    block_end = b
    for p in range(n_levels):
        m = 1 << p
        second_half = (row & m) != 0
        r = jnp.where(second_half, roll_rows(block_end, m), block_end)
        decay = jnp.exp2(-jnp.abs(b - r))
        kd_split = split_heads(k * decay)
        if m >= SUBLANES:
            blocks = [slice(lo, lo + m) for lo in range(m, C, 2 * m)]
            take = lambda x: jnp.concatenate([x[rows] for rows in blocks], axis=0)
            qd_bf = (take(q) * take(decay)).astype(BF16)
            for h in range(H):
                s_lvl = scores(qd_bf, kd_split, h)
                for n, rows in enumerate(blocks):
                    keys = slice(rows.start - m, rows.start)
                    att_scr[h, rows, keys] = s_lvl[n * m:(n + 1) * m, keys]
        else:
            qd_bf = (q * decay).astype(BF16)
            at_level = level == p
            for h in range(H):
                att_scr[h] = jnp.where(at_level, scores(qd_bf, kd_split, h), att_scr[h])
        fill()
        if p + 1 < n_levels:
            block_end = jnp.where(second_half, block_end, roll_rows(block_end, C - m))

    state = state_scr[...]
    o_inter = _dot((q * jnp.exp2(b)).astype(BF16), state.astype(BF16))

    for h in range(H):
        cols = slice(h * dv, (h + 1) * dv)
        o_h = o_inter[:, cols] + _dot(att_scr[h].astype(BF16), v[:, cols])
        store_out(cols, (_rmsnorm_rows(o_h, nw) * sg[:, cols]).astype(BF16))

    bT = b.T
    kT = k.T
    b_last = bT[:, C - 1:C]
    kdT = (kT * jnp.exp2(b_last - bT)).astype(BF16)
    ds = _dot(kdT, v)
    srow = lax.broadcasted_iota(jnp.int32, (H * dk, H * dv), 0) // dk
    scol = lax.broadcasted_iota(jnp.int32, (H * dk, H * dv), 1) // dv
    state_scr[...] = jnp.exp2(b_last) * state + jnp.where(srow == scol, ds, 0.0)


def _tn_dot(aT, b):
    return lax.dot_general(aT, b, (((0,), (0,)), ((), ())), preferred_element_type=F32)


def _out_kernel(x_ref, maT_ref, mb_ref, qcT_ref, sgcT_ref, km_ref, vmT_ref, w_ref, fw_ref, o_ref,
                wo_ref):
    @pl.when((pl.program_id(0) == 0) & (pl.program_id(1) == 0))
    def _():
        for lo in range(0, MIX_WIDTH, CAST_ROWS):
            wo_ref[lo:lo + CAST_ROWS, :] = w_ref[0, lo:lo + CAST_ROWS, :].astype(BF16)

    dh = C_HEAD_DIM
    qcT = qcT_ref[0]
    km = km_ref[0]
    vmT = vmT_ref[0]
    heads = [slice(h * dh, (h + 1) * dh) for h in range(C_HEADS)]
    scores = [_dot(km[:, sl], qcT[sl, :]) for sl in heads]
    h_ab = (_tn_dot(maT_ref[0], wo_ref[0:A_WIDTH, :])
            + _dot(mb_ref[0], wo_ref[A_WIDTH:A_WIDTH + B_WIDTH, :]))
    ocT = []
    for s, sl in zip(scores, heads):
        p = jnp.exp(s - jnp.max(s, axis=0, keepdims=True))
        l = jnp.sum(p, axis=0, keepdims=True)
        ocT.append(_dot(vmT[sl, :], p.astype(BF16)) / l)
    mcT = (jnp.concatenate(ocT, axis=0) * sgcT_ref[0]).astype(BF16)
    tm = mcT.shape[1]
    for rows in (slice(0, tm // 2), slice(tm // 2, tm)):
        h_new = x_ref[0, rows, :] + h_ab[rows] + _tn_dot(mcT[:, rows], wo_ref[A_WIDTH + B_WIDTH:, :])
        o_ref[0, rows, :] = _rmsnorm_rows(h_new, fw_ref[...])


def _out(x, ma, mb, qc, sgc, km, vm, w_out, fw):
    B, S, D = x.shape
    tm = OUT_TM
    row = lambda width: pl.BlockSpec((1, tm, width), lambda b, i: (b, i, 0))
    col = lambda width: pl.BlockSpec((1, width, tm), lambda b, i: (b, 0, i))
    return pl.pallas_call(
        _out_kernel,
        grid=(B, S // tm),
        in_specs=[row(D), col(A_WIDTH), row(B_WIDTH), col(C_WIDTH), col(C_WIDTH),
                  pl.BlockSpec((1, MEM_LEN, C_WIDTH), lambda b, i: (b, 0, 0)),
                  pl.BlockSpec((1, C_WIDTH, MEM_LEN), lambda b, i: (b, 0, 0)),
                  pl.BlockSpec(w_out.shape, lambda b, i: (0, 0, 0), pipeline_mode=pl.Buffered(1)),
                  pl.BlockSpec((1, D), lambda b, i: (0, 0))],
        out_specs=row(D),
        out_shape=jax.ShapeDtypeStruct((B, S, D), F32),
        scratch_shapes=[pltpu.VMEM((MIX_WIDTH, D), BF16)],
        compiler_params=pltpu.CompilerParams(
            dimension_semantics=("arbitrary", "arbitrary"), vmem_limit_bytes=VMEM_LIMIT),
        name="out",
    )(x, ma, mb, qc, sgc, km, vm, w_out, fw)


def kernel(x, mem, norm_w, w_in, w_alpha2, b_alpha, gla_norm_w, mem_norm_w, w_mem_kv, w_out,
           rel_bias, final_norm_w):
    assert norm_w.shape[0] == 1, "single layer"
    assert w_in.shape[2] == W_IN_ALIGNED + GLA_RANK + 2 * C_WIDTH

    bias_own, bias_prev, km, vm = _prep(rel_bias, mem, mem_norm_w[0][None, :], w_mem_kv)
    qaT, vaT, ka, sga, qc, sgc, mb = _proj(x, norm_w[0][None, :], w_in[0].T, w_alpha2[0], b_alpha[0][None, :],
                                           gla_norm_w[0][None, :])
    ma = _moba(rel_bias, qaT, ka, vaT, bias_own, bias_prev, sga)
    return _out(x, ma, mb, qc, sgc, km, vm, w_out, final_norm_w[None, :])
```

```python
import functools
import math

import jax
import jax.numpy as jnp
from jax import lax
from jax.experimental import pallas as pl
from jax.experimental.pallas import tpu as pltpu

F32 = jnp.float32
BF16 = jnp.bfloat16

D_MODEL = 1024
MEM_LEN = 256
A_HEADS = 8
A_HEAD_DIM = 64
A_WIDTH = 512
MOBA_BLOCK = 256
MOBA_TOPK = 3
B_HEADS = 4
B_KEY_DIM = 64
B_VAL_DIM = 128
B_KEY_WIDTH = 256
B_WIDTH = 512
GLA_RANK = 16
GLA_TAU = 16.0
C_HEADS = 4
C_HEAD_DIM = 128
C_WIDTH = 512
MIX_WIDTH = A_WIDTH + B_WIDTH + C_WIDTH
REL_BUCKETS = 32
REL_MAX_DIST = 128
RMS_EPS = 1e-6
NEG = -1e30

LANES = 128
SUBLANES = 8
BF16_SUBLANES = 16
PROJ_TM = 512
OUT_TM = 512
GLA_CHUNK = 256
CAST_ROWS = 512
VMEM_LIMIT = 56 * 1024 * 1024
LOG2E = 1.0 / math.log(2.0)
HEADS_PER_TILE = LANES // A_HEAD_DIM
PV_ROWS = A_HEAD_DIM + BF16_SUBLANES
MOBA_MAX_BLOCKS = 16
QK_LOOKAHEAD = 5
FAR_UNROLL = 7


def _dot(a, b):
    return jnp.dot(a, b, preferred_element_type=F32)


def _dot_nt(a, b):
    return lax.dot_general(a, b, (((1,), (1,)), ((), ())), preferred_element_type=F32)


def _rmsnorm_rows(x, w):
    return x * lax.rsqrt(jnp.mean(x * x, axis=-1, keepdims=True) + RMS_EPS) * w


def _silu(x):
    return x * jax.nn.sigmoid(x)


def _t5_bucket(n):
    max_exact = REL_BUCKETS // 2
    nf = jnp.maximum(n, max_exact).astype(F32)
    large = max_exact + jnp.floor(jnp.log(nf / max_exact) / math.log(REL_MAX_DIST / max_exact)
                                  * (REL_BUCKETS - max_exact)).astype(jnp.int32)
    large = jnp.minimum(large, REL_BUCKETS - 1)
    return jnp.where(n < max_exact, n, large)


def _prep_kernel(n_batches, rb_ref, mem_ref, mnw_ref, w_ref, own_ref, prev_ref, km_ref, vm_ref):
    h = pl.program_id(0)

    @pl.when(h < n_batches)
    def _():
        u = _rmsnorm_rows(mem_ref[0], mnw_ref[...]).astype(BF16)
        km_ref[0] = _dot(u, w_ref[0, :, :C_WIDTH].astype(BF16)).astype(BF16)
        vm_ref[0] = lax.dot_general(w_ref[0, :, C_WIDTH:].astype(BF16), u, (((0,), (1,)), ((), ())),
                                    preferred_element_type=F32).astype(BF16)

    half = MOBA_BLOCK // 2
    key = lax.broadcasted_iota(jnp.int32, (half, MOBA_BLOCK), 0)
    qry = lax.broadcasted_iota(jnp.int32, (half, MOBA_BLOCK), 1)
    rel = qry - key
    b_own = _t5_bucket(jnp.maximum(rel, 0))
    assert REL_MAX_DIST <= half
    corner = (lax.broadcasted_iota(jnp.int32, (half, half), 1)
              - lax.broadcasted_iota(jnp.int32, (half, half), 0) + half)
    b_prev = _t5_bucket(corner)
    own = jnp.zeros((half, MOBA_BLOCK), F32)
    prev = jnp.zeros((half, half), F32)
    for bk in range(REL_BUCKETS):
        val = rb_ref[bk, h] * LOG2E
        own = jnp.where(b_own == bk, val, own)
        prev = jnp.where(b_prev == bk, val, prev)
    own = jnp.where(rel >= 0, own, NEG)
    own_ref[0, :half, :] = own
    own_ref[0, half:, :half] = jnp.full((half, half), NEG, F32)
    own_ref[0, half:, half:] = own[:, :half]
    prev_ref[0] = prev - rb_ref[REL_BUCKETS - 1, h] * LOG2E


def _prep(rel_bias, mem, mnw, w_mem_kv):
    B = mem.shape[0]
    assert B <= A_HEADS
    batch = lambda h: (jnp.minimum(h, B - 1), 0, 0)
    half = MOBA_BLOCK // 2
    tile = pl.BlockSpec((1, MOBA_BLOCK, MOBA_BLOCK), lambda h: (h, 0, 0))
    corner = pl.BlockSpec((1, half, half), lambda h: (h, 0, 0))
    kv = pl.BlockSpec((1, MEM_LEN, C_WIDTH), batch)
    return pl.pallas_call(
        functools.partial(_prep_kernel, B),
        grid=(A_HEADS,),
        in_specs=[pl.BlockSpec(memory_space=pltpu.SMEM),
                  pl.BlockSpec((1, MEM_LEN, D_MODEL), batch),
                  pl.BlockSpec((1, D_MODEL), lambda h: (0, 0)),
                  pl.BlockSpec(w_mem_kv.shape, lambda h: (0, 0, 0))],
        out_specs=[tile, corner, kv, pl.BlockSpec((1, C_WIDTH, MEM_LEN), batch)],
        out_shape=[jax.ShapeDtypeStruct((A_HEADS, MOBA_BLOCK, MOBA_BLOCK), F32),
                   jax.ShapeDtypeStruct((A_HEADS, half, half), F32),
                   jax.ShapeDtypeStruct((B, MEM_LEN, C_WIDTH), BF16),
                   jax.ShapeDtypeStruct((B, C_WIDTH, MEM_LEN), BF16)],
        compiler_params=pltpu.CompilerParams(dimension_semantics=("arbitrary",)),
        name="prep",
    )(rel_bias, mem, mnw, w_mem_kv)


_ROW_COLS = (("qa", A_WIDTH), ("ka", A_WIDTH), ("va", A_WIDTH), ("ga", A_WIDTH),
             ("qb", B_KEY_WIDTH), ("kb", B_KEY_WIDTH), ("vb", B_WIDTH), ("gb", B_WIDTH),
             ("qc", C_WIDTH), ("gc", C_WIDTH), ("z", B_KEY_WIDTH))
_ROW_OFF = {}
_off = 0
for _name, _width in _ROW_COLS:
    _ROW_OFF[_name] = (_off, _off + _width)
    _off += _width
ROW_COLS_TOTAL = _off


W_IN_ALIGNED = 4 * A_WIDTH + 2 * B_KEY_WIDTH + 2 * B_WIDTH


def _proj_kernel(x_ref, nw_ref, w_ref, wa_ref, ba_ref, gnw_ref,
                 qaT_ref, vaT_ref, ka_ref, ga_ref, qc_ref, gc_ref, mb_ref,
                 wr_ref, qb_ref, kb_ref, vb_ref, gb_ref, g_ref, state_scr, att_scr):
    @pl.when(pl.program_id(1) == 0)
    def _():
        state_scr[...] = jnp.zeros_like(state_scr)

    @pl.when((pl.program_id(0) == 0) & (pl.program_id(1) == 0))
    def _():
        chunk = CAST_ROWS
        for lo in range(0, W_IN_ALIGNED, chunk):
            wr_ref[lo:lo + chunk, :] = w_ref[lo:lo + chunk, :].astype(BF16)
        tail = W_IN_ALIGNED + GLA_RANK
        for lo in range(0, 2 * C_WIDTH, chunk):
            wr_ref[W_IN_ALIGNED + lo:W_IN_ALIGNED + lo + chunk, :] = (
                w_ref[tail + lo:tail + lo + chunk, :].astype(BF16))
        z_lo, z_hi = _ROW_OFF["z"]
        wr_ref[z_lo:z_hi, :] = jnp.dot(wa_ref[...].T, w_ref[W_IN_ALIGNED:tail, :],
                                       preferred_element_type=F32,
                                       precision=lax.Precision.HIGHEST).astype(BF16)

    u = _rmsnorm_rows(x_ref[0], nw_ref[...]).astype(BF16)
    tm = u.shape[0]

    def row(name, rows=slice(None)):
        lo, hi = _ROW_OFF[name]
        return _dot_nt(u[rows], wr_ref[lo:hi, :])

    def col(name, rows=slice(None)):
        lo, hi = _ROW_OFF[name]
        return _dot_nt(wr_ref[lo:hi, :], u[rows])

    z = row("z") + ba_ref[...]
    g_ref[...] = (jnp.minimum(z, 0.0) - jnp.log1p(jnp.exp(-jnp.abs(z)))) * (LOG2E / GLA_TAU)
    qb_ref[...] = (row("qb") * (B_KEY_DIM ** -0.5)).astype(BF16)
    kb_ref[...] = row("kb").astype(BF16)
    vb_ref[...] = row("vb").astype(BF16)
    gb_ref[...] = _silu(row("gb")).astype(BF16)

    def emit_qaT(rows):
        qaT_ref[0, :, rows] = col("qa", rows)

    def emit_vaT(rows):
        vaT = col("va", rows).astype(BF16)
        ones = jnp.ones((PV_ROWS - A_HEAD_DIM, vaT.shape[1]), BF16)
        pieces = []
        for h in range(A_HEADS):
            pieces += [vaT[h * A_HEAD_DIM:(h + 1) * A_HEAD_DIM], ones]
        vaT_ref[0, :, rows] = jnp.concatenate(pieces, axis=0)

    def emit_ka(rows):
        ka = row("ka", rows)
        n = ka.shape[0]
        nb = MOBA_MAX_BLOCKS
        key_pos = (pl.program_id(1) * tm + rows.start
                   + lax.broadcasted_iota(jnp.int32, (n, LANES - A_HEAD_DIM), 0))
        blk_id = jnp.right_shift(key_pos, MOBA_BLOCK.bit_length() - 1)
        lane = lax.broadcasted_iota(jnp.int32, (n, LANES - A_HEAD_DIM), 1)
        onehot = jnp.where((lane < 2 * nb) & ((lane & (nb - 1)) == blk_id), 1.0, 0.0)
        pieces = []
        for h in range(A_HEADS):
            pieces += [ka[:, h * A_HEAD_DIM:(h + 1) * A_HEAD_DIM], onehot]
        ka_ref[0, rows, :] = jnp.concatenate(pieces, axis=1).astype(BF16)

    def emit_ga(rows):
        ga_ref[0, :, rows] = _silu(col("ga", rows)).astype(BF16)

    def emit_qc(rows):
        qc_ref[0, :, rows] = (col("qc", rows) * (C_HEAD_DIM ** -0.5)).astype(BF16)

    def emit_gc(rows):
        gc_ref[0, :, rows] = _silu(col("gc", rows)).astype(BF16)

    chunks = [slice(lo, lo + GLA_CHUNK) for lo in range(0, tm, GLA_CHUNK)]
    pending = [functools.partial(emit, rows) for emit in
               (emit_qaT, emit_vaT, emit_ka, emit_ga, emit_qc, emit_gc) for rows in chunks]

    def fill():
        if pending:
            pending.pop(0)()

    nw_gla = gnw_ref[...]
    for rows in chunks:
        def store_out(cols, val, rows=rows):
            mb_ref[0, rows, cols] = val

        _gla_chunk(qb_ref[rows, :], kb_ref[rows, :], vb_ref[rows, :], g_ref[rows, :], gb_ref[rows, :],
                   nw_gla, state_scr, att_scr, store_out, fill)
    while pending:
        fill()


def _proj(x, nw, w_in, wa, ba, gnw):
    B, S, D = x.shape
    tm = PROJ_TM
    row_spec = lambda width: pl.BlockSpec((1, tm, width), lambda b, i: (b, i, 0))
    col_spec = lambda rows: pl.BlockSpec((1, rows, tm), lambda b, i: (b, 0, i))
    const = lambda shape: pl.BlockSpec(shape, lambda b, i: (0,) * len(shape))
    sds = jax.ShapeDtypeStruct
    return pl.pallas_call(
        _proj_kernel,
        grid=(B, S // tm),
        in_specs=[
            pl.BlockSpec((1, tm, D), lambda b, i: (b, i, 0)),
            const((1, D)),
            pl.BlockSpec(w_in.shape, lambda b, i: (0, 0), pipeline_mode=pl.Buffered(1)),
            const((GLA_RANK, B_KEY_WIDTH)),
            const((1, B_KEY_WIDTH)),
            const((1, B_VAL_DIM)),
        ],
        out_specs=[col_spec(A_WIDTH), col_spec(A_HEADS * PV_ROWS), row_spec(A_HEADS * LANES),
                   col_spec(A_WIDTH), col_spec(C_WIDTH), col_spec(C_WIDTH), row_spec(B_WIDTH)],
        out_shape=[sds((B, A_WIDTH, S), F32), sds((B, A_HEADS * PV_ROWS, S), BF16),
                   sds((B, S, A_HEADS * LANES), BF16), sds((B, A_WIDTH, S), BF16),
                   sds((B, C_WIDTH, S), BF16), sds((B, C_WIDTH, S), BF16),
                   sds((B, S, B_WIDTH), BF16)],
        scratch_shapes=[pltpu.VMEM((ROW_COLS_TOTAL, D), BF16),
                        pltpu.VMEM((tm, B_KEY_WIDTH), BF16), pltpu.VMEM((tm, B_KEY_WIDTH), BF16),
                        pltpu.VMEM((tm, B_WIDTH), BF16), pltpu.VMEM((tm, B_WIDTH), BF16),
                        pltpu.VMEM((tm, B_KEY_WIDTH), F32),
                        pltpu.VMEM((B_KEY_WIDTH, B_WIDTH), F32),
                        pltpu.VMEM((B_HEADS, GLA_CHUNK, GLA_CHUNK), F32)],
        compiler_params=pltpu.CompilerParams(
            dimension_semantics=("arbitrary", "arbitrary"), vmem_limit_bytes=VMEM_LIMIT),
        name="proj_gla",
    )(x, nw, w_in, wa, ba, gnw)


def _moba_kernel(rb_ref, qT_ref, k_ref, vT_ref, own_ref, prev_ref, sg_ref, o_ref,
                 kmean_scr, kmbd_scr, qo_scr, qh_scr, m_scr, acc_scr, s_scr):
    i = pl.program_id(1)
    nb = kmean_scr.shape[0]
    blk = MOBA_BLOCK
    dh = A_HEAD_DIM
    n_tiles = A_HEADS // HEADS_PER_TILE

    @pl.when(i == 0)
    def _():
        for n in range(nb):
            kb = jnp.concatenate(
                [k_ref[0, n * blk:(n + 1) * blk, h * LANES:h * LANES + dh] for h in range(A_HEADS)],
                axis=1).astype(F32)
            kmean_scr[n:n + 1, :] = jnp.mean(kb, axis=0, keepdims=True)
        lane_head = jnp.right_shift(lax.broadcasted_iota(jnp.int32, (nb, A_WIDTH), 1),
                                    dh.bit_length() - 1)
        for h in range(A_HEADS):
            km = jnp.where(lane_head == h, kmean_scr[...], 0.0)
            km_hi = km.astype(BF16)
            km_lo = (km - km_hi.astype(F32)).astype(BF16)
            kmbd_scr[h * nb:(h + 1) * nb, :] = jnp.concatenate([km_hi, km_lo, km_hi], axis=1)

    q_f32 = qT_ref[0]
    q_hi = q_f32.astype(BF16)
    q_lo = (q_f32 - q_hi.astype(F32)).astype(BF16)
    gate_all = _dot(kmbd_scr[...], jnp.concatenate([q_hi, q_hi, q_lo], axis=0))
    brow = lax.broadcasted_iota(jnp.int32, (nb, blk), 0)
    browf = brow.astype(F32)
    zeros = jnp.zeros((LANES - dh - 2 * nb, blk), F32)

    for h in range(A_HEADS):
        q_h = qT_ref[0, h * dh:(h + 1) * dh, :] * (dh ** -0.5 * LOG2E)
        qo_scr[h] = jnp.concatenate([q_h, jnp.zeros((LANES - dh, blk), F32)], axis=0).astype(BF16)

    for h in range(A_HEADS):
        g = jnp.where(brow < i, gate_all[h * nb:(h + 1) * nb], NEG)
        sel = brow < 0
        for _ in range(MOBA_TOPK):
            m = jnp.max(g, axis=0, keepdims=True)
            idx = jnp.min(jnp.where(g == m, browf, float(nb)), axis=0, keepdims=True)
            pick = browf == idx
            sel = sel | pick
            g = jnp.where(pick, -jnp.inf, g)
        valid = sel & (brow < i)
        far_bias = rb_ref[REL_BUCKETS - 1, h] * LOG2E
        pen = jnp.where(valid, far_bias, NEG)
        pen_hi = pen.astype(BF16).astype(F32)
        q_h = qT_ref[0, h * dh:(h + 1) * dh, :] * (dh ** -0.5 * LOG2E)
        qh_scr[h] = jnp.concatenate([q_h, pen_hi, pen - pen_hi, zeros], axis=0).astype(BF16)

    half = blk // 2

    def logits(j, h, kind):
        rows = pl.ds(pl.multiple_of(j * blk, blk), blk)
        keys = k_ref[0, rows, h * LANES:(h + 1) * LANES]
        if kind == "own":
            return _dot(keys, qo_scr[h]) + own_ref[h]
        s = _dot(keys, qh_scr[h])
        if kind == "prev":
            near = s[half:, :half] + prev_ref[h]
            s = jnp.concatenate(
                [s[:half], jnp.concatenate([near, s[half:, half:]], axis=1)], axis=0)
        return s

    def fold(j, h, s, first):
        rows = pl.ds(pl.multiple_of(j * blk, blk), blk)
        vj = vT_ref[0, h * PV_ROWS:(h + 1) * PV_ROWS, rows]
        m_new, alpha, p = [], [], []
        for lo in range(0, blk, LANES):
            s_t = s[:, lo:lo + LANES]
            bm = jnp.max(s_t, axis=0, keepdims=True)
            if first:
                m_t = bm
            else:
                m_old = m_scr[h, :, lo:lo + LANES]
                m_t = jnp.maximum(m_old, bm)
                alpha.append(jnp.exp2(m_old - m_t))
            m_new.append(m_t)
            p.append(jnp.exp2(s_t - m_t).astype(BF16))
        m_scr[h] = jnp.concatenate(m_new, axis=1)
        pv = _dot(vj, jnp.concatenate(p, axis=1))
        if first:
            acc_scr[h] = pv
        else:
            acc_scr[h] = jnp.concatenate(alpha, axis=1) * acc_scr[h] + pv

    n_far = i - 1
    j_prev = jnp.maximum(i - 1, 0)
    items = ([(i, "own", h) for h in range(A_HEADS)] + [(j_prev, "prev", h) for h in range(A_HEADS)]
             + [(0, "far", h) for h in range(QK_LOOKAHEAD)])
    n_fold = 2 * A_HEADS
    pending = []

    def issue(n):
        j, kind, h = items[n]
        s = logits(j, h, kind)
        if n < n_fold:
            pending.append(s)
        else:
            s_scr[h] = s

    for n in range(QK_LOOKAHEAD):
        issue(n)
    for n in range(n_fold):
        issue(n + QK_LOOKAHEAD)
        j, kind, h = items[n]
        fold(j, h, pending.pop(0), kind == "own")

    def far_group(j0, n_blocks):
        j_after = jnp.minimum(j0 + n_blocks, n_far - 1)
        stream = ([(j0 + t, h) for t in range(n_blocks) for h in range(A_HEADS)]
                  + [(j_after, h) for h in range(QK_LOOKAHEAD)])
        n_items = n_blocks * A_HEADS
        pending = [s_scr[h] for h in range(QK_LOOKAHEAD)]
        for n in range(n_items):
            j, h = stream[n + QK_LOOKAHEAD]
            if n + QK_LOOKAHEAD < n_items:
                pending.append(logits(j, h, "far"))
            else:
                s_scr[h] = logits(j, h, "far")
            j, h = stream[n]
            fold(j, h, pending.pop(0), False)

    def far_groups(g, carry):
        far_group(g * FAR_UNROLL, FAR_UNROLL)
        return carry

    n_groups = jnp.maximum(n_far, 0) // FAR_UNROLL
    lax.fori_loop(0, n_groups, far_groups, 0)
    for rest in range(1, FAR_UNROLL):
        @pl.when(n_far - n_groups * FAR_UNROLL == rest)
        def _():
            far_group(n_groups * FAR_UNROLL, rest)

    for t in range(n_tiles):
        oT = []
        for hh in range(HEADS_PER_TILE):
            acc = acc_scr[t * HEADS_PER_TILE + hh]
            oT.append(acc[:dh] / acc[dh:dh + 1])
        rows = slice(t * LANES, (t + 1) * LANES)
        o_ref[0, rows, :] = (jnp.concatenate(oT, axis=0) * sg_ref[0, rows, :]).astype(BF16)


def _moba(rel_bias, qaT, ka, vaT, bias_own, bias_prev, sga):
    B, S, _ = ka.shape
    nb = S // MOBA_BLOCK
    assert nb == MOBA_MAX_BLOCKS
    bias_spec = pl.BlockSpec((A_HEADS, MOBA_BLOCK, MOBA_BLOCK), lambda b, i: (0, 0, 0))
    corner_spec = pl.BlockSpec((A_HEADS, MOBA_BLOCK // 2, MOBA_BLOCK // 2), lambda b, i: (0, 0, 0))
    return pl.pallas_call(
        _moba_kernel,
        grid=(B, nb),
        in_specs=[
            pl.BlockSpec(memory_space=pltpu.SMEM),
            pl.BlockSpec((1, A_WIDTH, MOBA_BLOCK), lambda b, i: (b, 0, i)),
            pl.BlockSpec((1, S, A_HEADS * LANES), lambda b, i: (b, 0, 0)),
            pl.BlockSpec((1, A_HEADS * PV_ROWS, S), lambda b, i: (b, 0, 0)),
            bias_spec, corner_spec,
            pl.BlockSpec((1, A_WIDTH, MOBA_BLOCK), lambda b, i: (b, 0, i)),
        ],
        out_specs=pl.BlockSpec((1, A_WIDTH, MOBA_BLOCK), lambda b, i: (b, 0, i)),
        out_shape=jax.ShapeDtypeStruct((B, A_WIDTH, S), BF16),
        scratch_shapes=[pltpu.VMEM((nb, A_WIDTH), F32),
                        pltpu.VMEM((A_HEADS * nb, 3 * A_WIDTH), BF16),
                        pltpu.VMEM((A_HEADS, LANES, MOBA_BLOCK), BF16),
                        pltpu.VMEM((A_HEADS, LANES, MOBA_BLOCK), BF16),
                        pltpu.VMEM((A_HEADS, 1, MOBA_BLOCK), F32),
                        pltpu.VMEM((A_HEADS, PV_ROWS, MOBA_BLOCK), F32),
                        pltpu.VMEM((QK_LOOKAHEAD, MOBA_BLOCK, MOBA_BLOCK), F32)],
        compiler_params=pltpu.CompilerParams(
            dimension_semantics=("arbitrary", "arbitrary"),
            vmem_limit_bytes=VMEM_LIMIT),
        name="moba",
    )(rel_bias, qaT, ka, vaT, bias_own, bias_prev, sga)


def _gla_chunk(q_bf, k_bf, v, g, sg, nw, state_scr, att_scr, store_out, fill):
    C = GLA_CHUNK
    dk, dv, H = B_KEY_DIM, B_VAL_DIM, B_HEADS
    n_levels = C.bit_length() - 1

    q = q_bf.astype(F32)
    k = k_bf.astype(F32)

    row = lax.broadcasted_iota(jnp.int32, (C, H * dk), 0)
    tt = lax.broadcasted_iota(jnp.int32, (C, C), 0)
    ss = lax.broadcasted_iota(jnp.int32, (C, C), 1)

    n_tiles = C // SUBLANES
    sub = lax.broadcasted_iota(jnp.int32, (n_tiles, SUBLANES, H * dk), 1)
    scan = g.reshape(n_tiles, SUBLANES, H * dk)
    sh = 1
    while sh < SUBLANES:
        scan = scan + jnp.where(sub >= sh, pltpu.roll(scan, sh, 1), 0.0)
        sh *= 2
    totals = scan[:, SUBLANES - 1, :]
    t_hi = totals.astype(BF16)
    t_mid = (totals - t_hi.astype(F32)).astype(BF16)
    t_lo = (totals - t_hi.astype(F32) - t_mid.astype(F32)).astype(BF16)
    before = (lax.broadcasted_iota(jnp.int32, (n_tiles, n_tiles), 1)
              < lax.broadcasted_iota(jnp.int32, (n_tiles, n_tiles), 0))
    p3 = _dot(jnp.where(before, 1.0, 0.0).astype(BF16),
              jnp.concatenate([t_hi, t_mid, t_lo], axis=1))
    prefix = p3[:, :H * dk] + p3[:, H * dk:2 * H * dk] + p3[:, 2 * H * dk:]
    b = (scan + prefix[:, None, :]).reshape(C, H * dk)

    def roll_rows(x, shift):
        if shift % C < SUBLANES:
            x3 = x.reshape(C // SUBLANES, SUBLANES, x.shape[1])
            return pltpu.roll(x3, shift % C, 1).reshape(x.shape)
        if (C - shift % C) < SUBLANES:
            x3 = x.reshape(C // SUBLANES, SUBLANES, x.shape[1])
            return pltpu.roll(x3, SUBLANES - (C - shift % C), 1).reshape(x.shape)
        return pltpu.roll(x, shift, 0)

    txs = tt ^ ss
    level = jnp.full((C, C), -1, jnp.int32)
    for p in range(n_levels):
        level = level + (txs >= (1 << p)).astype(jnp.int32)
    level = jnp.where(ss <= tt, level, -2)

    heads_per_tile = LANES // dk
    lane = lax.broadcasted_iota(jnp.int32, (C, H * dk), 1)
    head_in_tile = jnp.right_shift(lane, dk.bit_length() - 1) & (heads_per_tile - 1)

    def split_heads(kk):
        return [jnp.where(head_in_tile == hh, kk, 0.0).astype(BF16) for hh in range(heads_per_tile)]

    def scores(q_bf, k_split, h):
        t, hh = divmod(h, heads_per_tile)
        cols = slice(t * LANES, (t + 1) * LANES)
        return _dot_nt(q_bf[:, cols], k_split[hh][:, cols])

    on_diag = level == -1
    k_split = split_heads(k)
    for h in range(H):
        att_scr[h] = jnp.where(on_diag, scores(q_bf, k_split, h), 0.0)
    fill()

    block_end = b
    for p in range(n_levels):
        m = 1 << p
        second_half = (row & m) != 0
        r = jnp.where(second_half, roll_rows(block_end, m), block_end)
        decay = jnp.exp2(-jnp.abs(b - r))
        kd_split = split_heads(k * decay)
        if m >= SUBLANES:
            blocks = [slice(lo, lo + m) for lo in range(m, C, 2 * m)]
            take = lambda x: jnp.concatenate([x[rows] for rows in blocks], axis=0)
            qd_bf = (take(q) * take(decay)).astype(BF16)
            for h in range(H):
                s_lvl = scores(qd_bf, kd_split, h)
                for n, rows in enumerate(blocks):
                    keys = slice(rows.start - m, rows.start)
                    att_scr[h, rows, keys] = s_lvl[n * m:(n + 1) * m, keys]
        else:
            qd_bf = (q * decay).astype(BF16)
            at_level = level == p
            for h in range(H):
                att_scr[h] = jnp.where(at_level, scores(qd_bf, kd_split, h), att_scr[h])
        fill()
        if p + 1 < n_levels:
            block_end = jnp.where(second_half, block_end, roll_rows(block_end, C - m))

    state = state_scr[...]
    o_inter = _dot((q * jnp.exp2(b)).astype(BF16), state.astype(BF16))

    for h in range(H):
        cols = slice(h * dv, (h + 1) * dv)
        o_h = o_inter[:, cols] + _dot(att_scr[h].astype(BF16), v[:, cols])
        store_out(cols, (_rmsnorm_rows(o_h, nw) * sg[:, cols]).astype(BF16))

    bT = b.T
    kT = k.T
    b_last = bT[:, C - 1:C]
    kdT = (kT * jnp.exp2(b_last - bT)).astype(BF16)
    ds = _dot(kdT, v)
    srow = lax.broadcasted_iota(jnp.int32, (H * dk, H * dv), 0) // dk
    scol = lax.broadcasted_iota(jnp.int32, (H * dk, H * dv), 1) // dv
    state_scr[...] = jnp.exp2(b_last) * state + jnp.where(srow == scol, ds, 0.0)


def _tn_dot(aT, b):
    return lax.dot_general(aT, b, (((0,), (0,)), ((), ())), preferred_element_type=F32)


def _out_kernel(x_ref, maT_ref, mb_ref, qcT_ref, sgcT_ref, km_ref, vmT_ref, w_ref, fw_ref, o_ref,
                wo_ref):
    @pl.when((pl.program_id(0) == 0) & (pl.program_id(1) == 0))
    def _():
        for lo in range(0, MIX_WIDTH, CAST_ROWS):
            wo_ref[lo:lo + CAST_ROWS, :] = w_ref[0, lo:lo + CAST_ROWS, :].astype(BF16)

    dh = C_HEAD_DIM
    qcT = qcT_ref[0]
    km = km_ref[0]
    vmT = vmT_ref[0]
    heads = [slice(h * dh, (h + 1) * dh) for h in range(C_HEADS)]
    scores = [_dot(km[:, sl], qcT[sl, :]) for sl in heads]
    h_ab = (_tn_dot(maT_ref[0], wo_ref[0:A_WIDTH, :])
            + _dot(mb_ref[0], wo_ref[A_WIDTH:A_WIDTH + B_WIDTH, :]))
    ocT = []
    for s, sl in zip(scores, heads):
        p = jnp.exp(s - jnp.max(s, axis=0, keepdims=True))
        l = jnp.sum(p, axis=0, keepdims=True)
        ocT.append(_dot(vmT[sl, :], p.astype(BF16)) / l)
    mcT = (jnp.concatenate(ocT, axis=0) * sgcT_ref[0]).astype(BF16)
    tm = mcT.shape[1]
    for rows in (slice(0, tm // 2), slice(tm // 2, tm)):
        h_new = x_ref[0, rows, :] + h_ab[rows] + _tn_dot(mcT[:, rows], wo_ref[A_WIDTH + B_WIDTH:, :])
        o_ref[0, rows, :] = _rmsnorm_rows(h_new, fw_ref[...])


def _out(x, ma, mb, qc, sgc, km, vm, w_out, fw):
    B, S, D = x.shape
    tm = OUT_TM
    row = lambda width: pl.BlockSpec((1, tm, width), lambda b, i: (b, i, 0))
    col = lambda width: pl.BlockSpec((1, width, tm), lambda b, i: (b, 0, i))
    return pl.pallas_call(
        _out_kernel,
        grid=(B, S // tm),
        in_specs=[row(D), col(A_WIDTH), row(B_WIDTH), col(C_WIDTH), col(C_WIDTH),
                  pl.BlockSpec((1, MEM_LEN, C_WIDTH), lambda b, i: (b, 0, 0)),
                  pl.BlockSpec((1, C_WIDTH, MEM_LEN), lambda b, i: (b, 0, 0)),
                  pl.BlockSpec(w_out.shape, lambda b, i: (0, 0, 0), pipeline_mode=pl.Buffered(1)),
                  pl.BlockSpec((1, D), lambda b, i: (0, 0))],
        out_specs=row(D),
        out_shape=jax.ShapeDtypeStruct((B, S, D), F32),
        scratch_shapes=[pltpu.VMEM((MIX_WIDTH, D), BF16)],
        compiler_params=pltpu.CompilerParams(
            dimension_semantics=("arbitrary", "arbitrary"), vmem_limit_bytes=VMEM_LIMIT),
        name="out",
    )(x, ma, mb, qc, sgc, km, vm, w_out, fw)


def kernel(x, mem, norm_w, w_in, w_alpha2, b_alpha, gla_norm_w, mem_norm_w, w_mem_kv, w_out,
           rel_bias, final_norm_w):
    assert norm_w.shape[0] == 1, "single layer"
    assert w_in.shape[2] == W_IN_ALIGNED + GLA_RANK + 2 * C_WIDTH

    bias_own, bias_prev, km, vm = _prep(rel_bias, mem, mem_norm_w[0][None, :], w_mem_kv)
    qaT, vaT, ka, sga, qc, sgc, mb = _proj(x, norm_w[0][None, :], w_in[0].T, w_alpha2[0], b_alpha[0][None, :],
                                           gla_norm_w[0][None, :])
    ma = _moba(rel_bias, qaT, ka, vaT, bias_own, bias_prev, sga)
    return _out(x, ma, mb, qc, sgc, km, vm, w_out, final_norm_w[None, :])
```

```python
import functools
import math

import jax
import jax.numpy as jnp
from jax import lax
from jax.experimental import pallas as pl
from jax.experimental.pallas import tpu as pltpu

F32 = jnp.float32
BF16 = jnp.bfloat16

D_MODEL = 1024
MEM_LEN = 256
A_HEADS = 8
A_HEAD_DIM = 64
A_WIDTH = 512
MOBA_BLOCK = 256
MOBA_TOPK = 3
B_HEADS = 4
B_KEY_DIM = 64
B_VAL_DIM = 128
B_KEY_WIDTH = 256
B_WIDTH = 512
GLA_RANK = 16
GLA_TAU = 16.0
C_HEADS = 4
C_HEAD_DIM = 128
C_WIDTH = 512
MIX_WIDTH = A_WIDTH + B_WIDTH + C_WIDTH
REL_BUCKETS = 32
REL_MAX_DIST = 128
RMS_EPS = 1e-6
NEG = -1e30

LANES = 128
SUBLANES = 8
BF16_SUBLANES = 16
PROJ_TM = 512
OUT_TM = 512
GLA_CHUNK = 256
CAST_ROWS = 512
VMEM_LIMIT = 56 * 1024 * 1024
LOG2E = 1.0 / math.log(2.0)
HEADS_PER_TILE = LANES // A_HEAD_DIM
PV_ROWS = A_HEAD_DIM + BF16_SUBLANES
MOBA_MAX_BLOCKS = 16
QK_LOOKAHEAD = 5
FAR_UNROLL = 7


def _dot(a, b):
    return jnp.dot(a, b, preferred_element_type=F32)


def _dot_nt(a, b):
    return lax.dot_general(a, b, (((1,), (1,)), ((), ())), preferred_element_type=F32)


def _rmsnorm_rows(x, w):
    return x * lax.rsqrt(jnp.mean(x * x, axis=-1, keepdims=True) + RMS_EPS) * w


def _silu(x):
    return x * jax.nn.sigmoid(x)


def _t5_bucket(n):
    max_exact = REL_BUCKETS // 2
    nf = jnp.maximum(n, max_exact).astype(F32)
    large = max_exact + jnp.floor(jnp.log(nf / max_exact) / math.log(REL_MAX_DIST / max_exact)
                                  * (REL_BUCKETS - max_exact)).astype(jnp.int32)
    large = jnp.minimum(large, REL_BUCKETS - 1)
    return jnp.where(n < max_exact, n, large)


def _prep_kernel(n_batches, rb_ref, mem_ref, mnw_ref, w_ref, own_ref, prev_ref, km_ref, vm_ref):
    h = pl.program_id(0)

    @pl.when(h < n_batches)
    def _():
        u = _rmsnorm_rows(mem_ref[0], mnw_ref[...]).astype(BF16)
        km_ref[0] = _dot(u, w_ref[0, :, :C_WIDTH].astype(BF16)).astype(BF16)
        vm_ref[0] = lax.dot_general(w_ref[0, :, C_WIDTH:].astype(BF16), u, (((0,), (1,)), ((), ())),
                                    preferred_element_type=F32).astype(BF16)

    half = MOBA_BLOCK // 2
    key = lax.broadcasted_iota(jnp.int32, (half, MOBA_BLOCK), 0)
    qry = lax.broadcasted_iota(jnp.int32, (half, MOBA_BLOCK), 1)
    rel = qry - key
    b_own = _t5_bucket(jnp.maximum(rel, 0))
    assert REL_MAX_DIST <= half
    corner = (lax.broadcasted_iota(jnp.int32, (half, half), 1)
              - lax.broadcasted_iota(jnp.int32, (half, half), 0) + half)
    b_prev = _t5_bucket(corner)
    own = jnp.zeros((half, MOBA_BLOCK), F32)
    prev = jnp.zeros((half, half), F32)
    for bk in range(REL_BUCKETS):
        val = rb_ref[bk, h] * LOG2E
        own = jnp.where(b_own == bk, val, own)
        prev = jnp.where(b_prev == bk, val, prev)
    own = jnp.where(rel >= 0, own, NEG)
    own_ref[0, :half, :] = own
    own_ref[0, half:, :half] = jnp.full((half, half), NEG, F32)
    own_ref[0, half:, half:] = own[:, :half]
    prev_ref[0] = prev - rb_ref[REL_BUCKETS - 1, h] * LOG2E


def _prep(rel_bias, mem, mnw, w_mem_kv):
    B = mem.shape[0]
    assert B <= A_HEADS
    batch = lambda h: (jnp.minimum(h, B - 1), 0, 0)
    half = MOBA_BLOCK // 2
    tile = pl.BlockSpec((1, MOBA_BLOCK, MOBA_BLOCK), lambda h: (h, 0, 0))
    corner = pl.BlockSpec((1, half, half), lambda h: (h, 0, 0))
    kv = pl.BlockSpec((1, MEM_LEN, C_WIDTH), batch)
    return pl.pallas_call(
        functools.partial(_prep_kernel, B),
        grid=(A_HEADS,),
        in_specs=[pl.BlockSpec(memory_space=pltpu.SMEM),
                  pl.BlockSpec((1, MEM_LEN, D_MODEL), batch),
                  pl.BlockSpec((1, D_MODEL), lambda h: (0, 0)),
                  pl.BlockSpec(w_mem_kv.shape, lambda h: (0, 0, 0))],
        out_specs=[tile, corner, kv, pl.BlockSpec((1, C_WIDTH, MEM_LEN), batch)],
        out_shape=[jax.ShapeDtypeStruct((A_HEADS, MOBA_BLOCK, MOBA_BLOCK), F32),
                   jax.ShapeDtypeStruct((A_HEADS, half, half), F32),
                   jax.ShapeDtypeStruct((B, MEM_LEN, C_WIDTH), BF16),
                   jax.ShapeDtypeStruct((B, C_WIDTH, MEM_LEN), BF16)],
        compiler_params=pltpu.CompilerParams(dimension_semantics=("arbitrary",)),
        name="prep",
    )(rel_bias, mem, mnw, w_mem_kv)


_ROW_COLS = (("qa", A_WIDTH), ("ka", A_WIDTH), ("va", A_WIDTH), ("ga", A_WIDTH),
             ("qb", B_KEY_WIDTH), ("kb", B_KEY_WIDTH), ("vb", B_WIDTH), ("gb", B_WIDTH),
             ("qc", C_WIDTH), ("gc", C_WIDTH), ("z", B_KEY_WIDTH))
_ROW_OFF = {}
_off = 0
for _name, _width in _ROW_COLS:
    _ROW_OFF[_name] = (_off, _off + _width)
    _off += _width
ROW_COLS_TOTAL = _off


W_IN_ALIGNED = 4 * A_WIDTH + 2 * B_KEY_WIDTH + 2 * B_WIDTH


def _proj_kernel(x_ref, nw_ref, w_ref, wa_ref, ba_ref, gnw_ref,
                 qaT_ref, vaT_ref, ka_ref, ga_ref, qc_ref, gc_ref, mb_ref, kmean_ref,
                 wr_ref, qb_ref, kb_ref, vb_ref, gb_ref, g_ref, state_scr, att_scr):
    @pl.when(pl.program_id(1) == 0)
    def _():
        state_scr[...] = jnp.zeros_like(state_scr)

    @pl.when((pl.program_id(0) == 0) & (pl.program_id(1) == 0))
    def _():
        chunk = CAST_ROWS
        for lo in range(0, W_IN_ALIGNED, chunk):
            wr_ref[lo:lo + chunk, :] = w_ref[lo:lo + chunk, :].astype(BF16)
        tail = W_IN_ALIGNED + GLA_RANK
        for lo in range(0, 2 * C_WIDTH, chunk):
            wr_ref[W_IN_ALIGNED + lo:W_IN_ALIGNED + lo + chunk, :] = (
                w_ref[tail + lo:tail + lo + chunk, :].astype(BF16))
        z_lo, z_hi = _ROW_OFF["z"]
        wr_ref[z_lo:z_hi, :] = jnp.dot(wa_ref[...].T, w_ref[W_IN_ALIGNED:tail, :],
                                       preferred_element_type=F32,
                                       precision=lax.Precision.HIGHEST).astype(BF16)

    u = _rmsnorm_rows(x_ref[0], nw_ref[...]).astype(BF16)
    tm = u.shape[0]

    def row(name, rows=slice(None)):
        lo, hi = _ROW_OFF[name]
        return _dot_nt(u[rows], wr_ref[lo:hi, :])

    def col(name, rows=slice(None)):
        lo, hi = _ROW_OFF[name]
        return _dot_nt(wr_ref[lo:hi, :], u[rows])

    z = row("z") + ba_ref[...]
    g_ref[...] = (jnp.minimum(z, 0.0) - jnp.log1p(jnp.exp(-jnp.abs(z)))) * (LOG2E / GLA_TAU)
    qb_ref[...] = (row("qb") * (B_KEY_DIM ** -0.5)).astype(BF16)
    kb_ref[...] = row("kb").astype(BF16)
    vb_ref[...] = row("vb").astype(BF16)
    gb_ref[...] = _silu(row("gb")).astype(BF16)

    def emit_qaT(rows):
        qaT_ref[0, :, rows] = col("qa", rows)

    def emit_vaT(rows):
        vaT = col("va", rows).astype(BF16)
        ones = jnp.ones((PV_ROWS - A_HEAD_DIM, vaT.shape[1]), BF16)
        pieces = []
        for h in range(A_HEADS):
            pieces += [vaT[h * A_HEAD_DIM:(h + 1) * A_HEAD_DIM], ones]
        vaT_ref[0, :, rows] = jnp.concatenate(pieces, axis=0)

    def emit_ka(rows):
        ka = row("ka", rows)
        n = ka.shape[0]
        nb = MOBA_MAX_BLOCKS
        key_pos = (pl.program_id(1) * tm + rows.start
                   + lax.broadcasted_iota(jnp.int32, (n, LANES - A_HEAD_DIM), 0))
        blk_id = jnp.right_shift(key_pos, MOBA_BLOCK.bit_length() - 1)
        lane = lax.broadcasted_iota(jnp.int32, (n, LANES - A_HEAD_DIM), 1)
        onehot = jnp.where((lane < 2 * nb) & ((lane & (nb - 1)) == blk_id), 1.0, 0.0)
        pieces = []
        for h in range(A_HEADS):
            pieces += [ka[:, h * A_HEAD_DIM:(h + 1) * A_HEAD_DIM], onehot]
        ka_ref[0, rows, :] = jnp.concatenate(pieces, axis=1).astype(BF16)
        assert n == MOBA_BLOCK
        kmean_ref[0, rows.start // MOBA_BLOCK] = jnp.broadcast_to(
            jnp.mean(ka, axis=0, keepdims=True), (SUBLANES, A_WIDTH))

    def emit_ga(rows):
        ga_ref[0, :, rows] = _silu(col("ga", rows)).astype(BF16)

    def emit_qc(rows):
        qc_ref[0, :, rows] = (col("qc", rows) * (C_HEAD_DIM ** -0.5)).astype(BF16)

    def emit_gc(rows):
        gc_ref[0, :, rows] = _silu(col("gc", rows)).astype(BF16)

    chunks = [slice(lo, lo + GLA_CHUNK) for lo in range(0, tm, GLA_CHUNK)]
    pending = [functools.partial(emit, rows) for emit in
               (emit_qaT, emit_vaT, emit_ka, emit_ga, emit_qc, emit_gc) for rows in chunks]

    def fill():
        if pending:
            pending.pop(0)()

    nw_gla = gnw_ref[...]
    for rows in chunks:
        def store_out(cols, val, rows=rows):
            mb_ref[0, rows, cols] = val

        _gla_chunk(qb_ref[rows, :], kb_ref[rows, :], vb_ref[rows, :], g_ref[rows, :], gb_ref[rows, :],
                   nw_gla, state_scr, att_scr, store_out, fill)
    while pending:
        fill()


def _proj(x, nw, w_in, wa, ba, gnw):
    B, S, D = x.shape
    tm = PROJ_TM
    row_spec = lambda width: pl.BlockSpec((1, tm, width), lambda b, i: (b, i, 0))
    col_spec = lambda rows: pl.BlockSpec((1, rows, tm), lambda b, i: (b, 0, i))
    const = lambda shape: pl.BlockSpec(shape, lambda b, i: (0,) * len(shape))
    sds = jax.ShapeDtypeStruct
    return pl.pallas_call(
        _proj_kernel,
        grid=(B, S // tm),
        in_specs=[
            pl.BlockSpec((1, tm, D), lambda b, i: (b, i, 0)),
            const((1, D)),
            pl.BlockSpec(w_in.shape, lambda b, i: (0, 0), pipeline_mode=pl.Buffered(1)),
            const((GLA_RANK, B_KEY_WIDTH)),
            const((1, B_KEY_WIDTH)),
            const((1, B_VAL_DIM)),
        ],
        out_specs=[col_spec(A_WIDTH), col_spec(A_HEADS * PV_ROWS), row_spec(A_HEADS * LANES),
                   col_spec(A_WIDTH), col_spec(C_WIDTH), col_spec(C_WIDTH), row_spec(B_WIDTH),
                   pl.BlockSpec((1, tm // MOBA_BLOCK, SUBLANES, A_WIDTH), lambda b, i: (b, i, 0, 0))],
        out_shape=[sds((B, A_WIDTH, S), F32), sds((B, A_HEADS * PV_ROWS, S), BF16),
                   sds((B, S, A_HEADS * LANES), BF16), sds((B, A_WIDTH, S), BF16),
                   sds((B, C_WIDTH, S), BF16), sds((B, C_WIDTH, S), BF16),
                   sds((B, S, B_WIDTH), BF16),
                   sds((B, S // MOBA_BLOCK, SUBLANES, A_WIDTH), F32)],
        scratch_shapes=[pltpu.VMEM((ROW_COLS_TOTAL, D), BF16),
                        pltpu.VMEM((tm, B_KEY_WIDTH), BF16), pltpu.VMEM((tm, B_KEY_WIDTH), BF16),
                        pltpu.VMEM((tm, B_WIDTH), BF16), pltpu.VMEM((tm, B_WIDTH), BF16),
                        pltpu.VMEM((tm, B_KEY_WIDTH), F32),
                        pltpu.VMEM((B_KEY_WIDTH, B_WIDTH), F32),
                        pltpu.VMEM((B_HEADS, GLA_CHUNK, GLA_CHUNK), F32)],
        compiler_params=pltpu.CompilerParams(
            dimension_semantics=("arbitrary", "arbitrary"), vmem_limit_bytes=VMEM_LIMIT),
        name="proj_gla",
    )(x, nw, w_in, wa, ba, gnw)


def _moba_kernel(rb_ref, qT_ref, k_ref, vT_ref, own_ref, prev_ref, sg_ref, km_ref, o_ref,
                 kmean_scr, kmbd_scr, qo_scr, qh_scr, m_scr, acc_scr, s_scr):
    i = pl.program_id(1)
    nb = kmean_scr.shape[0]
    blk = MOBA_BLOCK
    dh = A_HEAD_DIM
    n_tiles = A_HEADS // HEADS_PER_TILE

    @pl.when(i == 0)
    def _():
        for n in range(nb):
            kmean_scr[n:n + 1, :] = km_ref[0, n, 0:1, :]
        lane_head = jnp.right_shift(lax.broadcasted_iota(jnp.int32, (nb, A_WIDTH), 1),
                                    dh.bit_length() - 1)
        for h in range(A_HEADS):
            km = jnp.where(lane_head == h, kmean_scr[...], 0.0)
            km_hi = km.astype(BF16)
            km_lo = (km - km_hi.astype(F32)).astype(BF16)
            kmbd_scr[h * nb:(h + 1) * nb, :] = jnp.concatenate([km_hi, km_lo, km_hi], axis=1)

    q_f32 = qT_ref[0]
    q_hi = q_f32.astype(BF16)
    q_lo = (q_f32 - q_hi.astype(F32)).astype(BF16)
    gate_all = _dot(kmbd_scr[...], jnp.concatenate([q_hi, q_hi, q_lo], axis=0))
    brow = lax.broadcasted_iota(jnp.int32, (nb, blk), 0)
    browf = brow.astype(F32)
    zeros = jnp.zeros((LANES - dh - 2 * nb, blk), F32)

    for h in range(A_HEADS):
        q_h = qT_ref[0, h * dh:(h + 1) * dh, :] * (dh ** -0.5 * LOG2E)
        qo_scr[h] = jnp.concatenate([q_h, jnp.zeros((LANES - dh, blk), F32)], axis=0).astype(BF16)

    for h in range(A_HEADS):
        g = jnp.where(brow < i, gate_all[h * nb:(h + 1) * nb], NEG)
        sel = brow < 0
        for _ in range(MOBA_TOPK):
            m = jnp.max(g, axis=0, keepdims=True)
            idx = jnp.min(jnp.where(g == m, browf, float(nb)), axis=0, keepdims=True)
            pick = browf == idx
            sel = sel | pick
            g = jnp.where(pick, -jnp.inf, g)
        valid = sel & (brow < i)
        far_bias = rb_ref[REL_BUCKETS - 1, h] * LOG2E
        pen = jnp.where(valid, far_bias, NEG)
        pen_hi = pen.astype(BF16).astype(F32)
        q_h = qT_ref[0, h * dh:(h + 1) * dh, :] * (dh ** -0.5 * LOG2E)
        qh_scr[h] = jnp.concatenate([q_h, pen_hi, pen - pen_hi, zeros], axis=0).astype(BF16)

    half = blk // 2

    def logits(j, h, kind):
        rows = pl.ds(pl.multiple_of(j * blk, blk), blk)
        keys = k_ref[0, rows, h * LANES:(h + 1) * LANES]
        if kind == "own":
            return _dot(keys, qo_scr[h]) + own_ref[h]
        s = _dot(keys, qh_scr[h])
        if kind == "prev":
            near = s[half:, :half] + prev_ref[h]
            s = jnp.concatenate(
                [s[:half], jnp.concatenate([near, s[half:, half:]], axis=1)], axis=0)
        return s

    def fold(j, h, s, first):
        rows = pl.ds(pl.multiple_of(j * blk, blk), blk)
        bm = jnp.max(s, axis=0, keepdims=True)
        vj = vT_ref[0, h * PV_ROWS:(h + 1) * PV_ROWS, rows]
        if first:
            m_scr[h] = bm
            acc_scr[h] = _dot(vj, jnp.exp2(s - bm).astype(BF16))
        else:
            m_old = m_scr[h]
            m_new = jnp.maximum(m_old, bm)
            m_scr[h] = m_new
            acc_scr[h] = (jnp.exp2(m_old - m_new) * acc_scr[h]
                          + _dot(vj, jnp.exp2(s - m_new).astype(BF16)))

    n_far = i - 1
    j_prev = jnp.maximum(i - 1, 0)
    items = ([(i, "own", h) for h in range(A_HEADS)] + [(j_prev, "prev", h) for h in range(A_HEADS)]
             + [(0, "far", h) for h in range(QK_LOOKAHEAD)])
    n_fold = 2 * A_HEADS
    pending = []

    def issue(n):
        j, kind, h = items[n]
        s = logits(j, h, kind)
        if n < n_fold:
            pending.append(s)
        else:
            s_scr[h] = s

    for n in range(QK_LOOKAHEAD):
        issue(n)
    for n in range(n_fold):
        issue(n + QK_LOOKAHEAD)
        j, kind, h = items[n]
        fold(j, h, pending.pop(0), kind == "own")

    def far_group(j0, n_blocks):
        j_after = jnp.minimum(j0 + n_blocks, n_far - 1)
        stream = ([(j0 + t, h) for t in range(n_blocks) for h in range(A_HEADS)]
                  + [(j_after, h) for h in range(QK_LOOKAHEAD)])
        n_items = n_blocks * A_HEADS
        pending = [s_scr[h] for h in range(QK_LOOKAHEAD)]
        for n in range(n_items):
            j, h = stream[n + QK_LOOKAHEAD]
            if n + QK_LOOKAHEAD < n_items:
                pending.append(logits(j, h, "far"))
            else:
                s_scr[h] = logits(j, h, "far")
            j, h = stream[n]
            fold(j, h, pending.pop(0), False)

    def far_groups(g, carry):
        far_group(g * FAR_UNROLL, FAR_UNROLL)
        return carry

    n_groups = jnp.maximum(n_far, 0) // FAR_UNROLL
    lax.fori_loop(0, n_groups, far_groups, 0)
    for rest in range(1, FAR_UNROLL):
        @pl.when(n_far - n_groups * FAR_UNROLL == rest)
        def _():
            far_group(n_groups * FAR_UNROLL, rest)

    for t in range(n_tiles):
        oT = []
        for hh in range(HEADS_PER_TILE):
            acc = acc_scr[t * HEADS_PER_TILE + hh]
            oT.append(acc[:dh] / acc[dh:dh + 1])
        rows = slice(t * LANES, (t + 1) * LANES)
        o_ref[0, rows, :] = (jnp.concatenate(oT, axis=0) * sg_ref[0, rows, :]).astype(BF16)


def _moba(rel_bias, qaT, ka, vaT, bias_own, bias_prev, sga, kmean):
    B, S, _ = ka.shape
    nb = S // MOBA_BLOCK
    assert nb == MOBA_MAX_BLOCKS
    bias_spec = pl.BlockSpec((A_HEADS, MOBA_BLOCK, MOBA_BLOCK), lambda b, i: (0, 0, 0))
    corner_spec = pl.BlockSpec((A_HEADS, MOBA_BLOCK // 2, MOBA_BLOCK // 2), lambda b, i: (0, 0, 0))
    return pl.pallas_call(
        _moba_kernel,
        grid=(B, nb),
        in_specs=[
            pl.BlockSpec(memory_space=pltpu.SMEM),
            pl.BlockSpec((1, A_WIDTH, MOBA_BLOCK), lambda b, i: (b, 0, i)),
            pl.BlockSpec((1, S, A_HEADS * LANES), lambda b, i: (b, 0, 0)),
            pl.BlockSpec((1, A_HEADS * PV_ROWS, S), lambda b, i: (b, 0, 0)),
            bias_spec, corner_spec,
            pl.BlockSpec((1, A_WIDTH, MOBA_BLOCK), lambda b, i: (b, 0, i)),
            pl.BlockSpec((1, nb, SUBLANES, A_WIDTH), lambda b, i: (b, 0, 0, 0)),
        ],
        out_specs=pl.BlockSpec((1, A_WIDTH, MOBA_BLOCK), lambda b, i: (b, 0, i)),
        out_shape=jax.ShapeDtypeStruct((B, A_WIDTH, S), BF16),
        scratch_shapes=[pltpu.VMEM((nb, A_WIDTH), F32),
                        pltpu.VMEM((A_HEADS * nb, 3 * A_WIDTH), BF16),
                        pltpu.VMEM((A_HEADS, LANES, MOBA_BLOCK), BF16),
                        pltpu.VMEM((A_HEADS, LANES, MOBA_BLOCK), BF16),
                        pltpu.VMEM((A_HEADS, 1, MOBA_BLOCK), F32),
                        pltpu.VMEM((A_HEADS, PV_ROWS, MOBA_BLOCK), F32),
                        pltpu.VMEM((QK_LOOKAHEAD, MOBA_BLOCK, MOBA_BLOCK), F32)],
        compiler_params=pltpu.CompilerParams(
            dimension_semantics=("arbitrary", "arbitrary"),
            vmem_limit_bytes=VMEM_LIMIT),
        name="moba",
    )(rel_bias, qaT, ka, vaT, bias_own, bias_prev, sga, kmean)


def _gla_chunk(q_bf, k_bf, v, g, sg, nw, state_scr, att_scr, store_out, fill):
    C = GLA_CHUNK
    dk, dv, H = B_KEY_DIM, B_VAL_DIM, B_HEADS
    n_levels = C.bit_length() - 1

    q = q_bf.astype(F32)
    k = k_bf.astype(F32)

    row = lax.broadcasted_iota(jnp.int32, (C, H * dk), 0)
    tt = lax.broadcasted_iota(jnp.int32, (C, C), 0)
    ss = lax.broadcasted_iota(jnp.int32, (C, C), 1)

    n_tiles = C // SUBLANES
    sub = lax.broadcasted_iota(jnp.int32, (n_tiles, SUBLANES, H * dk), 1)
    scan = g.reshape(n_tiles, SUBLANES, H * dk)
    sh = 1
    while sh < SUBLANES:
        scan = scan + jnp.where(sub >= sh, pltpu.roll(scan, sh, 1), 0.0)
        sh *= 2
    totals = scan[:, SUBLANES - 1, :]
    t_hi = totals.astype(BF16)
    t_mid = (totals - t_hi.astype(F32)).astype(BF16)
    t_lo = (totals - t_hi.astype(F32) - t_mid.astype(F32)).astype(BF16)
    before = (lax.broadcasted_iota(jnp.int32, (n_tiles, n_tiles), 1)
              < lax.broadcasted_iota(jnp.int32, (n_tiles, n_tiles), 0))
    p3 = _dot(jnp.where(before, 1.0, 0.0).astype(BF16),
              jnp.concatenate([t_hi, t_mid, t_lo], axis=1))
    prefix = p3[:, :H * dk] + p3[:, H * dk:2 * H * dk] + p3[:, 2 * H * dk:]
    b = (scan + prefix[:, None, :]).reshape(C, H * dk)

    def roll_rows(x, shift):
        if shift % C < SUBLANES:
            x3 = x.reshape(C // SUBLANES, SUBLANES, x.shape[1])
            return pltpu.roll(x3, shift % C, 1).reshape(x.shape)
        if (C - shift % C) < SUBLANES:
            x3 = x.reshape(C // SUBLANES, SUBLANES, x.shape[1])
            return pltpu.roll(x3, SUBLANES - (C - shift % C), 1).reshape(x.shape)
        return pltpu.roll(x, shift, 0)

    txs = tt ^ ss
    level = jnp.full((C, C), -1, jnp.int32)
    for p in range(n_levels):
        level = level + (txs >= (1 << p)).astype(jnp.int32)
    level = jnp.where(ss <= tt, level, -2)

    heads_per_tile = LANES // dk
    lane = lax.broadcasted_iota(jnp.int32, (C, H * dk), 1)
    head_in_tile = jnp.right_shift(lane, dk.bit_length() - 1) & (heads_per_tile - 1)

    def split_heads(kk):
        return [jnp.where(head_in_tile == hh, kk, 0.0).astype(BF16) for hh in range(heads_per_tile)]

    def scores(q_bf, k_split, h):
        t, hh = divmod(h, heads_per_tile)
        cols = slice(t * LANES, (t + 1) * LANES)
        return _dot_nt(q_bf[:, cols], k_split[hh][:, cols])

    on_diag = level == -1
    k_split = split_heads(k)
    for h in range(H):
        att_scr[h] = jnp.where(on_diag, scores(q_bf, k_split, h), 0.0)
    fill()

    block_end = b
    for p in range(n_levels):
        m = 1 << p
        second_half = (row & m) != 0
        r = jnp.where(second_half, roll_rows(block_end, m), block_end)
        decay = jnp.exp2(-jnp.abs(b - r))
        kd_split = split_heads(k * decay)
        if m >= SUBLANES:
            blocks = [slice(lo, lo + m) for lo in range(m, C, 2 * m)]
            take = lambda x: jnp.concatenate([x[rows] for rows in blocks], axis=0)
            qd_bf = (take(q) * take(decay)).astype(BF16)
            for h in range(H):
                s_lvl = scores(qd_bf, kd_split, h)
                for n, rows in enumerate(blocks):
                    keys = slice(rows.start - m, rows.start)
                    att_scr[h, rows, keys] = s_lvl[n * m:(n + 1) * m, keys]
        else:
            qd_bf = (q * decay).astype(BF16)
            at_level = level == p
            for h in range(H):
                att_scr[h] = jnp.where(at_level, scores(qd_bf, kd_split, h), att_scr[h])
        fill()
        if p + 1 < n_levels:
            block_end = jnp.where(second_half, block_end, roll_rows(block_end, C - m))

    state = state_scr[...]
    o_inter = _dot((q * jnp.exp2(b)).astype(BF16), state.astype(BF16))

    for h in range(H):
        cols = slice(h * dv, (h + 1) * dv)
        o_h = o_inter[:, cols] + _dot(att_scr[h].astype(BF16), v[:, cols])
        store_out(cols, (_rmsnorm_rows(o_h, nw) * sg[:, cols]).astype(BF16))

    bT = b.T
    kT = k.T
    b_last = bT[:, C - 1:C]
    kdT = (kT * jnp.exp2(b_last - bT)).astype(BF16)
    ds = _dot(kdT, v)
    srow = lax.broadcasted_iota(jnp.int32, (H * dk, H * dv), 0) // dk
    scol = lax.broadcasted_iota(jnp.int32, (H * dk, H * dv), 1) // dv
    state_scr[...] = jnp.exp2(b_last) * state + jnp.where(srow == scol, ds, 0.0)


def _tn_dot(aT, b):
    return lax.dot_general(aT, b, (((0,), (0,)), ((), ())), preferred_element_type=F32)


def _out_kernel(x_ref, maT_ref, mb_ref, qcT_ref, sgcT_ref, km_ref, vmT_ref, w_ref, fw_ref, o_ref,
                wo_ref):
    @pl.when((pl.program_id(0) == 0) & (pl.program_id(1) == 0))
    def _():
        for lo in range(0, MIX_WIDTH, CAST_ROWS):
            wo_ref[lo:lo + CAST_ROWS, :] = w_ref[0, lo:lo + CAST_ROWS, :].astype(BF16)

    dh = C_HEAD_DIM
    qcT = qcT_ref[0]
    km = km_ref[0]
    vmT = vmT_ref[0]
    heads = [slice(h * dh, (h + 1) * dh) for h in range(C_HEADS)]
    scores = [_dot(km[:, sl], qcT[sl, :]) for sl in heads]
    h_ab = (_tn_dot(maT_ref[0], wo_ref[0:A_WIDTH, :])
            + _dot(mb_ref[0], wo_ref[A_WIDTH:A_WIDTH + B_WIDTH, :]))
    ocT = []
    for s, sl in zip(scores, heads):
        p = jnp.exp(s - jnp.max(s, axis=0, keepdims=True))
        l = jnp.sum(p, axis=0, keepdims=True)
        ocT.append(_dot(vmT[sl, :], p.astype(BF16)) / l)
    mcT = (jnp.concatenate(ocT, axis=0) * sgcT_ref[0]).astype(BF16)
    tm = mcT.shape[1]
    for rows in (slice(0, tm // 2), slice(tm // 2, tm)):
        h_new = x_ref[0, rows, :] + h_ab[rows] + _tn_dot(mcT[:, rows], wo_ref[A_WIDTH + B_WIDTH:, :])
        o_ref[0, rows, :] = _rmsnorm_rows(h_new, fw_ref[...])


def _out(x, ma, mb, qc, sgc, km, vm, w_out, fw):
    B, S, D = x.shape
    tm = OUT_TM
    row = lambda width: pl.BlockSpec((1, tm, width), lambda b, i: (b, i, 0))
    col = lambda width: pl.BlockSpec((1, width, tm), lambda b, i: (b, 0, i))
    return pl.pallas_call(
        _out_kernel,
        grid=(B, S // tm),
        in_specs=[row(D), col(A_WIDTH), row(B_WIDTH), col(C_WIDTH), col(C_WIDTH),
                  pl.BlockSpec((1, MEM_LEN, C_WIDTH), lambda b, i: (b, 0, 0)),
                  pl.BlockSpec((1, C_WIDTH, MEM_LEN), lambda b, i: (b, 0, 0)),
                  pl.BlockSpec(w_out.shape, lambda b, i: (0, 0, 0), pipeline_mode=pl.Buffered(1)),
                  pl.BlockSpec((1, D), lambda b, i: (0, 0))],
        out_specs=row(D),
        out_shape=jax.ShapeDtypeStruct((B, S, D), F32),
        scratch_shapes=[pltpu.VMEM((MIX_WIDTH, D), BF16)],
        compiler_params=pltpu.CompilerParams(
            dimension_semantics=("arbitrary", "arbitrary"), vmem_limit_bytes=VMEM_LIMIT),
        name="out",
    )(x, ma, mb, qc, sgc, km, vm, w_out, fw)


def kernel(x, mem, norm_w, w_in, w_alpha2, b_alpha, gla_norm_w, mem_norm_w, w_mem_kv, w_out,
           rel_bias, final_norm_w):
    assert norm_w.shape[0] == 1, "single layer"
    assert w_in.shape[2] == W_IN_ALIGNED + GLA_RANK + 2 * C_WIDTH

    bias_own, bias_prev, km, vm = _prep(rel_bias, mem, mem_norm_w[0][None, :], w_mem_kv)
    qaT, vaT, ka, sga, qc, sgc, mb, kmean = _proj(
        x, norm_w[0][None, :], w_in[0].T, w_alpha2[0], b_alpha[0][None, :], gla_norm_w[0][None, :])
    ma = _moba(rel_bias, qaT, ka, vaT, bias_own, bias_prev, sga, kmean)
    return _out(x, ma, mb, qc, sgc, km, vm, w_out, final_norm_w[None, :])
```
